```python
import math
import jax, jax.numpy as jnp
from jax import lax
import numpy as np

D_MODEL = 2048
BATCH = 4
SEQ = 2048
DEPTH = 2

DN_ALPHA = (2.0 * DEPTH) ** 0.25
DN_BETA = (8.0 * DEPTH) ** -0.25
N_EVEN = (DEPTH + 1) // 2
N_ODD = DEPTH // 2

GDN_HEAD_DIM = 128
GDN_HEADS = D_MODEL // (2 * GDN_HEAD_DIM)
GDN_WIDTH = GDN_HEADS * GDN_HEAD_DIM
GDN_CONV = 4
GDN_CHUNK = 64
SC_WIDTH = D_MODEL - GDN_WIDTH
SC_CONV = 3
HYB_SPLITS = [3 * GDN_WIDTH, 4 * GDN_WIDTH, 4 * GDN_WIDTH + GDN_HEADS,
              4 * GDN_WIDTH + 2 * GDN_HEADS, 4 * GDN_WIDTH + 2 * GDN_HEADS + SC_WIDTH,
              4 * GDN_WIDTH + 2 * GDN_HEADS + 2 * SC_WIDTH]
HYB_IN = 4 * GDN_WIDTH + 2 * GDN_HEADS + 3 * SC_WIDTH
HYB_OUT = GDN_WIDTH + SC_WIDTH

MLA_HEADS = 16
MLA_Q_RANK = 512
MLA_KV_RANK = 512
MLA_NOPE = 128
MLA_ROPE = 64
MLA_V = 128
MLA_IN = MLA_Q_RANK + MLA_KV_RANK + MLA_ROPE
ROPE_THETA = 10000.0
Q_BLOCK = 128
MAX_POS_OFFSET = 4096

MOE_GROUPS = 8
MOE_PER_GROUP = 8
MOE_EXPERTS = MOE_GROUPS * MOE_PER_GROUP
MOE_TOPK = 2
MOE_FF = 512
MOE_BLOCK = 128

kernel_name = "hybrid_gdn_shortconv_mla_hmoe_deepnorm_adaln"

F32 = jnp.float32


def _layer_norm(x, g, b, eps=1e-5):
    xf = x.astype(F32)
    mu = jnp.mean(xf, -1, keepdims=True)
    var = jnp.mean(jnp.square(xf - mu), -1, keepdims=True)
    return ((xf - mu) * lax.rsqrt(var + eps) * g.astype(F32) + b.astype(F32)).astype(x.dtype)


def _rms_norm(x, w, eps=1e-6):
    xf = x.astype(F32)
    return (xf * lax.rsqrt(jnp.mean(xf * xf, -1, keepdims=True) + eps) * w.astype(F32)).astype(x.dtype)


def _l2norm(x, eps=1e-6):
    xf = x.astype(F32)
    return xf * lax.rsqrt(jnp.sum(xf * xf, -1, keepdims=True) + eps)


def _causal_dwconv(x, w):
    k, s = w.shape[0], x.shape[1]
    xp = jnp.pad(x, ((0, 0), (k - 1, 0), (0, 0)))
    y = w[0] * xp[:, 0:s]
    for j in range(1, k):
        y = y + w[j] * xp[:, j:j + s]
    return y


def _gated_delta_rule(q, k, v, g, beta):
    b, s, h, dk = q.shape
    dv = v.shape[-1]
    L = GDN_CHUNK
    n = s // L

    def chunks(t):
        t = t.astype(F32).reshape(b, n, L, h, *t.shape[3:])
        return jnp.moveaxis(t, (1, 3), (0, 2))

    qc = chunks(q) * (dk ** -0.5)
    kc, vc, bc = chunks(k), chunks(v), chunks(beta)
    gc = jnp.cumsum(chunks(g), -1)
    causal = jnp.tril(jnp.ones((L, L), bool))
    strict = jnp.tril(jnp.ones((L, L), bool), -1)
    diff = gc[..., :, None] - gc[..., None, :]
    decay = jnp.where(causal, jnp.exp(jnp.where(causal, diff, 0.0)), 0.0)
    kb = kc * bc[..., None]
    m = jnp.where(strict, jnp.einsum('nbhid,nbhjd->nbhij', kb, kc) * decay, 0.0)
    a = m + jnp.eye(L, dtype=F32)
    rhs = jnp.concatenate([vc * bc[..., None], kb * jnp.exp(gc)[..., None]], -1)
    sol = lax.linalg.triangular_solve(a, rhs, left_side=True, lower=True, unit_diagonal=True)
    u, w = sol[..., :dv], sol[..., dv:]
    qk = jnp.where(causal, jnp.einsum('nbhid,nbhjd->nbhij', qc, kc) * decay, 0.0)
    g_last = gc[..., -1]

    def step(state, inp):
        q_i, k_i, u_i, w_i, qk_i, g_i, gl_i = inp
        v_new = u_i - jnp.einsum('bhld,bhde->bhle', w_i, state)
        o = (jnp.einsum('bhld,bhde->bhle', q_i * jnp.exp(g_i)[..., None], state)
             + jnp.einsum('bhij,bhje->bhie', qk_i, v_new))
        k_dec = k_i * jnp.exp(gl_i[..., None] - g_i)[..., None]
        state = state * jnp.exp(gl_i)[..., None, None] + jnp.einsum('bhld,bhle->bhde', k_dec, v_new)
        return state, o

    state0 = jnp.zeros((b, h, dk, dv), F32)
    _, o = lax.scan(step, state0, (qc, kc, u, w, qk, gc, g_last))
    return jnp.moveaxis(o, (0, 2), (1, 3)).reshape(b, s, h, dv)


def _gdn_shortconv_mixer(h, w_in, conv_w, a_log, dt_bias, norm_w, sc_w, w_out):
    b, s, _ = h.shape
    qkv, z, beta_raw, a_raw, sc_b, sc_c, sc_h = jnp.split(h @ w_in, HYB_SPLITS, -1)
    qkv = jax.nn.silu(_causal_dwconv(qkv, conv_w))
    q, k, v = jnp.split(qkv.reshape(b, s, 3 * GDN_HEADS, GDN_HEAD_DIM), 3, axis=2)
    beta = jax.nn.sigmoid(beta_raw.astype(F32))
    g = -jnp.exp(a_log.astype(F32)) * jax.nn.softplus(a_raw.astype(F32) + dt_bias.astype(F32))
    o = _gated_delta_rule(_l2norm(q), _l2norm(k), v, g, beta)
    o = _rms_norm(o, norm_w) * jax.nn.silu(z.reshape(b, s, GDN_HEADS, GDN_HEAD_DIM).astype(F32))
    y_a = o.reshape(b, s, GDN_WIDTH).astype(h.dtype)
    y_b = sc_b * _causal_dwconv(sc_c * sc_h, sc_w)
    return jnp.concatenate([y_a, y_b], -1) @ w_out


def _rope_tables(pos):
    half = MLA_ROPE // 2
    inv = ROPE_THETA ** (-jnp.arange(half, dtype=F32) * (2.0 / MLA_ROPE))
    ang = pos.astype(F32)[..., None] * inv
    return jnp.cos(ang), jnp.sin(ang)


def _rope(x, cos, sin):
    x1, x2 = jnp.split(x.astype(F32), 2, -1)
    return jnp.concatenate([x1 * cos - x2 * sin, x1 * sin + x2 * cos], -1).astype(x.dtype)


def _mla_attention(q_nope, q_pe, k_nope, k_pe, v):
    b, s, h, _ = q_nope.shape
    nb = s // Q_BLOCK
    scale = (MLA_NOPE + MLA_ROPE) ** -0.5
    kpos = jnp.arange(s)

    def to_blocks(t):
        return jnp.moveaxis(t.reshape(b, nb, Q_BLOCK, *t.shape[2:]), 1, 0)

    def attend(args):
        i, qn, qp = args
        sc = (jnp.einsum('bqhd,bkhd->bhqk', qn, k_nope, preferred_element_type=F32)
              + jnp.einsum('bqhr,bkr->bhqk', qp, k_pe, preferred_element_type=F32)) * scale
        qpos = i * Q_BLOCK + jnp.arange(Q_BLOCK)
        sc = jnp.where(qpos[:, None] >= kpos[None, :], sc, -jnp.inf)
        p = jax.nn.softmax(sc, -1)
        return jnp.einsum('bhqk,bkhd->bqhd', p.astype(v.dtype), v)

    o = lax.map(attend, (jnp.arange(nb), to_blocks(q_nope), to_blocks(q_pe)))
    return jnp.moveaxis(o, 0, 1).reshape(b, s, h, MLA_V)


def _mla_mixer(h, positions, w_in, q_norm, kv_norm, w_uq, w_ukv, w_out):
    b, s, _ = h.shape
    cq, ckv, k_pe = jnp.split(h @ w_in, [MLA_Q_RANK, MLA_Q_RANK + MLA_KV_RANK], -1)
    q = (_rms_norm(cq, q_norm) @ w_uq).reshape(b, s, MLA_HEADS, MLA_NOPE + MLA_ROPE)
    kv = (_rms_norm(ckv, kv_norm) @ w_ukv).reshape(b, s, MLA_HEADS, MLA_NOPE + MLA_V)
    q_nope, q_pe = jnp.split(q, [MLA_NOPE], -1)
    k_nope, v = jnp.split(kv, [MLA_NOPE], -1)
    cos, sin = _rope_tables(positions)
    q_pe = _rope(q_pe, cos[:, :, None], sin[:, :, None])
    k_pe = _rope(k_pe, cos, sin)
    o = _mla_attention(q_nope, q_pe, k_nope, k_pe, v)
    return o.reshape(b, s, MLA_HEADS * MLA_V) @ w_out


def _hier_moe(h, wr_g, br_g, wr_e, br_e, w_gate, w_up, w_down):
    b, s, d = h.shape
    t = b * s
    xt = h.reshape(t, d)
    lg = (xt @ wr_g).astype(F32) + br_g.astype(F32)
    pg = jax.nn.softmax(lg, -1)
    grp = jnp.argmax(lg, -1).astype(jnp.int32)
    pg_sel = jnp.take_along_axis(pg, grp[:, None], -1)
    le = ((xt @ wr_e).astype(F32) + br_e.astype(F32)).reshape(t, MOE_GROUPS, MOE_PER_GROUP)
    le = jnp.take_along_axis(le, grp[:, None, None], 1)[:, 0]
    top_p, top_i = lax.top_k(jax.nn.softmax(le, -1), MOE_TOPK)
    gates = pg_sel * top_p / jnp.sum(top_p, -1, keepdims=True)
    experts = grp[:, None] * MOE_PER_GROUP + top_i.astype(jnp.int32)
    n_assign = t * MOE_TOPK
    e_flat = experts.reshape(-1)
    g_flat = gates.reshape(-1)
    tok_flat = jnp.repeat(jnp.arange(t, dtype=jnp.int32), MOE_TOPK)
    order = jnp.argsort(e_flat)
    e_sorted = e_flat[order]
    counts = jnp.bincount(e_flat, length=MOE_EXPERTS)
    padded = (counts + MOE_BLOCK - 1) // MOE_BLOCK * MOE_BLOCK
    pad_end = jnp.cumsum(padded)
    pad_start = pad_end - padded
    start = jnp.cumsum(counts) - counts
    dest = pad_start[e_sorted] + jnp.arange(n_assign) - start[e_sorted]
    n_rows = n_assign + MOE_EXPERTS * MOE_BLOCK
    n_blocks = n_rows // MOE_BLOCK
    row_tok = jnp.full((n_rows,), t, jnp.int32).at[dest].set(tok_flat[order])
    row_gate = jnp.zeros((n_rows,), F32).at[dest].set(g_flat[order])
    block_expert = jnp.minimum(
        jnp.searchsorted(pad_end, jnp.arange(n_blocks) * MOE_BLOCK, side='right'), MOE_EXPERTS - 1)
    x_pad = jnp.concatenate([xt, jnp.zeros((1, d), xt.dtype)], 0)
    xs = x_pad[row_tok].reshape(n_blocks, MOE_BLOCK, d)

    def expert_block(args):
        e, xb = args
        hid = jax.nn.silu(xb @ w_gate[e]) * (xb @ w_up[e])
        return hid @ w_down[e]

    ys = lax.map(expert_block, (block_expert, xs)).reshape(n_rows, d)
    ys = ys * row_gate[:, None].astype(ys.dtype)
    out = jax.ops.segment_sum(ys, row_tok, num_segments=t + 1)[:t]
    return out.reshape(b, s, d)


def setup_inputs(seed: int = 0) -> dict:
    key = jax.random.key(seed)
    ks = iter(jax.random.split(key, 48))

    def nrm(shape, std):
        return jax.random.normal(next(ks), shape, F32) * std

    ne, no, D = N_EVEN, N_ODD, D_MODEL
    x = nrm((BATCH, SEQ, D), 1.0)
    c = nrm((BATCH, D), 1.0)
    positions = (jax.random.randint(next(ks), (BATCH, 1), 0, MAX_POS_OFFSET, jnp.int32)
                 + jnp.arange(SEQ, dtype=jnp.int32)[None, :])
    ada_w = nrm((DEPTH, D, 6 * D), 0.1 * D ** -0.5)
    ada_b = nrm((DEPTH, 6 * D), 0.01)
    ln_g = 1.0 + nrm((DEPTH, 2, D), 0.02)
    ln_b = nrm((DEPTH, 2, D), 0.02)
    hyb_w_in = nrm((ne, D, HYB_IN), D ** -0.5)
    gdn_conv_w = nrm((ne, GDN_CONV, 3 * GDN_WIDTH), GDN_CONV ** -0.5)
    gdn_a_log = jnp.log(jax.random.uniform(next(ks), (ne, GDN_HEADS), F32, 1.0, 16.0))
    dt = jnp.exp(jax.random.uniform(next(ks), (ne, GDN_HEADS), F32, math.log(1e-3), math.log(1e-1)))
    gdn_dt_bias = dt + jnp.log(-jnp.expm1(-dt))
    gdn_norm_w = 1.0 + nrm((ne, GDN_HEAD_DIM), 0.02)
    sc_conv_w = nrm((ne, SC_CONV, SC_WIDTH), SC_CONV ** -0.5)
    hyb_w_out = nrm((ne, HYB_OUT, D), DN_BETA * HYB_OUT ** -0.5)
    mla_w_in = nrm((no, D, MLA_IN), D ** -0.5)
    mla_q_norm = 1.0 + nrm((no, MLA_Q_RANK), 0.02)
    mla_kv_norm = 1.0 + nrm((no, MLA_KV_RANK), 0.02)
    mla_w_uq = nrm((no, MLA_Q_RANK, MLA_HEADS * (MLA_NOPE + MLA_ROPE)), MLA_Q_RANK ** -0.5)
    mla_w_ukv = nrm((no, MLA_KV_RANK, MLA_HEADS * (MLA_NOPE + MLA_V)), MLA_KV_RANK ** -0.5)
    mla_w_out = nrm((no, MLA_HEADS * MLA_V, D), DN_BETA * (MLA_HEADS * MLA_V) ** -0.5)
    moe_router_g = nrm((DEPTH, D, MOE_GROUPS), D ** -0.5)
    moe_bias_g = nrm((DEPTH, MOE_GROUPS), 0.01)
    moe_router_e = nrm((DEPTH, D, MOE_EXPERTS), D ** -0.5)
    moe_bias_e = nrm((DEPTH, MOE_EXPERTS), 0.01)
    moe_w_gate = nrm((DEPTH, MOE_EXPERTS, D, MOE_FF), D ** -0.5)
    moe_w_up = nrm((DEPTH, MOE_EXPERTS, D, MOE_FF), D ** -0.5)
    moe_w_down = nrm((DEPTH, MOE_EXPERTS, MOE_FF, D), DN_BETA * MOE_FF ** -0.5)
    return {"x": x, "c": c, "positions": positions, "ada_w": ada_w, "ada_b": ada_b,
            "ln_g": ln_g, "ln_b": ln_b, "hyb_w_in": hyb_w_in, "gdn_conv_w": gdn_conv_w,
            "gdn_a_log": gdn_a_log, "gdn_dt_bias": gdn_dt_bias, "gdn_norm_w": gdn_norm_w,
            "sc_conv_w": sc_conv_w, "hyb_w_out": hyb_w_out, "mla_w_in": mla_w_in,
            "mla_q_norm": mla_q_norm, "mla_kv_norm": mla_kv_norm, "mla_w_uq": mla_w_uq,
            "mla_w_ukv": mla_w_ukv, "mla_w_out": mla_w_out, "moe_router_g": moe_router_g,
            "moe_bias_g": moe_bias_g, "moe_router_e": moe_router_e, "moe_bias_e": moe_bias_e,
            "moe_w_gate": moe_w_gate, "moe_w_up": moe_w_up, "moe_w_down": moe_w_down}


def reference(x, c, positions, ada_w, ada_b, ln_g, ln_b, hyb_w_in, gdn_conv_w, gdn_a_log,
              gdn_dt_bias, gdn_norm_w, sc_conv_w, hyb_w_out, mla_w_in, mla_q_norm, mla_kv_norm,
              mla_w_uq, mla_w_ukv, mla_w_out, moe_router_g, moe_bias_g, moe_router_e, moe_bias_e,
              moe_w_gate, moe_w_up, moe_w_down):
    c_act = jax.nn.silu(c)
    for layer in range(DEPTH):
        mod = (c_act @ ada_w[layer] + ada_b[layer])[:, None, :]
        sh1, sc1, g1, sh2, sc2, g2 = jnp.split(mod, 6, -1)
        hin = x * (1.0 + sc1) + sh1
        i = layer // 2
        if layer % 2 == 0:
            y = _gdn_shortconv_mixer(hin, hyb_w_in[i], gdn_conv_w[i], gdn_a_log[i], gdn_dt_bias[i],
                                     gdn_norm_w[i], sc_conv_w[i], hyb_w_out[i])
        else:
            y = _mla_mixer(hin, positions, mla_w_in[i], mla_q_norm[i], mla_kv_norm[i],
                           mla_w_uq[i], mla_w_ukv[i], mla_w_out[i])
        x = _layer_norm(DN_ALPHA * x + (1.0 + g1) * y, ln_g[layer, 0], ln_b[layer, 0])
        hin = x * (1.0 + sc2) + sh2
        y = _hier_moe(hin, moe_router_g[layer], moe_bias_g[layer], moe_router_e[layer],
                      moe_bias_e[layer], moe_w_gate[layer], moe_w_up[layer], moe_w_down[layer])
        x = _layer_norm(DN_ALPHA * x + (1.0 + g2) * y, ln_g[layer, 1], ln_b[layer, 1])
    return x
```

```python
import functools

import jax
import jax.numpy as jnp
from jax import lax
from jax.experimental import pallas as pl
from jax.experimental.pallas import tpu as pltpu

F32, BF16, I32 = jnp.float32, jnp.bfloat16, jnp.int32
HIGHEST = lax.Precision.HIGHEST

LANES = 128
SUBLANES = 8
VMEM_LIMIT_BYTES = 56 * 1024 * 1024

GDN_HEAD_DIM = 128
GDN_CONV = 4
GDN_CHUNK = 64
SC_CONV = 3
MLA_NOPE = 128
MLA_ROPE = 64
MLA_V = 128
MLA_Q_RANK = 512
MLA_KV_RANK = 512
ROPE_THETA = 10000.0
MOE_TOPK = 2
MOE_BLOCK = 128


def _tile(n, pref):
    if n <= pref:
        return n
    for t in range(pref, 0, -LANES):
        if n % t == 0:
            return t
    return n


def _cparams(*sem):
    return pltpu.CompilerParams(dimension_semantics=sem, vmem_limit_bytes=VMEM_LIMIT_BYTES)


def _sigmoid(x):
    return 1.0 / (1.0 + jnp.exp(-x))


def _silu(x):
    return x * _sigmoid(x)


def _softplus(x):
    return jnp.maximum(x, 0.0) + jnp.log(1.0 + jnp.exp(-jnp.abs(x)))


def _dot(a, b):
    return jnp.dot(a.astype(BF16), b.astype(BF16), preferred_element_type=F32)


def _dot_nt(a, b):
    return lax.dot_general(a.astype(BF16), b.astype(BF16), (((1,), (1,)), ((), ())),
                           preferred_element_type=F32)


def _mod_kernel(c_ref, w_ref, b_ref, o_ref):
    c = c_ref[...]
    o_ref[0] = _dot(_silu(c), w_ref[0]) + b_ref[0]


def _modulation(c, ada_w, ada_b):
    depth, d, n = ada_w.shape
    b = c.shape[0]
    rows = -(-b // SUBLANES) * SUBLANES
    cp = jnp.zeros((rows, d), F32).at[:b].set(c)
    tn = _tile(n, 1024)
    out = pl.pallas_call(
        _mod_kernel,
        out_shape=jax.ShapeDtypeStruct((depth, rows, n), F32),
        grid=(depth, n // tn),
        in_specs=[pl.BlockSpec((rows, d), lambda l, j: (0, 0)),
                  pl.BlockSpec((1, d, tn), lambda l, j: (l, 0, j)),
                  pl.BlockSpec((1, 1, tn), lambda l, j: (l, 0, j))],
        out_specs=pl.BlockSpec((1, rows, tn), lambda l, j: (l, 0, j)),
        compiler_params=_cparams("parallel", "parallel"),
        name="adaln_mod",
    )(cp, ada_w, ada_b.reshape(depth, 1, n))
    return out[:, :b]


def _modmm_kernel(x_ref, sc_ref, sh_ref, w_ref, o_ref, xb_ref):
    @pl.when(pl.program_id(2) == 0)
    def _():
        xb_ref[...] = (x_ref[0] * (1.0 + sc_ref[0]) + sh_ref[0]).astype(BF16)

    o_ref[0] = jnp.dot(xb_ref[...], w_ref[...], preferred_element_type=F32)


def _mod_matmul(x, sc, sh, w, name):
    b, s, d = x.shape
    n = w.shape[1]
    tm = _tile(s, 1024)
    tn = _tile(n, 768)
    return pl.pallas_call(
        _modmm_kernel,
        out_shape=jax.ShapeDtypeStruct((b, s, n), F32),
        grid=(b, s // tm, n // tn),
        in_specs=[pl.BlockSpec((1, tm, d), lambda bi, i, j: (bi, i, 0)),
                  pl.BlockSpec((1, 1, d), lambda bi, i, j: (bi, 0, 0)),
                  pl.BlockSpec((1, 1, d), lambda bi, i, j: (bi, 0, 0)),
                  pl.BlockSpec((d, tn), lambda bi, i, j: (0, j))],
        out_specs=pl.BlockSpec((1, tm, tn), lambda bi, i, j: (bi, i, j)),
        scratch_shapes=[pltpu.VMEM((tm, d), BF16)],
        compiler_params=_cparams("parallel", "parallel", "arbitrary"),
        name=name,
    )(x, sc, sh, w)


def _rmsmm_kernel(x_ref, nw_ref, w_ref, o_ref, xb_ref):
    @pl.when(pl.program_id(2) == 0)
    def _():
        x = x_ref[0]
        xb_ref[...] = (x * lax.rsqrt(jnp.mean(x * x, -1, keepdims=True) + 1e-6) * nw_ref[...]).astype(BF16)

    o_ref[0] = jnp.dot(xb_ref[...], w_ref[...], preferred_element_type=F32)


def _rms_matmul(x, col_block, k, norm_w, w, name):
    b, s, _ = x.shape
    n = w.shape[1]
    tm = _tile(s, 1024)
    tn = _tile(n, 1024)
    return pl.pallas_call(
        _rmsmm_kernel,
        out_shape=jax.ShapeDtypeStruct((b, s, n), F32),
        grid=(b, s // tm, n // tn),
        in_specs=[pl.BlockSpec((1, tm, k), lambda bi, i, j: (bi, i, col_block)),
                  pl.BlockSpec((1, k), lambda bi, i, j: (0, 0)),
                  pl.BlockSpec((k, tn), lambda bi, i, j: (0, j))],
        out_specs=pl.BlockSpec((1, tm, tn), lambda bi, i, j: (bi, i, j)),
        scratch_shapes=[pltpu.VMEM((tm, k), BF16)],
        compiler_params=_cparams("parallel", "parallel", "arbitrary"),
        name=name,
    )(x, norm_w.reshape(1, k), w)


def _layer_norm_rows(r, g, b):
    mu = jnp.mean(r, -1, keepdims=True)
    var = jnp.mean(jnp.square(r - mu), -1, keepdims=True)
    return (r - mu) * lax.rsqrt(var + 1e-5) * g + b


def _outln_kernel(y_ref, w_ref, xres_ref, g_ref, lng_ref, lnb_ref, sc_ref, sh_ref, wr_ref, br_ref,
                  xo_ref, hin_ref, ids_ref, gates_ref, *, alpha, n_groups, per_group):
    y = jnp.dot(y_ref[0].astype(BF16), w_ref[...], preferred_element_type=F32)
    r = alpha * xres_ref[0] + (1.0 + g_ref[0]) * y
    xn = _layer_norm_rows(r, lng_ref[...], lnb_ref[...])
    xo_ref[0] = xn
    hin = xn * (1.0 + sc_ref[0]) + sh_ref[0]
    hin_ref[0] = hin
    logits = jnp.dot(hin, wr_ref[...], precision=HIGHEST, preferred_element_type=F32) + br_ref[...]
    lane = lax.broadcasted_iota(I32, logits.shape, 1)
    big = jnp.int32(4 * LANES)
    neg = jnp.float32(-jnp.inf)
    n_exp = n_groups * per_group
    gmask = lane < n_groups
    lg = jnp.where(gmask, logits, neg)
    mg = jnp.max(lg, -1, keepdims=True)
    grp = jnp.min(jnp.where(gmask & (lg == mg), lane, big), -1, keepdims=True)
    pg_sel = 1.0 / jnp.sum(jnp.where(gmask, jnp.exp(lg - mg), 0.0), -1, keepdims=True)
    lo = n_groups + grp * per_group
    emask = (lane >= lo) & (lane < lo + per_group) & (lane < n_groups + n_exp)
    le = jnp.where(emask, logits, neg)
    me = jnp.max(le, -1, keepdims=True)
    ee = jnp.where(emask, jnp.exp(le - me), 0.0)
    p = ee / jnp.sum(ee, -1, keepdims=True)
    pm = jnp.where(emask, p, -1.0)
    p1 = jnp.max(pm, -1, keepdims=True)
    i1 = jnp.min(jnp.where(emask & (pm == p1), lane, big), -1, keepdims=True)
    pm2 = jnp.where(lane == i1, -1.0, pm)
    p2 = jnp.max(pm2, -1, keepdims=True)
    i2 = jnp.min(jnp.where(emask & (lane != i1) & (pm2 == p2), lane, big), -1, keepdims=True)
    den = p1 + p2
    g1 = pg_sel * p1 / den
    g2 = pg_sel * p2 / den
    ids_ref[0] = jnp.where(lane == 0, i1 - n_groups, jnp.where(lane == 1, i2 - n_groups, 0))
    gates_ref[0] = jnp.where(lane == 0, g1, jnp.where(lane == 1, g2, 0.0))


def _out_ln_route(y, w, xres, gate, ln_g, ln_b, sc, sh, wr, br, alpha, n_groups, per_group, name):
    b, s, k = y.shape
    d = w.shape[1]
    tm = _tile(s, 256)
    kern = functools.partial(_outln_kernel, alpha=alpha, n_groups=n_groups, per_group=per_group)
    row = lambda bi, i: (bi, i, 0)
    per_b = lambda bi, i: (bi, 0, 0)
    const = lambda bi, i: (0, 0)
    return pl.pallas_call(
        kern,
        out_shape=(jax.ShapeDtypeStruct((b, s, d), F32), jax.ShapeDtypeStruct((b, s, d), F32),
                   jax.ShapeDtypeStruct((b, s, LANES), I32), jax.ShapeDtypeStruct((b, s, LANES), F32)),
        grid=(b, s // tm),
        in_specs=[pl.BlockSpec((1, tm, k), row),
                  pl.BlockSpec((k, d), const),
                  pl.BlockSpec((1, tm, d), row),
                  pl.BlockSpec((1, 1, d), per_b),
                  pl.BlockSpec((1, d), const),
                  pl.BlockSpec((1, d), const),
                  pl.BlockSpec((1, 1, d), per_b),
                  pl.BlockSpec((1, 1, d), per_b),
                  pl.BlockSpec((d, LANES), const),
                  pl.BlockSpec((1, LANES), const)],
        out_specs=(pl.BlockSpec((1, tm, d), row), pl.BlockSpec((1, tm, d), row),
                   pl.BlockSpec((1, tm, LANES), row), pl.BlockSpec((1, tm, LANES), row)),
        compiler_params=_cparams("parallel", "parallel"),
        name=name,
    )(y, w, xres, gate, ln_g.reshape(1, d), ln_b.reshape(1, d), sc, sh, wr, br)


def _combine_kernel(y2_ref, x_ref, g_ref, lng_ref, lnb_ref, o_ref, *, alpha, d):
    y = y2_ref[0, :, 0:d] + y2_ref[0, :, d:2 * d]
    r = alpha * x_ref[0] + (1.0 + g_ref[0]) * y
    o_ref[0] = _layer_norm_rows(r, lng_ref[...], lnb_ref[...])


def _combine_ln(y2, x, gate, ln_g, ln_b, alpha, name):
    b, s, d = x.shape
    tm = _tile(s, 512)
    row = lambda bi, i: (bi, i, 0)
    return pl.pallas_call(
        functools.partial(_combine_kernel, alpha=alpha, d=d),
        out_shape=jax.ShapeDtypeStruct((b, s, d), F32),
        grid=(b, s // tm),
        in_specs=[pl.BlockSpec((1, tm, 2 * d), row),
                  pl.BlockSpec((1, tm, d), row),
                  pl.BlockSpec((1, 1, d), lambda bi, i: (bi, 0, 0)),
                  pl.BlockSpec((1, d), lambda bi, i: (0, 0)),
                  pl.BlockSpec((1, d), lambda bi, i: (0, 0))],
        out_specs=pl.BlockSpec((1, tm, d), row),
        compiler_params=_cparams("parallel", "parallel"),
        name=name,
    )(y2.reshape(b, s, 2 * d), x, gate, ln_g.reshape(1, d), ln_b.reshape(1, d))


def _moe_kernel(bexp_ref, bcnt_ref, rsrc_ref, rdst_ref,
                hin_hbm, wg_ref, wu_ref, wd_ref, rgate_ref,
                y2_hbm,
                xbuf, ybuf, wgb, wub, wdb, gsem, ssem, *, blk, nb):
    i = pl.program_id(0)
    slot = i % 2

    def gather(block, sl, start):
        base = block * blk

        def body(r, c):
            cp = pltpu.make_async_copy(hin_hbm.at[pl.ds(rsrc_ref[base + r], 1), :],
                                       xbuf.at[sl, pl.ds(r, 1), :], gsem.at[sl])
            if start:
                cp.start()
            else:
                cp.wait()
            return c

        lax.fori_loop(0, bcnt_ref[block], body, 0)

    def scatter(block, sl, start):
        base = block * blk

        def body(r, c):
            cp = pltpu.make_async_copy(ybuf.at[sl, pl.ds(r, 1), :],
                                       y2_hbm.at[pl.ds(rdst_ref[base + r], 1), :], ssem.at[sl])
            if start:
                cp.start()
            else:
                cp.wait()
            return c

        lax.fori_loop(0, bcnt_ref[block], body, 0)

    @pl.when(i == 0)
    def _():
        xbuf[...] = jnp.zeros_like(xbuf)
        gather(0, 0, True)

    @pl.when(i + 1 < nb)
    def _():
        gather(i + 1, 1 - slot, True)

    gather(i, slot, False)

    changed = jnp.logical_or(i == 0, bexp_ref[i] != bexp_ref[jnp.maximum(i - 1, 0)])

    @pl.when(changed)
    def _():
        wgb[...] = wg_ref[0].astype(BF16)
        wub[...] = wu_ref[0].astype(BF16)
        wdb[...] = wd_ref[0].astype(BF16)

    @pl.when(i >= 2)
    def _():
        scatter(i - 2, slot, False)

    @pl.when(bcnt_ref[i] > 0)
    def _():
        xb = xbuf[slot].astype(BF16)
        hg = jnp.dot(xb, wgb[...], preferred_element_type=F32)
        hu = jnp.dot(xb, wub[...], preferred_element_type=F32)
        hid = (_silu(hg) * hu).astype(BF16)
        y = jnp.dot(hid, wdb[...], preferred_element_type=F32)
        ybuf[slot] = y * rgate_ref[0]
        scatter(i, slot, True)

    @pl.when(i == nb - 1)
    def _():
        if nb >= 2:
            scatter(i - 1, 1 - slot, False)
        scatter(i, slot, False)


def _moe_experts(hin2, w_gate, w_up, w_down, blk_exp, blk_cnt, row_src, row_dst, row_gate, name):
    t, d = hin2.shape
    n_exp, _, ff = w_gate.shape
    nb = blk_exp.shape[0]
    blk = MOE_BLOCK
    grid_spec = pltpu.PrefetchScalarGridSpec(
        num_scalar_prefetch=4,
        grid=(nb,),
        in_specs=[pl.BlockSpec(memory_space=pl.ANY),
                  pl.BlockSpec((1, d, ff), lambda i, be, bc, rs, rd: (be[i], 0, 0)),
                  pl.BlockSpec((1, d, ff), lambda i, be, bc, rs, rd: (be[i], 0, 0)),
                  pl.BlockSpec((1, ff, d), lambda i, be, bc, rs, rd: (be[i], 0, 0)),
                  pl.BlockSpec((1, blk, 1), lambda i, be, bc, rs, rd: (i, 0, 0))],
        out_specs=pl.BlockSpec(memory_space=pl.ANY),
        scratch_shapes=[pltpu.VMEM((2, blk, d), F32),
                        pltpu.VMEM((2, blk, d), F32),
                        pltpu.VMEM((d, ff), BF16),
                        pltpu.VMEM((d, ff), BF16),
                        pltpu.VMEM((ff, d), BF16),
                        pltpu.SemaphoreType.DMA((2,)),
                        pltpu.SemaphoreType.DMA((2,))],
    )
    return pl.pallas_call(
        functools.partial(_moe_kernel, blk=blk, nb=nb),
        out_shape=jax.ShapeDtypeStruct((t * MOE_TOPK, d), F32),
        grid_spec=grid_spec,
        compiler_params=_cparams("arbitrary"),
        name=name,
    )(blk_exp, blk_cnt, row_src, row_dst, hin2, w_gate, w_up, w_down, row_gate.reshape(nb, blk, 1))


def _dispatch_tables(ids, gates, n_exp):
    t = ids.shape[0]
    n_assign = t * MOE_TOPK
    blk = MOE_BLOCK
    e_flat = ids.reshape(-1)
    g_flat = gates.reshape(-1)
    order = jnp.argsort(e_flat).astype(I32)
    e_sorted = e_flat[order]
    counts = jnp.bincount(e_flat, length=n_exp).astype(I32)
    padded = (counts + blk - 1) // blk * blk
    pad_end = jnp.cumsum(padded)
    pad_start = pad_end - padded
    start = jnp.cumsum(counts) - counts
    dest = pad_start[e_sorted] + jnp.arange(n_assign, dtype=I32) - start[e_sorted]
    n_rows = n_assign + n_exp * blk
    nb = n_rows // blk
    row_src = jnp.zeros((n_rows,), I32).at[dest].set(order // MOE_TOPK)
    row_dst = jnp.zeros((n_rows,), I32).at[dest].set(order)
    row_gate = jnp.zeros((n_rows,), F32).at[dest].set(g_flat[order])
    blk_first = jnp.arange(nb, dtype=I32) * blk
    blk_exp = jnp.minimum(jnp.searchsorted(pad_end, blk_first, side='right'), n_exp - 1).astype(I32)
    in_use = blk_first < pad_end[-1]
    blk_cnt = jnp.where(in_use, jnp.clip(counts[blk_exp] - (blk_first - pad_start[blk_exp]), 0, blk), 0)
    return blk_exp, blk_cnt.astype(I32), row_src, row_dst, row_gate


def _gdn_kernel(q_ref, k_ref, v_ref, z_ref, bg_ref, cg_ref, hh_ref, ba_ref,
                cw_ref, alog_ref, dtb_ref, nw_ref, scw_ref,
                o_ref,
                ext_ref, qa_ref, ka_ref, va_ref, hist_ref, hist2_ref, gate_ref, state_ref,
                *, heads, ts, gw):
    L = GDN_CHUNK
    dk = GDN_HEAD_DIM

    @pl.when(pl.program_id(1) == 0)
    def _():
        hist_ref[...] = jnp.zeros_like(hist_ref)
        hist2_ref[...] = jnp.zeros_like(hist2_ref)
        state_ref[...] = jnp.zeros_like(state_ref)

    def causal_conv(x, hist, w, taps):
        ext_ref[0:SUBLANES, :] = hist
        ext_ref[SUBLANES:SUBLANES + ts, :] = x
        y = w[0:1] * ext_ref[SUBLANES - taps + 1:SUBLANES - taps + 1 + ts, :]
        for j in range(1, taps - 1):
            y = y + w[j:j + 1] * ext_ref[SUBLANES - taps + 1 + j:SUBLANES - taps + 1 + j + ts, :]
        return y + w[taps - 1:taps] * x

    for idx, (src, dst) in enumerate(((q_ref, qa_ref), (k_ref, ka_ref), (v_ref, va_ref))):
        x = src[0]
        y = causal_conv(x, hist_ref[idx], cw_ref[:, idx * gw:(idx + 1) * gw], GDN_CONV)
        hist_ref[idx] = x[ts - SUBLANES:ts, :]
        dst[...] = _silu(y)

    c = cg_ref[0] * hh_ref[0]
    yb = causal_conv(c, hist2_ref[...], scw_ref[...], SC_CONV)
    hist2_ref[...] = c[ts - SUBLANES:ts, :]
    o_ref[0, :, gw:2 * gw] = bg_ref[0] * yb

    ba = ba_ref[0]
    gate_ref[0] = _sigmoid(ba)
    gate_ref[1] = -jnp.exp(alog_ref[...]) * _softplus(ba + dtb_ref[...])

    ii = lax.broadcasted_iota(I32, (L, L), 0)
    jj = lax.broadcasted_iota(I32, (L, L), 1)
    causal = ii >= jj
    strict = ii > jj
    tri_incl = causal.astype(F32)
    tri_strict = strict.astype(F32)
    eye = (ii == jj).astype(F32)
    nw = nw_ref[...]

    def chunk(ci, carry):
        r0 = pl.multiple_of(ci * L, L)
        rows = pl.ds(r0, L)
        for h in range(heads):
            cols = slice(h * dk, (h + 1) * dk)
            qh = qa_ref[rows, cols]
            kh = ka_ref[rows, cols]
            vh = va_ref[rows, cols]
            beta = gate_ref[0, rows, h:h + 1]
            g = gate_ref[1, rows, heads + h:heads + h + 1]
            rhs_g = jnp.concatenate([g * tri_strict, jnp.broadcast_to(g, (L, L))], axis=1)
            cum = jnp.dot(tri_incl, rhs_g, precision=HIGHEST, preferred_element_type=F32)
            diff = cum[:, 0:L]
            gc = cum[:, L:L + 1]
            gl = gc[L - 1:L, :]
            decay = jnp.where(causal, jnp.exp(jnp.where(causal, diff, 0.0)), 0.0)
            egc = jnp.exp(gc)
            qn = qh * lax.rsqrt(jnp.sum(qh * qh, -1, keepdims=True) + 1e-6)
            kn = kh * lax.rsqrt(jnp.sum(kh * kh, -1, keepdims=True) + 1e-6)
            qc = qn * (dk ** -0.5)
            kb = kn * beta
            m = jnp.where(strict, _dot_nt(kb, kn) * decay, 0.0)
            t_inv = eye - m
            mp = m
            for _ in range(5):
                mp = _dot(mp, mp)
                t_inv = t_inv + _dot(t_inv, mp)
            sol = _dot(t_inv, jnp.concatenate([vh * beta, kb * egc], axis=1))
            u = sol[:, 0:dk]
            w = sol[:, dk:2 * dk]
            qk = jnp.where(causal, _dot_nt(qc, kn) * decay, 0.0)
            st = state_ref[h]
            v_new = u - _dot(w, st)
            o = _dot(qc * egc, st) + _dot(qk, v_new)
            k_dec = kn * jnp.exp(gl - gc)
            state_ref[h] = st * jnp.exp(gl) + _dot(k_dec.T, v_new)
            on = o * lax.rsqrt(jnp.mean(o * o, -1, keepdims=True) + 1e-6) * nw
            zh = z_ref[0, rows, cols]
            o_ref[0, rows, cols] = on * _silu(zh)
        return carry

    lax.fori_loop(0, ts // L, chunk, 0)


def _gdn_shortconv(proj, conv_w, a_log, dt_bias, norm_w, sc_w, heads):
    b, s, _ = proj.shape
    gw = heads * GDN_HEAD_DIM
    ts = _tile(s, 256)
    alog_p = jnp.zeros((1, LANES), F32).at[0, heads:2 * heads].set(a_log)
    dtb_p = jnp.zeros((1, LANES), F32).at[0, heads:2 * heads].set(dt_bias)
    col = lambda cb: (lambda bi, i: (bi, i, cb))
    const = lambda bi, i: (0, 0)
    kern = functools.partial(_gdn_kernel, heads=heads, ts=ts, gw=gw)
    return pl.pallas_call(
        kern,
        out_shape=jax.ShapeDtypeStruct((b, s, 2 * gw), F32),
        grid=(b, s // ts),
        in_specs=[pl.BlockSpec((1, ts, gw), col(cb)) for cb in range(7)]
        + [pl.BlockSpec((1, ts, LANES), col(7 * gw // LANES)),
           pl.BlockSpec((GDN_CONV, 3 * gw), const),
           pl.BlockSpec((1, LANES), const),
           pl.BlockSpec((1, LANES), const),
           pl.BlockSpec((1, GDN_HEAD_DIM), const),
           pl.BlockSpec((SC_CONV, gw), const)],
        out_specs=pl.BlockSpec((1, ts, 2 * gw), lambda bi, i: (bi, i, 0)),
        scratch_shapes=[pltpu.VMEM((ts + SUBLANES, gw), F32),
                        pltpu.VMEM((ts, gw), F32),
                        pltpu.VMEM((ts, gw), F32),
                        pltpu.VMEM((ts, gw), F32),
                        pltpu.VMEM((3, SUBLANES, gw), F32),
                        pltpu.VMEM((SUBLANES, gw), F32),
                        pltpu.VMEM((2, ts, LANES), F32),
                        pltpu.VMEM((heads, GDN_HEAD_DIM, GDN_HEAD_DIM), F32)],
        compiler_params=_cparams("parallel", "arbitrary"),
        name="gdn_shortconv",
    )(proj, proj, proj, proj, proj, proj, proj, proj,
      conv_w, alog_p, dtb_p, norm_w.reshape(1, GDN_HEAD_DIM), sc_w)


def _rope_kernel(pos_ref, inv_ref, qpe_ref, kpe_ref, qo_ref, ko_ref, *, heads):
    ang = pos_ref[0].astype(F32) * inv_ref[...]
    lane = lax.broadcasted_iota(I32, ang.shape, 1)
    half = MLA_ROPE // 2
    cos = jnp.where(lane < MLA_ROPE, jnp.cos(ang), 0.0)
    sin = jnp.sin(ang)
    sgn = jnp.where(lane < half, -sin, jnp.where(lane < MLA_ROPE, sin, 0.0))

    def rot(x):
        swapped = jnp.where(lane < half, pltpu.roll(x, LANES - half, 1), pltpu.roll(x, half, 1))
        return x * cos + swapped * sgn

    for h in range(heads):
        cols = slice(h * LANES, (h + 1) * LANES)
        qo_ref[0, :, cols] = rot(qpe_ref[0, :, cols])
    ko_ref[0] = rot(kpe_ref[0])


def _rope(positions, qup, heads, cproj, kpe_block):
    b, s = positions.shape
    ts = _tile(s, 512)
    half = MLA_ROPE // 2
    inv = ROPE_THETA ** (-jnp.arange(half, dtype=F32) * (2.0 / MLA_ROPE))
    inv_tab = jnp.zeros((1, LANES), F32).at[0, 0:half].set(inv).at[0, half:MLA_ROPE].set(inv)
    w = heads * LANES
    return pl.pallas_call(
        functools.partial(_rope_kernel, heads=heads),
        out_shape=(jax.ShapeDtypeStruct((b, s, w), F32), jax.ShapeDtypeStruct((b, s, LANES), F32)),
        grid=(b, s // ts),
        in_specs=[pl.BlockSpec((1, ts, 1), lambda bi, i: (bi, i, 0)),
                  pl.BlockSpec((1, LANES), lambda bi, i: (0, 0)),
                  pl.BlockSpec((1, ts, w), lambda bi, i: (bi, i, 1)),
                  pl.BlockSpec((1, ts, LANES), lambda bi, i: (bi, i, kpe_block))],
        out_specs=(pl.BlockSpec((1, ts, w), lambda bi, i: (bi, i, 0)),
                   pl.BlockSpec((1, ts, LANES), lambda bi, i: (bi, i, 0))),
        compiler_params=_cparams("parallel", "parallel"),
        name="mla_rope",
    )(positions.reshape(b, s, 1), inv_tab, qup, cproj)


def _attn_kernel(qn_ref, qp_ref, kn_ref, kp_ref, v_ref, o_ref, m_ref, l_ref, acc_ref, *, tq, scale):
    qi = pl.program_id(2)
    qn = qn_ref[0].astype(BF16)
    qp = qp_ref[0].astype(BF16)
    m_ref[...] = jnp.full_like(m_ref, -jnp.inf)
    l_ref[...] = jnp.zeros_like(l_ref)
    acc_ref[...] = jnp.zeros_like(acc_ref)
    qpos = qi * tq + lax.broadcasted_iota(I32, (tq, tq), 0)
    koff = lax.broadcasted_iota(I32, (tq, tq), 1)

    def body(kj, carry):
        ks = pl.multiple_of(kj * tq, tq)
        kn = kn_ref[0, pl.ds(ks, tq), :]
        kp = kp_ref[0, pl.ds(ks, tq), :]
        v = v_ref[0, pl.ds(ks, tq), :]
        s = (_dot_nt(qn, kn) + _dot_nt(qp, kp)) * scale
        s = jnp.where(qpos >= ks + koff, s, -jnp.inf)
        m_old = m_ref[...]
        m_new = jnp.maximum(m_old, jnp.max(s, -1, keepdims=True))
        a = jnp.exp(m_old - m_new)
        p = jnp.exp(s - m_new)
        l_ref[...] = a * l_ref[...] + jnp.sum(p, -1, keepdims=True)
        acc_ref[...] = a * acc_ref[...] + _dot(p, v)
        m_ref[...] = m_new
        return carry

    lax.fori_loop(0, qi + 1, body, 0)
    o_ref[0] = acc_ref[...] / l_ref[...]


def _attention(qup, qpe, kvup, kpe, heads):
    b, s, _ = qup.shape
    tq = _tile(s, 512)
    scale = (MLA_NOPE + MLA_ROPE) ** -0.5
    return pl.pallas_call(
        functools.partial(_attn_kernel, tq=tq, scale=scale),
        out_shape=jax.ShapeDtypeStruct((b, s, heads * MLA_V), F32),
        grid=(b, heads, s // tq),
        in_specs=[pl.BlockSpec((1, tq, MLA_NOPE), lambda bi, h, i: (bi, i, h)),
                  pl.BlockSpec((1, tq, LANES), lambda bi, h, i: (bi, i, h)),
                  pl.BlockSpec((1, s, MLA_NOPE), lambda bi, h, i: (bi, 0, h)),
                  pl.BlockSpec((1, s, LANES), lambda bi, h, i: (bi, 0, 0)),
                  pl.BlockSpec((1, s, MLA_V), lambda bi, h, i: (bi, 0, heads + h))],
        out_specs=pl.BlockSpec((1, tq, MLA_V), lambda bi, h, i: (bi, i, h)),
        scratch_shapes=[pltpu.VMEM((tq, 1), F32), pltpu.VMEM((tq, 1), F32), pltpu.VMEM((tq, MLA_V), F32)],
        compiler_params=_cparams("parallel", "parallel", "parallel"),
        name="mla_attention",
    )(qup, qpe, kvup, kpe, kvup)


def _hyb_in_weight(w_in, heads):
    d = w_in.shape[0]
    gw = heads * GDN_HEAD_DIM
    main = jnp.concatenate([w_in[:, :4 * gw], w_in[:, 4 * gw + 2 * heads:]], axis=1)
    gates = w_in[:, 4 * gw:4 * gw + 2 * heads]
    n = main.shape[1] + LANES
    n_pad = -(-n // 768) * 768
    tail = jnp.zeros((d, n_pad - main.shape[1]), w_in.dtype).at[:, :2 * heads].set(gates)
    return jnp.concatenate([main, tail], axis=1).astype(BF16)


def _mla_in_weight(w_in):
    d = w_in.shape[0]
    pad = jnp.zeros((d, LANES - MLA_ROPE), w_in.dtype)
    return jnp.concatenate([w_in, pad], axis=1).astype(BF16)


def _mla_uq_weight(w_uq, heads):
    r = w_uq.shape[0]
    w = w_uq.reshape(r, heads, MLA_NOPE + MLA_ROPE)
    nope = w[:, :, :MLA_NOPE].reshape(r, heads * MLA_NOPE)
    pe = jnp.concatenate([w[:, :, MLA_NOPE:], jnp.zeros((r, heads, LANES - MLA_ROPE), w.dtype)], axis=2)
    return jnp.concatenate([nope, pe.reshape(r, heads * LANES)], axis=1).astype(BF16)


def _mla_ukv_weight(w_ukv, heads):
    r = w_ukv.shape[0]
    w = w_ukv.reshape(r, heads, MLA_NOPE + MLA_V)
    return jnp.concatenate([w[:, :, :MLA_NOPE].reshape(r, heads * MLA_NOPE),
                            w[:, :, MLA_NOPE:].reshape(r, heads * MLA_V)], axis=1).astype(BF16)


def _router_weight(wr_g, br_g, wr_e, br_e):
    d = wr_g.shape[0]
    n = wr_g.shape[1] + wr_e.shape[1]
    w = jnp.zeros((d, LANES), F32).at[:, :n].set(jnp.concatenate([wr_g, wr_e], axis=1))
    bias = jnp.zeros((1, LANES), F32).at[0, :n].set(jnp.concatenate([br_g, br_e]))
    return w, bias


def kernel(x, c, positions, ada_w, ada_b, ln_g, ln_b, hyb_w_in, gdn_conv_w, gdn_a_log, gdn_dt_bias,
           gdn_norm_w, sc_conv_w, hyb_w_out, mla_w_in, mla_q_norm, mla_kv_norm, mla_w_uq, mla_w_ukv,
           mla_w_out, moe_router_g, moe_bias_g, moe_router_e, moe_bias_e, moe_w_gate, moe_w_up,
           moe_w_down):
    b, s, d = x.shape
    depth = ada_w.shape[0]
    alpha = (2.0 * depth) ** 0.25
    gdn_heads = d // (2 * GDN_HEAD_DIM)
    mla_heads = mla_w_out.shape[1] // MLA_V
    n_groups = moe_router_g.shape[2]
    n_exp = moe_router_e.shape[2]
    per_group = n_exp // n_groups

    mod = _modulation(c, ada_w, ada_b)
    for layer in range(depth):
        sh1, sc1, g1, sh2, sc2, g2 = [mod[layer, :, j * d:(j + 1) * d].reshape(b, 1, d) for j in range(6)]
        i = layer // 2
        wr, br = _router_weight(moe_router_g[layer], moe_bias_g[layer], moe_router_e[layer], moe_bias_e[layer])
        if layer % 2 == 0:
            proj = _mod_matmul(x, sc1, sh1, _hyb_in_weight(hyb_w_in[i], gdn_heads), "hyb_in_proj")
            y = _gdn_shortconv(proj, gdn_conv_w[i], gdn_a_log[i], gdn_dt_bias[i], gdn_norm_w[i],
                               sc_conv_w[i], gdn_heads)
            w_out = hyb_w_out[i]
        else:
            cproj = _mod_matmul(x, sc1, sh1, _mla_in_weight(mla_w_in[i]), "mla_in_proj")
            qup = _rms_matmul(cproj, 0, MLA_Q_RANK, mla_q_norm[i], _mla_uq_weight(mla_w_uq[i], mla_heads),
                              "mla_q_up")
            kvup = _rms_matmul(cproj, 1, MLA_KV_RANK, mla_kv_norm[i], _mla_ukv_weight(mla_w_ukv[i], mla_heads),
                               "mla_kv_up")
            qpe, kpe = _rope(positions, qup, mla_heads, cproj, (MLA_Q_RANK + MLA_KV_RANK) // LANES)
            y = _attention(qup, qpe, kvup, kpe, mla_heads)
            w_out = mla_w_out[i]
        x, hin2, ids, gates = _out_ln_route(y, w_out.astype(BF16), x, g1, ln_g[layer, 0], ln_b[layer, 0],
                                            sc2, sh2, wr, br, alpha, n_groups, per_group,
                                            "mixer_out_ln_route")
        t = b * s
        tables = _dispatch_tables(ids.reshape(t, LANES)[:, :MOE_TOPK], gates.reshape(t, LANES)[:, :MOE_TOPK],
                                  n_exp)
        y2 = _moe_experts(hin2.reshape(t, d), moe_w_gate[layer], moe_w_up[layer], moe_w_down[layer],
                          *tables, "moe_experts")
        x = _combine_ln(y2, x, g2, ln_g[layer, 1], ln_b[layer, 1], alpha, "moe_combine_ln")
    return x
```

```python
import functools

import jax
import jax.numpy as jnp
from jax import lax
from jax.experimental import pallas as pl
from jax.experimental.pallas import tpu as pltpu

F32, BF16, I32 = jnp.float32, jnp.bfloat16, jnp.int32
HIGHEST = lax.Precision.HIGHEST

LANES = 128
SUBLANES = 8
VMEM_LIMIT_BYTES = 56 * 1024 * 1024

GDN_HEAD_DIM = 128
GDN_CONV = 4
GDN_CHUNK = 64
SC_CONV = 3
MLA_NOPE = 128
MLA_ROPE = 64
MLA_V = 128
MLA_Q_RANK = 512
MLA_KV_RANK = 512
ROPE_THETA = 10000.0
MOE_TOPK = 2
MOE_BLOCK = 128
DMA_UNROLL = 8


def _tile(n, pref):
    if n <= pref:
        return n
    for t in range(pref, 0, -LANES):
        if n % t == 0:
            return t
    return n


def _cparams(*sem):
    return pltpu.CompilerParams(dimension_semantics=sem, vmem_limit_bytes=VMEM_LIMIT_BYTES)


def _sigmoid(x):
    return 1.0 / (1.0 + jnp.exp(-x))


def _silu(x):
    return x * _sigmoid(x)


def _softplus(x):
    return jnp.maximum(x, 0.0) + jnp.log(1.0 + jnp.exp(-jnp.abs(x)))


def _dot(a, b):
    return jnp.dot(a.astype(BF16), b.astype(BF16), preferred_element_type=F32)


def _dot_nt(a, b):
    return lax.dot_general(a.astype(BF16), b.astype(BF16), (((1,), (1,)), ((), ())),
                           preferred_element_type=F32)


def _mod_kernel(c_ref, w_ref, b_ref, o_ref):
    c = c_ref[...]
    o_ref[0] = _dot(_silu(c), w_ref[0]) + b_ref[0]


def _modulation(c, ada_w, ada_b):
    depth, d, n = ada_w.shape
    b = c.shape[0]
    rows = -(-b // SUBLANES) * SUBLANES
    cp = jnp.zeros((rows, d), F32).at[:b].set(c)
    tn = _tile(n, 1024)
    out = pl.pallas_call(
        _mod_kernel,
        out_shape=jax.ShapeDtypeStruct((depth, rows, n), F32),
        grid=(depth, n // tn),
        in_specs=[pl.BlockSpec((rows, d), lambda l, j: (0, 0)),
                  pl.BlockSpec((1, d, tn), lambda l, j: (l, 0, j)),
                  pl.BlockSpec((1, 1, tn), lambda l, j: (l, 0, j))],
        out_specs=pl.BlockSpec((1, rows, tn), lambda l, j: (l, 0, j)),
        compiler_params=_cparams("parallel", "parallel"),
        name="adaln_mod",
    )(cp, ada_w, ada_b.reshape(depth, 1, n))
    return out[:, :b]


def _modmm_kernel(x_ref, sc_ref, sh_ref, w_ref, o_ref, xb_ref):
    @pl.when(pl.program_id(2) == 0)
    def _():
        xb_ref[...] = (x_ref[0] * (1.0 + sc_ref[0]) + sh_ref[0]).astype(BF16)

    o_ref[0] = jnp.dot(xb_ref[...], w_ref[...], preferred_element_type=F32)


def _mod_matmul(x, sc, sh, w, name):
    b, s, d = x.shape
    n = w.shape[1]
    tm = _tile(s, 1024)
    tn = _tile(n, 768)
    return pl.pallas_call(
        _modmm_kernel,
        out_shape=jax.ShapeDtypeStruct((b, s, n), F32),
        grid=(b, s // tm, n // tn),
        in_specs=[pl.BlockSpec((1, tm, d), lambda bi, i, j: (bi, i, 0)),
                  pl.BlockSpec((1, 1, d), lambda bi, i, j: (bi, 0, 0)),
                  pl.BlockSpec((1, 1, d), lambda bi, i, j: (bi, 0, 0)),
                  pl.BlockSpec((d, tn), lambda bi, i, j: (0, j))],
        out_specs=pl.BlockSpec((1, tm, tn), lambda bi, i, j: (bi, i, j)),
        scratch_shapes=[pltpu.VMEM((tm, d), BF16)],
        compiler_params=_cparams("parallel", "parallel", "arbitrary"),
        name=name,
    )(x, sc, sh, w)


def _rmsmm_kernel(x_ref, nw_ref, w_ref, o_ref, xb_ref):
    @pl.when(pl.program_id(2) == 0)
    def _():
        x = x_ref[0]
        xb_ref[...] = (x * lax.rsqrt(jnp.mean(x * x, -1, keepdims=True) + 1e-6) * nw_ref[...]).astype(BF16)

    o_ref[0] = jnp.dot(xb_ref[...], w_ref[...], preferred_element_type=F32)


def _rms_matmul(x, col_block, k, norm_w, w, name):
    b, s, _ = x.shape
    n = w.shape[1]
    tm = _tile(s, 1024)
    tn = _tile(n, 1024)
    return pl.pallas_call(
        _rmsmm_kernel,
        out_shape=jax.ShapeDtypeStruct((b, s, n), F32),
        grid=(b, s // tm, n // tn),
        in_specs=[pl.BlockSpec((1, tm, k), lambda bi, i, j: (bi, i, col_block)),
                  pl.BlockSpec((1, k), lambda bi, i, j: (0, 0)),
                  pl.BlockSpec((k, tn), lambda bi, i, j: (0, j))],
        out_specs=pl.BlockSpec((1, tm, tn), lambda bi, i, j: (bi, i, j)),
        scratch_shapes=[pltpu.VMEM((tm, k), BF16)],
        compiler_params=_cparams("parallel", "parallel", "arbitrary"),
        name=name,
    )(x, norm_w.reshape(1, k), w)


def _layer_norm_rows(r, g, b):
    mu = jnp.mean(r, -1, keepdims=True)
    var = jnp.mean(jnp.square(r - mu), -1, keepdims=True)
    return (r - mu) * lax.rsqrt(var + 1e-5) * g + b


def _outln_kernel(y_ref, w_ref, xres_ref, g_ref, lng_ref, lnb_ref, sc_ref, sh_ref, wr_ref, br_ref,
                  xo_ref, hin_ref, ids_ref, gates_ref, *, alpha, n_groups, per_group):
    y = jnp.dot(y_ref[0].astype(BF16), w_ref[...], preferred_element_type=F32)
    r = alpha * xres_ref[0] + (1.0 + g_ref[0]) * y
    xn = _layer_norm_rows(r, lng_ref[...], lnb_ref[...])
    xo_ref[0] = xn
    hin = xn * (1.0 + sc_ref[0]) + sh_ref[0]
    hin_ref[0] = hin
    h_hi = hin.astype(BF16)
    h_lo = (hin - h_hi.astype(F32)).astype(BF16)
    w_hi = wr_ref[0]
    logits = (jnp.dot(h_hi, w_hi, preferred_element_type=F32)
              + jnp.dot(h_hi, wr_ref[1], preferred_element_type=F32)
              + jnp.dot(h_lo, w_hi, preferred_element_type=F32)) + br_ref[...]
    lane = lax.broadcasted_iota(I32, logits.shape, 1)
    big = jnp.int32(4 * LANES)
    neg = jnp.float32(-jnp.inf)
    n_exp = n_groups * per_group
    gmask = lane < n_groups
    lg = jnp.where(gmask, logits, neg)
    mg = jnp.max(lg, -1, keepdims=True)
    grp = jnp.min(jnp.where(gmask & (lg == mg), lane, big), -1, keepdims=True)
    pg_sel = 1.0 / jnp.sum(jnp.where(gmask, jnp.exp(lg - mg), 0.0), -1, keepdims=True)
    lo = n_groups + grp * per_group
    emask = (lane >= lo) & (lane < lo + per_group) & (lane < n_groups + n_exp)
    le = jnp.where(emask, logits, neg)
    me = jnp.max(le, -1, keepdims=True)
    ee = jnp.where(emask, jnp.exp(le - me), 0.0)
    p = ee / jnp.sum(ee, -1, keepdims=True)
    pm = jnp.where(emask, p, -1.0)
    p1 = jnp.max(pm, -1, keepdims=True)
    i1 = jnp.min(jnp.where(emask & (pm == p1), lane, big), -1, keepdims=True)
    pm2 = jnp.where(lane == i1, -1.0, pm)
    p2 = jnp.max(pm2, -1, keepdims=True)
    i2 = jnp.min(jnp.where(emask & (lane != i1) & (pm2 == p2), lane, big), -1, keepdims=True)
    den = p1 + p2
    g1 = pg_sel * p1 / den
    g2 = pg_sel * p2 / den
    ids_ref[0] = jnp.where(lane == 0, i1 - n_groups, jnp.where(lane == 1, i2 - n_groups, 0))
    gates_ref[0] = jnp.where(lane == 0, g1, jnp.where(lane == 1, g2, 0.0))


def _out_ln_route(y, w, xres, gate, ln_g, ln_b, sc, sh, wr, br, alpha, n_groups, per_group, name):
    b, s, k = y.shape
    d = w.shape[1]
    tm = _tile(s, 256)
    kern = functools.partial(_outln_kernel, alpha=alpha, n_groups=n_groups, per_group=per_group)
    row = lambda bi, i: (bi, i, 0)
    per_b = lambda bi, i: (bi, 0, 0)
    const = lambda bi, i: (0, 0)
    return pl.pallas_call(
        kern,
        out_shape=(jax.ShapeDtypeStruct((b, s, d), F32), jax.ShapeDtypeStruct((b, s, d), F32),
                   jax.ShapeDtypeStruct((b, s, LANES), I32), jax.ShapeDtypeStruct((b, s, LANES), F32)),
        grid=(b, s // tm),
        in_specs=[pl.BlockSpec((1, tm, k), row),
                  pl.BlockSpec((k, d), const),
                  pl.BlockSpec((1, tm, d), row),
                  pl.BlockSpec((1, 1, d), per_b),
                  pl.BlockSpec((1, d), const),
                  pl.BlockSpec((1, d), const),
                  pl.BlockSpec((1, 1, d), per_b),
                  pl.BlockSpec((1, 1, d), per_b),
                  pl.BlockSpec((2, d, LANES), lambda bi, i: (0, 0, 0)),
                  pl.BlockSpec((1, LANES), const)],
        out_specs=(pl.BlockSpec((1, tm, d), row), pl.BlockSpec((1, tm, d), row),
                   pl.BlockSpec((1, tm, LANES), row), pl.BlockSpec((1, tm, LANES), row)),
        compiler_params=_cparams("parallel", "parallel"),
        name=name,
    )(y, w, xres, gate, ln_g.reshape(1, d), ln_b.reshape(1, d), sc, sh, wr, br)


def _combine_kernel(ya_ref, yb_ref, rg_ref, x_ref, g_ref, lng_ref, lnb_ref, o_ref, *, alpha):
    rg = rg_ref[0]
    y = ya_ref[0] * rg[:, 0:1] + yb_ref[0] * rg[:, 1:2]
    r = alpha * x_ref[0] + (1.0 + g_ref[0]) * y
    o_ref[0] = _layer_norm_rows(r, lng_ref[...], lnb_ref[...])


def _combine_ln(y2, route_gates, x, gate, ln_g, ln_b, alpha, name):
    b, s, d = x.shape
    tm = _tile(s, 512)
    row = lambda bi, i: (bi, i, 0)
    nt = s // tm
    return pl.pallas_call(
        functools.partial(_combine_kernel, alpha=alpha),
        out_shape=jax.ShapeDtypeStruct((b, s, d), F32),
        grid=(b, nt),
        in_specs=[pl.BlockSpec((1, tm, d), lambda bi, i: (0, bi * nt + i, 0)),
                  pl.BlockSpec((1, tm, d), lambda bi, i: (1, bi * nt + i, 0)),
                  pl.BlockSpec((1, tm, LANES), row),
                  pl.BlockSpec((1, tm, d), row),
                  pl.BlockSpec((1, 1, d), lambda bi, i: (bi, 0, 0)),
                  pl.BlockSpec((1, d), lambda bi, i: (0, 0)),
                  pl.BlockSpec((1, d), lambda bi, i: (0, 0))],
        out_specs=pl.BlockSpec((1, tm, d), row),
        compiler_params=_cparams("parallel", "parallel"),
        name=name,
    )(y2, y2, route_gates, x, gate, ln_g.reshape(1, d), ln_b.reshape(1, d))


def _moe_kernel(bexp_ref, bcnt_ref, rsrc_ref, rdst_ref,
                hin_hbm, wg_ref, wu_ref, wd_ref,
                y2_hbm,
                xbuf, ybuf, wgb, wub, wdb, gsem, ssem, *, blk, nb, tp):
    i = pl.program_id(0)
    slot = i % 2

    def start_gather(block, sl):
        base = block * blk

        def group(gi, c):
            for u in range(DMA_UNROLL):
                r = gi * DMA_UNROLL + u
                pltpu.make_async_copy(hin_hbm.at[pl.ds(rsrc_ref[base + r], 1), :],
                                      xbuf.at[sl, pl.ds(r, 1), :], gsem.at[sl]).start()
            return c

        lax.fori_loop(0, blk // DMA_UNROLL, group, 0)

    def wait_gather(sl):
        pltpu.make_async_copy(hin_hbm.at[pl.ds(0, blk), :], xbuf.at[sl], gsem.at[sl]).wait()

    def start_scatter(block, sl):
        base = block * blk

        def group(gi, c):
            for u in range(DMA_UNROLL):
                r = gi * DMA_UNROLL + u
                pltpu.make_async_copy(ybuf.at[sl, pl.ds(r, 1), :],
                                      y2_hbm.at[pl.ds(rdst_ref[base + r], 1), :], ssem.at[sl]).start()
            return c

        lax.fori_loop(0, blk // DMA_UNROLL, group, 0)

    def wait_scatter(sl):
        pltpu.make_async_copy(ybuf.at[sl], y2_hbm.at[pl.ds(0, blk), :], ssem.at[sl]).wait()

    used = bcnt_ref[i] > 0

    @pl.when(i == 0)
    def _():
        ybuf[...] = jnp.zeros_like(ybuf)
        for k in range(MOE_TOPK):
            for sl in range(2):
                cp = pltpu.make_async_copy(ybuf.at[sl], y2_hbm.at[pl.ds(k * tp + tp - (2 - sl) * blk, blk), :],
                                           ssem.at[sl])
                cp.start()
                cp.wait()

    @pl.when(jnp.logical_and(i == 0, used))
    def _():
        start_gather(0, 0)

    nxt = jnp.minimum(i + 1, nb - 1)

    @pl.when(jnp.logical_and(i + 1 < nb, bcnt_ref[nxt] > 0))
    def _():
        start_gather(nxt, 1 - slot)

    changed = jnp.logical_or(i == 0, bexp_ref[i] != bexp_ref[jnp.maximum(i - 1, 0)])

    @pl.when(jnp.logical_and(changed, used))
    def _():
        wgb[...] = wg_ref[0, 0].astype(BF16)
        wub[...] = wu_ref[0, 0].astype(BF16)
        wdb[...] = wd_ref[0, 0].astype(BF16)

    @pl.when(jnp.logical_and(i >= 2, bcnt_ref[jnp.maximum(i - 2, 0)] > 0))
    def _():
        wait_scatter(slot)

    @pl.when(used)
    def _():
        wait_gather(slot)
        xb = xbuf[slot].astype(BF16)
        hg = jnp.dot(xb, wgb[...], preferred_element_type=F32)
        hu = jnp.dot(xb, wub[...], preferred_element_type=F32)
        hid = (_silu(hg) * hu).astype(BF16)
        ybuf[slot] = jnp.dot(hid, wdb[...], preferred_element_type=F32)
        start_scatter(i, slot)

    @pl.when(i == nb - 1)
    def _():
        if nb >= 2:
            @pl.when(bcnt_ref[jnp.maximum(i - 1, 0)] > 0)
            def _():
                wait_scatter(1 - slot)

        @pl.when(used)
        def _():
            wait_scatter(slot)


def _moe_experts(hin2, w_gate, w_up, w_down, layer, blk_exp, blk_cnt, row_src, row_dst, name):
    t, d = hin2.shape
    ff = w_gate.shape[3]
    nb = blk_exp.shape[0]
    blk = MOE_BLOCK
    tp = t + 2 * blk
    wmap = lambda i, be, bc, rs, rd: (layer, be[i], 0, 0)
    grid_spec = pltpu.PrefetchScalarGridSpec(
        num_scalar_prefetch=4,
        grid=(nb,),
        in_specs=[pl.BlockSpec(memory_space=pl.ANY),
                  pl.BlockSpec((1, 1, d, ff), wmap),
                  pl.BlockSpec((1, 1, d, ff), wmap),
                  pl.BlockSpec((1, 1, ff, d), wmap)],
        out_specs=pl.BlockSpec(memory_space=pl.ANY),
        scratch_shapes=[pltpu.VMEM((2, blk, d), F32),
                        pltpu.VMEM((2, blk, d), F32),
                        pltpu.VMEM((d, ff), BF16),
                        pltpu.VMEM((d, ff), BF16),
                        pltpu.VMEM((ff, d), BF16),
                        pltpu.SemaphoreType.DMA((2,)),
                        pltpu.SemaphoreType.DMA((2,))],
    )
    y2 = pl.pallas_call(
        functools.partial(_moe_kernel, blk=blk, nb=nb, tp=tp),
        out_shape=jax.ShapeDtypeStruct((MOE_TOPK * tp, d), F32),
        grid_spec=grid_spec,
        compiler_params=_cparams("arbitrary"),
        name=name,
    )(blk_exp, blk_cnt, row_src, row_dst, hin2, w_gate, w_up, w_down)
    return y2.reshape(MOE_TOPK, tp, d)


def _dispatch_tables(ids, n_exp):
    t = ids.shape[0]
    n_assign = t * MOE_TOPK
    blk = MOE_BLOCK
    tp = t + 2 * blk
    e_flat = ids.reshape(-1)
    order = jnp.argsort(e_flat).astype(I32)
    e_sorted = e_flat[order]
    counts = jnp.bincount(e_flat, length=n_exp).astype(I32)
    padded = (counts + blk - 1) // blk * blk
    pad_end = jnp.cumsum(padded)
    pad_start = pad_end - padded
    start = jnp.cumsum(counts) - counts
    dest = pad_start[e_sorted] + jnp.arange(n_assign, dtype=I32) - start[e_sorted]
    n_rows = n_assign + n_exp * blk
    nb = n_rows // blk
    rows = jnp.arange(n_rows, dtype=I32)
    spare = t + ((rows // blk) % 2) * blk + rows % blk
    row_src = jnp.zeros((n_rows,), I32).at[dest].set(order // MOE_TOPK)
    row_dst = spare.at[dest].set((order % MOE_TOPK) * tp + order // MOE_TOPK)
    blk_first = jnp.arange(nb, dtype=I32) * blk
    blk_exp = jnp.minimum(jnp.searchsorted(pad_end, blk_first, side='right'), n_exp - 1).astype(I32)
    in_use = blk_first < pad_end[-1]
    blk_cnt = jnp.where(in_use, jnp.clip(counts[blk_exp] - (blk_first - pad_start[blk_exp]), 0, blk), 0)
    return blk_exp, blk_cnt.astype(I32), row_src, row_dst


def _gdn_kernel(q_ref, k_ref, v_ref, z_ref, bg_ref, cg_ref, hh_ref, ba_ref,
                cw_ref, alog_ref, dtb_ref, nw_ref, scw_ref,
                o_ref,
                ext_ref, qa_ref, ka_ref, va_ref, hist_ref, hist2_ref, gate_ref, state_ref,
                *, heads, ts, gw):
    L = GDN_CHUNK
    dk = GDN_HEAD_DIM

    @pl.when(pl.program_id(1) == 0)
    def _():
        hist_ref[...] = jnp.zeros_like(hist_ref)
        hist2_ref[...] = jnp.zeros_like(hist2_ref)
        state_ref[...] = jnp.zeros_like(state_ref)

    def causal_conv(x, hist, w, taps):
        ext_ref[0:SUBLANES, :] = hist
        ext_ref[SUBLANES:SUBLANES + ts, :] = x
        y = w[0:1] * ext_ref[SUBLANES - taps + 1:SUBLANES - taps + 1 + ts, :]
        for j in range(1, taps - 1):
            y = y + w[j:j + 1] * ext_ref[SUBLANES - taps + 1 + j:SUBLANES - taps + 1 + j + ts, :]
        return y + w[taps - 1:taps] * x

    for idx, (src, dst) in enumerate(((q_ref, qa_ref), (k_ref, ka_ref), (v_ref, va_ref))):
        x = src[0]
        y = causal_conv(x, hist_ref[idx], cw_ref[:, idx * gw:(idx + 1) * gw], GDN_CONV)
        hist_ref[idx] = x[ts - SUBLANES:ts, :]
        dst[...] = _silu(y)

    c = cg_ref[0] * hh_ref[0]
    yb = causal_conv(c, hist2_ref[...], scw_ref[...], SC_CONV)
    hist2_ref[...] = c[ts - SUBLANES:ts, :]
    o_ref[0, :, gw:2 * gw] = bg_ref[0] * yb

    ba = ba_ref[0]
    gate_ref[0] = _sigmoid(ba)
    gate_ref[1] = -jnp.exp(alog_ref[...]) * _softplus(ba + dtb_ref[...])

    ii = lax.broadcasted_iota(I32, (L, L), 0)
    jj = lax.broadcasted_iota(I32, (L, L), 1)
    causal = ii >= jj
    strict = ii > jj
    tri_incl = causal.astype(F32)
    tri_strict = strict.astype(F32)
    eye = (ii == jj).astype(F32)
    nw = nw_ref[...]

    def chunk(ci, carry):
        r0 = pl.multiple_of(ci * L, L)
        rows = pl.ds(r0, L)
        for h in range(heads):
            cols = slice(h * dk, (h + 1) * dk)
            qh = qa_ref[rows, cols]
            kh = ka_ref[rows, cols]
            vh = va_ref[rows, cols]
            beta = gate_ref[0, rows, h:h + 1]
            g = gate_ref[1, rows, heads + h:heads + h + 1]
            rhs_g = jnp.concatenate([g * tri_strict, jnp.broadcast_to(g, (L, L))], axis=1)
            cum = jnp.dot(tri_incl, rhs_g, precision=HIGHEST, preferred_element_type=F32)
            diff = cum[:, 0:L]
            gc = cum[:, L:L + 1]
            gl = gc[L - 1:L, :]
            decay = jnp.where(causal, jnp.exp(jnp.where(causal, diff, 0.0)), 0.0)
            egc = jnp.exp(gc)
            qn = qh * lax.rsqrt(jnp.sum(qh * qh, -1, keepdims=True) + 1e-6)
            kn = kh * lax.rsqrt(jnp.sum(kh * kh, -1, keepdims=True) + 1e-6)
            qc = qn * (dk ** -0.5)
            kb = kn * beta
            m = jnp.where(strict, _dot_nt(kb, kn) * decay, 0.0)
            t_inv = eye - m
            mp = m
            for _ in range(5):
                mp = _dot(mp, mp)
                t_inv = t_inv + _dot(t_inv, mp)
            sol = _dot(t_inv, jnp.concatenate([vh * beta, kb * egc], axis=1))
            u = sol[:, 0:dk]
            w = sol[:, dk:2 * dk]
            qk = jnp.where(causal, _dot_nt(qc, kn) * decay, 0.0)
            st = state_ref[h]
            v_new = u - _dot(w, st)
            o = _dot(qc * egc, st) + _dot(qk, v_new)
            k_dec = kn * jnp.exp(gl - gc)
            state_ref[h] = st * jnp.exp(gl) + _dot(k_dec.T, v_new)
            on = o * lax.rsqrt(jnp.mean(o * o, -1, keepdims=True) + 1e-6) * nw
            zh = z_ref[0, rows, cols]
            o_ref[0, rows, cols] = on * _silu(zh)
        return carry

    lax.fori_loop(0, ts // L, chunk, 0)


def _gdn_shortconv(proj, conv_w, a_log, dt_bias, norm_w, sc_w, heads):
    b, s, _ = proj.shape
    gw = heads * GDN_HEAD_DIM
    ts = _tile(s, 256)
    alog_p = jnp.zeros((1, LANES), F32).at[0, heads:2 * heads].set(a_log)
    dtb_p = jnp.zeros((1, LANES), F32).at[0, heads:2 * heads].set(dt_bias)
    col = lambda cb: (lambda bi, i: (bi, i, cb))
    const = lambda bi, i: (0, 0)
    kern = functools.partial(_gdn_kernel, heads=heads, ts=ts, gw=gw)
    return pl.pallas_call(
        kern,
        out_shape=jax.ShapeDtypeStruct((b, s, 2 * gw), F32),
        grid=(b, s // ts),
        in_specs=[pl.BlockSpec((1, ts, gw), col(cb)) for cb in range(7)]
        + [pl.BlockSpec((1, ts, LANES), col(7 * gw // LANES)),
           pl.BlockSpec((GDN_CONV, 3 * gw), const),
           pl.BlockSpec((1, LANES), const),
           pl.BlockSpec((1, LANES), const),
           pl.BlockSpec((1, GDN_HEAD_DIM), const),
           pl.BlockSpec((SC_CONV, gw), const)],
        out_specs=pl.BlockSpec((1, ts, 2 * gw), lambda bi, i: (bi, i, 0)),
        scratch_shapes=[pltpu.VMEM((ts + SUBLANES, gw), F32),
                        pltpu.VMEM((ts, gw), F32),
                        pltpu.VMEM((ts, gw), F32),
                        pltpu.VMEM((ts, gw), F32),
                        pltpu.VMEM((3, SUBLANES, gw), F32),
                        pltpu.VMEM((SUBLANES, gw), F32),
                        pltpu.VMEM((2, ts, LANES), F32),
                        pltpu.VMEM((heads, GDN_HEAD_DIM, GDN_HEAD_DIM), F32)],
        compiler_params=_cparams("parallel", "arbitrary"),
        name="gdn_shortconv",
    )(proj, proj, proj, proj, proj, proj, proj, proj,
      conv_w, alog_p, dtb_p, norm_w.reshape(1, GDN_HEAD_DIM), sc_w)


def _rope_kernel(pos_ref, inv_ref, qpe_ref, kpe_ref, qo_ref, ko_ref, *, heads):
    ang = pos_ref[0].astype(F32) * inv_ref[...]
    lane = lax.broadcasted_iota(I32, ang.shape, 1)
    half = MLA_ROPE // 2
    cos = jnp.where(lane < MLA_ROPE, jnp.cos(ang), 0.0)
    sin = jnp.sin(ang)
    sgn = jnp.where(lane < half, -sin, jnp.where(lane < MLA_ROPE, sin, 0.0))

    def rot(x):
        swapped = jnp.where(lane < half, pltpu.roll(x, LANES - half, 1), pltpu.roll(x, half, 1))
        return x * cos + swapped * sgn

    for h in range(heads):
        cols = slice(h * LANES, (h + 1) * LANES)
        qo_ref[0, :, cols] = rot(qpe_ref[0, :, cols])
    ko_ref[0] = rot(kpe_ref[0])


def _rope(positions, qup, heads, cproj, kpe_block):
    b, s = positions.shape
    ts = _tile(s, 512)
    half = MLA_ROPE // 2
    inv = ROPE_THETA ** (-jnp.arange(half, dtype=F32) * (2.0 / MLA_ROPE))
    inv_tab = jnp.zeros((1, LANES), F32).at[0, 0:half].set(inv).at[0, half:MLA_ROPE].set(inv)
    w = heads * LANES
    return pl.pallas_call(
        functools.partial(_rope_kernel, heads=heads),
        out_shape=(jax.ShapeDtypeStruct((b, s, w), F32), jax.ShapeDtypeStruct((b, s, LANES), F32)),
        grid=(b, s // ts),
        in_specs=[pl.BlockSpec((1, ts, 1), lambda bi, i: (bi, i, 0)),
                  pl.BlockSpec((1, LANES), lambda bi, i: (0, 0)),
                  pl.BlockSpec((1, ts, w), lambda bi, i: (bi, i, 1)),
                  pl.BlockSpec((1, ts, LANES), lambda bi, i: (bi, i, kpe_block))],
        out_specs=(pl.BlockSpec((1, ts, w), lambda bi, i: (bi, i, 0)),
                   pl.BlockSpec((1, ts, LANES), lambda bi, i: (bi, i, 0))),
        compiler_params=_cparams("parallel", "parallel"),
        name="mla_rope",
    )(positions.reshape(b, s, 1), inv_tab, qup, cproj)


def _attn_kernel(qn_ref, qp_ref, kn_ref, kp_ref, v_ref, o_ref, m_ref, l_ref, acc_ref, *, tq, scale):
    qi = pl.program_id(2)
    qn = qn_ref[0].astype(BF16)
    qp = qp_ref[0].astype(BF16)
    m_ref[...] = jnp.full_like(m_ref, -jnp.inf)
    l_ref[...] = jnp.zeros_like(l_ref)
    acc_ref[...] = jnp.zeros_like(acc_ref)
    qpos = qi * tq + lax.broadcasted_iota(I32, (tq, tq), 0)
    koff = lax.broadcasted_iota(I32, (tq, tq), 1)

    def body(kj, carry):
        ks = pl.multiple_of(kj * tq, tq)
        kn = kn_ref[0, pl.ds(ks, tq), :]
        kp = kp_ref[0, pl.ds(ks, tq), :]
        v = v_ref[0, pl.ds(ks, tq), :]
        s = (_dot_nt(qn, kn) + _dot_nt(qp, kp)) * scale
        s = jnp.where(qpos >= ks + koff, s, -jnp.inf)
        m_old = m_ref[...]
        m_new = jnp.maximum(m_old, jnp.max(s, -1, keepdims=True))
        a = jnp.exp(m_old - m_new)
        p = jnp.exp(s - m_new)
        l_ref[...] = a * l_ref[...] + jnp.sum(p, -1, keepdims=True)
        acc_ref[...] = a * acc_ref[...] + _dot(p, v)
        m_ref[...] = m_new
        return carry

    lax.fori_loop(0, qi + 1, body, 0)
    o_ref[0] = acc_ref[...] / l_ref[...]


def _attention(qup, qpe, kvup, kpe, heads):
    b, s, _ = qup.shape
    tq = _tile(s, 512)
    scale = (MLA_NOPE + MLA_ROPE) ** -0.5
    return pl.pallas_call(
        functools.partial(_attn_kernel, tq=tq, scale=scale),
        out_shape=jax.ShapeDtypeStruct((b, s, heads * MLA_V), F32),
        grid=(b, heads, s // tq),
        in_specs=[pl.BlockSpec((1, tq, MLA_NOPE), lambda bi, h, i: (bi, i, h)),
                  pl.BlockSpec((1, tq, LANES), lambda bi, h, i: (bi, i, h)),
                  pl.BlockSpec((1, s, MLA_NOPE), lambda bi, h, i: (bi, 0, h)),
                  pl.BlockSpec((1, s, LANES), lambda bi, h, i: (bi, 0, 0)),
                  pl.BlockSpec((1, s, MLA_V), lambda bi, h, i: (bi, 0, heads + h))],
        out_specs=pl.BlockSpec((1, tq, MLA_V), lambda bi, h, i: (bi, i, h)),
        scratch_shapes=[pltpu.VMEM((tq, 1), F32), pltpu.VMEM((tq, 1), F32), pltpu.VMEM((tq, MLA_V), F32)],
        compiler_params=_cparams("parallel", "parallel", "parallel"),
        name="mla_attention",
    )(qup, qpe, kvup, kpe, kvup)


def _hyb_in_weight(w_in, heads):
    d = w_in.shape[0]
    gw = heads * GDN_HEAD_DIM
    main = jnp.concatenate([w_in[:, :4 * gw], w_in[:, 4 * gw + 2 * heads:]], axis=1)
    gates = w_in[:, 4 * gw:4 * gw + 2 * heads]
    n = main.shape[1] + LANES
    n_pad = -(-n // 768) * 768
    tail = jnp.zeros((d, n_pad - main.shape[1]), w_in.dtype).at[:, :2 * heads].set(gates)
    return jnp.concatenate([main, tail], axis=1).astype(BF16)


def _mla_in_weight(w_in):
    d = w_in.shape[0]
    pad = jnp.zeros((d, LANES - MLA_ROPE), w_in.dtype)
    return jnp.concatenate([w_in, pad], axis=1).astype(BF16)


def _mla_uq_weight(w_uq, heads):
    r = w_uq.shape[0]
    w = w_uq.reshape(r, heads, MLA_NOPE + MLA_ROPE)
    nope = w[:, :, :MLA_NOPE].reshape(r, heads * MLA_NOPE)
    pe = jnp.concatenate([w[:, :, MLA_NOPE:], jnp.zeros((r, heads, LANES - MLA_ROPE), w.dtype)], axis=2)
    return jnp.concatenate([nope, pe.reshape(r, heads * LANES)], axis=1).astype(BF16)


def _mla_ukv_weight(w_ukv, heads):
    r = w_ukv.shape[0]
    w = w_ukv.reshape(r, heads, MLA_NOPE + MLA_V)
    return jnp.concatenate([w[:, :, :MLA_NOPE].reshape(r, heads * MLA_NOPE),
                            w[:, :, MLA_NOPE:].reshape(r, heads * MLA_V)], axis=1).astype(BF16)


def _router_weight(wr_g, br_g, wr_e, br_e):
    d = wr_g.shape[0]
    n = wr_g.shape[1] + wr_e.shape[1]
    w = jnp.zeros((d, LANES), F32).at[:, :n].set(jnp.concatenate([wr_g, wr_e], axis=1))
    bias = jnp.zeros((1, LANES), F32).at[0, :n].set(jnp.concatenate([br_g, br_e]))
    w_hi = w.astype(BF16)
    w_lo = (w - w_hi.astype(F32)).astype(BF16)
    return jnp.stack([w_hi, w_lo]), bias


def kernel(x, c, positions, ada_w, ada_b, ln_g, ln_b, hyb_w_in, gdn_conv_w, gdn_a_log, gdn_dt_bias,
           gdn_norm_w, sc_conv_w, hyb_w_out, mla_w_in, mla_q_norm, mla_kv_norm, mla_w_uq, mla_w_ukv,
           mla_w_out, moe_router_g, moe_bias_g, moe_router_e, moe_bias_e, moe_w_gate, moe_w_up,
           moe_w_down):
    b, s, d = x.shape
    depth = ada_w.shape[0]
    alpha = (2.0 * depth) ** 0.25
    gdn_heads = d // (2 * GDN_HEAD_DIM)
    mla_heads = mla_w_out.shape[1] // MLA_V
    n_groups = moe_router_g.shape[2]
    n_exp = moe_router_e.shape[2]
    per_group = n_exp // n_groups

    mod = _modulation(c, ada_w, ada_b)
    for layer in range(depth):
        sh1, sc1, g1, sh2, sc2, g2 = [mod[layer, :, j * d:(j + 1) * d].reshape(b, 1, d) for j in range(6)]
        i = layer // 2
        wr, br = _router_weight(moe_router_g[layer], moe_bias_g[layer], moe_router_e[layer], moe_bias_e[layer])
        if layer % 2 == 0:
            proj = _mod_matmul(x, sc1, sh1, _hyb_in_weight(hyb_w_in[i], gdn_heads), "hyb_in_proj")
            y = _gdn_shortconv(proj, gdn_conv_w[i], gdn_a_log[i], gdn_dt_bias[i], gdn_norm_w[i],
                               sc_conv_w[i], gdn_heads)
            w_out = hyb_w_out[i]
        else:
            cproj = _mod_matmul(x, sc1, sh1, _mla_in_weight(mla_w_in[i]), "mla_in_proj")
            qup = _rms_matmul(cproj, 0, MLA_Q_RANK, mla_q_norm[i], _mla_uq_weight(mla_w_uq[i], mla_heads),
                              "mla_q_up")
            kvup = _rms_matmul(cproj, 1, MLA_KV_RANK, mla_kv_norm[i], _mla_ukv_weight(mla_w_ukv[i], mla_heads),
                               "mla_kv_up")
            qpe, kpe = _rope(positions, qup, mla_heads, cproj, (MLA_Q_RANK + MLA_KV_RANK) // LANES)
            y = _attention(qup, qpe, kvup, kpe, mla_heads)
            w_out = mla_w_out[i]
        x, hin2, ids, gates = _out_ln_route(y, w_out.astype(BF16), x, g1, ln_g[layer, 0], ln_b[layer, 0],
                                            sc2, sh2, wr, br, alpha, n_groups, per_group,
                                            "mixer_out_ln_route")
        t = b * s
        tables = _dispatch_tables(ids.reshape(t, LANES)[:, :MOE_TOPK], n_exp)
        y2 = _moe_experts(hin2.reshape(t, d), moe_w_gate, moe_w_up, moe_w_down, layer, *tables, "moe_experts")
        x = _combine_ln(y2, gates, x, g2, ln_g[layer, 1], ln_b[layer, 1], alpha, "moe_combine_ln")
    return x
```

```python
import functools

import jax
import jax.numpy as jnp
from jax import lax
from jax.experimental import pallas as pl
from jax.experimental.pallas import tpu as pltpu

F32, BF16, I32 = jnp.float32, jnp.bfloat16, jnp.int32
HIGHEST = lax.Precision.HIGHEST

LANES = 128
SUBLANES = 8
VMEM_LIMIT_BYTES = 56 * 1024 * 1024

GDN_HEAD_DIM = 128
GDN_CONV = 4
GDN_CHUNK = 64
SC_CONV = 3
MLA_NOPE = 128
MLA_ROPE = 64
MLA_V = 128
MLA_Q_RANK = 512
MLA_KV_RANK = 512
ROPE_THETA = 10000.0
MOE_TOPK = 2
MOE_BLOCK = 128
DMA_UNROLL = 8


def _tile(n, pref):
    if n <= pref:
        return n
    for t in range(pref, 0, -LANES):
        if n % t == 0:
            return t
    return n


def _cparams(*sem):
    return pltpu.CompilerParams(dimension_semantics=sem, vmem_limit_bytes=VMEM_LIMIT_BYTES)


def _sigmoid(x):
    return 1.0 / (1.0 + jnp.exp(-x))


def _silu(x):
    return x * _sigmoid(x)


def _softplus(x):
    return jnp.maximum(x, 0.0) + jnp.log(1.0 + jnp.exp(-jnp.abs(x)))


def _dot(a, b):
    return jnp.dot(a.astype(BF16), b.astype(BF16), preferred_element_type=F32)


def _dot_nt(a, b):
    return lax.dot_general(a.astype(BF16), b.astype(BF16), (((1,), (1,)), ((), ())),
                           preferred_element_type=F32)


def _mod_kernel(c_ref, w_ref, b_ref, o_ref):
    c = c_ref[...]
    o_ref[0] = _dot(_silu(c), w_ref[0]) + b_ref[0]


def _modulation(c, ada_w, ada_b):
    depth, d, n = ada_w.shape
    b = c.shape[0]
    rows = -(-b // SUBLANES) * SUBLANES
    cp = jnp.zeros((rows, d), F32).at[:b].set(c)
    tn = _tile(n, 1024)
    out = pl.pallas_call(
        _mod_kernel,
        out_shape=jax.ShapeDtypeStruct((depth, rows, n), F32),
        grid=(depth, n // tn),
        in_specs=[pl.BlockSpec((rows, d), lambda l, j: (0, 0)),
                  pl.BlockSpec((1, d, tn), lambda l, j: (l, 0, j)),
                  pl.BlockSpec((1, 1, tn), lambda l, j: (l, 0, j))],
        out_specs=pl.BlockSpec((1, rows, tn), lambda l, j: (l, 0, j)),
        compiler_params=_cparams("parallel", "parallel"),
        name="adaln_mod",
    )(cp, ada_w, ada_b.reshape(depth, 1, n))
    return out[:, :b]


def _modmm_kernel(x_ref, sc_ref, sh_ref, w_ref, o_ref, xb_ref):
    @pl.when(pl.program_id(2) == 0)
    def _():
        xb_ref[...] = (x_ref[0] * (1.0 + sc_ref[0]) + sh_ref[0]).astype(BF16)

    o_ref[0] = jnp.dot(xb_ref[...], w_ref[...], preferred_element_type=F32)


def _mod_matmul(x, sc, sh, w, name):
    b, s, d = x.shape
    n = w.shape[1]
    tm = _tile(s, 1024)
    tn = _tile(n, 768)
    return pl.pallas_call(
        _modmm_kernel,
        out_shape=jax.ShapeDtypeStruct((b, s, n), F32),
        grid=(b, s // tm, n // tn),
        in_specs=[pl.BlockSpec((1, tm, d), lambda bi, i, j: (bi, i, 0)),
                  pl.BlockSpec((1, 1, d), lambda bi, i, j: (bi, 0, 0)),
                  pl.BlockSpec((1, 1, d), lambda bi, i, j: (bi, 0, 0)),
                  pl.BlockSpec((d, tn), lambda bi, i, j: (0, j))],
        out_specs=pl.BlockSpec((1, tm, tn), lambda bi, i, j: (bi, i, j)),
        scratch_shapes=[pltpu.VMEM((tm, d), BF16)],
        compiler_params=_cparams("parallel", "parallel", "arbitrary"),
        name=name,
    )(x, sc, sh, w)


def _rmsmm_kernel(x_ref, nw_ref, w_ref, o_ref, xb_ref):
    @pl.when(pl.program_id(2) == 0)
    def _():
        x = x_ref[0]
        xb_ref[...] = (x * lax.rsqrt(jnp.mean(x * x, -1, keepdims=True) + 1e-6) * nw_ref[...]).astype(BF16)

    o_ref[0] = jnp.dot(xb_ref[...], w_ref[...], preferred_element_type=F32)


def _rms_matmul(x, col_block, k, norm_w, w, name):
    b, s, _ = x.shape
    n = w.shape[1]
    tm = _tile(s, 1024)
    tn = _tile(n, 1024)
    return pl.pallas_call(
        _rmsmm_kernel,
        out_shape=jax.ShapeDtypeStruct((b, s, n), F32),
        grid=(b, s // tm, n // tn),
        in_specs=[pl.BlockSpec((1, tm, k), lambda bi, i, j: (bi, i, col_block)),
                  pl.BlockSpec((1, k), lambda bi, i, j: (0, 0)),
                  pl.BlockSpec((k, tn), lambda bi, i, j: (0, j))],
        out_specs=pl.BlockSpec((1, tm, tn), lambda bi, i, j: (bi, i, j)),
        scratch_shapes=[pltpu.VMEM((tm, k), BF16)],
        compiler_params=_cparams("parallel", "parallel", "arbitrary"),
        name=name,
    )(x, norm_w.reshape(1, k), w)


def _layer_norm_rows(r, g, b):
    mu = jnp.mean(r, -1, keepdims=True)
    var = jnp.mean(jnp.square(r - mu), -1, keepdims=True)
    return (r - mu) * lax.rsqrt(var + 1e-5) * g + b


def _outln_kernel(y_ref, w_ref, xres_ref, g_ref, lng_ref, lnb_ref, sc_ref, sh_ref, wr_ref, br_ref,
                  xo_ref, hin_ref, ids_ref, gates_ref, *, alpha, n_groups, per_group):
    y = jnp.dot(y_ref[0].astype(BF16), w_ref[...], preferred_element_type=F32)
    r = alpha * xres_ref[0] + (1.0 + g_ref[0]) * y
    xn = _layer_norm_rows(r, lng_ref[...], lnb_ref[...])
    xo_ref[0] = xn
    hin = xn * (1.0 + sc_ref[0]) + sh_ref[0]
    hin_ref[0] = hin
    h_hi = hin.astype(BF16)
    h_lo = (hin - h_hi.astype(F32)).astype(BF16)
    w_hi = wr_ref[0]
    logits = (jnp.dot(h_hi, w_hi, preferred_element_type=F32)
              + jnp.dot(h_hi, wr_ref[1], preferred_element_type=F32)
              + jnp.dot(h_lo, w_hi, preferred_element_type=F32)) + br_ref[...]
    lane = lax.broadcasted_iota(I32, logits.shape, 1)
    big = jnp.int32(4 * LANES)
    neg = jnp.float32(-jnp.inf)
    n_exp = n_groups * per_group
    gmask = lane < n_groups
    lg = jnp.where(gmask, logits, neg)
    mg = jnp.max(lg, -1, keepdims=True)
    grp = jnp.min(jnp.where(gmask & (lg == mg), lane, big), -1, keepdims=True)
    pg_sel = 1.0 / jnp.sum(jnp.where(gmask, jnp.exp(lg - mg), 0.0), -1, keepdims=True)
    lo = n_groups + grp * per_group
    emask = (lane >= lo) & (lane < lo + per_group) & (lane < n_groups + n_exp)
    le = jnp.where(emask, logits, neg)
    me = jnp.max(le, -1, keepdims=True)
    ee = jnp.where(emask, jnp.exp(le - me), 0.0)
    p = ee / jnp.sum(ee, -1, keepdims=True)
    pm = jnp.where(emask, p, -1.0)
    p1 = jnp.max(pm, -1, keepdims=True)
    i1 = jnp.min(jnp.where(emask & (pm == p1), lane, big), -1, keepdims=True)
    pm2 = jnp.where(lane == i1, -1.0, pm)
    p2 = jnp.max(pm2, -1, keepdims=True)
    i2 = jnp.min(jnp.where(emask & (lane != i1) & (pm2 == p2), lane, big), -1, keepdims=True)
    den = p1 + p2
    g1 = pg_sel * p1 / den
    g2 = pg_sel * p2 / den
    ids_ref[0] = jnp.where(lane == 0, i1 - n_groups, jnp.where(lane == 1, i2 - n_groups, 0))
    gates_ref[0] = jnp.where(lane == 0, g1, jnp.where(lane == 1, g2, 0.0))


def _out_ln_route(y, w, xres, gate, ln_g, ln_b, sc, sh, wr, br, alpha, n_groups, per_group, name):
    b, s, k = y.shape
    d = w.shape[1]
    tm = _tile(s, 256)
    kern = functools.partial(_outln_kernel, alpha=alpha, n_groups=n_groups, per_group=per_group)
    row = lambda bi, i: (bi, i, 0)
    per_b = lambda bi, i: (bi, 0, 0)
    const = lambda bi, i: (0, 0)
    return pl.pallas_call(
        kern,
        out_shape=(jax.ShapeDtypeStruct((b, s, d), F32), jax.ShapeDtypeStruct((b, s, d), F32),
                   jax.ShapeDtypeStruct((b, s, LANES), I32), jax.ShapeDtypeStruct((b, s, LANES), F32)),
        grid=(b, s // tm),
        in_specs=[pl.BlockSpec((1, tm, k), row),
                  pl.BlockSpec((k, d), const),
                  pl.BlockSpec((1, tm, d), row),
                  pl.BlockSpec((1, 1, d), per_b),
                  pl.BlockSpec((1, d), const),
                  pl.BlockSpec((1, d), const),
                  pl.BlockSpec((1, 1, d), per_b),
                  pl.BlockSpec((1, 1, d), per_b),
                  pl.BlockSpec((2, d, LANES), lambda bi, i: (0, 0, 0)),
                  pl.BlockSpec((1, LANES), const)],
        out_specs=(pl.BlockSpec((1, tm, d), row), pl.BlockSpec((1, tm, d), row),
                   pl.BlockSpec((1, tm, LANES), row), pl.BlockSpec((1, tm, LANES), row)),
        compiler_params=_cparams("parallel", "parallel"),
        name=name,
    )(y, w, xres, gate, ln_g.reshape(1, d), ln_b.reshape(1, d), sc, sh, wr, br)


def _combine_kernel(ya_ref, yb_ref, rg_ref, x_ref, g_ref, lng_ref, lnb_ref, o_ref, *, alpha):
    rg = rg_ref[0]
    y = ya_ref[0] * rg[:, 0:1] + yb_ref[0] * rg[:, 1:2]
    r = alpha * x_ref[0] + (1.0 + g_ref[0]) * y
    o_ref[0] = _layer_norm_rows(r, lng_ref[...], lnb_ref[...])


def _combine_ln(y2, route_gates, x, gate, ln_g, ln_b, alpha, name):
    b, s, d = x.shape
    tm = _tile(s, 512)
    row = lambda bi, i: (bi, i, 0)
    nt = s // tm
    return pl.pallas_call(
        functools.partial(_combine_kernel, alpha=alpha),
        out_shape=jax.ShapeDtypeStruct((b, s, d), F32),
        grid=(b, nt),
        in_specs=[pl.BlockSpec((1, tm, d), lambda bi, i: (0, bi * nt + i, 0)),
                  pl.BlockSpec((1, tm, d), lambda bi, i: (1, bi * nt + i, 0)),
                  pl.BlockSpec((1, tm, LANES), row),
                  pl.BlockSpec((1, tm, d), row),
                  pl.BlockSpec((1, 1, d), lambda bi, i: (bi, 0, 0)),
                  pl.BlockSpec((1, d), lambda bi, i: (0, 0)),
                  pl.BlockSpec((1, d), lambda bi, i: (0, 0))],
        out_specs=pl.BlockSpec((1, tm, d), row),
        compiler_params=_cparams("parallel", "parallel"),
        name=name,
    )(y2, y2, route_gates, x, gate, ln_g.reshape(1, d), ln_b.reshape(1, d))


def _moe_kernel(bexp_ref, bcnt_ref, rsrc_ref, rdst_ref,
                hin_hbm, wg_ref, wu_ref, wd_ref,
                y2_hbm,
                xbuf, ybuf, wgb, wub, wdb, gsem, ssem, *, blk, nb, tp):
    i = pl.program_id(0)
    slot = i % 2

    def start_gather(block, sl):
        base = block * blk

        def group(gi, c):
            for u in range(DMA_UNROLL):
                r = gi * DMA_UNROLL + u
                pltpu.make_async_copy(hin_hbm.at[pl.ds(rsrc_ref[base + r], 1), :],
                                      xbuf.at[sl, pl.ds(r, 1), :], gsem.at[sl]).start()
            return c

        lax.fori_loop(0, blk // DMA_UNROLL, group, 0)

    def wait_gather(sl):
        pltpu.make_async_copy(hin_hbm.at[pl.ds(0, blk), :], xbuf.at[sl], gsem.at[sl]).wait()

    def start_scatter(block, sl):
        base = block * blk

        def group(gi, c):
            for u in range(DMA_UNROLL):
                r = gi * DMA_UNROLL + u
                pltpu.make_async_copy(ybuf.at[sl, pl.ds(r, 1), :],
                                      y2_hbm.at[pl.ds(rdst_ref[base + r], 1), :], ssem.at[sl]).start()
            return c

        lax.fori_loop(0, blk // DMA_UNROLL, group, 0)

    def wait_scatter(sl):
        pltpu.make_async_copy(ybuf.at[sl], y2_hbm.at[pl.ds(0, blk), :], ssem.at[sl]).wait()

    used = bcnt_ref[i] > 0

    @pl.when(i == 0)
    def _():
        ybuf[...] = jnp.zeros_like(ybuf)
        for k in range(MOE_TOPK):
            for sl in range(2):
                cp = pltpu.make_async_copy(ybuf.at[sl], y2_hbm.at[pl.ds(k * tp + tp - (2 - sl) * blk, blk), :],
                                           ssem.at[sl])
                cp.start()
                cp.wait()

    @pl.when(jnp.logical_and(i == 0, used))
    def _():
        start_gather(0, 0)

    nxt = jnp.minimum(i + 1, nb - 1)

    @pl.when(jnp.logical_and(i + 1 < nb, bcnt_ref[nxt] > 0))
    def _():
        start_gather(nxt, 1 - slot)

    changed = jnp.logical_or(i == 0, bexp_ref[i] != bexp_ref[jnp.maximum(i - 1, 0)])

    @pl.when(jnp.logical_and(changed, used))
    def _():
        wgb[...] = wg_ref[0, 0].astype(BF16)
        wub[...] = wu_ref[0, 0].astype(BF16)
        wdb[...] = wd_ref[0, 0].astype(BF16)

    @pl.when(jnp.logical_and(i >= 2, bcnt_ref[jnp.maximum(i - 2, 0)] > 0))
    def _():
        wait_scatter(slot)

    @pl.when(used)
    def _():
        wait_gather(slot)
        xb = xbuf[slot].astype(BF16)
        hg = jnp.dot(xb, wgb[...], preferred_element_type=F32)
        hu = jnp.dot(xb, wub[...], preferred_element_type=F32)
        hid = (_silu(hg) * hu).astype(BF16)
        ybuf[slot] = jnp.dot(hid, wdb[...], preferred_element_type=F32)
        start_scatter(i, slot)

    @pl.when(i == nb - 1)
    def _():
        if nb >= 2:
            @pl.when(bcnt_ref[jnp.maximum(i - 1, 0)] > 0)
            def _():
                wait_scatter(1 - slot)

        @pl.when(used)
        def _():
            wait_scatter(slot)


def _moe_experts(hin2, w_gate, w_up, w_down, layer, blk_exp, blk_cnt, row_src, row_dst, name):
    t, d = hin2.shape
    ff = w_gate.shape[3]
    nb = blk_exp.shape[0]
    blk = MOE_BLOCK
    tp = t + 2 * blk
    wmap = lambda i, be, bc, rs, rd: (layer, be[i], 0, 0)
    grid_spec = pltpu.PrefetchScalarGridSpec(
        num_scalar_prefetch=4,
        grid=(nb,),
        in_specs=[pl.BlockSpec(memory_space=pl.ANY),
                  pl.BlockSpec((1, 1, d, ff), wmap),
                  pl.BlockSpec((1, 1, d, ff), wmap),
                  pl.BlockSpec((1, 1, ff, d), wmap)],
        out_specs=pl.BlockSpec(memory_space=pl.ANY),
        scratch_shapes=[pltpu.VMEM((2, blk, d), F32),
                        pltpu.VMEM((2, blk, d), F32),
                        pltpu.VMEM((d, ff), BF16),
                        pltpu.VMEM((d, ff), BF16),
                        pltpu.VMEM((ff, d), BF16),
                        pltpu.SemaphoreType.DMA((2,)),
                        pltpu.SemaphoreType.DMA((2,))],
    )
    y2 = pl.pallas_call(
        functools.partial(_moe_kernel, blk=blk, nb=nb, tp=tp),
        out_shape=jax.ShapeDtypeStruct((MOE_TOPK * tp, d), F32),
        grid_spec=grid_spec,
        compiler_params=_cparams("arbitrary"),
        name=name,
    )(blk_exp, blk_cnt, row_src, row_dst, hin2, w_gate, w_up, w_down)
    return y2.reshape(MOE_TOPK, tp, d)


def _dispatch_tables(ids, n_exp):
    t = ids.shape[0]
    n_assign = t * MOE_TOPK
    blk = MOE_BLOCK
    tp = t + 2 * blk
    e_flat = ids.reshape(-1)
    order = jnp.argsort(e_flat).astype(I32)
    e_sorted = e_flat[order]
    counts = jnp.bincount(e_flat, length=n_exp).astype(I32)
    padded = (counts + blk - 1) // blk * blk
    pad_end = jnp.cumsum(padded)
    pad_start = pad_end - padded
    start = jnp.cumsum(counts) - counts
    dest = pad_start[e_sorted] + jnp.arange(n_assign, dtype=I32) - start[e_sorted]
    n_rows = n_assign + n_exp * blk
    nb = n_rows // blk
    rows = jnp.arange(n_rows, dtype=I32)
    spare = t + ((rows // blk) % 2) * blk + rows % blk
    row_src = jnp.zeros((n_rows,), I32).at[dest].set(order // MOE_TOPK)
    row_dst = spare.at[dest].set((order % MOE_TOPK) * tp + order // MOE_TOPK)
    blk_first = jnp.arange(nb, dtype=I32) * blk
    blk_exp = jnp.minimum(jnp.searchsorted(pad_end, blk_first, side='right'), n_exp - 1).astype(I32)
    in_use = blk_first < pad_end[-1]
    blk_cnt = jnp.where(in_use, jnp.clip(counts[blk_exp] - (blk_first - pad_start[blk_exp]), 0, blk), 0)
    return blk_exp, blk_cnt.astype(I32), row_src, row_dst


def _gdn_kernel(q_ref, k_ref, v_ref, z_ref, bg_ref, cg_ref, hh_ref, ba_ref,
                cw_ref, alog_ref, dtb_ref, nw_ref, scw_ref,
                o_ref,
                ext_ref, qa_ref, ka_ref, va_ref, hist_ref, hist2_ref, gate_ref, state_ref,
                *, heads, ts, gw):
    L = GDN_CHUNK
    dk = GDN_HEAD_DIM

    @pl.when(pl.program_id(1) == 0)
    def _():
        hist_ref[...] = jnp.zeros_like(hist_ref)
        hist2_ref[...] = jnp.zeros_like(hist2_ref)
        state_ref[...] = jnp.zeros_like(state_ref)

    def causal_conv(x, hist, w, taps):
        ext_ref[0:SUBLANES, :] = hist
        ext_ref[SUBLANES:SUBLANES + ts, :] = x
        y = w[0:1] * ext_ref[SUBLANES - taps + 1:SUBLANES - taps + 1 + ts, :]
        for j in range(1, taps - 1):
            y = y + w[j:j + 1] * ext_ref[SUBLANES - taps + 1 + j:SUBLANES - taps + 1 + j + ts, :]
        return y + w[taps - 1:taps] * x

    for idx, (src, dst) in enumerate(((q_ref, qa_ref), (k_ref, ka_ref), (v_ref, va_ref))):
        x = src[0]
        y = causal_conv(x, hist_ref[idx], cw_ref[:, idx * gw:(idx + 1) * gw], GDN_CONV)
        hist_ref[idx] = x[ts - SUBLANES:ts, :]
        dst[...] = _silu(y)

    c = cg_ref[0] * hh_ref[0]
    yb = causal_conv(c, hist2_ref[...], scw_ref[...], SC_CONV)
    hist2_ref[...] = c[ts - SUBLANES:ts, :]
    o_ref[0, :, gw:2 * gw] = bg_ref[0] * yb

    ba = ba_ref[0]
    gate_ref[0] = _sigmoid(ba)
    gate_ref[1] = -jnp.exp(alog_ref[...]) * _softplus(ba + dtb_ref[...])

    ii = lax.broadcasted_iota(I32, (L, L), 0)
    jj = lax.broadcasted_iota(I32, (L, L), 1)
    causal = ii >= jj
    strict = ii > jj
    tri_incl = causal.astype(F32)
    eye = (ii == jj).astype(F32)
    nchunk = ts // L
    pairs = [(c, h) for c in range(nchunk) for h in range(heads)]

    def tiles(ref, lead=()):
        return jnp.stack([ref[lead + (slice(c * L, (c + 1) * L), slice(h * dk, (h + 1) * dk))]
                          for c, h in pairs], axis=0)

    def bmm(a, b):
        return jnp.einsum('bij,bjk->bik', a.astype(BF16), b.astype(BF16), preferred_element_type=F32)

    def bmm_nt(a, b):
        return jnp.einsum('bik,bjk->bij', a.astype(BF16), b.astype(BF16), preferred_element_type=F32)

    qh = tiles(qa_ref)
    kh = tiles(ka_ref)
    vh = tiles(va_ref)
    beta = jnp.stack([gate_ref[0, c * L:(c + 1) * L, h:h + 1] for c, h in pairs], axis=0)
    cums = [jnp.dot(tri_incl, gate_ref[1, c * L:(c + 1) * L, :], precision=HIGHEST,
                    preferred_element_type=F32) for c in range(nchunk)]
    cums_t = [cm.T for cm in cums]
    gc = jnp.stack([cums[c][:, heads + h:heads + h + 1] for c, h in pairs], axis=0)
    gr = jnp.stack([cums_t[c][heads + h:heads + h + 1, :] for c, h in pairs], axis=0)
    gl = gc[:, L - 1:L, :]
    decay = jnp.where(causal, jnp.exp(jnp.where(causal, gc - gr, 0.0)), 0.0)
    egc = jnp.exp(gc)
    qn = qh * lax.rsqrt(jnp.sum(qh * qh, -1, keepdims=True) + 1e-6)
    kn = kh * lax.rsqrt(jnp.sum(kh * kh, -1, keepdims=True) + 1e-6)
    qc = qn * (dk ** -0.5)
    kb = kn * beta
    m = jnp.where(strict, bmm_nt(kb, kn) * decay, 0.0)
    t_inv = eye - m
    mp = m
    for _ in range(5):
        mp = bmm(mp, mp)
        t_inv = t_inv + bmm(t_inv, mp)
    sol = bmm(t_inv, jnp.concatenate([vh * beta, kb * egc], axis=2))
    u = sol[:, :, 0:dk]
    w = sol[:, :, dk:2 * dk]
    qk = jnp.where(causal, bmm_nt(qc, kn) * decay, 0.0)
    qg = qc * egc
    k_dec = kn * jnp.exp(gl - gc)
    egl = jnp.exp(gl)

    state = [state_ref[h] for h in range(heads)]
    outs = []
    for b, (c, h) in enumerate(pairs):
        st = state[h]
        v_new = u[b] - _dot(w[b], st)
        outs.append(_dot(qg[b], st) + _dot(qk[b], v_new))
        state[h] = st * egl[b] + _dot(k_dec[b].T, v_new)
    for h in range(heads):
        state_ref[h] = state[h]

    o = jnp.stack(outs, axis=0)
    on = o * lax.rsqrt(jnp.mean(o * o, -1, keepdims=True) + 1e-6) * nw_ref[...]
    y = on * _silu(tiles(z_ref, (0,)))
    for b, (c, h) in enumerate(pairs):
        o_ref[0, c * L:(c + 1) * L, h * dk:(h + 1) * dk] = y[b]


def _gdn_shortconv(proj, conv_w, a_log, dt_bias, norm_w, sc_w, heads):
    b, s, _ = proj.shape
    gw = heads * GDN_HEAD_DIM
    ts = _tile(s, 256)
    alog_p = jnp.zeros((1, LANES), F32).at[0, heads:2 * heads].set(a_log)
    dtb_p = jnp.zeros((1, LANES), F32).at[0, heads:2 * heads].set(dt_bias)
    col = lambda cb: (lambda bi, i: (bi, i, cb))
    const = lambda bi, i: (0, 0)
    kern = functools.partial(_gdn_kernel, heads=heads, ts=ts, gw=gw)
    return pl.pallas_call(
        kern,
        out_shape=jax.ShapeDtypeStruct((b, s, 2 * gw), F32),
        grid=(b, s // ts),
        in_specs=[pl.BlockSpec((1, ts, gw), col(cb)) for cb in range(7)]
        + [pl.BlockSpec((1, ts, LANES), col(7 * gw // LANES)),
           pl.BlockSpec((GDN_CONV, 3 * gw), const),
           pl.BlockSpec((1, LANES), const),
           pl.BlockSpec((1, LANES), const),
           pl.BlockSpec((1, GDN_HEAD_DIM), const),
           pl.BlockSpec((SC_CONV, gw), const)],
        out_specs=pl.BlockSpec((1, ts, 2 * gw), lambda bi, i: (bi, i, 0)),
        scratch_shapes=[pltpu.VMEM((ts + SUBLANES, gw), F32),
                        pltpu.VMEM((ts, gw), F32),
                        pltpu.VMEM((ts, gw), F32),
                        pltpu.VMEM((ts, gw), F32),
                        pltpu.VMEM((3, SUBLANES, gw), F32),
                        pltpu.VMEM((SUBLANES, gw), F32),
                        pltpu.VMEM((2, ts, LANES), F32),
                        pltpu.VMEM((heads, GDN_HEAD_DIM, GDN_HEAD_DIM), F32)],
        compiler_params=_cparams("parallel", "arbitrary"),
        name="gdn_shortconv",
    )(proj, proj, proj, proj, proj, proj, proj, proj,
      conv_w, alog_p, dtb_p, norm_w.reshape(1, GDN_HEAD_DIM), sc_w)


def _rope_kernel(pos_ref, inv_ref, qpe_ref, kpe_ref, qo_ref, ko_ref, *, heads):
    ang = pos_ref[0].astype(F32) * inv_ref[...]
    lane = lax.broadcasted_iota(I32, ang.shape, 1)
    half = MLA_ROPE // 2
    cos = jnp.where(lane < MLA_ROPE, jnp.cos(ang), 0.0)
    sin = jnp.sin(ang)
    sgn = jnp.where(lane < half, -sin, jnp.where(lane < MLA_ROPE, sin, 0.0))

    def rot(x):
        swapped = jnp.where(lane < half, pltpu.roll(x, LANES - half, 1), pltpu.roll(x, half, 1))
        return x * cos + swapped * sgn

    for h in range(heads):
        cols = slice(h * LANES, (h + 1) * LANES)
        qo_ref[0, :, cols] = rot(qpe_ref[0, :, cols])
    ko_ref[0] = rot(kpe_ref[0])


def _rope(positions, qup, heads, cproj, kpe_block):
    b, s = positions.shape
    ts = _tile(s, 512)
    half = MLA_ROPE // 2
    inv = ROPE_THETA ** (-jnp.arange(half, dtype=F32) * (2.0 / MLA_ROPE))
    inv_tab = jnp.zeros((1, LANES), F32).at[0, 0:half].set(inv).at[0, half:MLA_ROPE].set(inv)
    w = heads * LANES
    return pl.pallas_call(
        functools.partial(_rope_kernel, heads=heads),
        out_shape=(jax.ShapeDtypeStruct((b, s, w), F32), jax.ShapeDtypeStruct((b, s, LANES), F32)),
        grid=(b, s // ts),
        in_specs=[pl.BlockSpec((1, ts, 1), lambda bi, i: (bi, i, 0)),
                  pl.BlockSpec((1, LANES), lambda bi, i: (0, 0)),
                  pl.BlockSpec((1, ts, w), lambda bi, i: (bi, i, 1)),
                  pl.BlockSpec((1, ts, LANES), lambda bi, i: (bi, i, kpe_block))],
        out_specs=(pl.BlockSpec((1, ts, w), lambda bi, i: (bi, i, 0)),
                   pl.BlockSpec((1, ts, LANES), lambda bi, i: (bi, i, 0))),
        compiler_params=_cparams("parallel", "parallel"),
        name="mla_rope",
    )(positions.reshape(b, s, 1), inv_tab, qup, cproj)


def _attn_kernel(qn_ref, qp_ref, kn_ref, kp_ref, v_ref, o_ref, kcat_ref, vb_ref, s_ref, *, tq, nq, scale):
    qi = pl.program_id(2)

    @pl.when(qi == 0)
    def _():
        kcat_ref[:, 0:MLA_NOPE] = kn_ref[0].astype(BF16)
        kcat_ref[:, MLA_NOPE:MLA_NOPE + LANES] = kp_ref[0].astype(BF16)
        vb_ref[...] = v_ref[0].astype(BF16)

    qcat = jnp.concatenate([qn_ref[0], qp_ref[0]], axis=1).astype(BF16)
    row = lax.broadcasted_iota(I32, (tq, tq), 0)
    col = lax.broadcasted_iota(I32, (tq, tq), 1)

    for qv in range(nq):
        @pl.when(qi == qv)
        def _(qv=qv):
            m = None
            for kb in range(qv + 1):
                ks = slice(kb * tq, (kb + 1) * tq)
                s = lax.dot_general(qcat, kcat_ref[ks, :], (((1,), (1,)), ((), ())),
                                    preferred_element_type=F32) * scale
                if kb == qv:
                    s = jnp.where(row >= col, s, -jnp.inf)
                s_ref[:, ks] = s
                mb = jnp.max(s, -1, keepdims=True)
                m = mb if m is None else jnp.maximum(m, mb)
            l = jnp.zeros((tq, 1), F32)
            acc = jnp.zeros((tq, MLA_V), F32)
            for kb in range(qv + 1):
                ks = slice(kb * tq, (kb + 1) * tq)
                p = jnp.exp(s_ref[:, ks] - m)
                l = l + jnp.sum(p, -1, keepdims=True)
                acc = acc + jnp.dot(p.astype(BF16), vb_ref[ks, :], preferred_element_type=F32)
            o_ref[0] = acc / l


def _attention(qup, qpe, kvup, kpe, heads):
    b, s, _ = qup.shape
    tq = _tile(s, 512)
    nq = s // tq
    scale = (MLA_NOPE + MLA_ROPE) ** -0.5
    return pl.pallas_call(
        functools.partial(_attn_kernel, tq=tq, nq=nq, scale=scale),
        out_shape=jax.ShapeDtypeStruct((b, s, heads * MLA_V), F32),
        grid=(b, heads, nq),
        in_specs=[pl.BlockSpec((1, tq, MLA_NOPE), lambda bi, h, i: (bi, i, h)),
                  pl.BlockSpec((1, tq, LANES), lambda bi, h, i: (bi, i, h)),
                  pl.BlockSpec((1, s, MLA_NOPE), lambda bi, h, i: (bi, 0, h)),
                  pl.BlockSpec((1, s, LANES), lambda bi, h, i: (bi, 0, 0)),
                  pl.BlockSpec((1, s, MLA_V), lambda bi, h, i: (bi, 0, heads + h))],
        out_specs=pl.BlockSpec((1, tq, MLA_V), lambda bi, h, i: (bi, i, h)),
        scratch_shapes=[pltpu.VMEM((s, MLA_NOPE + LANES), BF16), pltpu.VMEM((s, MLA_V), BF16),
                        pltpu.VMEM((tq, s), F32)],
        compiler_params=_cparams("parallel", "parallel", "arbitrary"),
        name="mla_attention",
    )(qup, qpe, kvup, kpe, kvup)


def _hyb_in_weight(w_in, heads):
    d = w_in.shape[0]
    gw = heads * GDN_HEAD_DIM
    main = jnp.concatenate([w_in[:, :4 * gw], w_in[:, 4 * gw + 2 * heads:]], axis=1)
    gates = w_in[:, 4 * gw:4 * gw + 2 * heads]
    n = main.shape[1] + LANES
    n_pad = -(-n // 768) * 768
    tail = jnp.zeros((d, n_pad - main.shape[1]), w_in.dtype).at[:, :2 * heads].set(gates)
    return jnp.concatenate([main, tail], axis=1).astype(BF16)


def _mla_in_weight(w_in):
    d = w_in.shape[0]
    pad = jnp.zeros((d, LANES - MLA_ROPE), w_in.dtype)
    return jnp.concatenate([w_in, pad], axis=1).astype(BF16)


def _mla_uq_weight(w_uq, heads):
    r = w_uq.shape[0]
    w = w_uq.reshape(r, heads, MLA_NOPE + MLA_ROPE)
    nope = w[:, :, :MLA_NOPE].reshape(r, heads * MLA_NOPE)
    pe = jnp.concatenate([w[:, :, MLA_NOPE:], jnp.zeros((r, heads, LANES - MLA_ROPE), w.dtype)], axis=2)
    return jnp.concatenate([nope, pe.reshape(r, heads * LANES)], axis=1).astype(BF16)


def _mla_ukv_weight(w_ukv, heads):
    r = w_ukv.shape[0]
    w = w_ukv.reshape(r, heads, MLA_NOPE + MLA_V)
    return jnp.concatenate([w[:, :, :MLA_NOPE].reshape(r, heads * MLA_NOPE),
                            w[:, :, MLA_NOPE:].reshape(r, heads * MLA_V)], axis=1).astype(BF16)


def _router_weight(wr_g, br_g, wr_e, br_e):
    d = wr_g.shape[0]
    n = wr_g.shape[1] + wr_e.shape[1]
    w = jnp.zeros((d, LANES), F32).at[:, :n].set(jnp.concatenate([wr_g, wr_e], axis=1))
    bias = jnp.zeros((1, LANES), F32).at[0, :n].set(jnp.concatenate([br_g, br_e]))
    w_hi = w.astype(BF16)
    w_lo = (w - w_hi.astype(F32)).astype(BF16)
    return jnp.stack([w_hi, w_lo]), bias


def kernel(x, c, positions, ada_w, ada_b, ln_g, ln_b, hyb_w_in, gdn_conv_w, gdn_a_log, gdn_dt_bias,
           gdn_norm_w, sc_conv_w, hyb_w_out, mla_w_in, mla_q_norm, mla_kv_norm, mla_w_uq, mla_w_ukv,
           mla_w_out, moe_router_g, moe_bias_g, moe_router_e, moe_bias_e, moe_w_gate, moe_w_up,
           moe_w_down):
    b, s, d = x.shape
    depth = ada_w.shape[0]
    alpha = (2.0 * depth) ** 0.25
    gdn_heads = d // (2 * GDN_HEAD_DIM)
    mla_heads = mla_w_out.shape[1] // MLA_V
    n_groups = moe_router_g.shape[2]
    n_exp = moe_router_e.shape[2]
    per_group = n_exp // n_groups

    mod = _modulation(c, ada_w, ada_b)
    for layer in range(depth):
        sh1, sc1, g1, sh2, sc2, g2 = [mod[layer, :, j * d:(j + 1) * d].reshape(b, 1, d) for j in range(6)]
        i = layer // 2
        wr, br = _router_weight(moe_router_g[layer], moe_bias_g[layer], moe_router_e[layer], moe_bias_e[layer])
        if layer % 2 == 0:
            proj = _mod_matmul(x, sc1, sh1, _hyb_in_weight(hyb_w_in[i], gdn_heads), "hyb_in_proj")
            y = _gdn_shortconv(proj, gdn_conv_w[i], gdn_a_log[i], gdn_dt_bias[i], gdn_norm_w[i],
                               sc_conv_w[i], gdn_heads)
            w_out = hyb_w_out[i]
        else:
            cproj = _mod_matmul(x, sc1, sh1, _mla_in_weight(mla_w_in[i]), "mla_in_proj")
            qup = _rms_matmul(cproj, 0, MLA_Q_RANK, mla_q_norm[i], _mla_uq_weight(mla_w_uq[i], mla_heads),
                              "mla_q_up")
            kvup = _rms_matmul(cproj, 1, MLA_KV_RANK, mla_kv_norm[i], _mla_ukv_weight(mla_w_ukv[i], mla_heads),
                               "mla_kv_up")
            qpe, kpe = _rope(positions, qup, mla_heads, cproj, (MLA_Q_RANK + MLA_KV_RANK) // LANES)
            y = _attention(qup, qpe, kvup, kpe, mla_heads)
            w_out = mla_w_out[i]
        x, hin2, ids, gates = _out_ln_route(y, w_out.astype(BF16), x, g1, ln_g[layer, 0], ln_b[layer, 0],
                                            sc2, sh2, wr, br, alpha, n_groups, per_group,
                                            "mixer_out_ln_route")
        t = b * s
        tables = _dispatch_tables(ids.reshape(t, LANES)[:, :MOE_TOPK], n_exp)
        y2 = _moe_experts(hin2.reshape(t, d), moe_w_gate, moe_w_up, moe_w_down, layer, *tables, "moe_experts")
        x = _combine_ln(y2, gates, x, g2, ln_g[layer, 1], ln_b[layer, 1], alpha, "moe_combine_ln")
    return x
```

```python
import functools

import jax
import jax.numpy as jnp
from jax import lax
from jax.experimental import pallas as pl
from jax.experimental.pallas import tpu as pltpu

F32, BF16, I32 = jnp.float32, jnp.bfloat16, jnp.int32
HIGHEST = lax.Precision.HIGHEST

LANES = 128
SUBLANES = 8
VMEM_LIMIT_BYTES = 56 * 1024 * 1024

GDN_HEAD_DIM = 128
GDN_CONV = 4
GDN_CHUNK = 64
SC_CONV = 3
MLA_NOPE = 128
MLA_ROPE = 64
MLA_V = 128
MLA_Q_RANK = 512
MLA_KV_RANK = 512
ROPE_THETA = 10000.0
MOE_TOPK = 2
MOE_BLOCK = 128
DMA_UNROLL = 8


def _tile(n, pref):
    if n <= pref:
        return n
    for t in range(pref, 0, -LANES):
        if n % t == 0:
            return t
    return n


def _cparams(*sem):
    return pltpu.CompilerParams(dimension_semantics=sem, vmem_limit_bytes=VMEM_LIMIT_BYTES)


def _sigmoid(x):
    return 1.0 / (1.0 + jnp.exp(-x))


def _silu(x):
    return x * _sigmoid(x)


def _softplus(x):
    return jnp.maximum(x, 0.0) + jnp.log(1.0 + jnp.exp(-jnp.abs(x)))


def _dot(a, b):
    return jnp.dot(a.astype(BF16), b.astype(BF16), preferred_element_type=F32)


def _dot_nt(a, b):
    return lax.dot_general(a.astype(BF16), b.astype(BF16), (((1,), (1,)), ((), ())),
                           preferred_element_type=F32)


def _mod_kernel(c_ref, w_ref, b_ref, o_ref):
    c = c_ref[...]
    o_ref[0] = _dot(_silu(c), w_ref[0]) + b_ref[0]


def _modulation(c, ada_w, ada_b):
    depth, d, n = ada_w.shape
    b = c.shape[0]
    rows = -(-b // SUBLANES) * SUBLANES
    cp = jnp.zeros((rows, d), F32).at[:b].set(c)
    tn = _tile(n, 1024)
    out = pl.pallas_call(
        _mod_kernel,
        out_shape=jax.ShapeDtypeStruct((depth, rows, n), F32),
        grid=(depth, n // tn),
        in_specs=[pl.BlockSpec((rows, d), lambda l, j: (0, 0)),
                  pl.BlockSpec((1, d, tn), lambda l, j: (l, 0, j)),
                  pl.BlockSpec((1, 1, tn), lambda l, j: (l, 0, j))],
        out_specs=pl.BlockSpec((1, rows, tn), lambda l, j: (l, 0, j)),
        compiler_params=_cparams("parallel", "parallel"),
        name="adaln_mod",
    )(cp, ada_w, ada_b.reshape(depth, 1, n))
    return out[:, :b]


def _modmm_kernel(x_ref, sc_ref, sh_ref, w_ref, o_ref, xb_ref):
    @pl.when(pl.program_id(2) == 0)
    def _():
        xb_ref[...] = (x_ref[0] * (1.0 + sc_ref[0]) + sh_ref[0]).astype(BF16)

    o_ref[0] = jnp.dot(xb_ref[...], w_ref[...], preferred_element_type=F32)


def _mod_matmul(x, sc, sh, w, name):
    b, s, d = x.shape
    n = w.shape[1]
    tm = _tile(s, 1024)
    tn = _tile(n, 768)
    return pl.pallas_call(
        _modmm_kernel,
        out_shape=jax.ShapeDtypeStruct((b, s, n), F32),
        grid=(b, s // tm, n // tn),
        in_specs=[pl.BlockSpec((1, tm, d), lambda bi, i, j: (bi, i, 0)),
                  pl.BlockSpec((1, 1, d), lambda bi, i, j: (bi, 0, 0)),
                  pl.BlockSpec((1, 1, d), lambda bi, i, j: (bi, 0, 0)),
                  pl.BlockSpec((d, tn), lambda bi, i, j: (0, j))],
        out_specs=pl.BlockSpec((1, tm, tn), lambda bi, i, j: (bi, i, j)),
        scratch_shapes=[pltpu.VMEM((tm, d), BF16)],
        compiler_params=_cparams("parallel", "parallel", "arbitrary"),
        name=name,
    )(x, sc, sh, w)


def _rmsmm_kernel(x_ref, nw_ref, w_ref, o_ref, xb_ref):
    @pl.when(pl.program_id(2) == 0)
    def _():
        x = x_ref[0]
        xb_ref[...] = (x * lax.rsqrt(jnp.mean(x * x, -1, keepdims=True) + 1e-6) * nw_ref[...]).astype(BF16)

    o_ref[0] = jnp.dot(xb_ref[...], w_ref[...], preferred_element_type=F32)


def _rms_matmul(x, col_block, k, norm_w, w, name):
    b, s, _ = x.shape
    n = w.shape[1]
    tm = _tile(s, 1024)
    tn = _tile(n, 1024)
    return pl.pallas_call(
        _rmsmm_kernel,
        out_shape=jax.ShapeDtypeStruct((b, s, n), F32),
        grid=(b, s // tm, n // tn),
        in_specs=[pl.BlockSpec((1, tm, k), lambda bi, i, j: (bi, i, col_block)),
                  pl.BlockSpec((1, k), lambda bi, i, j: (0, 0)),
                  pl.BlockSpec((k, tn), lambda bi, i, j: (0, j))],
        out_specs=pl.BlockSpec((1, tm, tn), lambda bi, i, j: (bi, i, j)),
        scratch_shapes=[pltpu.VMEM((tm, k), BF16)],
        compiler_params=_cparams("parallel", "parallel", "arbitrary"),
        name=name,
    )(x, norm_w.reshape(1, k), w)


def _layer_norm_rows(r, g, b):
    mu = jnp.mean(r, -1, keepdims=True)
    var = jnp.mean(jnp.square(r - mu), -1, keepdims=True)
    return (r - mu) * lax.rsqrt(var + 1e-5) * g + b


def _outln_kernel(y_ref, w_ref, xres_ref, g_ref, lng_ref, lnb_ref, sc_ref, sh_ref, wr_ref, br_ref,
                  xo_ref, hin_ref, ids_ref, gates_ref, *, alpha, n_groups, per_group):
    y = jnp.dot(y_ref[0].astype(BF16), w_ref[...], preferred_element_type=F32)
    r = alpha * xres_ref[0] + (1.0 + g_ref[0]) * y
    xn = _layer_norm_rows(r, lng_ref[...], lnb_ref[...])
    xo_ref[0] = xn
    hin = xn * (1.0 + sc_ref[0]) + sh_ref[0]
    hin_ref[0] = hin
    h_hi = hin.astype(BF16)
    h_lo = (hin - h_hi.astype(F32)).astype(BF16)
    w_hi = wr_ref[0]
    logits = (jnp.dot(h_hi, w_hi, preferred_element_type=F32)
              + jnp.dot(h_hi, wr_ref[1], preferred_element_type=F32)
              + jnp.dot(h_lo, w_hi, preferred_element_type=F32)) + br_ref[...]
    lane = lax.broadcasted_iota(I32, logits.shape, 1)
    big = jnp.int32(4 * LANES)
    neg = jnp.float32(-jnp.inf)
    n_exp = n_groups * per_group
    gmask = lane < n_groups
    lg = jnp.where(gmask, logits, neg)
    mg = jnp.max(lg, -1, keepdims=True)
    grp = jnp.min(jnp.where(gmask & (lg == mg), lane, big), -1, keepdims=True)
    pg_sel = 1.0 / jnp.sum(jnp.where(gmask, jnp.exp(lg - mg), 0.0), -1, keepdims=True)
    lo = n_groups + grp * per_group
    emask = (lane >= lo) & (lane < lo + per_group) & (lane < n_groups + n_exp)
    le = jnp.where(emask, logits, neg)
    me = jnp.max(le, -1, keepdims=True)
    ee = jnp.where(emask, jnp.exp(le - me), 0.0)
    p = ee / jnp.sum(ee, -1, keepdims=True)
    pm = jnp.where(emask, p, -1.0)
    p1 = jnp.max(pm, -1, keepdims=True)
    i1 = jnp.min(jnp.where(emask & (pm == p1), lane, big), -1, keepdims=True)
    pm2 = jnp.where(lane == i1, -1.0, pm)
    p2 = jnp.max(pm2, -1, keepdims=True)
    i2 = jnp.min(jnp.where(emask & (lane != i1) & (pm2 == p2), lane, big), -1, keepdims=True)
    den = p1 + p2
    g1 = pg_sel * p1 / den
    g2 = pg_sel * p2 / den
    ids_ref[0] = jnp.where(lane == 0, i1 - n_groups, jnp.where(lane == 1, i2 - n_groups, 0))
    gates_ref[0] = jnp.where(lane == 0, g1, jnp.where(lane == 1, g2, 0.0))


def _out_ln_route(y, w, xres, gate, ln_g, ln_b, sc, sh, wr, br, alpha, n_groups, per_group, name):
    b, s, k = y.shape
    d = w.shape[1]
    tm = _tile(s, 256)
    kern = functools.partial(_outln_kernel, alpha=alpha, n_groups=n_groups, per_group=per_group)
    row = lambda bi, i: (bi, i, 0)
    per_b = lambda bi, i: (bi, 0, 0)
    const = lambda bi, i: (0, 0)
    return pl.pallas_call(
        kern,
        out_shape=(jax.ShapeDtypeStruct((b, s, d), F32), jax.ShapeDtypeStruct((b, s, d), F32),
                   jax.ShapeDtypeStruct((b, s, LANES), I32), jax.ShapeDtypeStruct((b, s, LANES), F32)),
        grid=(b, s // tm),
        in_specs=[pl.BlockSpec((1, tm, k), row),
                  pl.BlockSpec((k, d), const),
                  pl.BlockSpec((1, tm, d), row),
                  pl.BlockSpec((1, 1, d), per_b),
                  pl.BlockSpec((1, d), const),
                  pl.BlockSpec((1, d), const),
                  pl.BlockSpec((1, 1, d), per_b),
                  pl.BlockSpec((1, 1, d), per_b),
                  pl.BlockSpec((2, d, LANES), lambda bi, i: (0, 0, 0)),
                  pl.BlockSpec((1, LANES), const)],
        out_specs=(pl.BlockSpec((1, tm, d), row), pl.BlockSpec((1, tm, d), row),
                   pl.BlockSpec((1, tm, LANES), row), pl.BlockSpec((1, tm, LANES), row)),
        compiler_params=_cparams("parallel", "parallel"),
        name=name,
    )(y, w, xres, gate, ln_g.reshape(1, d), ln_b.reshape(1, d), sc, sh, wr, br)


def _combine_kernel(ya_ref, yb_ref, rg_ref, x_ref, g_ref, lng_ref, lnb_ref, o_ref, *, alpha):
    rg = rg_ref[0]
    y = ya_ref[0] * rg[:, 0:1] + yb_ref[0] * rg[:, 1:2]
    r = alpha * x_ref[0] + (1.0 + g_ref[0]) * y
    o_ref[0] = _layer_norm_rows(r, lng_ref[...], lnb_ref[...])


def _combine_ln(y2, route_gates, x, gate, ln_g, ln_b, alpha, name):
    b, s, d = x.shape
    tm = _tile(s, 512)
    row = lambda bi, i: (bi, i, 0)
    nt = s // tm
    return pl.pallas_call(
        functools.partial(_combine_kernel, alpha=alpha),
        out_shape=jax.ShapeDtypeStruct((b, s, d), F32),
        grid=(b, nt),
        in_specs=[pl.BlockSpec((1, tm, d), lambda bi, i: (0, bi * nt + i, 0)),
                  pl.BlockSpec((1, tm, d), lambda bi, i: (1, bi * nt + i, 0)),
                  pl.BlockSpec((1, tm, LANES), row),
                  pl.BlockSpec((1, tm, d), row),
                  pl.BlockSpec((1, 1, d), lambda bi, i: (bi, 0, 0)),
                  pl.BlockSpec((1, d), lambda bi, i: (0, 0)),
                  pl.BlockSpec((1, d), lambda bi, i: (0, 0))],
        out_specs=pl.BlockSpec((1, tm, d), row),
        compiler_params=_cparams("parallel", "parallel"),
        name=name,
    )(y2, y2, route_gates, x, gate, ln_g.reshape(1, d), ln_b.reshape(1, d))


def _moe_kernel(bexp_ref, bcnt_ref, rsrc_ref, rdst_ref,
                hin_hbm, wg_ref, wu_ref, wd_ref,
                y2_hbm,
                xbuf, ybuf, wgb, wub, wdb, gsem, ssem, *, blk, nb, tp):
    i = pl.program_id(0)
    slot = i % 2

    def start_gather(block, sl):
        base = block * blk

        def group(gi, c):
            for u in range(DMA_UNROLL):
                r = gi * DMA_UNROLL + u
                pltpu.make_async_copy(hin_hbm.at[pl.ds(rsrc_ref[base + r], 1), :],
                                      xbuf.at[sl, pl.ds(r, 1), :], gsem.at[sl]).start()
            return c

        lax.fori_loop(0, blk // DMA_UNROLL, group, 0)

    def wait_gather(sl):
        pltpu.make_async_copy(hin_hbm.at[pl.ds(0, blk), :], xbuf.at[sl], gsem.at[sl]).wait()

    def start_scatter(block, sl):
        base = block * blk

        def group(gi, c):
            for u in range(DMA_UNROLL):
                r = gi * DMA_UNROLL + u
                pltpu.make_async_copy(ybuf.at[sl, pl.ds(r, 1), :],
                                      y2_hbm.at[pl.ds(rdst_ref[base + r], 1), :], ssem.at[sl]).start()
            return c

        lax.fori_loop(0, blk // DMA_UNROLL, group, 0)

    def wait_scatter(sl):
        pltpu.make_async_copy(ybuf.at[sl], y2_hbm.at[pl.ds(0, blk), :], ssem.at[sl]).wait()

    used = bcnt_ref[i] > 0

    @pl.when(i == 0)
    def _():
        ybuf[...] = jnp.zeros_like(ybuf)
        for k in range(MOE_TOPK):
            for sl in range(2):
                cp = pltpu.make_async_copy(ybuf.at[sl], y2_hbm.at[pl.ds(k * tp + tp - (2 - sl) * blk, blk), :],
                                           ssem.at[sl])
                cp.start()
                cp.wait()

    @pl.when(jnp.logical_and(i == 0, used))
    def _():
        start_gather(0, 0)

    nxt = jnp.minimum(i + 1, nb - 1)

    @pl.when(jnp.logical_and(i + 1 < nb, bcnt_ref[nxt] > 0))
    def _():
        start_gather(nxt, 1 - slot)

    changed = jnp.logical_or(i == 0, bexp_ref[i] != bexp_ref[jnp.maximum(i - 1, 0)])

    @pl.when(jnp.logical_and(changed, used))
    def _():
        wgb[...] = wg_ref[0, 0].astype(BF16)
        wub[...] = wu_ref[0, 0].astype(BF16)
        wdb[...] = wd_ref[0, 0].astype(BF16)

    @pl.when(jnp.logical_and(i >= 2, bcnt_ref[jnp.maximum(i - 2, 0)] > 0))
    def _():
        wait_scatter(slot)

    @pl.when(used)
    def _():
        wait_gather(slot)
        xb = xbuf[slot].astype(BF16)
        hg = jnp.dot(xb, wgb[...], preferred_element_type=F32)
        hu = jnp.dot(xb, wub[...], preferred_element_type=F32)
        hid = (_silu(hg) * hu).astype(BF16)
        ybuf[slot] = jnp.dot(hid, wdb[...], preferred_element_type=F32)
        start_scatter(i, slot)

    @pl.when(i == nb - 1)
    def _():
        if nb >= 2:
            @pl.when(bcnt_ref[jnp.maximum(i - 1, 0)] > 0)
            def _():
                wait_scatter(1 - slot)

        @pl.when(used)
        def _():
            wait_scatter(slot)


def _moe_experts(hin2, w_gate, w_up, w_down, layer, blk_exp, blk_cnt, row_src, row_dst, name):
    t, d = hin2.shape
    ff = w_gate.shape[3]
    nb = blk_exp.shape[0]
    blk = MOE_BLOCK
    tp = t + 2 * blk
    wmap = lambda i, be, bc, rs, rd: (layer, be[i], 0, 0)
    grid_spec = pltpu.PrefetchScalarGridSpec(
        num_scalar_prefetch=4,
        grid=(nb,),
        in_specs=[pl.BlockSpec(memory_space=pl.ANY),
                  pl.BlockSpec((1, 1, d, ff), wmap),
                  pl.BlockSpec((1, 1, d, ff), wmap),
                  pl.BlockSpec((1, 1, ff, d), wmap)],
        out_specs=pl.BlockSpec(memory_space=pl.ANY),
        scratch_shapes=[pltpu.VMEM((2, blk, d), F32),
                        pltpu.VMEM((2, blk, d), F32),
                        pltpu.VMEM((d, ff), BF16),
                        pltpu.VMEM((d, ff), BF16),
                        pltpu.VMEM((ff, d), BF16),
                        pltpu.SemaphoreType.DMA((2,)),
                        pltpu.SemaphoreType.DMA((2,))],
    )
    y2 = pl.pallas_call(
        functools.partial(_moe_kernel, blk=blk, nb=nb, tp=tp),
        out_shape=jax.ShapeDtypeStruct((MOE_TOPK * tp, d), F32),
        grid_spec=grid_spec,
        compiler_params=_cparams("arbitrary"),
        name=name,
    )(blk_exp, blk_cnt, row_src, row_dst, hin2, w_gate, w_up, w_down)
    return y2.reshape(MOE_TOPK, tp, d)


def _dispatch_tables(ids, n_exp):
    t = ids.shape[0]
    n_assign = t * MOE_TOPK
    blk = MOE_BLOCK
    tp = t + 2 * blk
    e_flat = ids.reshape(-1)
    order = jnp.argsort(e_flat).astype(I32)
    e_sorted = e_flat[order]
    counts = jnp.bincount(e_flat, length=n_exp).astype(I32)
    padded = (counts + blk - 1) // blk * blk
    pad_end = jnp.cumsum(padded)
    pad_start = pad_end - padded
    start = jnp.cumsum(counts) - counts
    dest = pad_start[e_sorted] + jnp.arange(n_assign, dtype=I32) - start[e_sorted]
    n_rows = n_assign + n_exp * blk
    nb = n_rows // blk
    rows = jnp.arange(n_rows, dtype=I32)
    spare = t + ((rows // blk) % 2) * blk + rows % blk
    row_src = jnp.zeros((n_rows,), I32).at[dest].set(order // MOE_TOPK)
    row_dst = spare.at[dest].set((order % MOE_TOPK) * tp + order // MOE_TOPK)
    blk_first = jnp.arange(nb, dtype=I32) * blk
    blk_exp = jnp.minimum(jnp.searchsorted(pad_end, blk_first, side='right'), n_exp - 1).astype(I32)
    in_use = blk_first < pad_end[-1]
    blk_cnt = jnp.where(in_use, jnp.clip(counts[blk_exp] - (blk_first - pad_start[blk_exp]), 0, blk), 0)
    return blk_exp, blk_cnt.astype(I32), row_src, row_dst


def _dispatch_kernel(ids_ref, dest_ref, meta_ref, rank_ref, *, t, tile, blk_shift):
    lane = lax.broadcasted_iota(I32, (tile, LANES), 1)
    ri = lax.broadcasted_iota(I32, (tile, tile), 0)
    ci = lax.broadcasted_iota(I32, (tile, tile), 1)
    before = (ri > ci).astype(BF16)

    def hits(rows):
        ids = ids_ref[rows, :]
        return lane == ids[:, 0:1], lane == ids[:, 1:2]

    def count(ti, run):
        rows = pl.ds(pl.multiple_of(ti * tile, tile), tile)
        h1, h2 = hits(rows)
        onehot = jnp.logical_or(h1, h2).astype(BF16)
        prefix = jnp.dot(before, onehot, preferred_element_type=F32) + run
        r1 = jnp.sum(jnp.where(h1, prefix, 0.0), -1, keepdims=True)
        r2 = jnp.sum(jnp.where(h2, prefix, 0.0), -1, keepdims=True)
        rank_ref[rows, :] = jnp.where(lane == 0, r1, jnp.where(lane == 1, r2, 0.0))
        return run + jnp.sum(onehot.astype(F32), 0, keepdims=True)

    counts = lax.fori_loop(0, t // tile, count, jnp.zeros((1, LANES), F32))
    cnt = jnp.broadcast_to(counts, (SUBLANES, LANES)).astype(I32)
    nblk = (cnt + ((1 << blk_shift) - 1)) >> blk_shift
    padded = nblk << blk_shift
    lane8 = lax.broadcasted_iota(I32, (SUBLANES, LANES), 1)
    incl = padded
    step = 1
    while step < LANES:
        incl = incl + jnp.where(lane8 >= step, pltpu.roll(incl, step, 1), 0)
        step *= 2
    pad_start = incl - padded
    row8 = lax.broadcasted_iota(I32, (SUBLANES, LANES), 0)
    meta_ref[...] = jnp.where(row8 == 0, pad_start, jnp.where(row8 == 1, nblk, cnt))
    start_f = pad_start[0:1, :].astype(F32)

    def place(ti, c):
        rows = pl.ds(pl.multiple_of(ti * tile, tile), tile)
        h1, h2 = hits(rows)
        rk = rank_ref[rows, :]
        d1 = jnp.sum(jnp.where(h1, start_f, 0.0), -1, keepdims=True) + rk[:, 0:1]
        d2 = jnp.sum(jnp.where(h2, start_f, 0.0), -1, keepdims=True) + rk[:, 1:2]
        dest_ref[rows, :] = jnp.where(lane == 0, d1, jnp.where(lane == 1, d2, 0.0)).astype(I32)
        return c

    lax.fori_loop(0, t // tile, place, 0)


def _dispatch(ids):
    t = ids.shape[0]
    tile = _tile(t, 256)
    blk_shift = MOE_BLOCK.bit_length() - 1
    assert (1 << blk_shift) == MOE_BLOCK
    return pl.pallas_call(
        functools.partial(_dispatch_kernel, t=t, tile=tile, blk_shift=blk_shift),
        out_shape=(jax.ShapeDtypeStruct((t, LANES), I32), jax.ShapeDtypeStruct((SUBLANES, LANES), I32)),
        grid=(1,),
        in_specs=[pl.BlockSpec((t, LANES), lambda i: (0, 0))],
        out_specs=(pl.BlockSpec((t, LANES), lambda i: (0, 0)), pl.BlockSpec((SUBLANES, LANES), lambda i: (0, 0))),
        scratch_shapes=[pltpu.VMEM((t, LANES), F32)],
        compiler_params=_cparams("arbitrary"),
        name="moe_dispatch",
    )(ids)


def _scatter_rows_kernel(dest_ref, hin_ref, xs_in, xs_out, sem, *, tm):
    del xs_in
    base = pl.program_id(0) * tm

    def group(gi, c):
        for u in range(DMA_UNROLL):
            r = gi * DMA_UNROLL + u
            for k in range(MOE_TOPK):
                pltpu.make_async_copy(hin_ref.at[pl.ds(r, 1), :],
                                      xs_out.at[pl.ds(dest_ref[(base + r) * MOE_TOPK + k], 1), :], sem).start()
        return c

    lax.fori_loop(0, tm // DMA_UNROLL, group, 0)
    for _ in range(MOE_TOPK):
        pltpu.make_async_copy(hin_ref, xs_out.at[pl.ds(0, tm), :], sem).wait()


def _scatter_rows(dest_flat, hin2, n_rows):
    t, d = hin2.shape
    tm = _tile(t, 256)
    grid_spec = pltpu.PrefetchScalarGridSpec(
        num_scalar_prefetch=1,
        grid=(t // tm,),
        in_specs=[pl.BlockSpec((tm, d), lambda i, dst: (i, 0)),
                  pl.BlockSpec(memory_space=pl.ANY)],
        out_specs=pl.BlockSpec(memory_space=pl.ANY),
        scratch_shapes=[pltpu.SemaphoreType.DMA],
    )
    return pl.pallas_call(
        functools.partial(_scatter_rows_kernel, tm=tm),
        out_shape=jax.ShapeDtypeStruct((n_rows, d), F32),
        grid_spec=grid_spec,
        input_output_aliases={2: 0},
        compiler_params=_cparams("arbitrary"),
        name="moe_scatter_rows",
    )(dest_flat, hin2, jnp.zeros((n_rows, d), F32))


def _ffn_kernel(eblk_ref, enb_ref, xs_hbm, wg_ref, wu_ref, wd_ref, ys_hbm,
                xbuf, ybuf, wgb, wub, wdb, isem, osem, *, blk, nb_total, n_exp):
    e = pl.program_id(0)
    n = enb_ref[e]
    b0 = eblk_ref[e]

    def in_cp(j, sl):
        return pltpu.make_async_copy(xs_hbm.at[pl.ds((b0 + j) * blk, blk), :], xbuf.at[sl], isem.at[sl])

    def out_cp(row_blk, sl):
        return pltpu.make_async_copy(ybuf.at[sl], ys_hbm.at[pl.ds(row_blk * blk, blk), :], osem.at[sl])

    @pl.when(n > 0)
    def _():
        in_cp(0, 0).start()
        wgb[...] = wg_ref[0, 0].astype(BF16)
        wub[...] = wu_ref[0, 0].astype(BF16)
        wdb[...] = wd_ref[0, 0].astype(BF16)

        def body(j, c):
            sl = j % 2

            @pl.when(j + 1 < n)
            def _():
                in_cp(j + 1, 1 - sl).start()

            in_cp(j, sl).wait()

            @pl.when(j >= 2)
            def _():
                out_cp(b0 + j - 2, sl).wait()

            xb = xbuf[sl].astype(BF16)
            hg = jnp.dot(xb, wgb[...], preferred_element_type=F32)
            hu = jnp.dot(xb, wub[...], preferred_element_type=F32)
            hid = (_silu(hg) * hu).astype(BF16)
            ybuf[sl] = jnp.dot(hid, wdb[...], preferred_element_type=F32)
            out_cp(b0 + j, sl).start()
            return c

        lax.fori_loop(0, n, body, 0)

        @pl.when(n >= 2)
        def _():
            out_cp(b0 + n - 2, n % 2).wait()

        out_cp(b0 + n - 1, (n - 1) % 2).wait()

    @pl.when(e == n_exp - 1)
    def _():
        ybuf[0] = jnp.zeros((blk, ybuf.shape[2]), F32)

        def fill(jb, c):
            cp = out_cp(jb, 0)
            cp.start()
            cp.wait()
            return c

        lax.fori_loop(b0 + n, nb_total, fill, 0)


def _moe_ffn(eblk, enb, xs, w_gate, w_up, w_down, layer):
    n_rows, d = xs.shape
    n_exp, ff = w_gate.shape[1], w_gate.shape[3]
    blk = MOE_BLOCK
    wmap = lambda e, eb, en: (layer, e, 0, 0)
    grid_spec = pltpu.PrefetchScalarGridSpec(
        num_scalar_prefetch=2,
        grid=(n_exp,),
        in_specs=[pl.BlockSpec(memory_space=pl.ANY),
                  pl.BlockSpec((1, 1, d, ff), wmap),
                  pl.BlockSpec((1, 1, d, ff), wmap),
                  pl.BlockSpec((1, 1, ff, d), wmap)],
        out_specs=pl.BlockSpec(memory_space=pl.ANY),
        scratch_shapes=[pltpu.VMEM((2, blk, d), F32),
                        pltpu.VMEM((2, blk, d), F32),
                        pltpu.VMEM((d, ff), BF16),
                        pltpu.VMEM((d, ff), BF16),
                        pltpu.VMEM((ff, d), BF16),
                        pltpu.SemaphoreType.DMA((2,)),
                        pltpu.SemaphoreType.DMA((2,))],
    )
    return pl.pallas_call(
        functools.partial(_ffn_kernel, blk=blk, nb_total=n_rows // blk, n_exp=n_exp),
        out_shape=jax.ShapeDtypeStruct((n_rows, d), F32),
        grid_spec=grid_spec,
        compiler_params=_cparams("arbitrary"),
        name="moe_ffn",
    )(eblk, enb, xs, w_gate, w_up, w_down)


def _gather_ln_kernel(dest_ref, ys_hbm, rg_ref, x_ref, g_ref, lng_ref, lnb_ref, o_ref, ybuf, sem,
                      *, alpha, tm, nt, n_steps):
    step = pl.program_id(0) * nt + pl.program_id(1)
    slot = step % 2

    def start_gather(tile_idx, sl):
        base = tile_idx * tm

        def group(gi, c):
            for u in range(DMA_UNROLL):
                r = gi * DMA_UNROLL + u
                for k in range(MOE_TOPK):
                    pltpu.make_async_copy(ys_hbm.at[pl.ds(dest_ref[(base + r) * MOE_TOPK + k], 1), :],
                                          ybuf.at[sl, k, pl.ds(r, 1), :], sem.at[sl]).start()
            return c

        lax.fori_loop(0, tm // DMA_UNROLL, group, 0)

    @pl.when(step == 0)
    def _():
        start_gather(0, 0)

    @pl.when(step + 1 < n_steps)
    def _():
        start_gather(step + 1, 1 - slot)

    for k in range(MOE_TOPK):
        pltpu.make_async_copy(ys_hbm.at[pl.ds(0, tm), :], ybuf.at[slot, k], sem.at[slot]).wait()

    rg = rg_ref[0]
    y = ybuf[slot, 0] * rg[:, 0:1] + ybuf[slot, 1] * rg[:, 1:2]
    r = alpha * x_ref[0] + (1.0 + g_ref[0]) * y
    o_ref[0] = _layer_norm_rows(r, lng_ref[...], lnb_ref[...])


def _gather_combine_ln(dest_flat, ys, route_gates, x, gate, ln_g, ln_b, alpha, name):
    b, s, d = x.shape
    tm = _tile(s, 256)
    nt = s // tm
    row = lambda bi, i, dst: (bi, i, 0)
    const = lambda bi, i, dst: (0, 0)
    grid_spec = pltpu.PrefetchScalarGridSpec(
        num_scalar_prefetch=1,
        grid=(b, nt),
        in_specs=[pl.BlockSpec(memory_space=pl.ANY),
                  pl.BlockSpec((1, tm, LANES), row),
                  pl.BlockSpec((1, tm, d), row),
                  pl.BlockSpec((1, 1, d), lambda bi, i, dst: (bi, 0, 0)),
                  pl.BlockSpec((1, d), const),
                  pl.BlockSpec((1, d), const)],
        out_specs=pl.BlockSpec((1, tm, d), row),
        scratch_shapes=[pltpu.VMEM((2, MOE_TOPK, tm, d), F32), pltpu.SemaphoreType.DMA((2,))],
    )
    return pl.pallas_call(
        functools.partial(_gather_ln_kernel, alpha=alpha, tm=tm, nt=nt, n_steps=b * nt),
        out_shape=jax.ShapeDtypeStruct((b, s, d), F32),
        grid_spec=grid_spec,
        compiler_params=_cparams("arbitrary", "arbitrary"),
        name=name,
    )(dest_flat, ys, route_gates, x, gate, ln_g.reshape(1, d), ln_b.reshape(1, d))


def _gdn_kernel(q_ref, k_ref, v_ref, z_ref, bg_ref, cg_ref, hh_ref, ba_ref,
                cw_ref, alog_ref, dtb_ref, nw_ref, scw_ref,
                o_ref,
                ext_ref, qa_ref, ka_ref, va_ref, hist_ref, hist2_ref, gate_ref, state_ref,
                *, heads, ts, gw):
    L = GDN_CHUNK
    dk = GDN_HEAD_DIM

    @pl.when(pl.program_id(1) == 0)
    def _():
        hist_ref[...] = jnp.zeros_like(hist_ref)
        hist2_ref[...] = jnp.zeros_like(hist2_ref)
        state_ref[...] = jnp.zeros_like(state_ref)

    def causal_conv(x, hist, w, taps):
        ext_ref[0:SUBLANES, :] = hist
        ext_ref[SUBLANES:SUBLANES + ts, :] = x
        y = w[0:1] * ext_ref[SUBLANES - taps + 1:SUBLANES - taps + 1 + ts, :]
        for j in range(1, taps - 1):
            y = y + w[j:j + 1] * ext_ref[SUBLANES - taps + 1 + j:SUBLANES - taps + 1 + j + ts, :]
        return y + w[taps - 1:taps] * x

    for idx, (src, dst) in enumerate(((q_ref, qa_ref), (k_ref, ka_ref), (v_ref, va_ref))):
        x = src[0]
        y = causal_conv(x, hist_ref[idx], cw_ref[:, idx * gw:(idx + 1) * gw], GDN_CONV)
        hist_ref[idx] = x[ts - SUBLANES:ts, :]
        dst[...] = _silu(y)

    c = cg_ref[0] * hh_ref[0]
    yb = causal_conv(c, hist2_ref[...], scw_ref[...], SC_CONV)
    hist2_ref[...] = c[ts - SUBLANES:ts, :]
    o_ref[0, :, gw:2 * gw] = bg_ref[0] * yb

    ba = ba_ref[0]
    gate_ref[0] = _sigmoid(ba)
    gate_ref[1] = -jnp.exp(alog_ref[...]) * _softplus(ba + dtb_ref[...])

    ii = lax.broadcasted_iota(I32, (L, L), 0)
    jj = lax.broadcasted_iota(I32, (L, L), 1)
    causal = ii >= jj
    strict = ii > jj
    tri_incl = causal.astype(F32)
    eye = (ii == jj).astype(F32)
    nchunk = ts // L
    pairs = [(c, h) for c in range(nchunk) for h in range(heads)]

    def tiles(ref, lead=()):
        return jnp.stack([ref[lead + (slice(c * L, (c + 1) * L), slice(h * dk, (h + 1) * dk))]
                          for c, h in pairs], axis=0)

    def bmm(a, b):
        return jnp.einsum('bij,bjk->bik', a.astype(BF16), b.astype(BF16), preferred_element_type=F32)

    def bmm_nt(a, b):
        return jnp.einsum('bik,bjk->bij', a.astype(BF16), b.astype(BF16), preferred_element_type=F32)

    qh = tiles(qa_ref)
    kh = tiles(ka_ref)
    vh = tiles(va_ref)
    beta = jnp.stack([gate_ref[0, c * L:(c + 1) * L, h:h + 1] for c, h in pairs], axis=0)
    cums = [jnp.dot(tri_incl, gate_ref[1, c * L:(c + 1) * L, :], precision=HIGHEST,
                    preferred_element_type=F32) for c in range(nchunk)]
    cums_t = [cm.T for cm in cums]
    gc = jnp.stack([cums[c][:, heads + h:heads + h + 1] for c, h in pairs], axis=0)
    gr = jnp.stack([cums_t[c][heads + h:heads + h + 1, :] for c, h in pairs], axis=0)
    gl = gc[:, L - 1:L, :]
    decay = jnp.where(causal, jnp.exp(jnp.where(causal, gc - gr, 0.0)), 0.0)
    egc = jnp.exp(gc)
    qn = qh * lax.rsqrt(jnp.sum(qh * qh, -1, keepdims=True) + 1e-6)
    kn = kh * lax.rsqrt(jnp.sum(kh * kh, -1, keepdims=True) + 1e-6)
    qc = qn * (dk ** -0.5)
    kb = kn * beta
    m = jnp.where(strict, bmm_nt(kb, kn) * decay, 0.0)
    t_inv = eye - m
    mp = m
    for _ in range(5):
        mp = bmm(mp, mp)
        t_inv = t_inv + bmm(t_inv, mp)
    sol = bmm(t_inv, jnp.concatenate([vh * beta, kb * egc], axis=2))
    u = sol[:, :, 0:dk]
    w = sol[:, :, dk:2 * dk]
    qk = jnp.where(causal, bmm_nt(qc, kn) * decay, 0.0)
    qg = qc * egc
    k_dec = kn * jnp.exp(gl - gc)
    egl = jnp.exp(gl)

    state = [state_ref[h] for h in range(heads)]
    outs = []
    for b, (c, h) in enumerate(pairs):
        st = state[h]
        v_new = u[b] - _dot(w[b], st)
        outs.append(_dot(qg[b], st) + _dot(qk[b], v_new))
        state[h] = st * egl[b] + _dot(k_dec[b].T, v_new)
    for h in range(heads):
        state_ref[h] = state[h]

    o = jnp.stack(outs, axis=0)
    on = o * lax.rsqrt(jnp.mean(o * o, -1, keepdims=True) + 1e-6) * nw_ref[...]
    y = on * _silu(tiles(z_ref, (0,)))
    for b, (c, h) in enumerate(pairs):
        o_ref[0, c * L:(c + 1) * L, h * dk:(h + 1) * dk] = y[b]


def _gdn_shortconv(proj, conv_w, a_log, dt_bias, norm_w, sc_w, heads):
    b, s, _ = proj.shape
    gw = heads * GDN_HEAD_DIM
    ts = _tile(s, 256)
    alog_p = jnp.zeros((1, LANES), F32).at[0, heads:2 * heads].set(a_log)
    dtb_p = jnp.zeros((1, LANES), F32).at[0, heads:2 * heads].set(dt_bias)
    col = lambda cb: (lambda bi, i: (bi, i, cb))
    const = lambda bi, i: (0, 0)
    kern = functools.partial(_gdn_kernel, heads=heads, ts=ts, gw=gw)
    return pl.pallas_call(
        kern,
        out_shape=jax.ShapeDtypeStruct((b, s, 2 * gw), F32),
        grid=(b, s // ts),
        in_specs=[pl.BlockSpec((1, ts, gw), col(cb)) for cb in range(7)]
        + [pl.BlockSpec((1, ts, LANES), col(7 * gw // LANES)),
           pl.BlockSpec((GDN_CONV, 3 * gw), const),
           pl.BlockSpec((1, LANES), const),
           pl.BlockSpec((1, LANES), const),
           pl.BlockSpec((1, GDN_HEAD_DIM), const),
           pl.BlockSpec((SC_CONV, gw), const)],
        out_specs=pl.BlockSpec((1, ts, 2 * gw), lambda bi, i: (bi, i, 0)),
        scratch_shapes=[pltpu.VMEM((ts + SUBLANES, gw), F32),
                        pltpu.VMEM((ts, gw), F32),
                        pltpu.VMEM((ts, gw), F32),
                        pltpu.VMEM((ts, gw), F32),
                        pltpu.VMEM((3, SUBLANES, gw), F32),
                        pltpu.VMEM((SUBLANES, gw), F32),
                        pltpu.VMEM((2, ts, LANES), F32),
                        pltpu.VMEM((heads, GDN_HEAD_DIM, GDN_HEAD_DIM), F32)],
        compiler_params=_cparams("parallel", "arbitrary"),
        name="gdn_shortconv",
    )(proj, proj, proj, proj, proj, proj, proj, proj,
      conv_w, alog_p, dtb_p, norm_w.reshape(1, GDN_HEAD_DIM), sc_w)


def _rope_kernel(pos_ref, inv_ref, qpe_ref, kpe_ref, qo_ref, ko_ref, *, heads):
    ang = pos_ref[0].astype(F32) * inv_ref[...]
    lane = lax.broadcasted_iota(I32, ang.shape, 1)
    half = MLA_ROPE // 2
    cos = jnp.where(lane < MLA_ROPE, jnp.cos(ang), 0.0)
    sin = jnp.sin(ang)
    sgn = jnp.where(lane < half, -sin, jnp.where(lane < MLA_ROPE, sin, 0.0))

    def rot(x):
        swapped = jnp.where(lane < half, pltpu.roll(x, LANES - half, 1), pltpu.roll(x, half, 1))
        return x * cos + swapped * sgn

    for h in range(heads):
        cols = slice(h * LANES, (h + 1) * LANES)
        qo_ref[0, :, cols] = rot(qpe_ref[0, :, cols])
    ko_ref[0] = rot(kpe_ref[0])


def _rope(positions, qup, heads, cproj, kpe_block):
    b, s = positions.shape
    ts = _tile(s, 512)
    half = MLA_ROPE // 2
    inv = ROPE_THETA ** (-jnp.arange(half, dtype=F32) * (2.0 / MLA_ROPE))
    inv_tab = jnp.zeros((1, LANES), F32).at[0, 0:half].set(inv).at[0, half:MLA_ROPE].set(inv)
    w = heads * LANES
    return pl.pallas_call(
        functools.partial(_rope_kernel, heads=heads),
        out_shape=(jax.ShapeDtypeStruct((b, s, w), F32), jax.ShapeDtypeStruct((b, s, LANES), F32)),
        grid=(b, s // ts),
        in_specs=[pl.BlockSpec((1, ts, 1), lambda bi, i: (bi, i, 0)),
                  pl.BlockSpec((1, LANES), lambda bi, i: (0, 0)),
                  pl.BlockSpec((1, ts, w), lambda bi, i: (bi, i, 1)),
                  pl.BlockSpec((1, ts, LANES), lambda bi, i: (bi, i, kpe_block))],
        out_specs=(pl.BlockSpec((1, ts, w), lambda bi, i: (bi, i, 0)),
                   pl.BlockSpec((1, ts, LANES), lambda bi, i: (bi, i, 0))),
        compiler_params=_cparams("parallel", "parallel"),
        name="mla_rope",
    )(positions.reshape(b, s, 1), inv_tab, qup, cproj)


def _attn_kernel(qn_ref, qp_ref, kn_ref, kp_ref, v_ref, o_ref, kcat_ref, vb_ref, s_ref, *, tq, nq, scale):
    qi = pl.program_id(2)

    @pl.when(qi == 0)
    def _():
        kcat_ref[:, 0:MLA_NOPE] = kn_ref[0].astype(BF16)
        kcat_ref[:, MLA_NOPE:MLA_NOPE + LANES] = kp_ref[0].astype(BF16)
        vb_ref[...] = v_ref[0].astype(BF16)

    qcat = jnp.concatenate([qn_ref[0], qp_ref[0]], axis=1).astype(BF16)
    row = lax.broadcasted_iota(I32, (tq, tq), 0)
    col = lax.broadcasted_iota(I32, (tq, tq), 1)

    for qv in range(nq):
        @pl.when(qi == qv)
        def _(qv=qv):
            m = None
            for kb in range(qv + 1):
                ks = slice(kb * tq, (kb + 1) * tq)
                s = lax.dot_general(qcat, kcat_ref[ks, :], (((1,), (1,)), ((), ())),
                                    preferred_element_type=F32) * scale
                if kb == qv:
                    s = jnp.where(row >= col, s, -jnp.inf)
                s_ref[:, ks] = s
                mb = jnp.max(s, -1, keepdims=True)
                m = mb if m is None else jnp.maximum(m, mb)
            l = jnp.zeros((tq, 1), F32)
            acc = jnp.zeros((tq, MLA_V), F32)
            for kb in range(qv + 1):
                ks = slice(kb * tq, (kb + 1) * tq)
                p = jnp.exp(s_ref[:, ks] - m)
                l = l + jnp.sum(p, -1, keepdims=True)
                acc = acc + jnp.dot(p.astype(BF16), vb_ref[ks, :], preferred_element_type=F32)
            o_ref[0] = acc / l


def _attention(qup, qpe, kvup, kpe, heads):
    b, s, _ = qup.shape
    tq = _tile(s, 512)
    nq = s // tq
    scale = (MLA_NOPE + MLA_ROPE) ** -0.5
    return pl.pallas_call(
        functools.partial(_attn_kernel, tq=tq, nq=nq, scale=scale),
        out_shape=jax.ShapeDtypeStruct((b, s, heads * MLA_V), F32),
        grid=(b, heads, nq),
        in_specs=[pl.BlockSpec((1, tq, MLA_NOPE), lambda bi, h, i: (bi, i, h)),
                  pl.BlockSpec((1, tq, LANES), lambda bi, h, i: (bi, i, h)),
                  pl.BlockSpec((1, s, MLA_NOPE), lambda bi, h, i: (bi, 0, h)),
                  pl.BlockSpec((1, s, LANES), lambda bi, h, i: (bi, 0, 0)),
                  pl.BlockSpec((1, s, MLA_V), lambda bi, h, i: (bi, 0, heads + h))],
        out_specs=pl.BlockSpec((1, tq, MLA_V), lambda bi, h, i: (bi, i, h)),
        scratch_shapes=[pltpu.VMEM((s, MLA_NOPE + LANES), BF16), pltpu.VMEM((s, MLA_V), BF16),
                        pltpu.VMEM((tq, s), F32)],
        compiler_params=_cparams("parallel", "parallel", "arbitrary"),
        name="mla_attention",
    )(qup, qpe, kvup, kpe, kvup)


def _hyb_in_weight(w_in, heads):
    d = w_in.shape[0]
    gw = heads * GDN_HEAD_DIM
    main = jnp.concatenate([w_in[:, :4 * gw], w_in[:, 4 * gw + 2 * heads:]], axis=1)
    gates = w_in[:, 4 * gw:4 * gw + 2 * heads]
    n = main.shape[1] + LANES
    n_pad = -(-n // 768) * 768
    tail = jnp.zeros((d, n_pad - main.shape[1]), w_in.dtype).at[:, :2 * heads].set(gates)
    return jnp.concatenate([main, tail], axis=1).astype(BF16)


def _mla_in_weight(w_in):
    d = w_in.shape[0]
    pad = jnp.zeros((d, LANES - MLA_ROPE), w_in.dtype)
    return jnp.concatenate([w_in, pad], axis=1).astype(BF16)


def _mla_uq_weight(w_uq, heads):
    r = w_uq.shape[0]
    w = w_uq.reshape(r, heads, MLA_NOPE + MLA_ROPE)
    nope = w[:, :, :MLA_NOPE].reshape(r, heads * MLA_NOPE)
    pe = jnp.concatenate([w[:, :, MLA_NOPE:], jnp.zeros((r, heads, LANES - MLA_ROPE), w.dtype)], axis=2)
    return jnp.concatenate([nope, pe.reshape(r, heads * LANES)], axis=1).astype(BF16)


def _mla_ukv_weight(w_ukv, heads):
    r = w_ukv.shape[0]
    w = w_ukv.reshape(r, heads, MLA_NOPE + MLA_V)
    return jnp.concatenate([w[:, :, :MLA_NOPE].reshape(r, heads * MLA_NOPE),
                            w[:, :, MLA_NOPE:].reshape(r, heads * MLA_V)], axis=1).astype(BF16)


def _router_weight(wr_g, br_g, wr_e, br_e):
    d = wr_g.shape[0]
    n = wr_g.shape[1] + wr_e.shape[1]
    w = jnp.zeros((d, LANES), F32).at[:, :n].set(jnp.concatenate([wr_g, wr_e], axis=1))
    bias = jnp.zeros((1, LANES), F32).at[0, :n].set(jnp.concatenate([br_g, br_e]))
    w_hi = w.astype(BF16)
    w_lo = (w - w_hi.astype(F32)).astype(BF16)
    return jnp.stack([w_hi, w_lo]), bias


def kernel(x, c, positions, ada_w, ada_b, ln_g, ln_b, hyb_w_in, gdn_conv_w, gdn_a_log, gdn_dt_bias,
           gdn_norm_w, sc_conv_w, hyb_w_out, mla_w_in, mla_q_norm, mla_kv_norm, mla_w_uq, mla_w_ukv,
           mla_w_out, moe_router_g, moe_bias_g, moe_router_e, moe_bias_e, moe_w_gate, moe_w_up,
           moe_w_down):
    b, s, d = x.shape
    depth = ada_w.shape[0]
    alpha = (2.0 * depth) ** 0.25
    gdn_heads = d // (2 * GDN_HEAD_DIM)
    mla_heads = mla_w_out.shape[1] // MLA_V
    n_groups = moe_router_g.shape[2]
    n_exp = moe_router_e.shape[2]
    per_group = n_exp // n_groups

    mod = _modulation(c, ada_w, ada_b)
    for layer in range(depth):
        sh1, sc1, g1, sh2, sc2, g2 = [mod[layer, :, j * d:(j + 1) * d].reshape(b, 1, d) for j in range(6)]
        i = layer // 2
        wr, br = _router_weight(moe_router_g[layer], moe_bias_g[layer], moe_router_e[layer], moe_bias_e[layer])
        if layer % 2 == 0:
            proj = _mod_matmul(x, sc1, sh1, _hyb_in_weight(hyb_w_in[i], gdn_heads), "hyb_in_proj")
            y = _gdn_shortconv(proj, gdn_conv_w[i], gdn_a_log[i], gdn_dt_bias[i], gdn_norm_w[i],
                               sc_conv_w[i], gdn_heads)
            w_out = hyb_w_out[i]
        else:
            cproj = _mod_matmul(x, sc1, sh1, _mla_in_weight(mla_w_in[i]), "mla_in_proj")
            qup = _rms_matmul(cproj, 0, MLA_Q_RANK, mla_q_norm[i], _mla_uq_weight(mla_w_uq[i], mla_heads),
                              "mla_q_up")
            kvup = _rms_matmul(cproj, 1, MLA_KV_RANK, mla_kv_norm[i], _mla_ukv_weight(mla_w_ukv[i], mla_heads),
                               "mla_kv_up")
            qpe, kpe = _rope(positions, qup, mla_heads, cproj, (MLA_Q_RANK + MLA_KV_RANK) // LANES)
            y = _attention(qup, qpe, kvup, kpe, mla_heads)
            w_out = mla_w_out[i]
        x, hin2, ids, gates = _out_ln_route(y, w_out.astype(BF16), x, g1, ln_g[layer, 0], ln_b[layer, 0],
                                            sc2, sh2, wr, br, alpha, n_groups, per_group,
                                            "mixer_out_ln_route")
        t = b * s
        dest, meta = _dispatch(ids.reshape(t, LANES))
        dest_flat = dest[:, :MOE_TOPK].reshape(-1)
        eblk = meta[0, :n_exp] // MOE_BLOCK
        enb = meta[1, :n_exp]
        n_rows = t * MOE_TOPK + n_exp * MOE_BLOCK
        xs = _scatter_rows(dest_flat, hin2.reshape(t, d), n_rows)
        ys = _moe_ffn(eblk, enb, xs, moe_w_gate, moe_w_up, moe_w_down, layer)
        x = _gather_combine_ln(dest_flat, ys, gates, x, g2, ln_g[layer, 1], ln_b[layer, 1], alpha,
                               "moe_gather_combine_ln")
    return x
```

```python
import functools

import jax
import jax.numpy as jnp
from jax import lax
from jax.experimental import pallas as pl
from jax.experimental.pallas import tpu as pltpu

F32, BF16, I32 = jnp.float32, jnp.bfloat16, jnp.int32
HIGHEST = lax.Precision.HIGHEST

LANES = 128
SUBLANES = 8
VMEM_LIMIT_BYTES = 56 * 1024 * 1024

GDN_HEAD_DIM = 128
GDN_CONV = 4
GDN_CHUNK = 64
SC_CONV = 3
MLA_NOPE = 128
MLA_ROPE = 64
MLA_V = 128
MLA_Q_RANK = 512
MLA_KV_RANK = 512
ROPE_THETA = 10000.0
MOE_TOPK = 2
MOE_BLOCK = 128
DMA_UNROLL = 8


def _tile(n, pref):
    if n <= pref:
        return n
    for t in range(pref, 0, -LANES):
        if n % t == 0:
            return t
    return n


def _cparams(*sem):
    return pltpu.CompilerParams(dimension_semantics=sem, vmem_limit_bytes=VMEM_LIMIT_BYTES)


def _sigmoid(x):
    return 1.0 / (1.0 + jnp.exp(-x))


def _silu(x):
    return x * _sigmoid(x)


def _softplus(x):
    return jnp.maximum(x, 0.0) + jnp.log(1.0 + jnp.exp(-jnp.abs(x)))


def _dot(a, b):
    return jnp.dot(a.astype(BF16), b.astype(BF16), preferred_element_type=F32)


def _dot_nt(a, b):
    return lax.dot_general(a.astype(BF16), b.astype(BF16), (((1,), (1,)), ((), ())),
                           preferred_element_type=F32)


def _mod_kernel(c_ref, w_ref, b_ref, o_ref):
    c = c_ref[...]
    o_ref[0] = _dot(_silu(c), w_ref[0]) + b_ref[0]


def _modulation(c, ada_w, ada_b):
    depth, d, n = ada_w.shape
    b = c.shape[0]
    rows = -(-b // SUBLANES) * SUBLANES
    cp = jnp.zeros((rows, d), F32).at[:b].set(c)
    tn = _tile(n, 1024)
    out = pl.pallas_call(
        _mod_kernel,
        out_shape=jax.ShapeDtypeStruct((depth, rows, n), F32),
        grid=(depth, n // tn),
        in_specs=[pl.BlockSpec((rows, d), lambda l, j: (0, 0)),
                  pl.BlockSpec((1, d, tn), lambda l, j: (l, 0, j)),
                  pl.BlockSpec((1, 1, tn), lambda l, j: (l, 0, j))],
        out_specs=pl.BlockSpec((1, rows, tn), lambda l, j: (l, 0, j)),
        compiler_params=_cparams("parallel", "parallel"),
        name="adaln_mod",
    )(cp, ada_w, ada_b.reshape(depth, 1, n))
    return out[:, :b]


def _modmm_kernel(x_ref, sc_ref, sh_ref, w_ref, o_ref, xb_ref):
    @pl.when(pl.program_id(2) == 0)
    def _():
        xb_ref[...] = (x_ref[0] * (1.0 + sc_ref[0]) + sh_ref[0]).astype(BF16)

    o_ref[0] = jnp.dot(xb_ref[...], w_ref[...], preferred_element_type=F32)


def _mod_matmul(x, sc, sh, w, name):
    b, s, d = x.shape
    n = w.shape[1]
    tm = _tile(s, 1024)
    tn = _tile(n, 768)
    return pl.pallas_call(
        _modmm_kernel,
        out_shape=jax.ShapeDtypeStruct((b, s, n), F32),
        grid=(b, s // tm, n // tn),
        in_specs=[pl.BlockSpec((1, tm, d), lambda bi, i, j: (bi, i, 0)),
                  pl.BlockSpec((1, 1, d), lambda bi, i, j: (bi, 0, 0)),
                  pl.BlockSpec((1, 1, d), lambda bi, i, j: (bi, 0, 0)),
                  pl.BlockSpec((d, tn), lambda bi, i, j: (0, j))],
        out_specs=pl.BlockSpec((1, tm, tn), lambda bi, i, j: (bi, i, j)),
        scratch_shapes=[pltpu.VMEM((tm, d), BF16)],
        compiler_params=_cparams("parallel", "parallel", "arbitrary"),
        name=name,
    )(x, sc, sh, w)


def _rmsmm_kernel(x_ref, nw_ref, w_ref, o_ref, xb_ref):
    @pl.when(pl.program_id(2) == 0)
    def _():
        x = x_ref[0]
        xb_ref[...] = (x * lax.rsqrt(jnp.mean(x * x, -1, keepdims=True) + 1e-6) * nw_ref[...]).astype(BF16)

    o_ref[0] = jnp.dot(xb_ref[...], w_ref[...], preferred_element_type=F32)


def _rms_matmul(x, col_block, k, norm_w, w, name):
    b, s, _ = x.shape
    n = w.shape[1]
    tm = _tile(s, 1024)
    tn = _tile(n, 1024)
    return pl.pallas_call(
        _rmsmm_kernel,
        out_shape=jax.ShapeDtypeStruct((b, s, n), F32),
        grid=(b, s // tm, n // tn),
        in_specs=[pl.BlockSpec((1, tm, k), lambda bi, i, j: (bi, i, col_block)),
                  pl.BlockSpec((1, k), lambda bi, i, j: (0, 0)),
                  pl.BlockSpec((k, tn), lambda bi, i, j: (0, j))],
        out_specs=pl.BlockSpec((1, tm, tn), lambda bi, i, j: (bi, i, j)),
        scratch_shapes=[pltpu.VMEM((tm, k), BF16)],
        compiler_params=_cparams("parallel", "parallel", "arbitrary"),
        name=name,
    )(x, norm_w.reshape(1, k), w)


def _layer_norm_rows(r, g, b):
    mu = jnp.mean(r, -1, keepdims=True)
    var = jnp.mean(jnp.square(r - mu), -1, keepdims=True)
    return (r - mu) * lax.rsqrt(var + 1e-5) * g + b


def _outln_kernel(y_ref, w_ref, xres_ref, g_ref, lng_ref, lnb_ref, sc_ref, sh_ref, wr_ref, br_ref,
                  xo_ref, hin_ref, ids_ref, gates_ref, *, alpha, n_groups, per_group):
    y = jnp.dot(y_ref[0].astype(BF16), w_ref[...], preferred_element_type=F32)
    r = alpha * xres_ref[0] + (1.0 + g_ref[0]) * y
    xn = _layer_norm_rows(r, lng_ref[...], lnb_ref[...])
    xo_ref[0] = xn
    hin = xn * (1.0 + sc_ref[0]) + sh_ref[0]
    hin_ref[0] = hin
    h_hi = hin.astype(BF16)
    h_lo = (hin - h_hi.astype(F32)).astype(BF16)
    w_hi = wr_ref[0]
    logits = (jnp.dot(h_hi, w_hi, preferred_element_type=F32)
              + jnp.dot(h_hi, wr_ref[1], preferred_element_type=F32)
              + jnp.dot(h_lo, w_hi, preferred_element_type=F32)) + br_ref[...]
    lane = lax.broadcasted_iota(I32, logits.shape, 1)
    big = jnp.int32(4 * LANES)
    neg = jnp.float32(-jnp.inf)
    n_exp = n_groups * per_group
    gmask = lane < n_groups
    lg = jnp.where(gmask, logits, neg)
    mg = jnp.max(lg, -1, keepdims=True)
    grp = jnp.min(jnp.where(gmask & (lg == mg), lane, big), -1, keepdims=True)
    pg_sel = 1.0 / jnp.sum(jnp.where(gmask, jnp.exp(lg - mg), 0.0), -1, keepdims=True)
    lo = n_groups + grp * per_group
    emask = (lane >= lo) & (lane < lo + per_group) & (lane < n_groups + n_exp)
    le = jnp.where(emask, logits, neg)
    me = jnp.max(le, -1, keepdims=True)
    ee = jnp.where(emask, jnp.exp(le - me), 0.0)
    p = ee / jnp.sum(ee, -1, keepdims=True)
    pm = jnp.where(emask, p, -1.0)
    p1 = jnp.max(pm, -1, keepdims=True)
    i1 = jnp.min(jnp.where(emask & (pm == p1), lane, big), -1, keepdims=True)
    pm2 = jnp.where(lane == i1, -1.0, pm)
    p2 = jnp.max(pm2, -1, keepdims=True)
    i2 = jnp.min(jnp.where(emask & (lane != i1) & (pm2 == p2), lane, big), -1, keepdims=True)
    den = p1 + p2
    g1 = pg_sel * p1 / den
    g2 = pg_sel * p2 / den
    ids_ref[0] = jnp.where(lane == 0, i1 - n_groups, jnp.where(lane == 1, i2 - n_groups, 0))
    gates_ref[0] = jnp.where(lane == 0, g1, jnp.where(lane == 1, g2, 0.0))


def _out_ln_route(y, w, xres, gate, ln_g, ln_b, sc, sh, wr, br, alpha, n_groups, per_group, name):
    b, s, k = y.shape
    d = w.shape[1]
    tm = _tile(s, 256)
    kern = functools.partial(_outln_kernel, alpha=alpha, n_groups=n_groups, per_group=per_group)
    row = lambda bi, i: (bi, i, 0)
    per_b = lambda bi, i: (bi, 0, 0)
    const = lambda bi, i: (0, 0)
    return pl.pallas_call(
        kern,
        out_shape=(jax.ShapeDtypeStruct((b, s, d), F32), jax.ShapeDtypeStruct((b, s, d), F32),
                   jax.ShapeDtypeStruct((b, s, LANES), I32), jax.ShapeDtypeStruct((b, s, LANES), F32)),
        grid=(b, s // tm),
        in_specs=[pl.BlockSpec((1, tm, k), row),
                  pl.BlockSpec((k, d), const),
                  pl.BlockSpec((1, tm, d), row),
                  pl.BlockSpec((1, 1, d), per_b),
                  pl.BlockSpec((1, d), const),
                  pl.BlockSpec((1, d), const),
                  pl.BlockSpec((1, 1, d), per_b),
                  pl.BlockSpec((1, 1, d), per_b),
                  pl.BlockSpec((2, d, LANES), lambda bi, i: (0, 0, 0)),
                  pl.BlockSpec((1, LANES), const)],
        out_specs=(pl.BlockSpec((1, tm, d), row), pl.BlockSpec((1, tm, d), row),
                   pl.BlockSpec((1, tm, LANES), row), pl.BlockSpec((1, tm, LANES), row)),
        compiler_params=_cparams("parallel", "parallel"),
        name=name,
    )(y, w, xres, gate, ln_g.reshape(1, d), ln_b.reshape(1, d), sc, sh, wr, br)


def _combine_kernel(ya_ref, yb_ref, rg_ref, x_ref, g_ref, lng_ref, lnb_ref, o_ref, *, alpha):
    rg = rg_ref[0]
    y = ya_ref[0] * rg[:, 0:1] + yb_ref[0] * rg[:, 1:2]
    r = alpha * x_ref[0] + (1.0 + g_ref[0]) * y
    o_ref[0] = _layer_norm_rows(r, lng_ref[...], lnb_ref[...])


def _combine_ln(y2, route_gates, x, gate, ln_g, ln_b, alpha, name):
    b, s, d = x.shape
    tm = _tile(s, 512)
    row = lambda bi, i: (bi, i, 0)
    nt = s // tm
    return pl.pallas_call(
        functools.partial(_combine_kernel, alpha=alpha),
        out_shape=jax.ShapeDtypeStruct((b, s, d), F32),
        grid=(b, nt),
        in_specs=[pl.BlockSpec((1, tm, d), lambda bi, i: (0, bi * nt + i, 0)),
                  pl.BlockSpec((1, tm, d), lambda bi, i: (1, bi * nt + i, 0)),
                  pl.BlockSpec((1, tm, LANES), row),
                  pl.BlockSpec((1, tm, d), row),
                  pl.BlockSpec((1, 1, d), lambda bi, i: (bi, 0, 0)),
                  pl.BlockSpec((1, d), lambda bi, i: (0, 0)),
                  pl.BlockSpec((1, d), lambda bi, i: (0, 0))],
        out_specs=pl.BlockSpec((1, tm, d), row),
        compiler_params=_cparams("parallel", "parallel"),
        name=name,
    )(y2, y2, route_gates, x, gate, ln_g.reshape(1, d), ln_b.reshape(1, d))


def _moe_kernel(bexp_ref, bcnt_ref, rsrc_ref, rdst_ref,
                hin_hbm, wg_ref, wu_ref, wd_ref,
                y2_hbm,
                xbuf, ybuf, wgb, wub, wdb, gsem, ssem, *, blk, nb, tp):
    i = pl.program_id(0)
    slot = i % 2

    def start_gather(block, sl):
        base = block * blk

        def group(gi, c):
            for u in range(DMA_UNROLL):
                r = gi * DMA_UNROLL + u
                pltpu.make_async_copy(hin_hbm.at[pl.ds(rsrc_ref[base + r], 1), :],
                                      xbuf.at[sl, pl.ds(r, 1), :], gsem.at[sl]).start()
            return c

        lax.fori_loop(0, blk // DMA_UNROLL, group, 0)

    def wait_gather(sl):
        pltpu.make_async_copy(hin_hbm.at[pl.ds(0, blk), :], xbuf.at[sl], gsem.at[sl]).wait()

    def start_scatter(block, sl):
        base = block * blk

        def group(gi, c):
            for u in range(DMA_UNROLL):
                r = gi * DMA_UNROLL + u
                pltpu.make_async_copy(ybuf.at[sl, pl.ds(r, 1), :],
                                      y2_hbm.at[pl.ds(rdst_ref[base + r], 1), :], ssem.at[sl]).start()
            return c

        lax.fori_loop(0, blk // DMA_UNROLL, group, 0)

    def wait_scatter(sl):
        pltpu.make_async_copy(ybuf.at[sl], y2_hbm.at[pl.ds(0, blk), :], ssem.at[sl]).wait()

    used = bcnt_ref[i] > 0

    @pl.when(i == 0)
    def _():
        ybuf[...] = jnp.zeros_like(ybuf)
        for k in range(MOE_TOPK):
            for sl in range(2):
                cp = pltpu.make_async_copy(ybuf.at[sl], y2_hbm.at[pl.ds(k * tp + tp - (2 - sl) * blk, blk), :],
                                           ssem.at[sl])
                cp.start()
                cp.wait()

    @pl.when(jnp.logical_and(i == 0, used))
    def _():
        start_gather(0, 0)

    nxt = jnp.minimum(i + 1, nb - 1)

    @pl.when(jnp.logical_and(i + 1 < nb, bcnt_ref[nxt] > 0))
    def _():
        start_gather(nxt, 1 - slot)

    changed = jnp.logical_or(i == 0, bexp_ref[i] != bexp_ref[jnp.maximum(i - 1, 0)])

    @pl.when(jnp.logical_and(changed, used))
    def _():
        wgb[...] = wg_ref[0, 0].astype(BF16)
        wub[...] = wu_ref[0, 0].astype(BF16)
        wdb[...] = wd_ref[0, 0].astype(BF16)

    @pl.when(jnp.logical_and(i >= 2, bcnt_ref[jnp.maximum(i - 2, 0)] > 0))
    def _():
        wait_scatter(slot)

    @pl.when(used)
    def _():
        wait_gather(slot)
        xb = xbuf[slot].astype(BF16)
        hg = jnp.dot(xb, wgb[...], preferred_element_type=F32)
        hu = jnp.dot(xb, wub[...], preferred_element_type=F32)
        hid = (_silu(hg) * hu).astype(BF16)
        ybuf[slot] = jnp.dot(hid, wdb[...], preferred_element_type=F32)
        start_scatter(i, slot)

    @pl.when(i == nb - 1)
    def _():
        if nb >= 2:
            @pl.when(bcnt_ref[jnp.maximum(i - 1, 0)] > 0)
            def _():
                wait_scatter(1 - slot)

        @pl.when(used)
        def _():
            wait_scatter(slot)


def _moe_experts(hin2, w_gate, w_up, w_down, layer, blk_exp, blk_cnt, row_src, row_dst, name):
    t, d = hin2.shape
    ff = w_gate.shape[3]
    nb = blk_exp.shape[0]
    blk = MOE_BLOCK
    tp = t + 2 * blk
    wmap = lambda i, be, bc, rs, rd: (layer, be[i], 0, 0)
    grid_spec = pltpu.PrefetchScalarGridSpec(
        num_scalar_prefetch=4,
        grid=(nb,),
        in_specs=[pl.BlockSpec(memory_space=pl.ANY),
                  pl.BlockSpec((1, 1, d, ff), wmap),
                  pl.BlockSpec((1, 1, d, ff), wmap),
                  pl.BlockSpec((1, 1, ff, d), wmap)],
        out_specs=pl.BlockSpec(memory_space=pl.ANY),
        scratch_shapes=[pltpu.VMEM((2, blk, d), F32),
                        pltpu.VMEM((2, blk, d), F32),
                        pltpu.VMEM((d, ff), BF16),
                        pltpu.VMEM((d, ff), BF16),
                        pltpu.VMEM((ff, d), BF16),
                        pltpu.SemaphoreType.DMA((2,)),
                        pltpu.SemaphoreType.DMA((2,))],
    )
    y2 = pl.pallas_call(
        functools.partial(_moe_kernel, blk=blk, nb=nb, tp=tp),
        out_shape=jax.ShapeDtypeStruct((MOE_TOPK * tp, d), F32),
        grid_spec=grid_spec,
        compiler_params=_cparams("arbitrary"),
        name=name,
    )(blk_exp, blk_cnt, row_src, row_dst, hin2, w_gate, w_up, w_down)
    return y2.reshape(MOE_TOPK, tp, d)


def _dispatch_tables(ids, n_exp):
    t = ids.shape[0]
    n_assign = t * MOE_TOPK
    blk = MOE_BLOCK
    tp = t + 2 * blk
    e_flat = ids.reshape(-1)
    order = jnp.argsort(e_flat).astype(I32)
    e_sorted = e_flat[order]
    counts = jnp.bincount(e_flat, length=n_exp).astype(I32)
    padded = (counts + blk - 1) // blk * blk
    pad_end = jnp.cumsum(padded)
    pad_start = pad_end - padded
    start = jnp.cumsum(counts) - counts
    dest = pad_start[e_sorted] + jnp.arange(n_assign, dtype=I32) - start[e_sorted]
    n_rows = n_assign + n_exp * blk
    nb = n_rows // blk
    rows = jnp.arange(n_rows, dtype=I32)
    spare = t + ((rows // blk) % 2) * blk + rows % blk
    row_src = jnp.zeros((n_rows,), I32).at[dest].set(order // MOE_TOPK)
    row_dst = spare.at[dest].set((order % MOE_TOPK) * tp + order // MOE_TOPK)
    blk_first = jnp.arange(nb, dtype=I32) * blk
    blk_exp = jnp.minimum(jnp.searchsorted(pad_end, blk_first, side='right'), n_exp - 1).astype(I32)
    in_use = blk_first < pad_end[-1]
    blk_cnt = jnp.where(in_use, jnp.clip(counts[blk_exp] - (blk_first - pad_start[blk_exp]), 0, blk), 0)
    return blk_exp, blk_cnt.astype(I32), row_src, row_dst


def _dispatch_kernel(ids_ref, dest_ref, meta_ref, rank_ref, *, t, tile, blk_shift):
    lane = lax.broadcasted_iota(I32, (tile, LANES), 1)
    ri = lax.broadcasted_iota(I32, (tile, tile), 0)
    ci = lax.broadcasted_iota(I32, (tile, tile), 1)
    before = (ri > ci).astype(BF16)

    def hits(rows):
        ids = ids_ref[rows, :]
        return lane == ids[:, 0:1], lane == ids[:, 1:2]

    def count(ti, run):
        rows = pl.ds(pl.multiple_of(ti * tile, tile), tile)
        h1, h2 = hits(rows)
        onehot = jnp.logical_or(h1, h2).astype(BF16)
        prefix = jnp.dot(before, onehot, preferred_element_type=F32) + run
        r1 = jnp.sum(jnp.where(h1, prefix, 0.0), -1, keepdims=True)
        r2 = jnp.sum(jnp.where(h2, prefix, 0.0), -1, keepdims=True)
        rank_ref[rows, :] = jnp.where(lane == 0, r1, jnp.where(lane == 1, r2, 0.0))
        return run + jnp.sum(onehot.astype(F32), 0, keepdims=True)

    counts = lax.fori_loop(0, t // tile, count, jnp.zeros((1, LANES), F32))
    cnt = jnp.broadcast_to(counts, (SUBLANES, LANES)).astype(I32)
    nblk = (cnt + ((1 << blk_shift) - 1)) >> blk_shift
    padded = nblk << blk_shift
    lane8 = lax.broadcasted_iota(I32, (SUBLANES, LANES), 1)
    incl = padded
    step = 1
    while step < LANES:
        incl = incl + jnp.where(lane8 >= step, pltpu.roll(incl, step, 1), 0)
        step *= 2
    pad_start = incl - padded
    row8 = lax.broadcasted_iota(I32, (SUBLANES, LANES), 0)
    meta_ref[...] = jnp.where(row8 == 0, pad_start, jnp.where(row8 == 1, nblk, cnt))
    start_f = pad_start[0:1, :].astype(F32)

    def place(ti, c):
        rows = pl.ds(pl.multiple_of(ti * tile, tile), tile)
        h1, h2 = hits(rows)
        rk = rank_ref[rows, :]
        d1 = jnp.sum(jnp.where(h1, start_f, 0.0), -1, keepdims=True) + rk[:, 0:1]
        d2 = jnp.sum(jnp.where(h2, start_f, 0.0), -1, keepdims=True) + rk[:, 1:2]
        dest_ref[rows, :] = jnp.where(lane == 0, d1, jnp.where(lane == 1, d2, 0.0)).astype(I32)
        return c

    lax.fori_loop(0, t // tile, place, 0)


def _dispatch(ids):
    t = ids.shape[0]
    tile = _tile(t, 256)
    blk_shift = MOE_BLOCK.bit_length() - 1
    assert (1 << blk_shift) == MOE_BLOCK
    return pl.pallas_call(
        functools.partial(_dispatch_kernel, t=t, tile=tile, blk_shift=blk_shift),
        out_shape=(jax.ShapeDtypeStruct((t, LANES), I32), jax.ShapeDtypeStruct((SUBLANES, LANES), I32)),
        grid=(1,),
        in_specs=[pl.BlockSpec((t, LANES), lambda i: (0, 0))],
        out_specs=(pl.BlockSpec((t, LANES), lambda i: (0, 0)), pl.BlockSpec((SUBLANES, LANES), lambda i: (0, 0))),
        scratch_shapes=[pltpu.VMEM((t, LANES), F32)],
        compiler_params=_cparams("arbitrary"),
        name="moe_dispatch",
    )(ids)


def _scatter_rows_kernel(dest_ref, hin_ref, xs_in, xs_out, sem, *, tm):
    del xs_in
    base = pl.program_id(0) * tm

    def group(gi, c):
        for u in range(DMA_UNROLL):
            r = gi * DMA_UNROLL + u
            for k in range(MOE_TOPK):
                pltpu.make_async_copy(hin_ref.at[pl.ds(r, 1), :],
                                      xs_out.at[pl.ds(dest_ref[(base + r) * MOE_TOPK + k], 1), :], sem).start()
        return c

    lax.fori_loop(0, tm // DMA_UNROLL, group, 0)
    for _ in range(MOE_TOPK):
        pltpu.make_async_copy(hin_ref, xs_out.at[pl.ds(0, tm), :], sem).wait()


def _scatter_rows(dest_flat, hin2, n_rows):
    t, d = hin2.shape
    tm = _tile(t, 256)
    grid_spec = pltpu.PrefetchScalarGridSpec(
        num_scalar_prefetch=1,
        grid=(t // tm,),
        in_specs=[pl.BlockSpec((tm, d), lambda i, dst: (i, 0)),
                  pl.BlockSpec(memory_space=pl.ANY)],
        out_specs=pl.BlockSpec(memory_space=pl.ANY),
        scratch_shapes=[pltpu.SemaphoreType.DMA],
    )
    return pl.pallas_call(
        functools.partial(_scatter_rows_kernel, tm=tm),
        out_shape=jax.ShapeDtypeStruct((n_rows, d), F32),
        grid_spec=grid_spec,
        input_output_aliases={2: 0},
        compiler_params=_cparams("arbitrary"),
        name="moe_scatter_rows",
    )(dest_flat, hin2, jnp.zeros((n_rows, d), F32))


def _ffn_kernel(eblk_ref, enb_ref, xs_hbm, wg_ref, wu_ref, wd_ref, ys_hbm,
                xbuf, ybuf, wgb, wub, wdb, isem, osem, *, blk, nb_total, n_exp):
    e = pl.program_id(0)
    n = enb_ref[e]
    b0 = eblk_ref[e]
    used = eblk_ref[n_exp - 1] + enb_ref[n_exp - 1]

    def in_cp(gb):
        sl = gb % 2
        return pltpu.make_async_copy(xs_hbm.at[pl.ds(gb * blk, blk), :], xbuf.at[sl], isem.at[sl])

    def out_cp(gb):
        sl = gb % 2
        return pltpu.make_async_copy(ybuf.at[sl], ys_hbm.at[pl.ds(gb * blk, blk), :], osem.at[sl])

    @pl.when(jnp.logical_and(e == 0, used > 0))
    def _():
        in_cp(0).start(priority=1)

    @pl.when(n > 0)
    def _():
        wgb[...] = wg_ref[0, 0].astype(BF16)
        wub[...] = wu_ref[0, 0].astype(BF16)
        wdb[...] = wd_ref[0, 0].astype(BF16)

        def body(j, c):
            gb = b0 + j
            sl = gb % 2

            @pl.when(gb + 1 < used)
            def _():
                in_cp(gb + 1).start(priority=1)

            in_cp(gb).wait()

            @pl.when(gb >= 2)
            def _():
                out_cp(gb - 2).wait()

            xb = xbuf[sl].astype(BF16)
            hg = jnp.dot(xb, wgb[...], preferred_element_type=F32)
            hu = jnp.dot(xb, wub[...], preferred_element_type=F32)
            hid = (_silu(hg) * hu).astype(BF16)
            ybuf[sl] = jnp.dot(hid, wdb[...], preferred_element_type=F32)
            out_cp(gb).start(priority=1)
            return c

        lax.fori_loop(0, n, body, 0)

    @pl.when(e == n_exp - 1)
    def _():
        @pl.when(used >= 2)
        def _():
            out_cp(used - 2).wait()

        @pl.when(used >= 1)
        def _():
            out_cp(used - 1).wait()

        ybuf[0] = jnp.zeros((blk, ybuf.shape[2]), F32)

        def fill(jb, c):
            cp = pltpu.make_async_copy(ybuf.at[0], ys_hbm.at[pl.ds(jb * blk, blk), :], osem.at[0])
            cp.start()
            cp.wait()
            return c

        lax.fori_loop(used, nb_total, fill, 0)


def _moe_ffn(eblk, enb, xs, w_gate, w_up, w_down, layer):
    n_rows, d = xs.shape
    n_exp, ff = w_gate.shape[1], w_gate.shape[3]
    blk = MOE_BLOCK
    wmap = lambda e, eb, en: (layer, e, 0, 0)
    grid_spec = pltpu.PrefetchScalarGridSpec(
        num_scalar_prefetch=2,
        grid=(n_exp,),
        in_specs=[pl.BlockSpec(memory_space=pl.ANY),
                  pl.BlockSpec((1, 1, d, ff), wmap),
                  pl.BlockSpec((1, 1, d, ff), wmap),
                  pl.BlockSpec((1, 1, ff, d), wmap)],
        out_specs=pl.BlockSpec(memory_space=pl.ANY),
        scratch_shapes=[pltpu.VMEM((2, blk, d), F32),
                        pltpu.VMEM((2, blk, d), F32),
                        pltpu.VMEM((d, ff), BF16),
                        pltpu.VMEM((d, ff), BF16),
                        pltpu.VMEM((ff, d), BF16),
                        pltpu.SemaphoreType.DMA((2,)),
                        pltpu.SemaphoreType.DMA((2,))],
    )
    return pl.pallas_call(
        functools.partial(_ffn_kernel, blk=blk, nb_total=n_rows // blk, n_exp=n_exp),
        out_shape=jax.ShapeDtypeStruct((n_rows, d), F32),
        grid_spec=grid_spec,
        compiler_params=_cparams("arbitrary"),
        name="moe_ffn",
    )(eblk, enb, xs, w_gate, w_up, w_down)


def _gather_ln_kernel(dest_ref, ys_hbm, rg_ref, x_ref, g_ref, lng_ref, lnb_ref, o_ref, ybuf, sem,
                      *, alpha, tm, nt, n_steps):
    step = pl.program_id(0) * nt + pl.program_id(1)
    slot = step % 2

    def start_gather(tile_idx, sl):
        base = tile_idx * tm

        def group(gi, c):
            for u in range(DMA_UNROLL):
                r = gi * DMA_UNROLL + u
                for k in range(MOE_TOPK):
                    pltpu.make_async_copy(ys_hbm.at[pl.ds(dest_ref[(base + r) * MOE_TOPK + k], 1), :],
                                          ybuf.at[sl, k, pl.ds(r, 1), :], sem.at[sl]).start()
            return c

        lax.fori_loop(0, tm // DMA_UNROLL, group, 0)

    @pl.when(step == 0)
    def _():
        start_gather(0, 0)

    @pl.when(step + 1 < n_steps)
    def _():
        start_gather(step + 1, 1 - slot)

    for k in range(MOE_TOPK):
        pltpu.make_async_copy(ys_hbm.at[pl.ds(0, tm), :], ybuf.at[slot, k], sem.at[slot]).wait()

    rg = rg_ref[0]
    y = ybuf[slot, 0] * rg[:, 0:1] + ybuf[slot, 1] * rg[:, 1:2]
    r = alpha * x_ref[0] + (1.0 + g_ref[0]) * y
    o_ref[0] = _layer_norm_rows(r, lng_ref[...], lnb_ref[...])


def _gather_combine_ln(dest_flat, ys, route_gates, x, gate, ln_g, ln_b, alpha, name):
    b, s, d = x.shape
    tm = _tile(s, 256)
    nt = s // tm
    row = lambda bi, i, dst: (bi, i, 0)
    const = lambda bi, i, dst: (0, 0)
    grid_spec = pltpu.PrefetchScalarGridSpec(
        num_scalar_prefetch=1,
        grid=(b, nt),
        in_specs=[pl.BlockSpec(memory_space=pl.ANY),
                  pl.BlockSpec((1, tm, LANES), row),
                  pl.BlockSpec((1, tm, d), row),
                  pl.BlockSpec((1, 1, d), lambda bi, i, dst: (bi, 0, 0)),
                  pl.BlockSpec((1, d), const),
                  pl.BlockSpec((1, d), const)],
        out_specs=pl.BlockSpec((1, tm, d), row),
        scratch_shapes=[pltpu.VMEM((2, MOE_TOPK, tm, d), F32), pltpu.SemaphoreType.DMA((2,))],
    )
    return pl.pallas_call(
        functools.partial(_gather_ln_kernel, alpha=alpha, tm=tm, nt=nt, n_steps=b * nt),
        out_shape=jax.ShapeDtypeStruct((b, s, d), F32),
        grid_spec=grid_spec,
        compiler_params=_cparams("arbitrary", "arbitrary"),
        name=name,
    )(dest_flat, ys, route_gates, x, gate, ln_g.reshape(1, d), ln_b.reshape(1, d))


def _gdn_kernel(q_ref, k_ref, v_ref, z_ref, bg_ref, cg_ref, hh_ref, ba_ref,
                cw_ref, alog_ref, dtb_ref, nw_ref, scw_ref,
                o_ref,
                ext_ref, qa_ref, ka_ref, va_ref, hist_ref, hist2_ref, gate_ref, state_ref,
                *, heads, ts, gw):
    L = GDN_CHUNK
    dk = GDN_HEAD_DIM

    @pl.when(pl.program_id(1) == 0)
    def _():
        hist_ref[...] = jnp.zeros_like(hist_ref)
        hist2_ref[...] = jnp.zeros_like(hist2_ref)
        state_ref[...] = jnp.zeros_like(state_ref)

    def causal_conv(x, hist, w, taps):
        ext_ref[0:SUBLANES, :] = hist
        ext_ref[SUBLANES:SUBLANES + ts, :] = x
        y = w[0:1] * ext_ref[SUBLANES - taps + 1:SUBLANES - taps + 1 + ts, :]
        for j in range(1, taps - 1):
            y = y + w[j:j + 1] * ext_ref[SUBLANES - taps + 1 + j:SUBLANES - taps + 1 + j + ts, :]
        return y + w[taps - 1:taps] * x

    for idx, (src, dst) in enumerate(((q_ref, qa_ref), (k_ref, ka_ref), (v_ref, va_ref))):
        x = src[0]
        y = causal_conv(x, hist_ref[idx], cw_ref[:, idx * gw:(idx + 1) * gw], GDN_CONV)
        hist_ref[idx] = x[ts - SUBLANES:ts, :]
        dst[...] = _silu(y)

    c = cg_ref[0] * hh_ref[0]
    yb = causal_conv(c, hist2_ref[...], scw_ref[...], SC_CONV)
    hist2_ref[...] = c[ts - SUBLANES:ts, :]
    o_ref[0, :, gw:2 * gw] = bg_ref[0] * yb

    ba = ba_ref[0]
    gate_ref[0] = _sigmoid(ba)
    gate_ref[1] = -jnp.exp(alog_ref[...]) * _softplus(ba + dtb_ref[...])

    ii = lax.broadcasted_iota(I32, (L, L), 0)
    jj = lax.broadcasted_iota(I32, (L, L), 1)
    causal = ii >= jj
    strict = ii > jj
    tri_incl = causal.astype(F32)
    eye = (ii == jj).astype(F32)
    nchunk = ts // L
    pairs = [(c, h) for c in range(nchunk) for h in range(heads)]

    def tiles(ref, lead=()):
        return jnp.stack([ref[lead + (slice(c * L, (c + 1) * L), slice(h * dk, (h + 1) * dk))]
                          for c, h in pairs], axis=0)

    def bmm(a, b):
        return jnp.einsum('bij,bjk->bik', a.astype(BF16), b.astype(BF16), preferred_element_type=F32)

    def bmm_nt(a, b):
        return jnp.einsum('bik,bjk->bij', a.astype(BF16), b.astype(BF16), preferred_element_type=F32)

    qh = tiles(qa_ref)
    kh = tiles(ka_ref)
    vh = tiles(va_ref)
    beta = jnp.stack([gate_ref[0, c * L:(c + 1) * L, h:h + 1] for c, h in pairs], axis=0)
    cums = [jnp.dot(tri_incl, gate_ref[1, c * L:(c + 1) * L, :], precision=HIGHEST,
                    preferred_element_type=F32) for c in range(nchunk)]
    cums_t = [cm.T for cm in cums]
    gc = jnp.stack([cums[c][:, heads + h:heads + h + 1] for c, h in pairs], axis=0)
    gr = jnp.stack([cums_t[c][heads + h:heads + h + 1, :] for c, h in pairs], axis=0)
    gl = gc[:, L - 1:L, :]
    decay = jnp.where(causal, jnp.exp(jnp.where(causal, gc - gr, 0.0)), 0.0)
    egc = jnp.exp(gc)
    qn = qh * lax.rsqrt(jnp.sum(qh * qh, -1, keepdims=True) + 1e-6)
    kn = kh * lax.rsqrt(jnp.sum(kh * kh, -1, keepdims=True) + 1e-6)
    qc = qn * (dk ** -0.5)
    kb = kn * beta
    m = jnp.where(strict, bmm_nt(kb, kn) * decay, 0.0)
    t_inv = eye - m
    mp = m
    for _ in range(5):
        mp = bmm(mp, mp)
        t_inv = t_inv + bmm(t_inv, mp)
    sol = bmm(t_inv, jnp.concatenate([vh * beta, kb * egc], axis=2))
    u = sol[:, :, 0:dk]
    w = sol[:, :, dk:2 * dk]
    qk = jnp.where(causal, bmm_nt(qc, kn) * decay, 0.0)
    qg = qc * egc
    k_dec = kn * jnp.exp(gl - gc)
    egl = jnp.exp(gl)

    state = [state_ref[h] for h in range(heads)]
    outs = []
    for b, (c, h) in enumerate(pairs):
        st = state[h]
        v_new = u[b] - _dot(w[b], st)
        outs.append(_dot(qg[b], st) + _dot(qk[b], v_new))
        state[h] = st * egl[b] + _dot(k_dec[b].T, v_new)
    for h in range(heads):
        state_ref[h] = state[h]

    o = jnp.stack(outs, axis=0)
    on = o * lax.rsqrt(jnp.mean(o * o, -1, keepdims=True) + 1e-6) * nw_ref[...]
    y = on * _silu(tiles(z_ref, (0,)))
    for b, (c, h) in enumerate(pairs):
        o_ref[0, c * L:(c + 1) * L, h * dk:(h + 1) * dk] = y[b]


def _gdn_shortconv(proj, conv_w, a_log, dt_bias, norm_w, sc_w, heads):
    b, s, _ = proj.shape
    gw = heads * GDN_HEAD_DIM
    ts = _tile(s, 256)
    alog_p = jnp.zeros((1, LANES), F32).at[0, heads:2 * heads].set(a_log)
    dtb_p = jnp.zeros((1, LANES), F32).at[0, heads:2 * heads].set(dt_bias)
    col = lambda cb: (lambda bi, i: (bi, i, cb))
    const = lambda bi, i: (0, 0)
    kern = functools.partial(_gdn_kernel, heads=heads, ts=ts, gw=gw)
    return pl.pallas_call(
        kern,
        out_shape=jax.ShapeDtypeStruct((b, s, 2 * gw), F32),
        grid=(b, s // ts),
        in_specs=[pl.BlockSpec((1, ts, gw), col(cb)) for cb in range(7)]
        + [pl.BlockSpec((1, ts, LANES), col(7 * gw // LANES)),
           pl.BlockSpec((GDN_CONV, 3 * gw), const),
           pl.BlockSpec((1, LANES), const),
           pl.BlockSpec((1, LANES), const),
           pl.BlockSpec((1, GDN_HEAD_DIM), const),
           pl.BlockSpec((SC_CONV, gw), const)],
        out_specs=pl.BlockSpec((1, ts, 2 * gw), lambda bi, i: (bi, i, 0)),
        scratch_shapes=[pltpu.VMEM((ts + SUBLANES, gw), F32),
                        pltpu.VMEM((ts, gw), F32),
                        pltpu.VMEM((ts, gw), F32),
                        pltpu.VMEM((ts, gw), F32),
                        pltpu.VMEM((3, SUBLANES, gw), F32),
                        pltpu.VMEM((SUBLANES, gw), F32),
                        pltpu.VMEM((2, ts, LANES), F32),
                        pltpu.VMEM((heads, GDN_HEAD_DIM, GDN_HEAD_DIM), F32)],
        compiler_params=_cparams("parallel", "arbitrary"),
        name="gdn_shortconv",
    )(proj, proj, proj, proj, proj, proj, proj, proj,
      conv_w, alog_p, dtb_p, norm_w.reshape(1, GDN_HEAD_DIM), sc_w)


def _rope_kernel(pos_ref, inv_ref, qpe_ref, kpe_ref, qo_ref, ko_ref, *, heads):
    ang = pos_ref[0].astype(F32) * inv_ref[...]
    lane = lax.broadcasted_iota(I32, ang.shape, 1)
    half = MLA_ROPE // 2
    cos = jnp.where(lane < MLA_ROPE, jnp.cos(ang), 0.0)
    sin = jnp.sin(ang)
    sgn = jnp.where(lane < half, -sin, jnp.where(lane < MLA_ROPE, sin, 0.0))

    def rot(x):
        swapped = jnp.where(lane < half, pltpu.roll(x, LANES - half, 1), pltpu.roll(x, half, 1))
        return x * cos + swapped * sgn

    for h in range(heads):
        cols = slice(h * LANES, (h + 1) * LANES)
        qo_ref[0, :, cols] = rot(qpe_ref[0, :, cols])
    ko_ref[0] = rot(kpe_ref[0])


def _rope(positions, qup, heads, cproj, kpe_block):
    b, s = positions.shape
    ts = _tile(s, 512)
    half = MLA_ROPE // 2
    inv = ROPE_THETA ** (-jnp.arange(half, dtype=F32) * (2.0 / MLA_ROPE))
    inv_tab = jnp.zeros((1, LANES), F32).at[0, 0:half].set(inv).at[0, half:MLA_ROPE].set(inv)
    w = heads * LANES
    return pl.pallas_call(
        functools.partial(_rope_kernel, heads=heads),
        out_shape=(jax.ShapeDtypeStruct((b, s, w), F32), jax.ShapeDtypeStruct((b, s, LANES), F32)),
        grid=(b, s // ts),
        in_specs=[pl.BlockSpec((1, ts, 1), lambda bi, i: (bi, i, 0)),
                  pl.BlockSpec((1, LANES), lambda bi, i: (0, 0)),
                  pl.BlockSpec((1, ts, w), lambda bi, i: (bi, i, 1)),
                  pl.BlockSpec((1, ts, LANES), lambda bi, i: (bi, i, kpe_block))],
        out_specs=(pl.BlockSpec((1, ts, w), lambda bi, i: (bi, i, 0)),
                   pl.BlockSpec((1, ts, LANES), lambda bi, i: (bi, i, 0))),
        compiler_params=_cparams("parallel", "parallel"),
        name="mla_rope",
    )(positions.reshape(b, s, 1), inv_tab, qup, cproj)


def _attn_kernel(qn_ref, qp_ref, kn_ref, kp_ref, v_ref, o_ref, kcat_ref, vb_ref, s_ref, *, tq, nq, scale):
    qi = pl.program_id(2)

    @pl.when(qi == 0)
    def _():
        kcat_ref[:, 0:MLA_NOPE] = kn_ref[0].astype(BF16)
        kcat_ref[:, MLA_NOPE:MLA_NOPE + LANES] = kp_ref[0].astype(BF16)
        vb_ref[...] = v_ref[0].astype(BF16)

    qcat = jnp.concatenate([qn_ref[0], qp_ref[0]], axis=1).astype(BF16)
    row = lax.broadcasted_iota(I32, (tq, tq), 0)
    col = lax.broadcasted_iota(I32, (tq, tq), 1)

    for qv in range(nq):
        @pl.when(qi == qv)
        def _(qv=qv):
            m = None
            for kb in range(qv + 1):
                ks = slice(kb * tq, (kb + 1) * tq)
                s = lax.dot_general(qcat, kcat_ref[ks, :], (((1,), (1,)), ((), ())),
                                    preferred_element_type=F32) * scale
                if kb == qv:
                    s = jnp.where(row >= col, s, -jnp.inf)
                s_ref[:, ks] = s
                mb = jnp.max(s, -1, keepdims=True)
                m = mb if m is None else jnp.maximum(m, mb)
            l = jnp.zeros((tq, 1), F32)
            acc = jnp.zeros((tq, MLA_V), F32)
            for kb in range(qv + 1):
                ks = slice(kb * tq, (kb + 1) * tq)
                p = jnp.exp(s_ref[:, ks] - m)
                l = l + jnp.sum(p, -1, keepdims=True)
                acc = acc + jnp.dot(p.astype(BF16), vb_ref[ks, :], preferred_element_type=F32)
            o_ref[0] = acc / l


def _attention(qup, qpe, kvup, kpe, heads):
    b, s, _ = qup.shape
    tq = _tile(s, 512)
    nq = s // tq
    scale = (MLA_NOPE + MLA_ROPE) ** -0.5
    return pl.pallas_call(
        functools.partial(_attn_kernel, tq=tq, nq=nq, scale=scale),
        out_shape=jax.ShapeDtypeStruct((b, s, heads * MLA_V), F32),
        grid=(b, heads, nq),
        in_specs=[pl.BlockSpec((1, tq, MLA_NOPE), lambda bi, h, i: (bi, i, h)),
                  pl.BlockSpec((1, tq, LANES), lambda bi, h, i: (bi, i, h)),
                  pl.BlockSpec((1, s, MLA_NOPE), lambda bi, h, i: (bi, 0, h)),
                  pl.BlockSpec((1, s, LANES), lambda bi, h, i: (bi, 0, 0)),
                  pl.BlockSpec((1, s, MLA_V), lambda bi, h, i: (bi, 0, heads + h))],
        out_specs=pl.BlockSpec((1, tq, MLA_V), lambda bi, h, i: (bi, i, h)),
        scratch_shapes=[pltpu.VMEM((s, MLA_NOPE + LANES), BF16), pltpu.VMEM((s, MLA_V), BF16),
                        pltpu.VMEM((tq, s), F32)],
        compiler_params=_cparams("parallel", "parallel", "arbitrary"),
        name="mla_attention",
    )(qup, qpe, kvup, kpe, kvup)


def _hyb_in_weight(w_in, heads):
    d = w_in.shape[0]
    gw = heads * GDN_HEAD_DIM
    main = jnp.concatenate([w_in[:, :4 * gw], w_in[:, 4 * gw + 2 * heads:]], axis=1)
    gates = w_in[:, 4 * gw:4 * gw + 2 * heads]
    n = main.shape[1] + LANES
    n_pad = -(-n // 768) * 768
    tail = jnp.zeros((d, n_pad - main.shape[1]), w_in.dtype).at[:, :2 * heads].set(gates)
    return jnp.concatenate([main, tail], axis=1).astype(BF16)


def _mla_in_weight(w_in):
    d = w_in.shape[0]
    pad = jnp.zeros((d, LANES - MLA_ROPE), w_in.dtype)
    return jnp.concatenate([w_in, pad], axis=1).astype(BF16)


def _mla_uq_weight(w_uq, heads):
    r = w_uq.shape[0]
    w = w_uq.reshape(r, heads, MLA_NOPE + MLA_ROPE)
    nope = w[:, :, :MLA_NOPE].reshape(r, heads * MLA_NOPE)
    pe = jnp.concatenate([w[:, :, MLA_NOPE:], jnp.zeros((r, heads, LANES - MLA_ROPE), w.dtype)], axis=2)
    return jnp.concatenate([nope, pe.reshape(r, heads * LANES)], axis=1).astype(BF16)


def _mla_ukv_weight(w_ukv, heads):
    r = w_ukv.shape[0]
    w = w_ukv.reshape(r, heads, MLA_NOPE + MLA_V)
    return jnp.concatenate([w[:, :, :MLA_NOPE].reshape(r, heads * MLA_NOPE),
                            w[:, :, MLA_NOPE:].reshape(r, heads * MLA_V)], axis=1).astype(BF16)


def _router_weight(wr_g, br_g, wr_e, br_e):
    d = wr_g.shape[0]
    n = wr_g.shape[1] + wr_e.shape[1]
    w = jnp.zeros((d, LANES), F32).at[:, :n].set(jnp.concatenate([wr_g, wr_e], axis=1))
    bias = jnp.zeros((1, LANES), F32).at[0, :n].set(jnp.concatenate([br_g, br_e]))
    w_hi = w.astype(BF16)
    w_lo = (w - w_hi.astype(F32)).astype(BF16)
    return jnp.stack([w_hi, w_lo]), bias


def kernel(x, c, positions, ada_w, ada_b, ln_g, ln_b, hyb_w_in, gdn_conv_w, gdn_a_log, gdn_dt_bias,
           gdn_norm_w, sc_conv_w, hyb_w_out, mla_w_in, mla_q_norm, mla_kv_norm, mla_w_uq, mla_w_ukv,
           mla_w_out, moe_router_g, moe_bias_g, moe_router_e, moe_bias_e, moe_w_gate, moe_w_up,
           moe_w_down):
    b, s, d = x.shape
    depth = ada_w.shape[0]
    alpha = (2.0 * depth) ** 0.25
    gdn_heads = d // (2 * GDN_HEAD_DIM)
    mla_heads = mla_w_out.shape[1] // MLA_V
    n_groups = moe_router_g.shape[2]
    n_exp = moe_router_e.shape[2]
    per_group = n_exp // n_groups

    mod = _modulation(c, ada_w, ada_b)
    for layer in range(depth):
        sh1, sc1, g1, sh2, sc2, g2 = [mod[layer, :, j * d:(j + 1) * d].reshape(b, 1, d) for j in range(6)]
        i = layer // 2
        wr, br = _router_weight(moe_router_g[layer], moe_bias_g[layer], moe_router_e[layer], moe_bias_e[layer])
        if layer % 2 == 0:
            proj = _mod_matmul(x, sc1, sh1, _hyb_in_weight(hyb_w_in[i], gdn_heads), "hyb_in_proj")
            y = _gdn_shortconv(proj, gdn_conv_w[i], gdn_a_log[i], gdn_dt_bias[i], gdn_norm_w[i],
                               sc_conv_w[i], gdn_heads)
            w_out = hyb_w_out[i]
        else:
            cproj = _mod_matmul(x, sc1, sh1, _mla_in_weight(mla_w_in[i]), "mla_in_proj")
            qup = _rms_matmul(cproj, 0, MLA_Q_RANK, mla_q_norm[i], _mla_uq_weight(mla_w_uq[i], mla_heads),
                              "mla_q_up")
            kvup = _rms_matmul(cproj, 1, MLA_KV_RANK, mla_kv_norm[i], _mla_ukv_weight(mla_w_ukv[i], mla_heads),
                               "mla_kv_up")
            qpe, kpe = _rope(positions, qup, mla_heads, cproj, (MLA_Q_RANK + MLA_KV_RANK) // LANES)
            y = _attention(qup, qpe, kvup, kpe, mla_heads)
            w_out = mla_w_out[i]
        x, hin2, ids, gates = _out_ln_route(y, w_out.astype(BF16), x, g1, ln_g[layer, 0], ln_b[layer, 0],
                                            sc2, sh2, wr, br, alpha, n_groups, per_group,
                                            "mixer_out_ln_route")
        t = b * s
        dest, meta = _dispatch(ids.reshape(t, LANES))
        dest_flat = dest[:, :MOE_TOPK].reshape(-1)
        eblk = meta[0, :n_exp] // MOE_BLOCK
        enb = meta[1, :n_exp]
        n_rows = t * MOE_TOPK + n_exp * MOE_BLOCK
        xs = _scatter_rows(dest_flat, hin2.reshape(t, d), n_rows)
        ys = _moe_ffn(eblk, enb, xs, moe_w_gate, moe_w_up, moe_w_down, layer)
        x = _gather_combine_ln(dest_flat, ys, gates, x, g2, ln_g[layer, 1], ln_b[layer, 1], alpha,
                               "moe_gather_combine_ln")
    return x
```

```python
import functools

import jax
import jax.numpy as jnp
from jax import lax
from jax.experimental import pallas as pl
from jax.experimental.pallas import tpu as pltpu

F32, BF16, I32, U32 = jnp.float32, jnp.bfloat16, jnp.int32, jnp.uint32
HIGHEST = lax.Precision.HIGHEST

LANES = 128
SUBLANES = 8
VMEM_LIMIT_BYTES = 56 * 1024 * 1024

GDN_HEAD_DIM = 128
GDN_CONV = 4
GDN_CHUNK = 64
SC_CONV = 3
MLA_NOPE = 128
MLA_ROPE = 64
MLA_V = 128
MLA_Q_RANK = 512
MLA_KV_RANK = 512
ROPE_THETA = 10000.0
MOE_TOPK = 2
MOE_BLOCK = 128
DMA_UNROLL = 8


def _tile(n, pref):
    if n <= pref:
        return n
    for t in range(pref, 0, -LANES):
        if n % t == 0:
            return t
    return n


def _cparams(*sem):
    return pltpu.CompilerParams(dimension_semantics=sem, vmem_limit_bytes=VMEM_LIMIT_BYTES)


def _sigmoid(x):
    return 1.0 / (1.0 + jnp.exp(-x))


def _silu(x):
    return x * _sigmoid(x)


def _softplus(x):
    return jnp.maximum(x, 0.0) + jnp.log(1.0 + jnp.exp(-jnp.abs(x)))


def _pack_bf16_pair(lo, hi):
    lo_bits = lax.bitcast_convert_type(lo.astype(BF16).astype(F32), U32)
    hi_bits = lax.bitcast_convert_type(hi.astype(BF16).astype(F32), U32)
    return (lo_bits >> 16) | (hi_bits & jnp.uint32(0xFFFF0000))


def _unpack_bf16_pair(w):
    lo = lax.bitcast_convert_type(w << 16, F32)
    hi = lax.bitcast_convert_type(w & jnp.uint32(0xFFFF0000), F32)
    return lo.astype(BF16), hi.astype(BF16)


def _dot(a, b):
    return jnp.dot(a.astype(BF16), b.astype(BF16), preferred_element_type=F32)


def _dot_nt(a, b):
    return lax.dot_general(a.astype(BF16), b.astype(BF16), (((1,), (1,)), ((), ())),
                           preferred_element_type=F32)


def _mod_kernel(c_ref, w_ref, b_ref, o_ref):
    c = c_ref[...]
    o_ref[0] = _dot(_silu(c), w_ref[0]) + b_ref[0]


def _modulation(c, ada_w, ada_b):
    depth, d, n = ada_w.shape
    b = c.shape[0]
    rows = -(-b // SUBLANES) * SUBLANES
    cp = jnp.zeros((rows, d), F32).at[:b].set(c)
    tn = _tile(n, 1024)
    out = pl.pallas_call(
        _mod_kernel,
        out_shape=jax.ShapeDtypeStruct((depth, rows, n), F32),
        grid=(depth, n // tn),
        in_specs=[pl.BlockSpec((rows, d), lambda l, j: (0, 0)),
                  pl.BlockSpec((1, d, tn), lambda l, j: (l, 0, j)),
                  pl.BlockSpec((1, 1, tn), lambda l, j: (l, 0, j))],
        out_specs=pl.BlockSpec((1, rows, tn), lambda l, j: (l, 0, j)),
        compiler_params=_cparams("parallel", "parallel"),
        name="adaln_mod",
    )(cp, ada_w, ada_b.reshape(depth, 1, n))
    return out[:, :b]


def _modmm_kernel(x_ref, sc_ref, sh_ref, w_ref, o_ref, xb_ref):
    @pl.when(pl.program_id(2) == 0)
    def _():
        xb_ref[...] = (x_ref[0] * (1.0 + sc_ref[0]) + sh_ref[0]).astype(BF16)

    o_ref[0] = jnp.dot(xb_ref[...], w_ref[...], preferred_element_type=F32)


def _mod_matmul(x, sc, sh, w, name):
    b, s, d = x.shape
    n = w.shape[1]
    tm = _tile(s, 1024)
    tn = _tile(n, 768)
    return pl.pallas_call(
        _modmm_kernel,
        out_shape=jax.ShapeDtypeStruct((b, s, n), F32),
        grid=(b, s // tm, n // tn),
        in_specs=[pl.BlockSpec((1, tm, d), lambda bi, i, j: (bi, i, 0)),
                  pl.BlockSpec((1, 1, d), lambda bi, i, j: (bi, 0, 0)),
                  pl.BlockSpec((1, 1, d), lambda bi, i, j: (bi, 0, 0)),
                  pl.BlockSpec((d, tn), lambda bi, i, j: (0, j))],
        out_specs=pl.BlockSpec((1, tm, tn), lambda bi, i, j: (bi, i, j)),
        scratch_shapes=[pltpu.VMEM((tm, d), BF16)],
        compiler_params=_cparams("parallel", "parallel", "arbitrary"),
        name=name,
    )(x, sc, sh, w)


def _rmsmm_kernel(x_ref, nw_ref, w_ref, o_ref, xb_ref):
    @pl.when(pl.program_id(2) == 0)
    def _():
        x = x_ref[0]
        xb_ref[...] = (x * lax.rsqrt(jnp.mean(x * x, -1, keepdims=True) + 1e-6) * nw_ref[...]).astype(BF16)

    o_ref[0] = jnp.dot(xb_ref[...], w_ref[...], preferred_element_type=F32)


def _rms_matmul(x, col_block, k, norm_w, w, name):
    b, s, _ = x.shape
    n = w.shape[1]
    tm = _tile(s, 1024)
    tn = _tile(n, 1024)
    return pl.pallas_call(
        _rmsmm_kernel,
        out_shape=jax.ShapeDtypeStruct((b, s, n), F32),
        grid=(b, s // tm, n // tn),
        in_specs=[pl.BlockSpec((1, tm, k), lambda bi, i, j: (bi, i, col_block)),
                  pl.BlockSpec((1, k), lambda bi, i, j: (0, 0)),
                  pl.BlockSpec((k, tn), lambda bi, i, j: (0, j))],
        out_specs=pl.BlockSpec((1, tm, tn), lambda bi, i, j: (bi, i, j)),
        scratch_shapes=[pltpu.VMEM((tm, k), BF16)],
        compiler_params=_cparams("parallel", "parallel", "arbitrary"),
        name=name,
    )(x, norm_w.reshape(1, k), w)


def _layer_norm_rows(r, g, b):
    mu = jnp.mean(r, -1, keepdims=True)
    var = jnp.mean(jnp.square(r - mu), -1, keepdims=True)
    return (r - mu) * lax.rsqrt(var + 1e-5) * g + b


def _outln_kernel(y_ref, w_ref, xres_ref, g_ref, lng_ref, lnb_ref, sc_ref, sh_ref, wr_ref, br_ref,
                  xo_ref, hin_ref, ids_ref, gates_ref, *, alpha, n_groups, per_group):
    y = jnp.dot(y_ref[0].astype(BF16), w_ref[...], preferred_element_type=F32)
    r = alpha * xres_ref[0] + (1.0 + g_ref[0]) * y
    xn = _layer_norm_rows(r, lng_ref[...], lnb_ref[...])
    xo_ref[0] = xn
    hin = xn * (1.0 + sc_ref[0]) + sh_ref[0]
    half = hin.shape[1] // 2
    hin_ref[0] = _pack_bf16_pair(hin[:, :half], hin[:, half:])
    h_hi = hin.astype(BF16)
    h_lo = (hin - h_hi.astype(F32)).astype(BF16)
    w_hi = wr_ref[0]
    logits = (jnp.dot(h_hi, w_hi, preferred_element_type=F32)
              + jnp.dot(h_hi, wr_ref[1], preferred_element_type=F32)
              + jnp.dot(h_lo, w_hi, preferred_element_type=F32)) + br_ref[...]
    lane = lax.broadcasted_iota(I32, logits.shape, 1)
    big = jnp.int32(4 * LANES)
    neg = jnp.float32(-jnp.inf)
    n_exp = n_groups * per_group
    gmask = lane < n_groups
    lg = jnp.where(gmask, logits, neg)
    mg = jnp.max(lg, -1, keepdims=True)
    grp = jnp.min(jnp.where(gmask & (lg == mg), lane, big), -1, keepdims=True)
    pg_sel = 1.0 / jnp.sum(jnp.where(gmask, jnp.exp(lg - mg), 0.0), -1, keepdims=True)
    lo = n_groups + grp * per_group
    emask = (lane >= lo) & (lane < lo + per_group) & (lane < n_groups + n_exp)
    le = jnp.where(emask, logits, neg)
    me = jnp.max(le, -1, keepdims=True)
    ee = jnp.where(emask, jnp.exp(le - me), 0.0)
    p = ee / jnp.sum(ee, -1, keepdims=True)
    pm = jnp.where(emask, p, -1.0)
    p1 = jnp.max(pm, -1, keepdims=True)
    i1 = jnp.min(jnp.where(emask & (pm == p1), lane, big), -1, keepdims=True)
    pm2 = jnp.where(lane == i1, -1.0, pm)
    p2 = jnp.max(pm2, -1, keepdims=True)
    i2 = jnp.min(jnp.where(emask & (lane != i1) & (pm2 == p2), lane, big), -1, keepdims=True)
    den = p1 + p2
    g1 = pg_sel * p1 / den
    g2 = pg_sel * p2 / den
    ids_ref[0] = jnp.where(lane == 0, i1 - n_groups, jnp.where(lane == 1, i2 - n_groups, 0))
    gates_ref[0] = jnp.where(lane == 0, g1, jnp.where(lane == 1, g2, 0.0))


def _out_ln_route(y, w, xres, gate, ln_g, ln_b, sc, sh, wr, br, alpha, n_groups, per_group, name):
    b, s, k = y.shape
    d = w.shape[1]
    tm = _tile(s, 256)
    kern = functools.partial(_outln_kernel, alpha=alpha, n_groups=n_groups, per_group=per_group)
    row = lambda bi, i: (bi, i, 0)
    per_b = lambda bi, i: (bi, 0, 0)
    const = lambda bi, i: (0, 0)
    return pl.pallas_call(
        kern,
        out_shape=(jax.ShapeDtypeStruct((b, s, d), F32), jax.ShapeDtypeStruct((b, s, d // 2), U32),
                   jax.ShapeDtypeStruct((b, s, LANES), I32), jax.ShapeDtypeStruct((b, s, LANES), F32)),
        grid=(b, s // tm),
        in_specs=[pl.BlockSpec((1, tm, k), row),
                  pl.BlockSpec((k, d), const),
                  pl.BlockSpec((1, tm, d), row),
                  pl.BlockSpec((1, 1, d), per_b),
                  pl.BlockSpec((1, d), const),
                  pl.BlockSpec((1, d), const),
                  pl.BlockSpec((1, 1, d), per_b),
                  pl.BlockSpec((1, 1, d), per_b),
                  pl.BlockSpec((2, d, LANES), lambda bi, i: (0, 0, 0)),
                  pl.BlockSpec((1, LANES), const)],
        out_specs=(pl.BlockSpec((1, tm, d), row), pl.BlockSpec((1, tm, d // 2), row),
                   pl.BlockSpec((1, tm, LANES), row), pl.BlockSpec((1, tm, LANES), row)),
        compiler_params=_cparams("parallel", "parallel"),
        name=name,
    )(y, w, xres, gate, ln_g.reshape(1, d), ln_b.reshape(1, d), sc, sh, wr, br)


def _combine_kernel(ya_ref, yb_ref, rg_ref, x_ref, g_ref, lng_ref, lnb_ref, o_ref, *, alpha):
    rg = rg_ref[0]
    y = ya_ref[0] * rg[:, 0:1] + yb_ref[0] * rg[:, 1:2]
    r = alpha * x_ref[0] + (1.0 + g_ref[0]) * y
    o_ref[0] = _layer_norm_rows(r, lng_ref[...], lnb_ref[...])


def _combine_ln(y2, route_gates, x, gate, ln_g, ln_b, alpha, name):
    b, s, d = x.shape
    tm = _tile(s, 512)
    row = lambda bi, i: (bi, i, 0)
    nt = s // tm
    return pl.pallas_call(
        functools.partial(_combine_kernel, alpha=alpha),
        out_shape=jax.ShapeDtypeStruct((b, s, d), F32),
        grid=(b, nt),
        in_specs=[pl.BlockSpec((1, tm, d), lambda bi, i: (0, bi * nt + i, 0)),
                  pl.BlockSpec((1, tm, d), lambda bi, i: (1, bi * nt + i, 0)),
                  pl.BlockSpec((1, tm, LANES), row),
                  pl.BlockSpec((1, tm, d), row),
                  pl.BlockSpec((1, 1, d), lambda bi, i: (bi, 0, 0)),
                  pl.BlockSpec((1, d), lambda bi, i: (0, 0)),
                  pl.BlockSpec((1, d), lambda bi, i: (0, 0))],
        out_specs=pl.BlockSpec((1, tm, d), row),
        compiler_params=_cparams("parallel", "parallel"),
        name=name,
    )(y2, y2, route_gates, x, gate, ln_g.reshape(1, d), ln_b.reshape(1, d))


def _moe_kernel(bexp_ref, bcnt_ref, rsrc_ref, rdst_ref,
                hin_hbm, wg_ref, wu_ref, wd_ref,
                y2_hbm,
                xbuf, ybuf, wgb, wub, wdb, gsem, ssem, *, blk, nb, tp):
    i = pl.program_id(0)
    slot = i % 2

    def start_gather(block, sl):
        base = block * blk

        def group(gi, c):
            for u in range(DMA_UNROLL):
                r = gi * DMA_UNROLL + u
                pltpu.make_async_copy(hin_hbm.at[pl.ds(rsrc_ref[base + r], 1), :],
                                      xbuf.at[sl, pl.ds(r, 1), :], gsem.at[sl]).start()
            return c

        lax.fori_loop(0, blk // DMA_UNROLL, group, 0)

    def wait_gather(sl):
        pltpu.make_async_copy(hin_hbm.at[pl.ds(0, blk), :], xbuf.at[sl], gsem.at[sl]).wait()

    def start_scatter(block, sl):
        base = block * blk

        def group(gi, c):
            for u in range(DMA_UNROLL):
                r = gi * DMA_UNROLL + u
                pltpu.make_async_copy(ybuf.at[sl, pl.ds(r, 1), :],
                                      y2_hbm.at[pl.ds(rdst_ref[base + r], 1), :], ssem.at[sl]).start()
            return c

        lax.fori_loop(0, blk // DMA_UNROLL, group, 0)

    def wait_scatter(sl):
        pltpu.make_async_copy(ybuf.at[sl], y2_hbm.at[pl.ds(0, blk), :], ssem.at[sl]).wait()

    used = bcnt_ref[i] > 0

    @pl.when(i == 0)
    def _():
        ybuf[...] = jnp.zeros_like(ybuf)
        for k in range(MOE_TOPK):
            for sl in range(2):
                cp = pltpu.make_async_copy(ybuf.at[sl], y2_hbm.at[pl.ds(k * tp + tp - (2 - sl) * blk, blk), :],
                                           ssem.at[sl])
                cp.start()
                cp.wait()

    @pl.when(jnp.logical_and(i == 0, used))
    def _():
        start_gather(0, 0)

    nxt = jnp.minimum(i + 1, nb - 1)

    @pl.when(jnp.logical_and(i + 1 < nb, bcnt_ref[nxt] > 0))
    def _():
        start_gather(nxt, 1 - slot)

    changed = jnp.logical_or(i == 0, bexp_ref[i] != bexp_ref[jnp.maximum(i - 1, 0)])

    @pl.when(jnp.logical_and(changed, used))
    def _():
        wgb[...] = wg_ref[0, 0].astype(BF16)
        wub[...] = wu_ref[0, 0].astype(BF16)
        wdb[...] = wd_ref[0, 0].astype(BF16)

    @pl.when(jnp.logical_and(i >= 2, bcnt_ref[jnp.maximum(i - 2, 0)] > 0))
    def _():
        wait_scatter(slot)

    @pl.when(used)
    def _():
        wait_gather(slot)
        xb = xbuf[slot].astype(BF16)
        hg = jnp.dot(xb, wgb[...], preferred_element_type=F32)
        hu = jnp.dot(xb, wub[...], preferred_element_type=F32)
        hid = (_silu(hg) * hu).astype(BF16)
        ybuf[slot] = jnp.dot(hid, wdb[...], preferred_element_type=F32)
        start_scatter(i, slot)

    @pl.when(i == nb - 1)
    def _():
        if nb >= 2:
            @pl.when(bcnt_ref[jnp.maximum(i - 1, 0)] > 0)
            def _():
                wait_scatter(1 - slot)

        @pl.when(used)
        def _():
            wait_scatter(slot)


def _moe_experts(hin2, w_gate, w_up, w_down, layer, blk_exp, blk_cnt, row_src, row_dst, name):
    t, d = hin2.shape
    ff = w_gate.shape[3]
    nb = blk_exp.shape[0]
    blk = MOE_BLOCK
    tp = t + 2 * blk
    wmap = lambda i, be, bc, rs, rd: (layer, be[i], 0, 0)
    grid_spec = pltpu.PrefetchScalarGridSpec(
        num_scalar_prefetch=4,
        grid=(nb,),
        in_specs=[pl.BlockSpec(memory_space=pl.ANY),
                  pl.BlockSpec((1, 1, d, ff), wmap),
                  pl.BlockSpec((1, 1, d, ff), wmap),
                  pl.BlockSpec((1, 1, ff, d), wmap)],
        out_specs=pl.BlockSpec(memory_space=pl.ANY),
        scratch_shapes=[pltpu.VMEM((2, blk, d), F32),
                        pltpu.VMEM((2, blk, d), F32),
                        pltpu.VMEM((d, ff), BF16),
                        pltpu.VMEM((d, ff), BF16),
                        pltpu.VMEM((ff, d), BF16),
                        pltpu.SemaphoreType.DMA((2,)),
                        pltpu.SemaphoreType.DMA((2,))],
    )
    y2 = pl.pallas_call(
        functools.partial(_moe_kernel, blk=blk, nb=nb, tp=tp),
        out_shape=jax.ShapeDtypeStruct((MOE_TOPK * tp, d), F32),
        grid_spec=grid_spec,
        compiler_params=_cparams("arbitrary"),
        name=name,
    )(blk_exp, blk_cnt, row_src, row_dst, hin2, w_gate, w_up, w_down)
    return y2.reshape(MOE_TOPK, tp, d)


def _dispatch_tables(ids, n_exp):
    t = ids.shape[0]
    n_assign = t * MOE_TOPK
    blk = MOE_BLOCK
    tp = t + 2 * blk
    e_flat = ids.reshape(-1)
    order = jnp.argsort(e_flat).astype(I32)
    e_sorted = e_flat[order]
    counts = jnp.bincount(e_flat, length=n_exp).astype(I32)
    padded = (counts + blk - 1) // blk * blk
    pad_end = jnp.cumsum(padded)
    pad_start = pad_end - padded
    start = jnp.cumsum(counts) - counts
    dest = pad_start[e_sorted] + jnp.arange(n_assign, dtype=I32) - start[e_sorted]
    n_rows = n_assign + n_exp * blk
    nb = n_rows // blk
    rows = jnp.arange(n_rows, dtype=I32)
    spare = t + ((rows // blk) % 2) * blk + rows % blk
    row_src = jnp.zeros((n_rows,), I32).at[dest].set(order // MOE_TOPK)
    row_dst = spare.at[dest].set((order % MOE_TOPK) * tp + order // MOE_TOPK)
    blk_first = jnp.arange(nb, dtype=I32) * blk
    blk_exp = jnp.minimum(jnp.searchsorted(pad_end, blk_first, side='right'), n_exp - 1).astype(I32)
    in_use = blk_first < pad_end[-1]
    blk_cnt = jnp.where(in_use, jnp.clip(counts[blk_exp] - (blk_first - pad_start[blk_exp]), 0, blk), 0)
    return blk_exp, blk_cnt.astype(I32), row_src, row_dst


def _dispatch_kernel(ids_ref, dest_ref, meta_ref, rank_ref, *, t, tile, blk_shift):
    lane = lax.broadcasted_iota(I32, (tile, LANES), 1)
    ri = lax.broadcasted_iota(I32, (tile, tile), 0)
    ci = lax.broadcasted_iota(I32, (tile, tile), 1)
    before = (ri > ci).astype(BF16)

    def hits(rows):
        ids = ids_ref[rows, :]
        return lane == ids[:, 0:1], lane == ids[:, 1:2]

    def count(ti, run):
        rows = pl.ds(pl.multiple_of(ti * tile, tile), tile)
        h1, h2 = hits(rows)
        onehot = jnp.logical_or(h1, h2).astype(BF16)
        prefix = jnp.dot(before, onehot, preferred_element_type=F32) + run
        r1 = jnp.sum(jnp.where(h1, prefix, 0.0), -1, keepdims=True)
        r2 = jnp.sum(jnp.where(h2, prefix, 0.0), -1, keepdims=True)
        rank_ref[rows, :] = jnp.where(lane == 0, r1, jnp.where(lane == 1, r2, 0.0))
        return run + jnp.sum(onehot.astype(F32), 0, keepdims=True)

    counts = lax.fori_loop(0, t // tile, count, jnp.zeros((1, LANES), F32))
    cnt = jnp.broadcast_to(counts, (SUBLANES, LANES)).astype(I32)
    nblk = (cnt + ((1 << blk_shift) - 1)) >> blk_shift
    padded = nblk << blk_shift
    lane8 = lax.broadcasted_iota(I32, (SUBLANES, LANES), 1)
    incl = padded
    step = 1
    while step < LANES:
        incl = incl + jnp.where(lane8 >= step, pltpu.roll(incl, step, 1), 0)
        step *= 2
    pad_start = incl - padded
    row8 = lax.broadcasted_iota(I32, (SUBLANES, LANES), 0)
    meta_ref[...] = jnp.where(row8 == 0, pad_start, jnp.where(row8 == 1, nblk, cnt))
    start_f = pad_start[0:1, :].astype(F32)

    def place(ti, c):
        rows = pl.ds(pl.multiple_of(ti * tile, tile), tile)
        h1, h2 = hits(rows)
        rk = rank_ref[rows, :]
        d1 = jnp.sum(jnp.where(h1, start_f, 0.0), -1, keepdims=True) + rk[:, 0:1]
        d2 = jnp.sum(jnp.where(h2, start_f, 0.0), -1, keepdims=True) + rk[:, 1:2]
        dest_ref[rows, :] = jnp.where(lane == 0, d1, jnp.where(lane == 1, d2, 0.0)).astype(I32)
        return c

    lax.fori_loop(0, t // tile, place, 0)


def _dispatch(ids):
    t = ids.shape[0]
    tile = _tile(t, 256)
    blk_shift = MOE_BLOCK.bit_length() - 1
    assert (1 << blk_shift) == MOE_BLOCK
    return pl.pallas_call(
        functools.partial(_dispatch_kernel, t=t, tile=tile, blk_shift=blk_shift),
        out_shape=(jax.ShapeDtypeStruct((t, LANES), I32), jax.ShapeDtypeStruct((SUBLANES, LANES), I32)),
        grid=(1,),
        in_specs=[pl.BlockSpec((t, LANES), lambda i: (0, 0))],
        out_specs=(pl.BlockSpec((t, LANES), lambda i: (0, 0)), pl.BlockSpec((SUBLANES, LANES), lambda i: (0, 0))),
        scratch_shapes=[pltpu.VMEM((t, LANES), F32)],
        compiler_params=_cparams("arbitrary"),
        name="moe_dispatch",
    )(ids)


def _scatter_rows_kernel(dest_ref, hin_ref, xs_in, xs_out, sem, *, tm):
    del xs_in
    base = pl.program_id(0) * tm

    def group(gi, c):
        for u in range(DMA_UNROLL):
            r = gi * DMA_UNROLL + u
            for k in range(MOE_TOPK):
                pltpu.make_async_copy(hin_ref.at[pl.ds(r, 1), :],
                                      xs_out.at[pl.ds(dest_ref[(base + r) * MOE_TOPK + k], 1), :], sem).start()
        return c

    lax.fori_loop(0, tm // DMA_UNROLL, group, 0)
    for _ in range(MOE_TOPK):
        pltpu.make_async_copy(hin_ref, xs_out.at[pl.ds(0, tm), :], sem).wait()


def _scatter_rows(dest_flat, hin2, n_rows):
    t, d = hin2.shape
    tm = _tile(t, 256)
    grid_spec = pltpu.PrefetchScalarGridSpec(
        num_scalar_prefetch=1,
        grid=(t // tm,),
        in_specs=[pl.BlockSpec((tm, d), lambda i, dst: (i, 0)),
                  pl.BlockSpec(memory_space=pl.ANY)],
        out_specs=pl.BlockSpec(memory_space=pl.ANY),
        scratch_shapes=[pltpu.SemaphoreType.DMA],
    )
    return pl.pallas_call(
        functools.partial(_scatter_rows_kernel, tm=tm),
        out_shape=jax.ShapeDtypeStruct((n_rows, d), hin2.dtype),
        grid_spec=grid_spec,
        input_output_aliases={2: 0},
        compiler_params=_cparams("arbitrary"),
        name="moe_scatter_rows",
    )(dest_flat, hin2, jnp.zeros((n_rows, d), hin2.dtype))


def _ffn_kernel(eblk_ref, enb_ref, xs_hbm, wg_ref, wu_ref, wd_ref, ys_hbm,
                xbuf, ybuf, wgb, wub, wdb, isem, osem, *, blk, nb_total, n_exp):
    e = pl.program_id(0)
    n = enb_ref[e]
    b0 = eblk_ref[e]
    used = eblk_ref[n_exp - 1] + enb_ref[n_exp - 1]

    def in_cp(gb):
        sl = gb % 2
        return pltpu.make_async_copy(xs_hbm.at[pl.ds(gb * blk, blk), :], xbuf.at[sl], isem.at[sl])

    def out_cp(gb):
        sl = gb % 2
        return pltpu.make_async_copy(ybuf.at[sl], ys_hbm.at[pl.ds(gb * blk, blk), :], osem.at[sl])

    @pl.when(jnp.logical_and(e == 0, used > 0))
    def _():
        in_cp(0).start(priority=1)

    @pl.when(n > 0)
    def _():
        wgb[...] = wg_ref[0, 0].astype(BF16)
        wub[...] = wu_ref[0, 0].astype(BF16)
        wdb[...] = wd_ref[0, 0].astype(BF16)

        def body(j, c):
            gb = b0 + j
            sl = gb % 2

            @pl.when(gb + 1 < used)
            def _():
                in_cp(gb + 1).start(priority=1)

            in_cp(gb).wait()

            @pl.when(gb >= 2)
            def _():
                out_cp(gb - 2).wait()

            xb = jnp.concatenate(_unpack_bf16_pair(xbuf[sl]), axis=1)
            hg = jnp.dot(xb, wgb[...], preferred_element_type=F32)
            hu = jnp.dot(xb, wub[...], preferred_element_type=F32)
            hid = (_silu(hg) * hu).astype(BF16)
            ybuf[sl] = jnp.dot(hid, wdb[...], preferred_element_type=F32)
            out_cp(gb).start(priority=1)
            return c

        lax.fori_loop(0, n, body, 0)

    @pl.when(e == n_exp - 1)
    def _():
        @pl.when(used >= 2)
        def _():
            out_cp(used - 2).wait()

        @pl.when(used >= 1)
        def _():
            out_cp(used - 1).wait()

        ybuf[0] = jnp.zeros((blk, ybuf.shape[2]), F32)

        def fill(jb, c):
            cp = pltpu.make_async_copy(ybuf.at[0], ys_hbm.at[pl.ds(jb * blk, blk), :], osem.at[0])
            cp.start()
            cp.wait()
            return c

        lax.fori_loop(used, nb_total, fill, 0)


def _moe_ffn(eblk, enb, xs, w_gate, w_up, w_down, layer):
    n_rows = xs.shape[0]
    n_exp, d, ff = w_gate.shape[1], w_gate.shape[2], w_gate.shape[3]
    blk = MOE_BLOCK
    wmap = lambda e, eb, en: (layer, e, 0, 0)
    grid_spec = pltpu.PrefetchScalarGridSpec(
        num_scalar_prefetch=2,
        grid=(n_exp,),
        in_specs=[pl.BlockSpec(memory_space=pl.ANY),
                  pl.BlockSpec((1, 1, d, ff), wmap),
                  pl.BlockSpec((1, 1, d, ff), wmap),
                  pl.BlockSpec((1, 1, ff, d), wmap)],
        out_specs=pl.BlockSpec(memory_space=pl.ANY),
        scratch_shapes=[pltpu.VMEM((2, blk, d // 2), U32),
                        pltpu.VMEM((2, blk, d), F32),
                        pltpu.VMEM((d, ff), BF16),
                        pltpu.VMEM((d, ff), BF16),
                        pltpu.VMEM((ff, d), BF16),
                        pltpu.SemaphoreType.DMA((2,)),
                        pltpu.SemaphoreType.DMA((2,))],
    )
    return pl.pallas_call(
        functools.partial(_ffn_kernel, blk=blk, nb_total=n_rows // blk, n_exp=n_exp),
        out_shape=jax.ShapeDtypeStruct((n_rows, d), F32),
        grid_spec=grid_spec,
        compiler_params=_cparams("arbitrary"),
        name="moe_ffn",
    )(eblk, enb, xs, w_gate, w_up, w_down)


def _gather_ln_kernel(dest_ref, ys_hbm, rg_ref, x_ref, g_ref, lng_ref, lnb_ref, o_ref, ybuf, sem,
                      *, alpha, tm, nt, n_steps):
    step = pl.program_id(0) * nt + pl.program_id(1)
    slot = step % 2

    def start_gather(tile_idx, sl):
        base = tile_idx * tm

        def group(gi, c):
            for u in range(DMA_UNROLL):
                r = gi * DMA_UNROLL + u
                for k in range(MOE_TOPK):
                    pltpu.make_async_copy(ys_hbm.at[pl.ds(dest_ref[(base + r) * MOE_TOPK + k], 1), :],
                                          ybuf.at[sl, k, pl.ds(r, 1), :], sem.at[sl]).start()
            return c

        lax.fori_loop(0, tm // DMA_UNROLL, group, 0)

    @pl.when(step == 0)
    def _():
        start_gather(0, 0)

    @pl.when(step + 1 < n_steps)
    def _():
        start_gather(step + 1, 1 - slot)

    for k in range(MOE_TOPK):
        pltpu.make_async_copy(ys_hbm.at[pl.ds(0, tm), :], ybuf.at[slot, k], sem.at[slot]).wait()

    rg = rg_ref[0]
    y = ybuf[slot, 0] * rg[:, 0:1] + ybuf[slot, 1] * rg[:, 1:2]
    r = alpha * x_ref[0] + (1.0 + g_ref[0]) * y
    o_ref[0] = _layer_norm_rows(r, lng_ref[...], lnb_ref[...])


def _gather_combine_ln(dest_flat, ys, route_gates, x, gate, ln_g, ln_b, alpha, name):
    b, s, d = x.shape
    tm = _tile(s, 256)
    nt = s // tm
    row = lambda bi, i, dst: (bi, i, 0)
    const = lambda bi, i, dst: (0, 0)
    grid_spec = pltpu.PrefetchScalarGridSpec(
        num_scalar_prefetch=1,
        grid=(b, nt),
        in_specs=[pl.BlockSpec(memory_space=pl.ANY),
                  pl.BlockSpec((1, tm, LANES), row),
                  pl.BlockSpec((1, tm, d), row),
                  pl.BlockSpec((1, 1, d), lambda bi, i, dst: (bi, 0, 0)),
                  pl.BlockSpec((1, d), const),
                  pl.BlockSpec((1, d), const)],
        out_specs=pl.BlockSpec((1, tm, d), row),
        scratch_shapes=[pltpu.VMEM((2, MOE_TOPK, tm, d), F32), pltpu.SemaphoreType.DMA((2,))],
    )
    return pl.pallas_call(
        functools.partial(_gather_ln_kernel, alpha=alpha, tm=tm, nt=nt, n_steps=b * nt),
        out_shape=jax.ShapeDtypeStruct((b, s, d), F32),
        grid_spec=grid_spec,
        compiler_params=_cparams("arbitrary", "arbitrary"),
        name=name,
    )(dest_flat, ys, route_gates, x, gate, ln_g.reshape(1, d), ln_b.reshape(1, d))


def _gdn_kernel(q_ref, k_ref, v_ref, z_ref, bg_ref, cg_ref, hh_ref, ba_ref,
                cw_ref, alog_ref, dtb_ref, nw_ref, scw_ref,
                o_ref,
                ext_ref, qa_ref, ka_ref, va_ref, hist_ref, hist2_ref, gate_ref, state_ref,
                *, heads, ts, gw):
    L = GDN_CHUNK
    dk = GDN_HEAD_DIM

    @pl.when(pl.program_id(1) == 0)
    def _():
        hist_ref[...] = jnp.zeros_like(hist_ref)
        hist2_ref[...] = jnp.zeros_like(hist2_ref)
        state_ref[...] = jnp.zeros_like(state_ref)

    def causal_conv(x, hist, w, taps):
        ext_ref[0:SUBLANES, :] = hist
        ext_ref[SUBLANES:SUBLANES + ts, :] = x
        y = w[0:1] * ext_ref[SUBLANES - taps + 1:SUBLANES - taps + 1 + ts, :]
        for j in range(1, taps - 1):
            y = y + w[j:j + 1] * ext_ref[SUBLANES - taps + 1 + j:SUBLANES - taps + 1 + j + ts, :]
        return y + w[taps - 1:taps] * x

    for idx, (src, dst) in enumerate(((q_ref, qa_ref), (k_ref, ka_ref), (v_ref, va_ref))):
        x = src[0]
        y = causal_conv(x, hist_ref[idx], cw_ref[:, idx * gw:(idx + 1) * gw], GDN_CONV)
        hist_ref[idx] = x[ts - SUBLANES:ts, :]
        dst[...] = _silu(y)

    c = cg_ref[0] * hh_ref[0]
    yb = causal_conv(c, hist2_ref[...], scw_ref[...], SC_CONV)
    hist2_ref[...] = c[ts - SUBLANES:ts, :]
    o_ref[0, :, gw:2 * gw] = bg_ref[0] * yb

    ba = ba_ref[0]
    gate_ref[0] = _sigmoid(ba)
    gate_ref[1] = -jnp.exp(alog_ref[...]) * _softplus(ba + dtb_ref[...])

    ii = lax.broadcasted_iota(I32, (L, L), 0)
    jj = lax.broadcasted_iota(I32, (L, L), 1)
    causal = ii >= jj
    strict = ii > jj
    tri_incl = causal.astype(F32)
    eye = (ii == jj).astype(F32)
    nchunk = ts // L
    pairs = [(c, h) for c in range(nchunk) for h in range(heads)]

    def tiles(ref, lead=()):
        return jnp.stack([ref[lead + (slice(c * L, (c + 1) * L), slice(h * dk, (h + 1) * dk))]
                          for c, h in pairs], axis=0)

    def bmm(a, b):
        return jnp.einsum('bij,bjk->bik', a.astype(BF16), b.astype(BF16), preferred_element_type=F32)

    def bmm_nt(a, b):
        return jnp.einsum('bik,bjk->bij', a.astype(BF16), b.astype(BF16), preferred_element_type=F32)

    qh = tiles(qa_ref)
    kh = tiles(ka_ref)
    vh = tiles(va_ref)
    beta = jnp.stack([gate_ref[0, c * L:(c + 1) * L, h:h + 1] for c, h in pairs], axis=0)
    cums = [jnp.dot(tri_incl, gate_ref[1, c * L:(c + 1) * L, :], precision=HIGHEST,
                    preferred_element_type=F32) for c in range(nchunk)]
    cums_t = [cm.T for cm in cums]
    gc = jnp.stack([cums[c][:, heads + h:heads + h + 1] for c, h in pairs], axis=0)
    gr = jnp.stack([cums_t[c][heads + h:heads + h + 1, :] for c, h in pairs], axis=0)
    gl = gc[:, L - 1:L, :]
    decay = jnp.where(causal, jnp.exp(jnp.where(causal, gc - gr, 0.0)), 0.0)
    egc = jnp.exp(gc)
    qn = qh * lax.rsqrt(jnp.sum(qh * qh, -1, keepdims=True) + 1e-6)
    kn = kh * lax.rsqrt(jnp.sum(kh * kh, -1, keepdims=True) + 1e-6)
    qc = qn * (dk ** -0.5)
    kb = kn * beta
    m = jnp.where(strict, bmm_nt(kb, kn) * decay, 0.0)
    t_inv = eye - m
    mp = m
    for _ in range(5):
        mp = bmm(mp, mp)
        t_inv = t_inv + bmm(t_inv, mp)
    sol = bmm(t_inv, jnp.concatenate([vh * beta, kb * egc], axis=2))
    u = sol[:, :, 0:dk]
    w = sol[:, :, dk:2 * dk]
    qk = jnp.where(causal, bmm_nt(qc, kn) * decay, 0.0)
    qg = qc * egc
    k_dec = kn * jnp.exp(gl - gc)
    egl = jnp.exp(gl)

    state = [state_ref[h] for h in range(heads)]
    outs = []
    for b, (c, h) in enumerate(pairs):
        st = state[h]
        v_new = u[b] - _dot(w[b], st)
        outs.append(_dot(qg[b], st) + _dot(qk[b], v_new))
        state[h] = st * egl[b] + _dot(k_dec[b].T, v_new)
    for h in range(heads):
        state_ref[h] = state[h]

    o = jnp.stack(outs, axis=0)
    on = o * lax.rsqrt(jnp.mean(o * o, -1, keepdims=True) + 1e-6) * nw_ref[...]
    y = on * _silu(tiles(z_ref, (0,)))
    for b, (c, h) in enumerate(pairs):
        o_ref[0, c * L:(c + 1) * L, h * dk:(h + 1) * dk] = y[b]


def _gdn_shortconv(proj, conv_w, a_log, dt_bias, norm_w, sc_w, heads):
    b, s, _ = proj.shape
    gw = heads * GDN_HEAD_DIM
    ts = _tile(s, 256)
    alog_p = jnp.zeros((1, LANES), F32).at[0, heads:2 * heads].set(a_log)
    dtb_p = jnp.zeros((1, LANES), F32).at[0, heads:2 * heads].set(dt_bias)
    col = lambda cb: (lambda bi, i: (bi, i, cb))
    const = lambda bi, i: (0, 0)
    kern = functools.partial(_gdn_kernel, heads=heads, ts=ts, gw=gw)
    return pl.pallas_call(
        kern,
        out_shape=jax.ShapeDtypeStruct((b, s, 2 * gw), F32),
        grid=(b, s // ts),
        in_specs=[pl.BlockSpec((1, ts, gw), col(cb)) for cb in range(7)]
        + [pl.BlockSpec((1, ts, LANES), col(7 * gw // LANES)),
           pl.BlockSpec((GDN_CONV, 3 * gw), const),
           pl.BlockSpec((1, LANES), const),
           pl.BlockSpec((1, LANES), const),
           pl.BlockSpec((1, GDN_HEAD_DIM), const),
           pl.BlockSpec((SC_CONV, gw), const)],
        out_specs=pl.BlockSpec((1, ts, 2 * gw), lambda bi, i: (bi, i, 0)),
        scratch_shapes=[pltpu.VMEM((ts + SUBLANES, gw), F32),
                        pltpu.VMEM((ts, gw), F32),
                        pltpu.VMEM((ts, gw), F32),
                        pltpu.VMEM((ts, gw), F32),
                        pltpu.VMEM((3, SUBLANES, gw), F32),
                        pltpu.VMEM((SUBLANES, gw), F32),
                        pltpu.VMEM((2, ts, LANES), F32),
                        pltpu.VMEM((heads, GDN_HEAD_DIM, GDN_HEAD_DIM), F32)],
        compiler_params=_cparams("parallel", "arbitrary"),
        name="gdn_shortconv",
    )(proj, proj, proj, proj, proj, proj, proj, proj,
      conv_w, alog_p, dtb_p, norm_w.reshape(1, GDN_HEAD_DIM), sc_w)


def _rope_kernel(pos_ref, inv_ref, qpe_ref, kpe_ref, qo_ref, ko_ref, *, heads):
    ang = pos_ref[0].astype(F32) * inv_ref[...]
    lane = lax.broadcasted_iota(I32, ang.shape, 1)
    half = MLA_ROPE // 2
    cos = jnp.where(lane < MLA_ROPE, jnp.cos(ang), 0.0)
    sin = jnp.sin(ang)
    sgn = jnp.where(lane < half, -sin, jnp.where(lane < MLA_ROPE, sin, 0.0))

    def rot(x):
        swapped = jnp.where(lane < half, pltpu.roll(x, LANES - half, 1), pltpu.roll(x, half, 1))
        return x * cos + swapped * sgn

    for h in range(heads):
        cols = slice(h * LANES, (h + 1) * LANES)
        qo_ref[0, :, cols] = rot(qpe_ref[0, :, cols])
    ko_ref[0] = rot(kpe_ref[0])


def _rope(positions, qup, heads, cproj, kpe_block):
    b, s = positions.shape
    ts = _tile(s, 512)
    half = MLA_ROPE // 2
    inv = ROPE_THETA ** (-jnp.arange(half, dtype=F32) * (2.0 / MLA_ROPE))
    inv_tab = jnp.zeros((1, LANES), F32).at[0, 0:half].set(inv).at[0, half:MLA_ROPE].set(inv)
    w = heads * LANES
    return pl.pallas_call(
        functools.partial(_rope_kernel, heads=heads),
        out_shape=(jax.ShapeDtypeStruct((b, s, w), F32), jax.ShapeDtypeStruct((b, s, LANES), F32)),
        grid=(b, s // ts),
        in_specs=[pl.BlockSpec((1, ts, 1), lambda bi, i: (bi, i, 0)),
                  pl.BlockSpec((1, LANES), lambda bi, i: (0, 0)),
                  pl.BlockSpec((1, ts, w), lambda bi, i: (bi, i, 1)),
                  pl.BlockSpec((1, ts, LANES), lambda bi, i: (bi, i, kpe_block))],
        out_specs=(pl.BlockSpec((1, ts, w), lambda bi, i: (bi, i, 0)),
                   pl.BlockSpec((1, ts, LANES), lambda bi, i: (bi, i, 0))),
        compiler_params=_cparams("parallel", "parallel"),
        name="mla_rope",
    )(positions.reshape(b, s, 1), inv_tab, qup, cproj)


def _attn_kernel(qn_ref, qp_ref, kn_ref, kp_ref, v_ref, o_ref, kcat_ref, vb_ref, s_ref, *, tq, nq, scale):
    qi = pl.program_id(2)

    @pl.when(qi == 0)
    def _():
        kcat_ref[:, 0:MLA_NOPE] = kn_ref[0].astype(BF16)
        kcat_ref[:, MLA_NOPE:MLA_NOPE + LANES] = kp_ref[0].astype(BF16)
        vb_ref[...] = v_ref[0].astype(BF16)

    qcat = jnp.concatenate([qn_ref[0], qp_ref[0]], axis=1).astype(BF16)
    row = lax.broadcasted_iota(I32, (tq, tq), 0)
    col = lax.broadcasted_iota(I32, (tq, tq), 1)

    for qv in range(nq):
        @pl.when(qi == qv)
        def _(qv=qv):
            m = None
            for kb in range(qv + 1):
                ks = slice(kb * tq, (kb + 1) * tq)
                s = lax.dot_general(qcat, kcat_ref[ks, :], (((1,), (1,)), ((), ())),
                                    preferred_element_type=F32) * scale
                if kb == qv:
                    s = jnp.where(row >= col, s, -jnp.inf)
                s_ref[:, ks] = s
                mb = jnp.max(s, -1, keepdims=True)
                m = mb if m is None else jnp.maximum(m, mb)
            l = jnp.zeros((tq, 1), F32)
            acc = jnp.zeros((tq, MLA_V), F32)
            for kb in range(qv + 1):
                ks = slice(kb * tq, (kb + 1) * tq)
                p = jnp.exp(s_ref[:, ks] - m)
                l = l + jnp.sum(p, -1, keepdims=True)
                acc = acc + jnp.dot(p.astype(BF16), vb_ref[ks, :], preferred_element_type=F32)
            o_ref[0] = acc / l


def _attention(qup, qpe, kvup, kpe, heads):
    b, s, _ = qup.shape
    tq = _tile(s, 512)
    nq = s // tq
    scale = (MLA_NOPE + MLA_ROPE) ** -0.5
    return pl.pallas_call(
        functools.partial(_attn_kernel, tq=tq, nq=nq, scale=scale),
        out_shape=jax.ShapeDtypeStruct((b, s, heads * MLA_V), F32),
        grid=(b, heads, nq),
        in_specs=[pl.BlockSpec((1, tq, MLA_NOPE), lambda bi, h, i: (bi, i, h)),
                  pl.BlockSpec((1, tq, LANES), lambda bi, h, i: (bi, i, h)),
                  pl.BlockSpec((1, s, MLA_NOPE), lambda bi, h, i: (bi, 0, h)),
                  pl.BlockSpec((1, s, LANES), lambda bi, h, i: (bi, 0, 0)),
                  pl.BlockSpec((1, s, MLA_V), lambda bi, h, i: (bi, 0, heads + h))],
        out_specs=pl.BlockSpec((1, tq, MLA_V), lambda bi, h, i: (bi, i, h)),
        scratch_shapes=[pltpu.VMEM((s, MLA_NOPE + LANES), BF16), pltpu.VMEM((s, MLA_V), BF16),
                        pltpu.VMEM((tq, s), F32)],
        compiler_params=_cparams("parallel", "parallel", "arbitrary"),
        name="mla_attention",
    )(qup, qpe, kvup, kpe, kvup)


def _hyb_in_weight(w_in, heads):
    d = w_in.shape[0]
    gw = heads * GDN_HEAD_DIM
    main = jnp.concatenate([w_in[:, :4 * gw], w_in[:, 4 * gw + 2 * heads:]], axis=1)
    gates = w_in[:, 4 * gw:4 * gw + 2 * heads]
    n = main.shape[1] + LANES
    n_pad = -(-n // 768) * 768
    tail = jnp.zeros((d, n_pad - main.shape[1]), w_in.dtype).at[:, :2 * heads].set(gates)
    return jnp.concatenate([main, tail], axis=1).astype(BF16)


def _mla_in_weight(w_in):
    d = w_in.shape[0]
    pad = jnp.zeros((d, LANES - MLA_ROPE), w_in.dtype)
    return jnp.concatenate([w_in, pad], axis=1).astype(BF16)


def _mla_uq_weight(w_uq, heads):
    r = w_uq.shape[0]
    w = w_uq.reshape(r, heads, MLA_NOPE + MLA_ROPE)
    nope = w[:, :, :MLA_NOPE].reshape(r, heads * MLA_NOPE)
    pe = jnp.concatenate([w[:, :, MLA_NOPE:], jnp.zeros((r, heads, LANES - MLA_ROPE), w.dtype)], axis=2)
    return jnp.concatenate([nope, pe.reshape(r, heads * LANES)], axis=1).astype(BF16)


def _mla_ukv_weight(w_ukv, heads):
    r = w_ukv.shape[0]
    w = w_ukv.reshape(r, heads, MLA_NOPE + MLA_V)
    return jnp.concatenate([w[:, :, :MLA_NOPE].reshape(r, heads * MLA_NOPE),
                            w[:, :, MLA_NOPE:].reshape(r, heads * MLA_V)], axis=1).astype(BF16)


def _router_weight(wr_g, br_g, wr_e, br_e):
    d = wr_g.shape[0]
    n = wr_g.shape[1] + wr_e.shape[1]
    w = jnp.zeros((d, LANES), F32).at[:, :n].set(jnp.concatenate([wr_g, wr_e], axis=1))
    bias = jnp.zeros((1, LANES), F32).at[0, :n].set(jnp.concatenate([br_g, br_e]))
    w_hi = w.astype(BF16)
    w_lo = (w - w_hi.astype(F32)).astype(BF16)
    return jnp.stack([w_hi, w_lo]), bias


def kernel(x, c, positions, ada_w, ada_b, ln_g, ln_b, hyb_w_in, gdn_conv_w, gdn_a_log, gdn_dt_bias,
           gdn_norm_w, sc_conv_w, hyb_w_out, mla_w_in, mla_q_norm, mla_kv_norm, mla_w_uq, mla_w_ukv,
           mla_w_out, moe_router_g, moe_bias_g, moe_router_e, moe_bias_e, moe_w_gate, moe_w_up,
           moe_w_down):
    b, s, d = x.shape
    depth = ada_w.shape[0]
    alpha = (2.0 * depth) ** 0.25
    gdn_heads = d // (2 * GDN_HEAD_DIM)
    mla_heads = mla_w_out.shape[1] // MLA_V
    n_groups = moe_router_g.shape[2]
    n_exp = moe_router_e.shape[2]
    per_group = n_exp // n_groups

    mod = _modulation(c, ada_w, ada_b)
    for layer in range(depth):
        sh1, sc1, g1, sh2, sc2, g2 = [mod[layer, :, j * d:(j + 1) * d].reshape(b, 1, d) for j in range(6)]
        i = layer // 2
        wr, br = _router_weight(moe_router_g[layer], moe_bias_g[layer], moe_router_e[layer], moe_bias_e[layer])
        if layer % 2 == 0:
            proj = _mod_matmul(x, sc1, sh1, _hyb_in_weight(hyb_w_in[i], gdn_heads), "hyb_in_proj")
            y = _gdn_shortconv(proj, gdn_conv_w[i], gdn_a_log[i], gdn_dt_bias[i], gdn_norm_w[i],
                               sc_conv_w[i], gdn_heads)
            w_out = hyb_w_out[i]
        else:
            cproj = _mod_matmul(x, sc1, sh1, _mla_in_weight(mla_w_in[i]), "mla_in_proj")
            qup = _rms_matmul(cproj, 0, MLA_Q_RANK, mla_q_norm[i], _mla_uq_weight(mla_w_uq[i], mla_heads),
                              "mla_q_up")
            kvup = _rms_matmul(cproj, 1, MLA_KV_RANK, mla_kv_norm[i], _mla_ukv_weight(mla_w_ukv[i], mla_heads),
                               "mla_kv_up")
            qpe, kpe = _rope(positions, qup, mla_heads, cproj, (MLA_Q_RANK + MLA_KV_RANK) // LANES)
            y = _attention(qup, qpe, kvup, kpe, mla_heads)
            w_out = mla_w_out[i]
        x, hin2, ids, gates = _out_ln_route(y, w_out.astype(BF16), x, g1, ln_g[layer, 0], ln_b[layer, 0],
                                            sc2, sh2, wr, br, alpha, n_groups, per_group,
                                            "mixer_out_ln_route")
        t = b * s
        dest, meta = _dispatch(ids.reshape(t, LANES))
        dest_flat = dest[:, :MOE_TOPK].reshape(-1)
        eblk = meta[0, :n_exp] // MOE_BLOCK
        enb = meta[1, :n_exp]
        n_rows = t * MOE_TOPK + n_exp * MOE_BLOCK
        xs = _scatter_rows(dest_flat, hin2.reshape(t, d // 2), n_rows)
        ys = _moe_ffn(eblk, enb, xs, moe_w_gate, moe_w_up, moe_w_down, layer)
        x = _gather_combine_ln(dest_flat, ys, gates, x, g2, ln_g[layer, 1], ln_b[layer, 1], alpha,
                               "moe_gather_combine_ln")
    return x
```

```python
import functools

import jax
import jax.numpy as jnp
from jax import lax
from jax.experimental import pallas as pl
from jax.experimental.pallas import tpu as pltpu

F32, BF16, I32, U32 = jnp.float32, jnp.bfloat16, jnp.int32, jnp.uint32
HIGHEST = lax.Precision.HIGHEST

LANES = 128
SUBLANES = 8
VMEM_LIMIT_BYTES = 56 * 1024 * 1024

GDN_HEAD_DIM = 128
GDN_CONV = 4
GDN_CHUNK = 64
SC_CONV = 3
MLA_NOPE = 128
MLA_ROPE = 64
MLA_V = 128
MLA_Q_RANK = 512
MLA_KV_RANK = 512
ROPE_THETA = 10000.0
MOE_TOPK = 2
MOE_BLOCK = 128
DMA_UNROLL = 8
FFN_RING = 6


def _tile(n, pref):
    if n <= pref:
        return n
    for t in range(pref, 0, -LANES):
        if n % t == 0:
            return t
    return n


def _cparams(*sem):
    return pltpu.CompilerParams(dimension_semantics=sem, vmem_limit_bytes=VMEM_LIMIT_BYTES)


def _sigmoid(x):
    return 1.0 / (1.0 + jnp.exp(-x))


def _silu(x):
    return x * _sigmoid(x)


def _softplus(x):
    return jnp.maximum(x, 0.0) + jnp.log(1.0 + jnp.exp(-jnp.abs(x)))


def _pack_bf16_pair(lo, hi):
    lo_bits = lax.bitcast_convert_type(lo.astype(BF16).astype(F32), U32)
    hi_bits = lax.bitcast_convert_type(hi.astype(BF16).astype(F32), U32)
    return (lo_bits >> 16) | (hi_bits & jnp.uint32(0xFFFF0000))


def _unpack_bf16_pair(w):
    lo = lax.bitcast_convert_type(w << 16, F32)
    hi = lax.bitcast_convert_type(w & jnp.uint32(0xFFFF0000), F32)
    return lo.astype(BF16), hi.astype(BF16)


def _dot(a, b):
    return jnp.dot(a.astype(BF16), b.astype(BF16), preferred_element_type=F32)


def _dot_nt(a, b):
    return lax.dot_general(a.astype(BF16), b.astype(BF16), (((1,), (1,)), ((), ())),
                           preferred_element_type=F32)


def _mod_kernel(c_ref, w_ref, b_ref, o_ref):
    c = c_ref[...]
    o_ref[0] = _dot(_silu(c), w_ref[0]) + b_ref[0]


def _modulation(c, ada_w, ada_b):
    depth, d, n = ada_w.shape
    b = c.shape[0]
    rows = -(-b // SUBLANES) * SUBLANES
    cp = jnp.zeros((rows, d), F32).at[:b].set(c)
    tn = _tile(n, 1024)
    out = pl.pallas_call(
        _mod_kernel,
        out_shape=jax.ShapeDtypeStruct((depth, rows, n), F32),
        grid=(depth, n // tn),
        in_specs=[pl.BlockSpec((rows, d), lambda l, j: (0, 0)),
                  pl.BlockSpec((1, d, tn), lambda l, j: (l, 0, j)),
                  pl.BlockSpec((1, 1, tn), lambda l, j: (l, 0, j))],
        out_specs=pl.BlockSpec((1, rows, tn), lambda l, j: (l, 0, j)),
        compiler_params=_cparams("parallel", "parallel"),
        name="adaln_mod",
    )(cp, ada_w, ada_b.reshape(depth, 1, n))
    return out[:, :b]


def _modmm_kernel(x_ref, sc_ref, sh_ref, w_ref, o_ref, xb_ref):
    @pl.when(pl.program_id(2) == 0)
    def _():
        xb_ref[...] = (x_ref[0] * (1.0 + sc_ref[0]) + sh_ref[0]).astype(BF16)

    o_ref[0] = jnp.dot(xb_ref[...], w_ref[...], preferred_element_type=F32)


def _mod_matmul(x, sc, sh, w, name):
    b, s, d = x.shape
    n = w.shape[1]
    tm = _tile(s, 1024)
    tn = _tile(n, 768)
    return pl.pallas_call(
        _modmm_kernel,
        out_shape=jax.ShapeDtypeStruct((b, s, n), F32),
        grid=(b, s // tm, n // tn),
        in_specs=[pl.BlockSpec((1, tm, d), lambda bi, i, j: (bi, i, 0)),
                  pl.BlockSpec((1, 1, d), lambda bi, i, j: (bi, 0, 0)),
                  pl.BlockSpec((1, 1, d), lambda bi, i, j: (bi, 0, 0)),
                  pl.BlockSpec((d, tn), lambda bi, i, j: (0, j))],
        out_specs=pl.BlockSpec((1, tm, tn), lambda bi, i, j: (bi, i, j)),
        scratch_shapes=[pltpu.VMEM((tm, d), BF16)],
        compiler_params=_cparams("parallel", "parallel", "arbitrary"),
        name=name,
    )(x, sc, sh, w)


def _rmsmm_kernel(x_ref, nw_ref, w_ref, o_ref, xb_ref):
    @pl.when(pl.program_id(2) == 0)
    def _():
        x = x_ref[0]
        xb_ref[...] = (x * lax.rsqrt(jnp.mean(x * x, -1, keepdims=True) + 1e-6) * nw_ref[...]).astype(BF16)

    o_ref[0] = jnp.dot(xb_ref[...], w_ref[...], preferred_element_type=F32)


def _rms_matmul(x, col_block, k, norm_w, w, name):
    b, s, _ = x.shape
    n = w.shape[1]
    tm = _tile(s, 1024)
    tn = _tile(n, 1024)
    return pl.pallas_call(
        _rmsmm_kernel,
        out_shape=jax.ShapeDtypeStruct((b, s, n), F32),
        grid=(b, s // tm, n // tn),
        in_specs=[pl.BlockSpec((1, tm, k), lambda bi, i, j: (bi, i, col_block)),
                  pl.BlockSpec((1, k), lambda bi, i, j: (0, 0)),
                  pl.BlockSpec((k, tn), lambda bi, i, j: (0, j))],
        out_specs=pl.BlockSpec((1, tm, tn), lambda bi, i, j: (bi, i, j)),
        scratch_shapes=[pltpu.VMEM((tm, k), BF16)],
        compiler_params=_cparams("parallel", "parallel", "arbitrary"),
        name=name,
    )(x, norm_w.reshape(1, k), w)


def _layer_norm_rows(r, g, b):
    mu = jnp.mean(r, -1, keepdims=True)
    var = jnp.mean(jnp.square(r - mu), -1, keepdims=True)
    return (r - mu) * lax.rsqrt(var + 1e-5) * g + b


def _outln_kernel(y_ref, w_ref, xres_ref, g_ref, lng_ref, lnb_ref, sc_ref, sh_ref, wr_ref, br_ref,
                  xo_ref, hin_ref, ids_ref, gates_ref, *, alpha, n_groups, per_group):
    y = jnp.dot(y_ref[0].astype(BF16), w_ref[...], preferred_element_type=F32)
    r = alpha * xres_ref[0] + (1.0 + g_ref[0]) * y
    xn = _layer_norm_rows(r, lng_ref[...], lnb_ref[...])
    xo_ref[0] = xn
    hin = xn * (1.0 + sc_ref[0]) + sh_ref[0]
    half = hin.shape[1] // 2
    hin_ref[0] = _pack_bf16_pair(hin[:, :half], hin[:, half:])
    h_hi = hin.astype(BF16)
    h_lo = (hin - h_hi.astype(F32)).astype(BF16)
    w_hi = wr_ref[0]
    logits = (jnp.dot(h_hi, w_hi, preferred_element_type=F32)
              + jnp.dot(h_hi, wr_ref[1], preferred_element_type=F32)
              + jnp.dot(h_lo, w_hi, preferred_element_type=F32)) + br_ref[...]
    lane = lax.broadcasted_iota(I32, logits.shape, 1)
    big = jnp.int32(4 * LANES)
    neg = jnp.float32(-jnp.inf)
    n_exp = n_groups * per_group
    gmask = lane < n_groups
    lg = jnp.where(gmask, logits, neg)
    mg = jnp.max(lg, -1, keepdims=True)
    grp = jnp.min(jnp.where(gmask & (lg == mg), lane, big), -1, keepdims=True)
    pg_sel = 1.0 / jnp.sum(jnp.where(gmask, jnp.exp(lg - mg), 0.0), -1, keepdims=True)
    lo = n_groups + grp * per_group
    emask = (lane >= lo) & (lane < lo + per_group) & (lane < n_groups + n_exp)
    le = jnp.where(emask, logits, neg)
    me = jnp.max(le, -1, keepdims=True)
    ee = jnp.where(emask, jnp.exp(le - me), 0.0)
    p = ee / jnp.sum(ee, -1, keepdims=True)
    pm = jnp.where(emask, p, -1.0)
    p1 = jnp.max(pm, -1, keepdims=True)
    i1 = jnp.min(jnp.where(emask & (pm == p1), lane, big), -1, keepdims=True)
    pm2 = jnp.where(lane == i1, -1.0, pm)
    p2 = jnp.max(pm2, -1, keepdims=True)
    i2 = jnp.min(jnp.where(emask & (lane != i1) & (pm2 == p2), lane, big), -1, keepdims=True)
    den = p1 + p2
    g1 = pg_sel * p1 / den
    g2 = pg_sel * p2 / den
    ids_ref[0] = jnp.where(lane == 0, i1 - n_groups, jnp.where(lane == 1, i2 - n_groups, 0))
    gates_ref[0] = jnp.where(lane == 0, g1, jnp.where(lane == 1, g2, 0.0))


def _out_ln_route(y, w, xres, gate, ln_g, ln_b, sc, sh, wr, br, alpha, n_groups, per_group, name):
    b, s, k = y.shape
    d = w.shape[1]
    tm = _tile(s, 256)
    kern = functools.partial(_outln_kernel, alpha=alpha, n_groups=n_groups, per_group=per_group)
    row = lambda bi, i: (bi, i, 0)
    per_b = lambda bi, i: (bi, 0, 0)
    const = lambda bi, i: (0, 0)
    return pl.pallas_call(
        kern,
        out_shape=(jax.ShapeDtypeStruct((b, s, d), F32), jax.ShapeDtypeStruct((b, s, d // 2), U32),
                   jax.ShapeDtypeStruct((b, s, LANES), I32), jax.ShapeDtypeStruct((b, s, LANES), F32)),
        grid=(b, s // tm),
        in_specs=[pl.BlockSpec((1, tm, k), row),
                  pl.BlockSpec((k, d), const),
                  pl.BlockSpec((1, tm, d), row),
                  pl.BlockSpec((1, 1, d), per_b),
                  pl.BlockSpec((1, d), const),
                  pl.BlockSpec((1, d), const),
                  pl.BlockSpec((1, 1, d), per_b),
                  pl.BlockSpec((1, 1, d), per_b),
                  pl.BlockSpec((2, d, LANES), lambda bi, i: (0, 0, 0)),
                  pl.BlockSpec((1, LANES), const)],
        out_specs=(pl.BlockSpec((1, tm, d), row), pl.BlockSpec((1, tm, d // 2), row),
                   pl.BlockSpec((1, tm, LANES), row), pl.BlockSpec((1, tm, LANES), row)),
        compiler_params=_cparams("parallel", "parallel"),
        name=name,
    )(y, w, xres, gate, ln_g.reshape(1, d), ln_b.reshape(1, d), sc, sh, wr, br)


def _combine_kernel(ya_ref, yb_ref, rg_ref, x_ref, g_ref, lng_ref, lnb_ref, o_ref, *, alpha):
    rg = rg_ref[0]
    y = ya_ref[0] * rg[:, 0:1] + yb_ref[0] * rg[:, 1:2]
    r = alpha * x_ref[0] + (1.0 + g_ref[0]) * y
    o_ref[0] = _layer_norm_rows(r, lng_ref[...], lnb_ref[...])


def _combine_ln(y2, route_gates, x, gate, ln_g, ln_b, alpha, name):
    b, s, d = x.shape
    tm = _tile(s, 512)
    row = lambda bi, i: (bi, i, 0)
    nt = s // tm
    return pl.pallas_call(
        functools.partial(_combine_kernel, alpha=alpha),
        out_shape=jax.ShapeDtypeStruct((b, s, d), F32),
        grid=(b, nt),
        in_specs=[pl.BlockSpec((1, tm, d), lambda bi, i: (0, bi * nt + i, 0)),
                  pl.BlockSpec((1, tm, d), lambda bi, i: (1, bi * nt + i, 0)),
                  pl.BlockSpec((1, tm, LANES), row),
                  pl.BlockSpec((1, tm, d), row),
                  pl.BlockSpec((1, 1, d), lambda bi, i: (bi, 0, 0)),
                  pl.BlockSpec((1, d), lambda bi, i: (0, 0)),
                  pl.BlockSpec((1, d), lambda bi, i: (0, 0))],
        out_specs=pl.BlockSpec((1, tm, d), row),
        compiler_params=_cparams("parallel", "parallel"),
        name=name,
    )(y2, y2, route_gates, x, gate, ln_g.reshape(1, d), ln_b.reshape(1, d))


def _moe_kernel(bexp_ref, bcnt_ref, rsrc_ref, rdst_ref,
                hin_hbm, wg_ref, wu_ref, wd_ref,
                y2_hbm,
                xbuf, ybuf, wgb, wub, wdb, gsem, ssem, *, blk, nb, tp):
    i = pl.program_id(0)
    slot = i % 2

    def start_gather(block, sl):
        base = block * blk

        def group(gi, c):
            for u in range(DMA_UNROLL):
                r = gi * DMA_UNROLL + u
                pltpu.make_async_copy(hin_hbm.at[pl.ds(rsrc_ref[base + r], 1), :],
                                      xbuf.at[sl, pl.ds(r, 1), :], gsem.at[sl]).start()
            return c

        lax.fori_loop(0, blk // DMA_UNROLL, group, 0)

    def wait_gather(sl):
        pltpu.make_async_copy(hin_hbm.at[pl.ds(0, blk), :], xbuf.at[sl], gsem.at[sl]).wait()

    def start_scatter(block, sl):
        base = block * blk

        def group(gi, c):
            for u in range(DMA_UNROLL):
                r = gi * DMA_UNROLL + u
                pltpu.make_async_copy(ybuf.at[sl, pl.ds(r, 1), :],
                                      y2_hbm.at[pl.ds(rdst_ref[base + r], 1), :], ssem.at[sl]).start()
            return c

        lax.fori_loop(0, blk // DMA_UNROLL, group, 0)

    def wait_scatter(sl):
        pltpu.make_async_copy(ybuf.at[sl], y2_hbm.at[pl.ds(0, blk), :], ssem.at[sl]).wait()

    used = bcnt_ref[i] > 0

    @pl.when(i == 0)
    def _():
        ybuf[...] = jnp.zeros_like(ybuf)
        for k in range(MOE_TOPK):
            for sl in range(2):
                cp = pltpu.make_async_copy(ybuf.at[sl], y2_hbm.at[pl.ds(k * tp + tp - (2 - sl) * blk, blk), :],
                                           ssem.at[sl])
                cp.start()
                cp.wait()

    @pl.when(jnp.logical_and(i == 0, used))
    def _():
        start_gather(0, 0)

    nxt = jnp.minimum(i + 1, nb - 1)

    @pl.when(jnp.logical_and(i + 1 < nb, bcnt_ref[nxt] > 0))
    def _():
        start_gather(nxt, 1 - slot)

    changed = jnp.logical_or(i == 0, bexp_ref[i] != bexp_ref[jnp.maximum(i - 1, 0)])

    @pl.when(jnp.logical_and(changed, used))
    def _():
        wgb[...] = wg_ref[0, 0].astype(BF16)
        wub[...] = wu_ref[0, 0].astype(BF16)
        wdb[...] = wd_ref[0, 0].astype(BF16)

    @pl.when(jnp.logical_and(i >= 2, bcnt_ref[jnp.maximum(i - 2, 0)] > 0))
    def _():
        wait_scatter(slot)

    @pl.when(used)
    def _():
        wait_gather(slot)
        xb = xbuf[slot].astype(BF16)
        hg = jnp.dot(xb, wgb[...], preferred_element_type=F32)
        hu = jnp.dot(xb, wub[...], preferred_element_type=F32)
        hid = (_silu(hg) * hu).astype(BF16)
        ybuf[slot] = jnp.dot(hid, wdb[...], preferred_element_type=F32)
        start_scatter(i, slot)

    @pl.when(i == nb - 1)
    def _():
        if nb >= 2:
            @pl.when(bcnt_ref[jnp.maximum(i - 1, 0)] > 0)
            def _():
                wait_scatter(1 - slot)

        @pl.when(used)
        def _():
            wait_scatter(slot)


def _moe_experts(hin2, w_gate, w_up, w_down, layer, blk_exp, blk_cnt, row_src, row_dst, name):
    t, d = hin2.shape
    ff = w_gate.shape[3]
    nb = blk_exp.shape[0]
    blk = MOE_BLOCK
    tp = t + 2 * blk
    wmap = lambda i, be, bc, rs, rd: (layer, be[i], 0, 0)
    grid_spec = pltpu.PrefetchScalarGridSpec(
        num_scalar_prefetch=4,
        grid=(nb,),
        in_specs=[pl.BlockSpec(memory_space=pl.ANY),
                  pl.BlockSpec((1, 1, d, ff), wmap),
                  pl.BlockSpec((1, 1, d, ff), wmap),
                  pl.BlockSpec((1, 1, ff, d), wmap)],
        out_specs=pl.BlockSpec(memory_space=pl.ANY),
        scratch_shapes=[pltpu.VMEM((2, blk, d), F32),
                        pltpu.VMEM((2, blk, d), F32),
                        pltpu.VMEM((d, ff), BF16),
                        pltpu.VMEM((d, ff), BF16),
                        pltpu.VMEM((ff, d), BF16),
                        pltpu.SemaphoreType.DMA((2,)),
                        pltpu.SemaphoreType.DMA((2,))],
    )
    y2 = pl.pallas_call(
        functools.partial(_moe_kernel, blk=blk, nb=nb, tp=tp),
        out_shape=jax.ShapeDtypeStruct((MOE_TOPK * tp, d), F32),
        grid_spec=grid_spec,
        compiler_params=_cparams("arbitrary"),
        name=name,
    )(blk_exp, blk_cnt, row_src, row_dst, hin2, w_gate, w_up, w_down)
    return y2.reshape(MOE_TOPK, tp, d)


def _dispatch_tables(ids, n_exp):
    t = ids.shape[0]
    n_assign = t * MOE_TOPK
    blk = MOE_BLOCK
    tp = t + 2 * blk
    e_flat = ids.reshape(-1)
    order = jnp.argsort(e_flat).astype(I32)
    e_sorted = e_flat[order]
    counts = jnp.bincount(e_flat, length=n_exp).astype(I32)
    padded = (counts + blk - 1) // blk * blk
    pad_end = jnp.cumsum(padded)
    pad_start = pad_end - padded
    start = jnp.cumsum(counts) - counts
    dest = pad_start[e_sorted] + jnp.arange(n_assign, dtype=I32) - start[e_sorted]
    n_rows = n_assign + n_exp * blk
    nb = n_rows // blk
    rows = jnp.arange(n_rows, dtype=I32)
    spare = t + ((rows // blk) % 2) * blk + rows % blk
    row_src = jnp.zeros((n_rows,), I32).at[dest].set(order // MOE_TOPK)
    row_dst = spare.at[dest].set((order % MOE_TOPK) * tp + order // MOE_TOPK)
    blk_first = jnp.arange(nb, dtype=I32) * blk
    blk_exp = jnp.minimum(jnp.searchsorted(pad_end, blk_first, side='right'), n_exp - 1).astype(I32)
    in_use = blk_first < pad_end[-1]
    blk_cnt = jnp.where(in_use, jnp.clip(counts[blk_exp] - (blk_first - pad_start[blk_exp]), 0, blk), 0)
    return blk_exp, blk_cnt.astype(I32), row_src, row_dst


def _dispatch_kernel(ids_ref, dest_ref, meta_ref, rank_ref, *, t, tile, blk_shift):
    lane = lax.broadcasted_iota(I32, (tile, LANES), 1)
    ri = lax.broadcasted_iota(I32, (tile, tile), 0)
    ci = lax.broadcasted_iota(I32, (tile, tile), 1)
    before = (ri > ci).astype(BF16)

    def hits(rows):
        ids = ids_ref[rows, :]
        return lane == ids[:, 0:1], lane == ids[:, 1:2]

    def count(ti, run):
        rows = pl.ds(pl.multiple_of(ti * tile, tile), tile)
        h1, h2 = hits(rows)
        onehot = jnp.logical_or(h1, h2).astype(BF16)
        prefix = jnp.dot(before, onehot, preferred_element_type=F32) + run
        r1 = jnp.sum(jnp.where(h1, prefix, 0.0), -1, keepdims=True)
        r2 = jnp.sum(jnp.where(h2, prefix, 0.0), -1, keepdims=True)
        rank_ref[rows, :] = jnp.where(lane == 0, r1, jnp.where(lane == 1, r2, 0.0))
        return run + jnp.sum(onehot.astype(F32), 0, keepdims=True)

    counts = lax.fori_loop(0, t // tile, count, jnp.zeros((1, LANES), F32))
    cnt = jnp.broadcast_to(counts, (SUBLANES, LANES)).astype(I32)
    nblk = (cnt + ((1 << blk_shift) - 1)) >> blk_shift
    padded = nblk << blk_shift
    lane8 = lax.broadcasted_iota(I32, (SUBLANES, LANES), 1)
    incl = padded
    step = 1
    while step < LANES:
        incl = incl + jnp.where(lane8 >= step, pltpu.roll(incl, step, 1), 0)
        step *= 2
    pad_start = incl - padded
    row8 = lax.broadcasted_iota(I32, (SUBLANES, LANES), 0)
    meta_ref[...] = jnp.where(row8 == 0, pad_start, jnp.where(row8 == 1, nblk, cnt))
    start_f = pad_start[0:1, :].astype(F32)

    def place(ti, c):
        rows = pl.ds(pl.multiple_of(ti * tile, tile), tile)
        h1, h2 = hits(rows)
        rk = rank_ref[rows, :]
        d1 = jnp.sum(jnp.where(h1, start_f, 0.0), -1, keepdims=True) + rk[:, 0:1]
        d2 = jnp.sum(jnp.where(h2, start_f, 0.0), -1, keepdims=True) + rk[:, 1:2]
        dest_ref[rows, :] = jnp.where(lane == 0, d1, jnp.where(lane == 1, d2, 0.0)).astype(I32)
        return c

    lax.fori_loop(0, t // tile, place, 0)


def _dispatch(ids):
    t = ids.shape[0]
    tile = _tile(t, 256)
    blk_shift = MOE_BLOCK.bit_length() - 1
    assert (1 << blk_shift) == MOE_BLOCK
    return pl.pallas_call(
        functools.partial(_dispatch_kernel, t=t, tile=tile, blk_shift=blk_shift),
        out_shape=(jax.ShapeDtypeStruct((t, LANES), I32), jax.ShapeDtypeStruct((SUBLANES, LANES), I32)),
        grid=(1,),
        in_specs=[pl.BlockSpec((t, LANES), lambda i: (0, 0))],
        out_specs=(pl.BlockSpec((t, LANES), lambda i: (0, 0)), pl.BlockSpec((SUBLANES, LANES), lambda i: (0, 0))),
        scratch_shapes=[pltpu.VMEM((t, LANES), F32)],
        compiler_params=_cparams("arbitrary"),
        name="moe_dispatch",
    )(ids)


def _scatter_rows_kernel(dest_ref, hin_ref, xs_in, xs_out, sem, *, tm):
    del xs_in
    base = pl.program_id(0) * tm

    def group(gi, c):
        for u in range(DMA_UNROLL):
            r = gi * DMA_UNROLL + u
            for k in range(MOE_TOPK):
                pltpu.make_async_copy(hin_ref.at[pl.ds(r, 1), :],
                                      xs_out.at[pl.ds(dest_ref[(base + r) * MOE_TOPK + k], 1), :], sem).start()
        return c

    lax.fori_loop(0, tm // DMA_UNROLL, group, 0)
    for _ in range(MOE_TOPK):
        pltpu.make_async_copy(hin_ref, xs_out.at[pl.ds(0, tm), :], sem).wait()


def _scatter_rows(dest_flat, hin2, n_rows):
    t, d = hin2.shape
    tm = _tile(t, 256)
    grid_spec = pltpu.PrefetchScalarGridSpec(
        num_scalar_prefetch=1,
        grid=(t // tm,),
        in_specs=[pl.BlockSpec((tm, d), lambda i, dst: (i, 0)),
                  pl.BlockSpec(memory_space=pl.ANY)],
        out_specs=pl.BlockSpec(memory_space=pl.ANY),
        scratch_shapes=[pltpu.SemaphoreType.DMA],
    )
    return pl.pallas_call(
        functools.partial(_scatter_rows_kernel, tm=tm),
        out_shape=jax.ShapeDtypeStruct((n_rows, d), hin2.dtype),
        grid_spec=grid_spec,
        input_output_aliases={2: 0},
        compiler_params=_cparams("arbitrary"),
        name="moe_scatter_rows",
    )(dest_flat, hin2, jnp.zeros((n_rows, d), hin2.dtype))


def _ffn_kernel(eblk_ref, enb_ref, xs_hbm, wg_ref, wu_ref, wd_ref, ys_hbm,
                xbuf, ybuf, wgb, wub, wdb, isem, osem, *, blk, nb_total, n_exp):
    e = pl.program_id(0)
    n = enb_ref[e]
    b0 = eblk_ref[e]
    used = eblk_ref[n_exp - 1] + enb_ref[n_exp - 1]

    def in_cp(gb):
        sl = gb % FFN_RING
        return pltpu.make_async_copy(xs_hbm.at[pl.ds(gb * blk, blk), :], xbuf.at[sl], isem.at[sl])

    def out_cp(gb):
        sl = gb % FFN_RING
        return pltpu.make_async_copy(ybuf.at[sl], ys_hbm.at[pl.ds(gb * blk, blk), :], osem.at[sl])

    @pl.when(e == 0)
    def _():
        for k in range(FFN_RING - 1):
            @pl.when(k < used)
            def _(k=k):
                in_cp(k).start(priority=1)

    @pl.when(n > 0)
    def _():
        wgb[...] = wg_ref[0, 0].astype(BF16)
        wub[...] = wu_ref[0, 0].astype(BF16)
        wdb[...] = wd_ref[0, 0].astype(BF16)

        def body(j, c):
            gb = b0 + j
            sl = gb % FFN_RING

            @pl.when(gb + FFN_RING - 1 < used)
            def _():
                in_cp(gb + FFN_RING - 1).start(priority=1)

            in_cp(gb).wait()

            @pl.when(gb >= FFN_RING)
            def _():
                out_cp(gb - FFN_RING).wait()

            xb = jnp.concatenate(_unpack_bf16_pair(xbuf[sl]), axis=1)
            hg = jnp.dot(xb, wgb[...], preferred_element_type=F32)
            hu = jnp.dot(xb, wub[...], preferred_element_type=F32)
            hid = (_silu(hg) * hu).astype(BF16)
            ybuf[sl] = jnp.dot(hid, wdb[...], preferred_element_type=F32)
            out_cp(gb).start(priority=1)
            return c

        lax.fori_loop(0, n, body, 0)

    @pl.when(e == n_exp - 1)
    def _():
        for k in range(FFN_RING, 0, -1):
            @pl.when(used >= k)
            def _(k=k):
                out_cp(used - k).wait()

        ybuf[0] = jnp.zeros((blk, ybuf.shape[2]), F32)

        def fill(jb, c):
            cp = pltpu.make_async_copy(ybuf.at[0], ys_hbm.at[pl.ds(jb * blk, blk), :], osem.at[0])
            cp.start()
            cp.wait()
            return c

        lax.fori_loop(used, nb_total, fill, 0)


def _moe_ffn(eblk, enb, xs, w_gate, w_up, w_down, layer):
    n_rows = xs.shape[0]
    n_exp, d, ff = w_gate.shape[1], w_gate.shape[2], w_gate.shape[3]
    blk = MOE_BLOCK
    wmap = lambda e, eb, en: (layer, e, 0, 0)
    grid_spec = pltpu.PrefetchScalarGridSpec(
        num_scalar_prefetch=2,
        grid=(n_exp,),
        in_specs=[pl.BlockSpec(memory_space=pl.ANY),
                  pl.BlockSpec((1, 1, d, ff), wmap),
                  pl.BlockSpec((1, 1, d, ff), wmap),
                  pl.BlockSpec((1, 1, ff, d), wmap)],
        out_specs=pl.BlockSpec(memory_space=pl.ANY),
        scratch_shapes=[pltpu.VMEM((FFN_RING, blk, d // 2), U32),
                        pltpu.VMEM((FFN_RING, blk, d), F32),
                        pltpu.VMEM((d, ff), BF16),
                        pltpu.VMEM((d, ff), BF16),
                        pltpu.VMEM((ff, d), BF16),
                        pltpu.SemaphoreType.DMA((FFN_RING,)),
                        pltpu.SemaphoreType.DMA((FFN_RING,))],
    )
    return pl.pallas_call(
        functools.partial(_ffn_kernel, blk=blk, nb_total=n_rows // blk, n_exp=n_exp),
        out_shape=jax.ShapeDtypeStruct((n_rows, d), F32),
        grid_spec=grid_spec,
        compiler_params=_cparams("arbitrary"),
        name="moe_ffn",
    )(eblk, enb, xs, w_gate, w_up, w_down)


def _gather_ln_kernel(dest_ref, ys_hbm, rg_ref, x_ref, g_ref, lng_ref, lnb_ref, o_ref, ybuf, sem,
                      *, alpha, tm, nt, n_steps):
    step = pl.program_id(0) * nt + pl.program_id(1)
    slot = step % 2

    def start_gather(tile_idx, sl):
        base = tile_idx * tm

        def group(gi, c):
            for u in range(DMA_UNROLL):
                r = gi * DMA_UNROLL + u
                for k in range(MOE_TOPK):
                    pltpu.make_async_copy(ys_hbm.at[pl.ds(dest_ref[(base + r) * MOE_TOPK + k], 1), :],
                                          ybuf.at[sl, k, pl.ds(r, 1), :], sem.at[sl]).start()
            return c

        lax.fori_loop(0, tm // DMA_UNROLL, group, 0)

    @pl.when(step == 0)
    def _():
        start_gather(0, 0)

    @pl.when(step + 1 < n_steps)
    def _():
        start_gather(step + 1, 1 - slot)

    for k in range(MOE_TOPK):
        pltpu.make_async_copy(ys_hbm.at[pl.ds(0, tm), :], ybuf.at[slot, k], sem.at[slot]).wait()

    rg = rg_ref[0]
    y = ybuf[slot, 0] * rg[:, 0:1] + ybuf[slot, 1] * rg[:, 1:2]
    r = alpha * x_ref[0] + (1.0 + g_ref[0]) * y
    o_ref[0] = _layer_norm_rows(r, lng_ref[...], lnb_ref[...])


def _gather_combine_ln(dest_flat, ys, route_gates, x, gate, ln_g, ln_b, alpha, name):
    b, s, d = x.shape
    tm = _tile(s, 256)
    nt = s // tm
    row = lambda bi, i, dst: (bi, i, 0)
    const = lambda bi, i, dst: (0, 0)
    grid_spec = pltpu.PrefetchScalarGridSpec(
        num_scalar_prefetch=1,
        grid=(b, nt),
        in_specs=[pl.BlockSpec(memory_space=pl.ANY),
                  pl.BlockSpec((1, tm, LANES), row),
                  pl.BlockSpec((1, tm, d), row),
                  pl.BlockSpec((1, 1, d), lambda bi, i, dst: (bi, 0, 0)),
                  pl.BlockSpec((1, d), const),
                  pl.BlockSpec((1, d), const)],
        out_specs=pl.BlockSpec((1, tm, d), row),
        scratch_shapes=[pltpu.VMEM((2, MOE_TOPK, tm, d), F32), pltpu.SemaphoreType.DMA((2,))],
    )
    return pl.pallas_call(
        functools.partial(_gather_ln_kernel, alpha=alpha, tm=tm, nt=nt, n_steps=b * nt),
        out_shape=jax.ShapeDtypeStruct((b, s, d), F32),
        grid_spec=grid_spec,
        compiler_params=_cparams("arbitrary", "arbitrary"),
        name=name,
    )(dest_flat, ys, route_gates, x, gate, ln_g.reshape(1, d), ln_b.reshape(1, d))


def _gdn_kernel(q_ref, k_ref, v_ref, z_ref, bg_ref, cg_ref, hh_ref, ba_ref,
                cw_ref, alog_ref, dtb_ref, nw_ref, scw_ref,
                o_ref,
                ext_ref, qa_ref, ka_ref, va_ref, hist_ref, hist2_ref, gate_ref, state_ref,
                *, heads, ts, gw):
    L = GDN_CHUNK
    dk = GDN_HEAD_DIM

    @pl.when(pl.program_id(1) == 0)
    def _():
        hist_ref[...] = jnp.zeros_like(hist_ref)
        hist2_ref[...] = jnp.zeros_like(hist2_ref)
        state_ref[...] = jnp.zeros_like(state_ref)

    def causal_conv(x, hist, w, taps):
        ext_ref[0:SUBLANES, :] = hist
        ext_ref[SUBLANES:SUBLANES + ts, :] = x
        y = w[0:1] * ext_ref[SUBLANES - taps + 1:SUBLANES - taps + 1 + ts, :]
        for j in range(1, taps - 1):
            y = y + w[j:j + 1] * ext_ref[SUBLANES - taps + 1 + j:SUBLANES - taps + 1 + j + ts, :]
        return y + w[taps - 1:taps] * x

    for idx, (src, dst) in enumerate(((q_ref, qa_ref), (k_ref, ka_ref), (v_ref, va_ref))):
        x = src[0]
        y = causal_conv(x, hist_ref[idx], cw_ref[:, idx * gw:(idx + 1) * gw], GDN_CONV)
        hist_ref[idx] = x[ts - SUBLANES:ts, :]
        dst[...] = _silu(y)

    c = cg_ref[0] * hh_ref[0]
    yb = causal_conv(c, hist2_ref[...], scw_ref[...], SC_CONV)
    hist2_ref[...] = c[ts - SUBLANES:ts, :]
    o_ref[0, :, gw:2 * gw] = bg_ref[0] * yb

    ba = ba_ref[0]
    gate_ref[0] = _sigmoid(ba)
    gate_ref[1] = -jnp.exp(alog_ref[...]) * _softplus(ba + dtb_ref[...])

    ii = lax.broadcasted_iota(I32, (L, L), 0)
    jj = lax.broadcasted_iota(I32, (L, L), 1)
    causal = ii >= jj
    strict = ii > jj
    tri_incl = causal.astype(F32)
    eye = (ii == jj).astype(F32)
    nchunk = ts // L
    pairs = [(c, h) for c in range(nchunk) for h in range(heads)]

    def tiles(ref, lead=()):
        return jnp.stack([ref[lead + (slice(c * L, (c + 1) * L), slice(h * dk, (h + 1) * dk))]
                          for c, h in pairs], axis=0)

    def bmm(a, b):
        return jnp.einsum('bij,bjk->bik', a.astype(BF16), b.astype(BF16), preferred_element_type=F32)

    def bmm_nt(a, b):
        return jnp.einsum('bik,bjk->bij', a.astype(BF16), b.astype(BF16), preferred_element_type=F32)

    qh = tiles(qa_ref)
    kh = tiles(ka_ref)
    vh = tiles(va_ref)
    beta = jnp.stack([gate_ref[0, c * L:(c + 1) * L, h:h + 1] for c, h in pairs], axis=0)
    cums = [jnp.dot(tri_incl, gate_ref[1, c * L:(c + 1) * L, :], precision=HIGHEST,
                    preferred_element_type=F32) for c in range(nchunk)]
    cums_t = [cm.T for cm in cums]
    gc = jnp.stack([cums[c][:, heads + h:heads + h + 1] for c, h in pairs], axis=0)
    gr = jnp.stack([cums_t[c][heads + h:heads + h + 1, :] for c, h in pairs], axis=0)
    gl = gc[:, L - 1:L, :]
    decay = jnp.where(causal, jnp.exp(jnp.where(causal, gc - gr, 0.0)), 0.0)
    egc = jnp.exp(gc)
    qn = qh * lax.rsqrt(jnp.sum(qh * qh, -1, keepdims=True) + 1e-6)
    kn = kh * lax.rsqrt(jnp.sum(kh * kh, -1, keepdims=True) + 1e-6)
    qc = qn * (dk ** -0.5)
    kb = kn * beta
    m = jnp.where(strict, bmm_nt(kb, kn) * decay, 0.0)
    t_inv = eye - m
    mp = m
    for _ in range(5):
        mp = bmm(mp, mp)
        t_inv = t_inv + bmm(t_inv, mp)
    sol = bmm(t_inv, jnp.concatenate([vh * beta, kb * egc], axis=2))
    u = sol[:, :, 0:dk]
    w = sol[:, :, dk:2 * dk]
    qk = jnp.where(causal, bmm_nt(qc, kn) * decay, 0.0)
    qg = qc * egc
    k_dec = kn * jnp.exp(gl - gc)
    egl = jnp.exp(gl)

    state = [state_ref[h] for h in range(heads)]
    outs = []
    for b, (c, h) in enumerate(pairs):
        st = state[h]
        v_new = u[b] - _dot(w[b], st)
        outs.append(_dot(qg[b], st) + _dot(qk[b], v_new))
        state[h] = st * egl[b] + _dot(k_dec[b].T, v_new)
    for h in range(heads):
        state_ref[h] = state[h]

    o = jnp.stack(outs, axis=0)
    on = o * lax.rsqrt(jnp.mean(o * o, -1, keepdims=True) + 1e-6) * nw_ref[...]
    y = on * _silu(tiles(z_ref, (0,)))
    for b, (c, h) in enumerate(pairs):
        o_ref[0, c * L:(c + 1) * L, h * dk:(h + 1) * dk] = y[b]


def _gdn_shortconv(proj, conv_w, a_log, dt_bias, norm_w, sc_w, heads):
    b, s, _ = proj.shape
    gw = heads * GDN_HEAD_DIM
    ts = _tile(s, 256)
    alog_p = jnp.zeros((1, LANES), F32).at[0, heads:2 * heads].set(a_log)
    dtb_p = jnp.zeros((1, LANES), F32).at[0, heads:2 * heads].set(dt_bias)
    col = lambda cb: (lambda bi, i: (bi, i, cb))
    const = lambda bi, i: (0, 0)
    kern = functools.partial(_gdn_kernel, heads=heads, ts=ts, gw=gw)
    return pl.pallas_call(
        kern,
        out_shape=jax.ShapeDtypeStruct((b, s, 2 * gw), F32),
        grid=(b, s // ts),
        in_specs=[pl.BlockSpec((1, ts, gw), col(cb)) for cb in range(7)]
        + [pl.BlockSpec((1, ts, LANES), col(7 * gw // LANES)),
           pl.BlockSpec((GDN_CONV, 3 * gw), const),
           pl.BlockSpec((1, LANES), const),
           pl.BlockSpec((1, LANES), const),
           pl.BlockSpec((1, GDN_HEAD_DIM), const),
           pl.BlockSpec((SC_CONV, gw), const)],
        out_specs=pl.BlockSpec((1, ts, 2 * gw), lambda bi, i: (bi, i, 0)),
        scratch_shapes=[pltpu.VMEM((ts + SUBLANES, gw), F32),
                        pltpu.VMEM((ts, gw), F32),
                        pltpu.VMEM((ts, gw), F32),
                        pltpu.VMEM((ts, gw), F32),
                        pltpu.VMEM((3, SUBLANES, gw), F32),
                        pltpu.VMEM((SUBLANES, gw), F32),
                        pltpu.VMEM((2, ts, LANES), F32),
                        pltpu.VMEM((heads, GDN_HEAD_DIM, GDN_HEAD_DIM), F32)],
        compiler_params=_cparams("parallel", "arbitrary"),
        name="gdn_shortconv",
    )(proj, proj, proj, proj, proj, proj, proj, proj,
      conv_w, alog_p, dtb_p, norm_w.reshape(1, GDN_HEAD_DIM), sc_w)


def _rope_kernel(pos_ref, inv_ref, qpe_ref, kpe_ref, qo_ref, ko_ref, *, heads):
    ang = pos_ref[0].astype(F32) * inv_ref[...]
    lane = lax.broadcasted_iota(I32, ang.shape, 1)
    half = MLA_ROPE // 2
    cos = jnp.where(lane < MLA_ROPE, jnp.cos(ang), 0.0)
    sin = jnp.sin(ang)
    sgn = jnp.where(lane < half, -sin, jnp.where(lane < MLA_ROPE, sin, 0.0))

    def rot(x):
        swapped = jnp.where(lane < half, pltpu.roll(x, LANES - half, 1), pltpu.roll(x, half, 1))
        return x * cos + swapped * sgn

    for h in range(heads):
        cols = slice(h * LANES, (h + 1) * LANES)
        qo_ref[0, :, cols] = rot(qpe_ref[0, :, cols])
    ko_ref[0] = rot(kpe_ref[0])


def _rope(positions, qup, heads, cproj, kpe_block):
    b, s = positions.shape
    ts = _tile(s, 512)
    half = MLA_ROPE // 2
    inv = ROPE_THETA ** (-jnp.arange(half, dtype=F32) * (2.0 / MLA_ROPE))
    inv_tab = jnp.zeros((1, LANES), F32).at[0, 0:half].set(inv).at[0, half:MLA_ROPE].set(inv)
    w = heads * LANES
    return pl.pallas_call(
        functools.partial(_rope_kernel, heads=heads),
        out_shape=(jax.ShapeDtypeStruct((b, s, w), F32), jax.ShapeDtypeStruct((b, s, LANES), F32)),
        grid=(b, s // ts),
        in_specs=[pl.BlockSpec((1, ts, 1), lambda bi, i: (bi, i, 0)),
                  pl.BlockSpec((1, LANES), lambda bi, i: (0, 0)),
                  pl.BlockSpec((1, ts, w), lambda bi, i: (bi, i, 1)),
                  pl.BlockSpec((1, ts, LANES), lambda bi, i: (bi, i, kpe_block))],
        out_specs=(pl.BlockSpec((1, ts, w), lambda bi, i: (bi, i, 0)),
                   pl.BlockSpec((1, ts, LANES), lambda bi, i: (bi, i, 0))),
        compiler_params=_cparams("parallel", "parallel"),
        name="mla_rope",
    )(positions.reshape(b, s, 1), inv_tab, qup, cproj)


def _attn_kernel(qn_ref, qp_ref, kn_ref, kp_ref, v_ref, o_ref, kcat_ref, vb_ref, s_ref, *, tq, nq, scale):
    qi = pl.program_id(2)

    @pl.when(qi == 0)
    def _():
        kcat_ref[:, 0:MLA_NOPE] = kn_ref[0].astype(BF16)
        kcat_ref[:, MLA_NOPE:MLA_NOPE + LANES] = kp_ref[0].astype(BF16)
        vb_ref[...] = v_ref[0].astype(BF16)

    qcat = jnp.concatenate([qn_ref[0], qp_ref[0]], axis=1).astype(BF16)
    row = lax.broadcasted_iota(I32, (tq, tq), 0)
    col = lax.broadcasted_iota(I32, (tq, tq), 1)

    for qv in range(nq):
        @pl.when(qi == qv)
        def _(qv=qv):
            m = None
            for kb in range(qv + 1):
                ks = slice(kb * tq, (kb + 1) * tq)
                s = lax.dot_general(qcat, kcat_ref[ks, :], (((1,), (1,)), ((), ())),
                                    preferred_element_type=F32) * scale
                if kb == qv:
                    s = jnp.where(row >= col, s, -jnp.inf)
                s_ref[:, ks] = s
                mb = jnp.max(s, -1, keepdims=True)
                m = mb if m is None else jnp.maximum(m, mb)
            l = jnp.zeros((tq, 1), F32)
            acc = jnp.zeros((tq, MLA_V), F32)
            for kb in range(qv + 1):
                ks = slice(kb * tq, (kb + 1) * tq)
                p = jnp.exp(s_ref[:, ks] - m)
                l = l + jnp.sum(p, -1, keepdims=True)
                acc = acc + jnp.dot(p.astype(BF16), vb_ref[ks, :], preferred_element_type=F32)
            o_ref[0] = acc / l


def _attention(qup, qpe, kvup, kpe, heads):
    b, s, _ = qup.shape
    tq = _tile(s, 512)
    nq = s // tq
    scale = (MLA_NOPE + MLA_ROPE) ** -0.5
    return pl.pallas_call(
        functools.partial(_attn_kernel, tq=tq, nq=nq, scale=scale),
        out_shape=jax.ShapeDtypeStruct((b, s, heads * MLA_V), F32),
        grid=(b, heads, nq),
        in_specs=[pl.BlockSpec((1, tq, MLA_NOPE), lambda bi, h, i: (bi, i, h)),
                  pl.BlockSpec((1, tq, LANES), lambda bi, h, i: (bi, i, h)),
                  pl.BlockSpec((1, s, MLA_NOPE), lambda bi, h, i: (bi, 0, h)),
                  pl.BlockSpec((1, s, LANES), lambda bi, h, i: (bi, 0, 0)),
                  pl.BlockSpec((1, s, MLA_V), lambda bi, h, i: (bi, 0, heads + h))],
        out_specs=pl.BlockSpec((1, tq, MLA_V), lambda bi, h, i: (bi, i, h)),
        scratch_shapes=[pltpu.VMEM((s, MLA_NOPE + LANES), BF16), pltpu.VMEM((s, MLA_V), BF16),
                        pltpu.VMEM((tq, s), F32)],
        compiler_params=_cparams("parallel", "parallel", "arbitrary"),
        name="mla_attention",
    )(qup, qpe, kvup, kpe, kvup)


def _hyb_in_weight(w_in, heads):
    d = w_in.shape[0]
    gw = heads * GDN_HEAD_DIM
    main = jnp.concatenate([w_in[:, :4 * gw], w_in[:, 4 * gw + 2 * heads:]], axis=1)
    gates = w_in[:, 4 * gw:4 * gw + 2 * heads]
    n = main.shape[1] + LANES
    n_pad = -(-n // 768) * 768
    tail = jnp.zeros((d, n_pad - main.shape[1]), w_in.dtype).at[:, :2 * heads].set(gates)
    return jnp.concatenate([main, tail], axis=1).astype(BF16)


def _mla_in_weight(w_in):
    d = w_in.shape[0]
    pad = jnp.zeros((d, LANES - MLA_ROPE), w_in.dtype)
    return jnp.concatenate([w_in, pad], axis=1).astype(BF16)


def _mla_uq_weight(w_uq, heads):
    r = w_uq.shape[0]
    w = w_uq.reshape(r, heads, MLA_NOPE + MLA_ROPE)
    nope = w[:, :, :MLA_NOPE].reshape(r, heads * MLA_NOPE)
    pe = jnp.concatenate([w[:, :, MLA_NOPE:], jnp.zeros((r, heads, LANES - MLA_ROPE), w.dtype)], axis=2)
    return jnp.concatenate([nope, pe.reshape(r, heads * LANES)], axis=1).astype(BF16)


def _mla_ukv_weight(w_ukv, heads):
    r = w_ukv.shape[0]
    w = w_ukv.reshape(r, heads, MLA_NOPE + MLA_V)
    return jnp.concatenate([w[:, :, :MLA_NOPE].reshape(r, heads * MLA_NOPE),
                            w[:, :, MLA_NOPE:].reshape(r, heads * MLA_V)], axis=1).astype(BF16)


def _router_weight(wr_g, br_g, wr_e, br_e):
    d = wr_g.shape[0]
    n = wr_g.shape[1] + wr_e.shape[1]
    w = jnp.zeros((d, LANES), F32).at[:, :n].set(jnp.concatenate([wr_g, wr_e], axis=1))
    bias = jnp.zeros((1, LANES), F32).at[0, :n].set(jnp.concatenate([br_g, br_e]))
    w_hi = w.astype(BF16)
    w_lo = (w - w_hi.astype(F32)).astype(BF16)
    return jnp.stack([w_hi, w_lo]), bias


def kernel(x, c, positions, ada_w, ada_b, ln_g, ln_b, hyb_w_in, gdn_conv_w, gdn_a_log, gdn_dt_bias,
           gdn_norm_w, sc_conv_w, hyb_w_out, mla_w_in, mla_q_norm, mla_kv_norm, mla_w_uq, mla_w_ukv,
           mla_w_out, moe_router_g, moe_bias_g, moe_router_e, moe_bias_e, moe_w_gate, moe_w_up,
           moe_w_down):
    b, s, d = x.shape
    depth = ada_w.shape[0]
    alpha = (2.0 * depth) ** 0.25
    gdn_heads = d // (2 * GDN_HEAD_DIM)
    mla_heads = mla_w_out.shape[1] // MLA_V
    n_groups = moe_router_g.shape[2]
    n_exp = moe_router_e.shape[2]
    per_group = n_exp // n_groups

    mod = _modulation(c, ada_w, ada_b)
    for layer in range(depth):
        sh1, sc1, g1, sh2, sc2, g2 = [mod[layer, :, j * d:(j + 1) * d].reshape(b, 1, d) for j in range(6)]
        i = layer // 2
        wr, br = _router_weight(moe_router_g[layer], moe_bias_g[layer], moe_router_e[layer], moe_bias_e[layer])
        if layer % 2 == 0:
            proj = _mod_matmul(x, sc1, sh1, _hyb_in_weight(hyb_w_in[i], gdn_heads), "hyb_in_proj")
            y = _gdn_shortconv(proj, gdn_conv_w[i], gdn_a_log[i], gdn_dt_bias[i], gdn_norm_w[i],
                               sc_conv_w[i], gdn_heads)
            w_out = hyb_w_out[i]
        else:
            cproj = _mod_matmul(x, sc1, sh1, _mla_in_weight(mla_w_in[i]), "mla_in_proj")
            qup = _rms_matmul(cproj, 0, MLA_Q_RANK, mla_q_norm[i], _mla_uq_weight(mla_w_uq[i], mla_heads),
                              "mla_q_up")
            kvup = _rms_matmul(cproj, 1, MLA_KV_RANK, mla_kv_norm[i], _mla_ukv_weight(mla_w_ukv[i], mla_heads),
                               "mla_kv_up")
            qpe, kpe = _rope(positions, qup, mla_heads, cproj, (MLA_Q_RANK + MLA_KV_RANK) // LANES)
            y = _attention(qup, qpe, kvup, kpe, mla_heads)
            w_out = mla_w_out[i]
        x, hin2, ids, gates = _out_ln_route(y, w_out.astype(BF16), x, g1, ln_g[layer, 0], ln_b[layer, 0],
                                            sc2, sh2, wr, br, alpha, n_groups, per_group,
                                            "mixer_out_ln_route")
        t = b * s
        dest, meta = _dispatch(ids.reshape(t, LANES))
        dest_flat = dest[:, :MOE_TOPK].reshape(-1)
        eblk = meta[0, :n_exp] // MOE_BLOCK
        enb = meta[1, :n_exp]
        n_rows = t * MOE_TOPK + n_exp * MOE_BLOCK
        xs = _scatter_rows(dest_flat, hin2.reshape(t, d // 2), n_rows)
        ys = _moe_ffn(eblk, enb, xs, moe_w_gate, moe_w_up, moe_w_down, layer)
        x = _gather_combine_ln(dest_flat, ys, gates, x, g2, ln_g[layer, 1], ln_b[layer, 1], alpha,
                               "moe_gather_combine_ln")
    return x
```

```python
import functools

import jax
import jax.numpy as jnp
from jax import lax
from jax.experimental import pallas as pl
from jax.experimental.pallas import tpu as pltpu

F32, BF16, I32, U32 = jnp.float32, jnp.bfloat16, jnp.int32, jnp.uint32
HIGHEST = lax.Precision.HIGHEST

LANES = 128
SUBLANES = 8
VMEM_LIMIT_BYTES = 56 * 1024 * 1024

GDN_HEAD_DIM = 128
GDN_CONV = 4
GDN_CHUNK = 64
SC_CONV = 3
MLA_NOPE = 128
MLA_ROPE = 64
MLA_V = 128
MLA_Q_RANK = 512
MLA_KV_RANK = 512
ROPE_THETA = 10000.0
MOE_TOPK = 2
MOE_BLOCK = 128
DMA_UNROLL = 8
FFN_RING = 6


def _tile(n, pref):
    if n <= pref:
        return n
    for t in range(pref, 0, -LANES):
        if n % t == 0:
            return t
    return n


def _cparams(*sem):
    return pltpu.CompilerParams(dimension_semantics=sem, vmem_limit_bytes=VMEM_LIMIT_BYTES)


def _sigmoid(x):
    return 1.0 / (1.0 + jnp.exp(-x))


def _silu(x):
    return x * _sigmoid(x)


def _softplus(x):
    return jnp.maximum(x, 0.0) + jnp.log(1.0 + jnp.exp(-jnp.abs(x)))


def _pack_bf16_pair(lo, hi):
    lo_bits = lax.bitcast_convert_type(lo.astype(BF16).astype(F32), U32)
    hi_bits = lax.bitcast_convert_type(hi.astype(BF16).astype(F32), U32)
    return (lo_bits >> 16) | (hi_bits & jnp.uint32(0xFFFF0000))


def _unpack_bf16_pair(w):
    lo = lax.bitcast_convert_type(w << 16, F32)
    hi = lax.bitcast_convert_type(w & jnp.uint32(0xFFFF0000), F32)
    return lo.astype(BF16), hi.astype(BF16)


def _dot(a, b):
    return jnp.dot(a.astype(BF16), b.astype(BF16), preferred_element_type=F32)


def _dot_nt(a, b):
    return lax.dot_general(a.astype(BF16), b.astype(BF16), (((1,), (1,)), ((), ())),
                           preferred_element_type=F32)


def _mod_kernel(c_ref, w_ref, b_ref, o_ref):
    c = c_ref[...]
    o_ref[0] = _dot(_silu(c), w_ref[0]) + b_ref[0]


def _modulation(c, ada_w, ada_b):
    depth, d, n = ada_w.shape
    b = c.shape[0]
    rows = -(-b // SUBLANES) * SUBLANES
    cp = jnp.zeros((rows, d), F32).at[:b].set(c)
    tn = _tile(n, 1024)
    out = pl.pallas_call(
        _mod_kernel,
        out_shape=jax.ShapeDtypeStruct((depth, rows, n), F32),
        grid=(depth, n // tn),
        in_specs=[pl.BlockSpec((rows, d), lambda l, j: (0, 0)),
                  pl.BlockSpec((1, d, tn), lambda l, j: (l, 0, j)),
                  pl.BlockSpec((1, 1, tn), lambda l, j: (l, 0, j))],
        out_specs=pl.BlockSpec((1, rows, tn), lambda l, j: (l, 0, j)),
        compiler_params=_cparams("parallel", "parallel"),
        name="adaln_mod",
    )(cp, ada_w, ada_b.reshape(depth, 1, n))
    return out[:, :b]


def _modmm_kernel(x_ref, sc_ref, sh_ref, w_ref, o_ref, xb_ref):
    @pl.when(pl.program_id(2) == 0)
    def _():
        xb_ref[...] = (x_ref[0] * (1.0 + sc_ref[0]) + sh_ref[0]).astype(BF16)

    o_ref[0] = jnp.dot(xb_ref[...], w_ref[...], preferred_element_type=F32)


def _mod_matmul(x, sc, sh, w, name):
    b, s, d = x.shape
    n = w.shape[1]
    tm = _tile(s, 1024)
    tn = _tile(n, 768)
    return pl.pallas_call(
        _modmm_kernel,
        out_shape=jax.ShapeDtypeStruct((b, s, n), F32),
        grid=(b, s // tm, n // tn),
        in_specs=[pl.BlockSpec((1, tm, d), lambda bi, i, j: (bi, i, 0)),
                  pl.BlockSpec((1, 1, d), lambda bi, i, j: (bi, 0, 0)),
                  pl.BlockSpec((1, 1, d), lambda bi, i, j: (bi, 0, 0)),
                  pl.BlockSpec((d, tn), lambda bi, i, j: (0, j))],
        out_specs=pl.BlockSpec((1, tm, tn), lambda bi, i, j: (bi, i, j)),
        scratch_shapes=[pltpu.VMEM((tm, d), BF16)],
        compiler_params=_cparams("parallel", "parallel", "arbitrary"),
        name=name,
    )(x, sc, sh, w)


def _rmsmm_kernel(x_ref, nw_ref, w_ref, o_ref, xb_ref):
    @pl.when(pl.program_id(2) == 0)
    def _():
        x = x_ref[0]
        xb_ref[...] = (x * lax.rsqrt(jnp.mean(x * x, -1, keepdims=True) + 1e-6) * nw_ref[...]).astype(BF16)

    o_ref[0] = jnp.dot(xb_ref[...], w_ref[...], preferred_element_type=F32)


def _rms_matmul(x, col_block, k, norm_w, w, name):
    b, s, _ = x.shape
    n = w.shape[1]
    tm = _tile(s, 1024)
    tn = _tile(n, 1024)
    return pl.pallas_call(
        _rmsmm_kernel,
        out_shape=jax.ShapeDtypeStruct((b, s, n), F32),
        grid=(b, s // tm, n // tn),
        in_specs=[pl.BlockSpec((1, tm, k), lambda bi, i, j: (bi, i, col_block)),
                  pl.BlockSpec((1, k), lambda bi, i, j: (0, 0)),
                  pl.BlockSpec((k, tn), lambda bi, i, j: (0, j))],
        out_specs=pl.BlockSpec((1, tm, tn), lambda bi, i, j: (bi, i, j)),
        scratch_shapes=[pltpu.VMEM((tm, k), BF16)],
        compiler_params=_cparams("parallel", "parallel", "arbitrary"),
        name=name,
    )(x, norm_w.reshape(1, k), w)


def _layer_norm_rows(r, g, b):
    mu = jnp.mean(r, -1, keepdims=True)
    var = jnp.mean(jnp.square(r - mu), -1, keepdims=True)
    return (r - mu) * lax.rsqrt(var + 1e-5) * g + b


def _outln_kernel(y_ref, w_ref, xres_ref, g_ref, lng_ref, lnb_ref, sc_ref, sh_ref, wr_ref, br_ref,
                  xo_ref, hin_ref, ids_ref, gates_ref, *, alpha, n_groups, per_group, parts):
    tp = y_ref.shape[1] // parts
    for part in range(parts):
        _outln_rows(slice(part * tp, (part + 1) * tp), y_ref, w_ref, xres_ref, g_ref, lng_ref, lnb_ref,
                    sc_ref, sh_ref, wr_ref, br_ref, xo_ref, hin_ref, ids_ref, gates_ref,
                    alpha=alpha, n_groups=n_groups, per_group=per_group)


def _outln_rows(rows, y_ref, w_ref, xres_ref, g_ref, lng_ref, lnb_ref, sc_ref, sh_ref, wr_ref, br_ref,
                xo_ref, hin_ref, ids_ref, gates_ref, *, alpha, n_groups, per_group):
    y = jnp.dot(y_ref[0, rows, :].astype(BF16), w_ref[...], preferred_element_type=F32)
    r = alpha * xres_ref[0, rows, :] + (1.0 + g_ref[0]) * y
    xn = _layer_norm_rows(r, lng_ref[...], lnb_ref[...])
    xo_ref[0, rows, :] = xn
    hin = xn * (1.0 + sc_ref[0]) + sh_ref[0]
    half = hin.shape[1] // 2
    hin_ref[0, rows, :] = _pack_bf16_pair(hin[:, :half], hin[:, half:])
    h_hi = hin.astype(BF16)
    h_lo = (hin - h_hi.astype(F32)).astype(BF16)
    ph = jnp.dot(h_hi, wr_ref[...], preferred_element_type=F32)
    plo = jnp.dot(h_lo, wr_ref[...], preferred_element_type=F32)
    logits = (ph[:, :LANES] + ph[:, LANES:]) + (plo[:, :LANES] + plo[:, LANES:]) + br_ref[...]
    lane = lax.broadcasted_iota(I32, logits.shape, 1)
    big = jnp.int32(4 * LANES)
    neg = jnp.float32(-jnp.inf)
    n_exp = n_groups * per_group
    gmask = lane < n_groups
    lg = jnp.where(gmask, logits, neg)
    mg = jnp.max(lg, -1, keepdims=True)
    grp = jnp.min(jnp.where(gmask & (lg == mg), lane, big), -1, keepdims=True)
    pg_sel = 1.0 / jnp.sum(jnp.where(gmask, jnp.exp(lg - mg), 0.0), -1, keepdims=True)
    lo = n_groups + grp * per_group
    emask = (lane >= lo) & (lane < lo + per_group) & (lane < n_groups + n_exp)
    le = jnp.where(emask, logits, neg)
    me = jnp.max(le, -1, keepdims=True)
    ee = jnp.where(emask, jnp.exp(le - me), 0.0)
    p = ee / jnp.sum(ee, -1, keepdims=True)
    pm = jnp.where(emask, p, -1.0)
    p1 = jnp.max(pm, -1, keepdims=True)
    i1 = jnp.min(jnp.where(emask & (pm == p1), lane, big), -1, keepdims=True)
    pm2 = jnp.where(lane == i1, -1.0, pm)
    p2 = jnp.max(pm2, -1, keepdims=True)
    i2 = jnp.min(jnp.where(emask & (lane != i1) & (pm2 == p2), lane, big), -1, keepdims=True)
    den = p1 + p2
    g1 = pg_sel * p1 / den
    g2 = pg_sel * p2 / den
    ids_ref[0, rows, :] = jnp.where(lane == 0, i1 - n_groups, jnp.where(lane == 1, i2 - n_groups, 0))
    gates_ref[0, rows, :] = jnp.where(lane == 0, g1, jnp.where(lane == 1, g2, 0.0))


def _out_ln_route(y, w, xres, gate, ln_g, ln_b, sc, sh, wr, br, alpha, n_groups, per_group, name):
    b, s, k = y.shape
    d = w.shape[1]
    tm = _tile(s, 512)
    parts = 2 if tm % (2 * SUBLANES) == 0 else 1
    kern = functools.partial(_outln_kernel, alpha=alpha, n_groups=n_groups, per_group=per_group, parts=parts)
    row = lambda bi, i: (bi, i, 0)
    per_b = lambda bi, i: (bi, 0, 0)
    const = lambda bi, i: (0, 0)
    return pl.pallas_call(
        kern,
        out_shape=(jax.ShapeDtypeStruct((b, s, d), F32), jax.ShapeDtypeStruct((b, s, d // 2), U32),
                   jax.ShapeDtypeStruct((b, s, LANES), I32), jax.ShapeDtypeStruct((b, s, LANES), F32)),
        grid=(b, s // tm),
        in_specs=[pl.BlockSpec((1, tm, k), row),
                  pl.BlockSpec((k, d), const),
                  pl.BlockSpec((1, tm, d), row),
                  pl.BlockSpec((1, 1, d), per_b),
                  pl.BlockSpec((1, d), const),
                  pl.BlockSpec((1, d), const),
                  pl.BlockSpec((1, 1, d), per_b),
                  pl.BlockSpec((1, 1, d), per_b),
                  pl.BlockSpec((d, 2 * LANES), const),
                  pl.BlockSpec((1, LANES), const)],
        out_specs=(pl.BlockSpec((1, tm, d), row), pl.BlockSpec((1, tm, d // 2), row),
                   pl.BlockSpec((1, tm, LANES), row), pl.BlockSpec((1, tm, LANES), row)),
        compiler_params=_cparams("parallel", "parallel"),
        name=name,
    )(y, w, xres, gate, ln_g.reshape(1, d), ln_b.reshape(1, d), sc, sh, wr, br)


def _combine_kernel(ya_ref, yb_ref, rg_ref, x_ref, g_ref, lng_ref, lnb_ref, o_ref, *, alpha):
    rg = rg_ref[0]
    y = ya_ref[0] * rg[:, 0:1] + yb_ref[0] * rg[:, 1:2]
    r = alpha * x_ref[0] + (1.0 + g_ref[0]) * y
    o_ref[0] = _layer_norm_rows(r, lng_ref[...], lnb_ref[...])


def _combine_ln(y2, route_gates, x, gate, ln_g, ln_b, alpha, name):
    b, s, d = x.shape
    tm = _tile(s, 512)
    row = lambda bi, i: (bi, i, 0)
    nt = s // tm
    return pl.pallas_call(
        functools.partial(_combine_kernel, alpha=alpha),
        out_shape=jax.ShapeDtypeStruct((b, s, d), F32),
        grid=(b, nt),
        in_specs=[pl.BlockSpec((1, tm, d), lambda bi, i: (0, bi * nt + i, 0)),
                  pl.BlockSpec((1, tm, d), lambda bi, i: (1, bi * nt + i, 0)),
                  pl.BlockSpec((1, tm, LANES), row),
                  pl.BlockSpec((1, tm, d), row),
                  pl.BlockSpec((1, 1, d), lambda bi, i: (bi, 0, 0)),
                  pl.BlockSpec((1, d), lambda bi, i: (0, 0)),
                  pl.BlockSpec((1, d), lambda bi, i: (0, 0))],
        out_specs=pl.BlockSpec((1, tm, d), row),
        compiler_params=_cparams("parallel", "parallel"),
        name=name,
    )(y2, y2, route_gates, x, gate, ln_g.reshape(1, d), ln_b.reshape(1, d))


def _moe_kernel(bexp_ref, bcnt_ref, rsrc_ref, rdst_ref,
                hin_hbm, wg_ref, wu_ref, wd_ref,
                y2_hbm,
                xbuf, ybuf, wgb, wub, wdb, gsem, ssem, *, blk, nb, tp):
    i = pl.program_id(0)
    slot = i % 2

    def start_gather(block, sl):
        base = block * blk

        def group(gi, c):
            for u in range(DMA_UNROLL):
                r = gi * DMA_UNROLL + u
                pltpu.make_async_copy(hin_hbm.at[pl.ds(rsrc_ref[base + r], 1), :],
                                      xbuf.at[sl, pl.ds(r, 1), :], gsem.at[sl]).start()
            return c

        lax.fori_loop(0, blk // DMA_UNROLL, group, 0)

    def wait_gather(sl):
        pltpu.make_async_copy(hin_hbm.at[pl.ds(0, blk), :], xbuf.at[sl], gsem.at[sl]).wait()

    def start_scatter(block, sl):
        base = block * blk

        def group(gi, c):
            for u in range(DMA_UNROLL):
                r = gi * DMA_UNROLL + u
                pltpu.make_async_copy(ybuf.at[sl, pl.ds(r, 1), :],
                                      y2_hbm.at[pl.ds(rdst_ref[base + r], 1), :], ssem.at[sl]).start()
            return c

        lax.fori_loop(0, blk // DMA_UNROLL, group, 0)

    def wait_scatter(sl):
        pltpu.make_async_copy(ybuf.at[sl], y2_hbm.at[pl.ds(0, blk), :], ssem.at[sl]).wait()

    used = bcnt_ref[i] > 0

    @pl.when(i == 0)
    def _():
        ybuf[...] = jnp.zeros_like(ybuf)
        for k in range(MOE_TOPK):
            for sl in range(2):
                cp = pltpu.make_async_copy(ybuf.at[sl], y2_hbm.at[pl.ds(k * tp + tp - (2 - sl) * blk, blk), :],
                                           ssem.at[sl])
                cp.start()
                cp.wait()

    @pl.when(jnp.logical_and(i == 0, used))
    def _():
        start_gather(0, 0)

    nxt = jnp.minimum(i + 1, nb - 1)

    @pl.when(jnp.logical_and(i + 1 < nb, bcnt_ref[nxt] > 0))
    def _():
        start_gather(nxt, 1 - slot)

    changed = jnp.logical_or(i == 0, bexp_ref[i] != bexp_ref[jnp.maximum(i - 1, 0)])

    @pl.when(jnp.logical_and(changed, used))
    def _():
        wgb[...] = wg_ref[0, 0].astype(BF16)
        wub[...] = wu_ref[0, 0].astype(BF16)
        wdb[...] = wd_ref[0, 0].astype(BF16)

    @pl.when(jnp.logical_and(i >= 2, bcnt_ref[jnp.maximum(i - 2, 0)] > 0))
    def _():
        wait_scatter(slot)

    @pl.when(used)
    def _():
        wait_gather(slot)
        xb = xbuf[slot].astype(BF16)
        hg = jnp.dot(xb, wgb[...], preferred_element_type=F32)
        hu = jnp.dot(xb, wub[...], preferred_element_type=F32)
        hid = (_silu(hg) * hu).astype(BF16)
        ybuf[slot] = jnp.dot(hid, wdb[...], preferred_element_type=F32)
        start_scatter(i, slot)

    @pl.when(i == nb - 1)
    def _():
        if nb >= 2:
            @pl.when(bcnt_ref[jnp.maximum(i - 1, 0)] > 0)
            def _():
                wait_scatter(1 - slot)

        @pl.when(used)
        def _():
            wait_scatter(slot)


def _moe_experts(hin2, w_gate, w_up, w_down, layer, blk_exp, blk_cnt, row_src, row_dst, name):
    t, d = hin2.shape
    ff = w_gate.shape[3]
    nb = blk_exp.shape[0]
    blk = MOE_BLOCK
    tp = t + 2 * blk
    wmap = lambda i, be, bc, rs, rd: (layer, be[i], 0, 0)
    grid_spec = pltpu.PrefetchScalarGridSpec(
        num_scalar_prefetch=4,
        grid=(nb,),
        in_specs=[pl.BlockSpec(memory_space=pl.ANY),
                  pl.BlockSpec((1, 1, d, ff), wmap),
                  pl.BlockSpec((1, 1, d, ff), wmap),
                  pl.BlockSpec((1, 1, ff, d), wmap)],
        out_specs=pl.BlockSpec(memory_space=pl.ANY),
        scratch_shapes=[pltpu.VMEM((2, blk, d), F32),
                        pltpu.VMEM((2, blk, d), F32),
                        pltpu.VMEM((d, ff), BF16),
                        pltpu.VMEM((d, ff), BF16),
                        pltpu.VMEM((ff, d), BF16),
                        pltpu.SemaphoreType.DMA((2,)),
                        pltpu.SemaphoreType.DMA((2,))],
    )
    y2 = pl.pallas_call(
        functools.partial(_moe_kernel, blk=blk, nb=nb, tp=tp),
        out_shape=jax.ShapeDtypeStruct((MOE_TOPK * tp, d), F32),
        grid_spec=grid_spec,
        compiler_params=_cparams("arbitrary"),
        name=name,
    )(blk_exp, blk_cnt, row_src, row_dst, hin2, w_gate, w_up, w_down)
    return y2.reshape(MOE_TOPK, tp, d)


def _dispatch_tables(ids, n_exp):
    t = ids.shape[0]
    n_assign = t * MOE_TOPK
    blk = MOE_BLOCK
    tp = t + 2 * blk
    e_flat = ids.reshape(-1)
    order = jnp.argsort(e_flat).astype(I32)
    e_sorted = e_flat[order]
    counts = jnp.bincount(e_flat, length=n_exp).astype(I32)
    padded = (counts + blk - 1) // blk * blk
    pad_end = jnp.cumsum(padded)
    pad_start = pad_end - padded
    start = jnp.cumsum(counts) - counts
    dest = pad_start[e_sorted] + jnp.arange(n_assign, dtype=I32) - start[e_sorted]
    n_rows = n_assign + n_exp * blk
    nb = n_rows // blk
    rows = jnp.arange(n_rows, dtype=I32)
    spare = t + ((rows // blk) % 2) * blk + rows % blk
    row_src = jnp.zeros((n_rows,), I32).at[dest].set(order // MOE_TOPK)
    row_dst = spare.at[dest].set((order % MOE_TOPK) * tp + order // MOE_TOPK)
    blk_first = jnp.arange(nb, dtype=I32) * blk
    blk_exp = jnp.minimum(jnp.searchsorted(pad_end, blk_first, side='right'), n_exp - 1).astype(I32)
    in_use = blk_first < pad_end[-1]
    blk_cnt = jnp.where(in_use, jnp.clip(counts[blk_exp] - (blk_first - pad_start[blk_exp]), 0, blk), 0)
    return blk_exp, blk_cnt.astype(I32), row_src, row_dst


def _dispatch_kernel(ids_ref, dest_ref, meta_ref, rank_ref, *, t, tile, blk_shift):
    lane = lax.broadcasted_iota(I32, (tile, LANES), 1)
    ri = lax.broadcasted_iota(I32, (tile, tile), 0)
    ci = lax.broadcasted_iota(I32, (tile, tile), 1)
    before = (ri > ci).astype(BF16)

    def hits(rows):
        ids = ids_ref[rows, :]
        return lane == ids[:, 0:1], lane == ids[:, 1:2]

    def count(ti, run):
        rows = pl.ds(pl.multiple_of(ti * tile, tile), tile)
        h1, h2 = hits(rows)
        onehot = jnp.logical_or(h1, h2).astype(BF16)
        prefix = jnp.dot(before, onehot, preferred_element_type=F32) + run
        r1 = jnp.sum(jnp.where(h1, prefix, 0.0), -1, keepdims=True)
        r2 = jnp.sum(jnp.where(h2, prefix, 0.0), -1, keepdims=True)
        rank_ref[rows, :] = jnp.where(lane == 0, r1, jnp.where(lane == 1, r2, 0.0))
        return run + jnp.sum(onehot.astype(F32), 0, keepdims=True)

    counts = lax.fori_loop(0, t // tile, count, jnp.zeros((1, LANES), F32))
    cnt = jnp.broadcast_to(counts, (SUBLANES, LANES)).astype(I32)
    nblk = (cnt + ((1 << blk_shift) - 1)) >> blk_shift
    padded = nblk << blk_shift
    lane8 = lax.broadcasted_iota(I32, (SUBLANES, LANES), 1)
    incl = padded
    step = 1
    while step < LANES:
        incl = incl + jnp.where(lane8 >= step, pltpu.roll(incl, step, 1), 0)
        step *= 2
    pad_start = incl - padded
    row8 = lax.broadcasted_iota(I32, (SUBLANES, LANES), 0)
    meta_ref[...] = jnp.where(row8 == 0, pad_start, jnp.where(row8 == 1, nblk, cnt))
    start_f = pad_start[0:1, :].astype(F32)

    def place(ti, c):
        rows = pl.ds(pl.multiple_of(ti * tile, tile), tile)
        h1, h2 = hits(rows)
        rk = rank_ref[rows, :]
        d1 = jnp.sum(jnp.where(h1, start_f, 0.0), -1, keepdims=True) + rk[:, 0:1]
        d2 = jnp.sum(jnp.where(h2, start_f, 0.0), -1, keepdims=True) + rk[:, 1:2]
        dest_ref[rows, :] = jnp.where(lane == 0, d1, jnp.where(lane == 1, d2, 0.0)).astype(I32)
        return c

    lax.fori_loop(0, t // tile, place, 0)


def _dispatch(ids):
    t = ids.shape[0]
    tile = _tile(t, 256)
    blk_shift = MOE_BLOCK.bit_length() - 1
    assert (1 << blk_shift) == MOE_BLOCK
    return pl.pallas_call(
        functools.partial(_dispatch_kernel, t=t, tile=tile, blk_shift=blk_shift),
        out_shape=(jax.ShapeDtypeStruct((t, LANES), I32), jax.ShapeDtypeStruct((SUBLANES, LANES), I32)),
        grid=(1,),
        in_specs=[pl.BlockSpec((t, LANES), lambda i: (0, 0))],
        out_specs=(pl.BlockSpec((t, LANES), lambda i: (0, 0)), pl.BlockSpec((SUBLANES, LANES), lambda i: (0, 0))),
        scratch_shapes=[pltpu.VMEM((t, LANES), F32)],
        compiler_params=_cparams("arbitrary"),
        name="moe_dispatch",
    )(ids)


def _scatter_rows_kernel(dest_ref, hin_ref, xs_in, xs_out, sem, *, tm):
    del xs_in
    base = pl.program_id(0) * (tm * MOE_TOPK)
    for r in range(tm):
        for k in range(MOE_TOPK):
            pltpu.make_async_copy(hin_ref.at[pl.ds(r, 1), :],
                                  xs_out.at[pl.ds(dest_ref[base + r * MOE_TOPK + k], 1), :], sem).start()
    for _ in range(MOE_TOPK):
        pltpu.make_async_copy(hin_ref, xs_out.at[pl.ds(0, tm), :], sem).wait()


def _scatter_rows(dest_flat, hin2, n_rows):
    t, d = hin2.shape
    tm = _tile(t, 256)
    grid_spec = pltpu.PrefetchScalarGridSpec(
        num_scalar_prefetch=1,
        grid=(t // tm,),
        in_specs=[pl.BlockSpec((tm, d), lambda i, dst: (i, 0)),
                  pl.BlockSpec(memory_space=pl.ANY)],
        out_specs=pl.BlockSpec(memory_space=pl.ANY),
        scratch_shapes=[pltpu.SemaphoreType.DMA],
    )
    return pl.pallas_call(
        functools.partial(_scatter_rows_kernel, tm=tm),
        out_shape=jax.ShapeDtypeStruct((n_rows, d), hin2.dtype),
        grid_spec=grid_spec,
        input_output_aliases={2: 0},
        compiler_params=_cparams("arbitrary"),
        name="moe_scatter_rows",
    )(dest_flat, hin2, jnp.zeros((n_rows, d), hin2.dtype))


def _ffn_kernel(eblk_ref, enb_ref, xs_hbm, wg_ref, wu_ref, wd_ref, ys_hbm,
                xbuf, ybuf, wgb, wub, wdb, isem, osem, *, blk, nb_total, n_exp):
    e = pl.program_id(0)
    n = enb_ref[e]
    b0 = eblk_ref[e]
    used = eblk_ref[n_exp - 1] + enb_ref[n_exp - 1]

    def in_cp(gb):
        sl = gb % FFN_RING
        return pltpu.make_async_copy(xs_hbm.at[pl.ds(gb * blk, blk), :], xbuf.at[sl], isem.at[sl])

    def out_cp(gb):
        sl = gb % FFN_RING
        return pltpu.make_async_copy(ybuf.at[sl], ys_hbm.at[pl.ds(gb * blk, blk), :], osem.at[sl])

    @pl.when(e == 0)
    def _():
        for k in range(FFN_RING - 1):
            @pl.when(k < used)
            def _(k=k):
                in_cp(k).start(priority=1)

    @pl.when(n > 0)
    def _():
        wgb[...] = wg_ref[0, 0].astype(BF16)
        wub[...] = wu_ref[0, 0].astype(BF16)
        wdb[...] = wd_ref[0, 0].astype(BF16)

        def body(j, c):
            gb = b0 + j
            sl = gb % FFN_RING

            @pl.when(gb + FFN_RING - 1 < used)
            def _():
                in_cp(gb + FFN_RING - 1).start(priority=1)

            in_cp(gb).wait()

            @pl.when(gb >= FFN_RING)
            def _():
                out_cp(gb - FFN_RING).wait()

            xb = jnp.concatenate(_unpack_bf16_pair(xbuf[sl]), axis=1)
            hg = jnp.dot(xb, wgb[...], preferred_element_type=F32)
            hu = jnp.dot(xb, wub[...], preferred_element_type=F32)
            hid = (_silu(hg) * hu).astype(BF16)
            ybuf[sl] = jnp.dot(hid, wdb[...], preferred_element_type=F32)
            out_cp(gb).start(priority=1)
            return c

        lax.fori_loop(0, n, body, 0)

    @pl.when(e == n_exp - 1)
    def _():
        for k in range(FFN_RING, 0, -1):
            @pl.when(used >= k)
            def _(k=k):
                out_cp(used - k).wait()

        ybuf[0] = jnp.zeros((blk, ybuf.shape[2]), F32)

        def fill(jb, c):
            cp = pltpu.make_async_copy(ybuf.at[0], ys_hbm.at[pl.ds(jb * blk, blk), :], osem.at[0])
            cp.start()
            cp.wait()
            return c

        lax.fori_loop(used, nb_total, fill, 0)


def _moe_ffn(eblk, enb, xs, w_gate, w_up, w_down, layer):
    n_rows = xs.shape[0]
    n_exp, d, ff = w_gate.shape[1], w_gate.shape[2], w_gate.shape[3]
    blk = MOE_BLOCK
    wmap = lambda e, eb, en: (layer, e, 0, 0)
    grid_spec = pltpu.PrefetchScalarGridSpec(
        num_scalar_prefetch=2,
        grid=(n_exp,),
        in_specs=[pl.BlockSpec(memory_space=pl.ANY),
                  pl.BlockSpec((1, 1, d, ff), wmap),
                  pl.BlockSpec((1, 1, d, ff), wmap),
                  pl.BlockSpec((1, 1, ff, d), wmap)],
        out_specs=pl.BlockSpec(memory_space=pl.ANY),
        scratch_shapes=[pltpu.VMEM((FFN_RING, blk, d // 2), U32),
                        pltpu.VMEM((FFN_RING, blk, d), F32),
                        pltpu.VMEM((d, ff), BF16),
                        pltpu.VMEM((d, ff), BF16),
                        pltpu.VMEM((ff, d), BF16),
                        pltpu.SemaphoreType.DMA((FFN_RING,)),
                        pltpu.SemaphoreType.DMA((FFN_RING,))],
    )
    return pl.pallas_call(
        functools.partial(_ffn_kernel, blk=blk, nb_total=n_rows // blk, n_exp=n_exp),
        out_shape=jax.ShapeDtypeStruct((n_rows, d), F32),
        grid_spec=grid_spec,
        compiler_params=_cparams("arbitrary"),
        name="moe_ffn",
    )(eblk, enb, xs, w_gate, w_up, w_down)


def _gather_ln_kernel(dest_ref, ys_hbm, rg_ref, x_ref, g_ref, lng_ref, lnb_ref, o_ref, ybuf, sem,
                      *, alpha, tm, nt, n_steps):
    step = pl.program_id(0) * nt + pl.program_id(1)
    slot = step % 2

    def start_gather(tile_idx, sl):
        base = tile_idx * (tm * MOE_TOPK)
        for r in range(tm):
            for k in range(MOE_TOPK):
                pltpu.make_async_copy(ys_hbm.at[pl.ds(dest_ref[base + r * MOE_TOPK + k], 1), :],
                                      ybuf.at[sl, k, pl.ds(r, 1), :], sem.at[sl]).start()

    @pl.when(step == 0)
    def _():
        start_gather(0, 0)

    for sl in range(2):
        @pl.when(jnp.logical_and(step + 1 < n_steps, slot == 1 - sl))
        def _(sl=sl):
            start_gather(step + 1, sl)

    for k in range(MOE_TOPK):
        pltpu.make_async_copy(ys_hbm.at[pl.ds(0, tm), :], ybuf.at[slot, k], sem.at[slot]).wait()

    rg = rg_ref[0]
    y = ybuf[slot, 0] * rg[:, 0:1] + ybuf[slot, 1] * rg[:, 1:2]
    r = alpha * x_ref[0] + (1.0 + g_ref[0]) * y
    o_ref[0] = _layer_norm_rows(r, lng_ref[...], lnb_ref[...])


def _gather_combine_ln(dest_flat, ys, route_gates, x, gate, ln_g, ln_b, alpha, name):
    b, s, d = x.shape
    tm = _tile(s, 256)
    nt = s // tm
    row = lambda bi, i, dst: (bi, i, 0)
    const = lambda bi, i, dst: (0, 0)
    grid_spec = pltpu.PrefetchScalarGridSpec(
        num_scalar_prefetch=1,
        grid=(b, nt),
        in_specs=[pl.BlockSpec(memory_space=pl.ANY),
                  pl.BlockSpec((1, tm, LANES), row),
                  pl.BlockSpec((1, tm, d), row),
                  pl.BlockSpec((1, 1, d), lambda bi, i, dst: (bi, 0, 0)),
                  pl.BlockSpec((1, d), const),
                  pl.BlockSpec((1, d), const)],
        out_specs=pl.BlockSpec((1, tm, d), row),
        scratch_shapes=[pltpu.VMEM((2, MOE_TOPK, tm, d), F32), pltpu.SemaphoreType.DMA((2,))],
    )
    return pl.pallas_call(
        functools.partial(_gather_ln_kernel, alpha=alpha, tm=tm, nt=nt, n_steps=b * nt),
        out_shape=jax.ShapeDtypeStruct((b, s, d), F32),
        grid_spec=grid_spec,
        compiler_params=_cparams("arbitrary", "arbitrary"),
        name=name,
    )(dest_flat, ys, route_gates, x, gate, ln_g.reshape(1, d), ln_b.reshape(1, d))


def _gdn_kernel(q_ref, k_ref, v_ref, z_ref, bg_ref, cg_ref, hh_ref, ba_ref,
                cw_ref, alog_ref, dtb_ref, nw_ref, scw_ref,
                o_ref,
                ext_ref, qa_ref, ka_ref, va_ref, hist_ref, hist2_ref, gate_ref, state_ref,
                *, heads, ts, gw):
    L = GDN_CHUNK
    dk = GDN_HEAD_DIM

    @pl.when(pl.program_id(1) == 0)
    def _():
        hist_ref[...] = jnp.zeros_like(hist_ref)
        hist2_ref[...] = jnp.zeros_like(hist2_ref)
        state_ref[...] = jnp.zeros_like(state_ref)

    def causal_conv(x, hist, w, taps):
        ext_ref[0:SUBLANES, :] = hist
        ext_ref[SUBLANES:SUBLANES + ts, :] = x
        y = w[0:1] * ext_ref[SUBLANES - taps + 1:SUBLANES - taps + 1 + ts, :]
        for j in range(1, taps - 1):
            y = y + w[j:j + 1] * ext_ref[SUBLANES - taps + 1 + j:SUBLANES - taps + 1 + j + ts, :]
        return y + w[taps - 1:taps] * x

    for idx, (src, dst) in enumerate(((q_ref, qa_ref), (k_ref, ka_ref), (v_ref, va_ref))):
        x = src[0]
        y = causal_conv(x, hist_ref[idx], cw_ref[:, idx * gw:(idx + 1) * gw], GDN_CONV)
        hist_ref[idx] = x[ts - SUBLANES:ts, :]
        dst[...] = _silu(y)

    c = cg_ref[0] * hh_ref[0]
    yb = causal_conv(c, hist2_ref[...], scw_ref[...], SC_CONV)
    hist2_ref[...] = c[ts - SUBLANES:ts, :]
    o_ref[0, :, gw:2 * gw] = bg_ref[0] * yb

    ba = ba_ref[0]
    gate_ref[0] = _sigmoid(ba)
    gate_ref[1] = -jnp.exp(alog_ref[...]) * _softplus(ba + dtb_ref[...])

    ii = lax.broadcasted_iota(I32, (L, L), 0)
    jj = lax.broadcasted_iota(I32, (L, L), 1)
    causal = ii >= jj
    strict = ii > jj
    tri_incl = causal.astype(F32)
    eye = (ii == jj).astype(F32)
    nchunk = ts // L
    pairs = [(c, h) for c in range(nchunk) for h in range(heads)]

    def tiles(ref, lead=()):
        return jnp.stack([ref[lead + (slice(c * L, (c + 1) * L), slice(h * dk, (h + 1) * dk))]
                          for c, h in pairs], axis=0)

    def bmm(a, b):
        return jnp.einsum('bij,bjk->bik', a.astype(BF16), b.astype(BF16), preferred_element_type=F32)

    def bmm_nt(a, b):
        return jnp.einsum('bik,bjk->bij', a.astype(BF16), b.astype(BF16), preferred_element_type=F32)

    qh = tiles(qa_ref)
    kh = tiles(ka_ref)
    vh = tiles(va_ref)
    beta = jnp.stack([gate_ref[0, c * L:(c + 1) * L, h:h + 1] for c, h in pairs], axis=0)
    cums = [jnp.dot(tri_incl, gate_ref[1, c * L:(c + 1) * L, :], precision=HIGHEST,
                    preferred_element_type=F32) for c in range(nchunk)]
    cums_t = [cm.T for cm in cums]
    gc = jnp.stack([cums[c][:, heads + h:heads + h + 1] for c, h in pairs], axis=0)
    gr = jnp.stack([cums_t[c][heads + h:heads + h + 1, :] for c, h in pairs], axis=0)
    gl = gc[:, L - 1:L, :]
    decay = jnp.where(causal, jnp.exp(jnp.where(causal, gc - gr, 0.0)), 0.0)
    egc = jnp.exp(gc)
    qn = qh * lax.rsqrt(jnp.sum(qh * qh, -1, keepdims=True) + 1e-6)
    kn = kh * lax.rsqrt(jnp.sum(kh * kh, -1, keepdims=True) + 1e-6)
    qc = qn * (dk ** -0.5)
    kb = kn * beta
    m = jnp.where(strict, bmm_nt(kb, kn) * decay, 0.0)
    t_inv = eye - m
    mp = m
    for _ in range(5):
        mp = bmm(mp, mp)
        t_inv = t_inv + bmm(t_inv, mp)
    sol = bmm(t_inv, jnp.concatenate([vh * beta, kb * egc], axis=2))
    u = sol[:, :, 0:dk]
    w = sol[:, :, dk:2 * dk]
    qk = jnp.where(causal, bmm_nt(qc, kn) * decay, 0.0)
    qg = qc * egc
    k_dec = kn * jnp.exp(gl - gc)
    egl = jnp.exp(gl)

    state = [state_ref[h] for h in range(heads)]
    outs = []
    for b, (c, h) in enumerate(pairs):
        st = state[h]
        v_new = u[b] - _dot(w[b], st)
        outs.append(_dot(qg[b], st) + _dot(qk[b], v_new))
        state[h] = st * egl[b] + _dot(k_dec[b].T, v_new)
    for h in range(heads):
        state_ref[h] = state[h]

    o = jnp.stack(outs, axis=0)
    on = o * lax.rsqrt(jnp.mean(o * o, -1, keepdims=True) + 1e-6) * nw_ref[...]
    y = on * _silu(tiles(z_ref, (0,)))
    for b, (c, h) in enumerate(pairs):
        o_ref[0, c * L:(c + 1) * L, h * dk:(h + 1) * dk] = y[b]


def _gdn_shortconv(proj, conv_w, a_log, dt_bias, norm_w, sc_w, heads):
    b, s, _ = proj.shape
    gw = heads * GDN_HEAD_DIM
    ts = _tile(s, 256)
    alog_p = jnp.zeros((1, LANES), F32).at[0, heads:2 * heads].set(a_log)
    dtb_p = jnp.zeros((1, LANES), F32).at[0, heads:2 * heads].set(dt_bias)
    col = lambda cb: (lambda bi, i: (bi, i, cb))
    const = lambda bi, i: (0, 0)
    kern = functools.partial(_gdn_kernel, heads=heads, ts=ts, gw=gw)
    return pl.pallas_call(
        kern,
        out_shape=jax.ShapeDtypeStruct((b, s, 2 * gw), F32),
        grid=(b, s // ts),
        in_specs=[pl.BlockSpec((1, ts, gw), col(cb)) for cb in range(7)]
        + [pl.BlockSpec((1, ts, LANES), col(7 * gw // LANES)),
           pl.BlockSpec((GDN_CONV, 3 * gw), const),
           pl.BlockSpec((1, LANES), const),
           pl.BlockSpec((1, LANES), const),
           pl.BlockSpec((1, GDN_HEAD_DIM), const),
           pl.BlockSpec((SC_CONV, gw), const)],
        out_specs=pl.BlockSpec((1, ts, 2 * gw), lambda bi, i: (bi, i, 0)),
        scratch_shapes=[pltpu.VMEM((ts + SUBLANES, gw), F32),
                        pltpu.VMEM((ts, gw), F32),
                        pltpu.VMEM((ts, gw), F32),
                        pltpu.VMEM((ts, gw), F32),
                        pltpu.VMEM((3, SUBLANES, gw), F32),
                        pltpu.VMEM((SUBLANES, gw), F32),
                        pltpu.VMEM((2, ts, LANES), F32),
                        pltpu.VMEM((heads, GDN_HEAD_DIM, GDN_HEAD_DIM), F32)],
        compiler_params=_cparams("parallel", "arbitrary"),
        name="gdn_shortconv",
    )(proj, proj, proj, proj, proj, proj, proj, proj,
      conv_w, alog_p, dtb_p, norm_w.reshape(1, GDN_HEAD_DIM), sc_w)


def _rope_kernel(pos_ref, inv_ref, qpe_ref, kpe_ref, qo_ref, ko_ref, *, heads):
    ang = pos_ref[0].astype(F32) * inv_ref[...]
    lane = lax.broadcasted_iota(I32, ang.shape, 1)
    half = MLA_ROPE // 2
    cos = jnp.where(lane < MLA_ROPE, jnp.cos(ang), 0.0)
    sin = jnp.sin(ang)
    sgn = jnp.where(lane < half, -sin, jnp.where(lane < MLA_ROPE, sin, 0.0))

    def rot(x):
        swapped = jnp.where(lane < half, pltpu.roll(x, LANES - half, 1), pltpu.roll(x, half, 1))
        return x * cos + swapped * sgn

    for h in range(heads):
        cols = slice(h * LANES, (h + 1) * LANES)
        qo_ref[0, :, cols] = rot(qpe_ref[0, :, cols])
    ko_ref[0] = rot(kpe_ref[0])


def _rope(positions, qup, heads, cproj, kpe_block):
    b, s = positions.shape
    ts = _tile(s, 512)
    half = MLA_ROPE // 2
    inv = ROPE_THETA ** (-jnp.arange(half, dtype=F32) * (2.0 / MLA_ROPE))
    inv_tab = jnp.zeros((1, LANES), F32).at[0, 0:half].set(inv).at[0, half:MLA_ROPE].set(inv)
    w = heads * LANES
    return pl.pallas_call(
        functools.partial(_rope_kernel, heads=heads),
        out_shape=(jax.ShapeDtypeStruct((b, s, w), F32), jax.ShapeDtypeStruct((b, s, LANES), F32)),
        grid=(b, s // ts),
        in_specs=[pl.BlockSpec((1, ts, 1), lambda bi, i: (bi, i, 0)),
                  pl.BlockSpec((1, LANES), lambda bi, i: (0, 0)),
                  pl.BlockSpec((1, ts, w), lambda bi, i: (bi, i, 1)),
                  pl.BlockSpec((1, ts, LANES), lambda bi, i: (bi, i, kpe_block))],
        out_specs=(pl.BlockSpec((1, ts, w), lambda bi, i: (bi, i, 0)),
                   pl.BlockSpec((1, ts, LANES), lambda bi, i: (bi, i, 0))),
        compiler_params=_cparams("parallel", "parallel"),
        name="mla_rope",
    )(positions.reshape(b, s, 1), inv_tab, qup, cproj)


def _attn_kernel(qn_ref, qp_ref, kn_ref, kp_ref, v_ref, o_ref, kcat_ref, vb_ref, s_ref, *, tq, nq, scale):
    qi = pl.program_id(2)

    @pl.when(qi == 0)
    def _():
        kcat_ref[:, 0:MLA_NOPE] = kn_ref[0].astype(BF16)
        kcat_ref[:, MLA_NOPE:MLA_NOPE + LANES] = kp_ref[0].astype(BF16)
        vb_ref[...] = v_ref[0].astype(BF16)

    qcat = jnp.concatenate([qn_ref[0], qp_ref[0]], axis=1).astype(BF16)
    hq = tq // 2
    tri = lax.broadcasted_iota(I32, (hq, hq), 0) >= lax.broadcasted_iota(I32, (hq, hq), 1)

    for qv in range(nq):
        @pl.when(qi == qv)
        def _(qv=qv):
            for h in range(2):
                rows = slice(h * hq, (h + 1) * hq)
                nk = qv * tq + (h + 1) * hq
                for c0 in range(0, nk, tq):
                    c1 = min(c0 + tq, nk)
                    s = lax.dot_general(qcat[rows], kcat_ref[c0:c1, :], (((1,), (1,)), ((), ())),
                                        preferred_element_type=F32) * scale
                    if c1 == nk:
                        w = c1 - c0
                        last = jnp.where(tri, s[:, w - hq:], -jnp.inf)
                        s = last if w == hq else jnp.concatenate([s[:, :w - hq], last], axis=1)
                    s_ref[rows, c0:c1] = s
                    mb = jnp.max(s, -1, keepdims=True)
                    m = mb if c0 == 0 else jnp.maximum(m, mb)
                l = jnp.zeros((hq, 1), F32)
                acc = jnp.zeros((hq, MLA_V), F32)
                for c0 in range(0, nk, hq):
                    p = jnp.exp(s_ref[rows, c0:c0 + hq] - m)
                    l = l + jnp.sum(p, -1, keepdims=True)
                    acc = acc + jnp.dot(p.astype(BF16), vb_ref[c0:c0 + hq, :], preferred_element_type=F32)
                o_ref[0, rows, :] = acc / l


def _attention(qup, qpe, kvup, kpe, heads):
    b, s, _ = qup.shape
    tq = _tile(s, 512)
    nq = s // tq
    scale = (MLA_NOPE + MLA_ROPE) ** -0.5
    return pl.pallas_call(
        functools.partial(_attn_kernel, tq=tq, nq=nq, scale=scale),
        out_shape=jax.ShapeDtypeStruct((b, s, heads * MLA_V), F32),
        grid=(b, heads, nq),
        in_specs=[pl.BlockSpec((1, tq, MLA_NOPE), lambda bi, h, i: (bi, i, h)),
                  pl.BlockSpec((1, tq, LANES), lambda bi, h, i: (bi, i, h)),
                  pl.BlockSpec((1, s, MLA_NOPE), lambda bi, h, i: (bi, 0, h)),
                  pl.BlockSpec((1, s, LANES), lambda bi, h, i: (bi, 0, 0)),
                  pl.BlockSpec((1, s, MLA_V), lambda bi, h, i: (bi, 0, heads + h))],
        out_specs=pl.BlockSpec((1, tq, MLA_V), lambda bi, h, i: (bi, i, h)),
        scratch_shapes=[pltpu.VMEM((s, MLA_NOPE + LANES), BF16), pltpu.VMEM((s, MLA_V), BF16),
                        pltpu.VMEM((tq, s), F32)],
        compiler_params=_cparams("parallel", "parallel", "arbitrary"),
        name="mla_attention",
    )(qup, qpe, kvup, kpe, kvup)


def _hyb_in_weight(w_in, heads):
    d = w_in.shape[0]
    gw = heads * GDN_HEAD_DIM
    main = jnp.concatenate([w_in[:, :4 * gw], w_in[:, 4 * gw + 2 * heads:]], axis=1)
    gates = w_in[:, 4 * gw:4 * gw + 2 * heads]
    n = main.shape[1] + LANES
    n_pad = -(-n // 768) * 768
    tail = jnp.zeros((d, n_pad - main.shape[1]), w_in.dtype).at[:, :2 * heads].set(gates)
    return jnp.concatenate([main, tail], axis=1).astype(BF16)


def _mla_in_weight(w_in):
    d = w_in.shape[0]
    pad = jnp.zeros((d, LANES - MLA_ROPE), w_in.dtype)
    return jnp.concatenate([w_in, pad], axis=1).astype(BF16)


def _mla_uq_weight(w_uq, heads):
    r = w_uq.shape[0]
    w = w_uq.reshape(r, heads, MLA_NOPE + MLA_ROPE)
    nope = w[:, :, :MLA_NOPE].reshape(r, heads * MLA_NOPE)
    pe = jnp.concatenate([w[:, :, MLA_NOPE:], jnp.zeros((r, heads, LANES - MLA_ROPE), w.dtype)], axis=2)
    return jnp.concatenate([nope, pe.reshape(r, heads * LANES)], axis=1).astype(BF16)


def _mla_ukv_weight(w_ukv, heads):
    r = w_ukv.shape[0]
    w = w_ukv.reshape(r, heads, MLA_NOPE + MLA_V)
    return jnp.concatenate([w[:, :, :MLA_NOPE].reshape(r, heads * MLA_NOPE),
                            w[:, :, MLA_NOPE:].reshape(r, heads * MLA_V)], axis=1).astype(BF16)


def _router_weight(wr_g, br_g, wr_e, br_e):
    d = wr_g.shape[0]
    n = wr_g.shape[1] + wr_e.shape[1]
    w = jnp.zeros((d, LANES), F32).at[:, :n].set(jnp.concatenate([wr_g, wr_e], axis=1))
    bias = jnp.zeros((1, LANES), F32).at[0, :n].set(jnp.concatenate([br_g, br_e]))
    w_hi = w.astype(BF16)
    w_lo = (w - w_hi.astype(F32)).astype(BF16)
    return jnp.concatenate([w_hi, w_lo], axis=1), bias


def kernel(x, c, positions, ada_w, ada_b, ln_g, ln_b, hyb_w_in, gdn_conv_w, gdn_a_log, gdn_dt_bias,
           gdn_norm_w, sc_conv_w, hyb_w_out, mla_w_in, mla_q_norm, mla_kv_norm, mla_w_uq, mla_w_ukv,
           mla_w_out, moe_router_g, moe_bias_g, moe_router_e, moe_bias_e, moe_w_gate, moe_w_up,
           moe_w_down):
    b, s, d = x.shape
    depth = ada_w.shape[0]
    alpha = (2.0 * depth) ** 0.25
    gdn_heads = d // (2 * GDN_HEAD_DIM)
    mla_heads = mla_w_out.shape[1] // MLA_V
    n_groups = moe_router_g.shape[2]
    n_exp = moe_router_e.shape[2]
    per_group = n_exp // n_groups

    mod = _modulation(c, ada_w, ada_b)
    for layer in range(depth):
        sh1, sc1, g1, sh2, sc2, g2 = [mod[layer, :, j * d:(j + 1) * d].reshape(b, 1, d) for j in range(6)]
        i = layer // 2
        wr, br = _router_weight(moe_router_g[layer], moe_bias_g[layer], moe_router_e[layer], moe_bias_e[layer])
        if layer % 2 == 0:
            proj = _mod_matmul(x, sc1, sh1, _hyb_in_weight(hyb_w_in[i], gdn_heads), "hyb_in_proj")
            y = _gdn_shortconv(proj, gdn_conv_w[i], gdn_a_log[i], gdn_dt_bias[i], gdn_norm_w[i],
                               sc_conv_w[i], gdn_heads)
            w_out = hyb_w_out[i]
        else:
            cproj = _mod_matmul(x, sc1, sh1, _mla_in_weight(mla_w_in[i]), "mla_in_proj")
            qup = _rms_matmul(cproj, 0, MLA_Q_RANK, mla_q_norm[i], _mla_uq_weight(mla_w_uq[i], mla_heads),
                              "mla_q_up")
            kvup = _rms_matmul(cproj, 1, MLA_KV_RANK, mla_kv_norm[i], _mla_ukv_weight(mla_w_ukv[i], mla_heads),
                               "mla_kv_up")
            qpe, kpe = _rope(positions, qup, mla_heads, cproj, (MLA_Q_RANK + MLA_KV_RANK) // LANES)
            y = _attention(qup, qpe, kvup, kpe, mla_heads)
            w_out = mla_w_out[i]
        x, hin2, ids, gates = _out_ln_route(y, w_out.astype(BF16), x, g1, ln_g[layer, 0], ln_b[layer, 0],
                                            sc2, sh2, wr, br, alpha, n_groups, per_group,
                                            "mixer_out_ln_route")
        t = b * s
        dest, meta = _dispatch(ids.reshape(t, LANES))
        dest_flat = dest[:, :MOE_TOPK].reshape(-1)
        eblk = meta[0, :n_exp] // MOE_BLOCK
        enb = meta[1, :n_exp]
        n_rows = t * MOE_TOPK + n_exp * MOE_BLOCK
        xs = _scatter_rows(dest_flat, hin2.reshape(t, d // 2), n_rows)
        ys = _moe_ffn(eblk, enb, xs, moe_w_gate, moe_w_up, moe_w_down, layer)
        x = _gather_combine_ln(dest_flat, ys, gates, x, g2, ln_g[layer, 1], ln_b[layer, 1], alpha,
                               "moe_gather_combine_ln")
    return x
```

```python
import functools

import jax
import jax.numpy as jnp
from jax import lax
from jax.experimental import pallas as pl
from jax.experimental.pallas import tpu as pltpu

F32, BF16, I32, U32 = jnp.float32, jnp.bfloat16, jnp.int32, jnp.uint32
HIGHEST = lax.Precision.HIGHEST

LANES = 128
SUBLANES = 8
VMEM_LIMIT_BYTES = 56 * 1024 * 1024

GDN_HEAD_DIM = 128
GDN_CONV = 4
GDN_CHUNK = 64
SC_CONV = 3
MLA_NOPE = 128
MLA_ROPE = 64
MLA_V = 128
MLA_Q_RANK = 512
MLA_KV_RANK = 512
ROPE_THETA = 10000.0
MOE_TOPK = 2
MOE_BLOCK = 128
FFN_RING = 6


def _tile(n, pref):
    if n <= pref:
        return n
    for t in range(pref, 0, -LANES):
        if n % t == 0:
            return t
    return n


def _cparams(*sem):
    return pltpu.CompilerParams(dimension_semantics=sem, vmem_limit_bytes=VMEM_LIMIT_BYTES)


def _sigmoid(x):
    return 1.0 / (1.0 + jnp.exp(-x))


def _silu(x):
    return x * _sigmoid(x)


def _softplus(x):
    return jnp.maximum(x, 0.0) + jnp.log(1.0 + jnp.exp(-jnp.abs(x)))


def _pack_bf16_pair(lo, hi):
    lo_bits = lax.bitcast_convert_type(lo.astype(BF16).astype(F32), U32)
    hi_bits = lax.bitcast_convert_type(hi.astype(BF16).astype(F32), U32)
    return (lo_bits >> 16) | (hi_bits & jnp.uint32(0xFFFF0000))


def _unpack_bf16_pair(w):
    lo = lax.bitcast_convert_type(w << 16, F32)
    hi = lax.bitcast_convert_type(w & jnp.uint32(0xFFFF0000), F32)
    return lo.astype(BF16), hi.astype(BF16)


def _dot(a, b):
    return jnp.dot(a.astype(BF16), b.astype(BF16), preferred_element_type=F32)


def _dot_nt(a, b):
    return lax.dot_general(a.astype(BF16), b.astype(BF16), (((1,), (1,)), ((), ())),
                           preferred_element_type=F32)


def _mod_kernel(c_ref, w_ref, b_ref, o_ref):
    c = c_ref[...]
    o_ref[0] = _dot(_silu(c), w_ref[0]) + b_ref[0]


def _modulation(c, ada_w, ada_b):
    depth, d, n = ada_w.shape
    b = c.shape[0]
    rows = -(-b // SUBLANES) * SUBLANES
    cp = jnp.zeros((rows, d), F32).at[:b].set(c)
    tn = _tile(n, 1024)
    out = pl.pallas_call(
        _mod_kernel,
        out_shape=jax.ShapeDtypeStruct((depth, rows, n), F32),
        grid=(depth, n // tn),
        in_specs=[pl.BlockSpec((rows, d), lambda l, j: (0, 0)),
                  pl.BlockSpec((1, d, tn), lambda l, j: (l, 0, j)),
                  pl.BlockSpec((1, 1, tn), lambda l, j: (l, 0, j))],
        out_specs=pl.BlockSpec((1, rows, tn), lambda l, j: (l, 0, j)),
        compiler_params=_cparams("parallel", "parallel"),
        name="adaln_mod",
    )(cp, ada_w, ada_b.reshape(depth, 1, n))
    return out[:, :b]


def _modmm_kernel(x_ref, sc_ref, sh_ref, w_ref, o_ref, xb_ref):
    @pl.when(pl.program_id(2) == 0)
    def _():
        xb_ref[...] = (x_ref[0] * (1.0 + sc_ref[0]) + sh_ref[0]).astype(BF16)

    o_ref[0] = jnp.dot(xb_ref[...], w_ref[...], preferred_element_type=F32)


def _mod_matmul(x, sc, sh, w, name):
    b, s, d = x.shape
    n = w.shape[1]
    tm = _tile(s, 1024)
    tn = _tile(n, 768)
    return pl.pallas_call(
        _modmm_kernel,
        out_shape=jax.ShapeDtypeStruct((b, s, n), F32),
        grid=(b, s // tm, n // tn),
        in_specs=[pl.BlockSpec((1, tm, d), lambda bi, i, j: (bi, i, 0)),
                  pl.BlockSpec((1, 1, d), lambda bi, i, j: (bi, 0, 0)),
                  pl.BlockSpec((1, 1, d), lambda bi, i, j: (bi, 0, 0)),
                  pl.BlockSpec((d, tn), lambda bi, i, j: (0, j))],
        out_specs=pl.BlockSpec((1, tm, tn), lambda bi, i, j: (bi, i, j)),
        scratch_shapes=[pltpu.VMEM((tm, d), BF16)],
        compiler_params=_cparams("parallel", "parallel", "arbitrary"),
        name=name,
    )(x, sc, sh, w)


def _rmsmm_kernel(x_ref, nw_ref, w_ref, o_ref, xb_ref):
    @pl.when(pl.program_id(2) == 0)
    def _():
        x = x_ref[0]
        xb_ref[...] = (x * lax.rsqrt(jnp.mean(x * x, -1, keepdims=True) + 1e-6) * nw_ref[...]).astype(BF16)

    o_ref[0] = jnp.dot(xb_ref[...], w_ref[...], preferred_element_type=F32).astype(o_ref.dtype)


def _rms_matmul(x, col_block, k, norm_w, w, out_dtype, name):
    b, s, _ = x.shape
    n = w.shape[1]
    tm = _tile(s, 1024)
    tn = _tile(n, 1024)
    return pl.pallas_call(
        _rmsmm_kernel,
        out_shape=jax.ShapeDtypeStruct((b, s, n), out_dtype),
        grid=(b, s // tm, n // tn),
        in_specs=[pl.BlockSpec((1, tm, k), lambda bi, i, j: (bi, i, col_block)),
                  pl.BlockSpec((1, k), lambda bi, i, j: (0, 0)),
                  pl.BlockSpec((k, tn), lambda bi, i, j: (0, j))],
        out_specs=pl.BlockSpec((1, tm, tn), lambda bi, i, j: (bi, i, j)),
        scratch_shapes=[pltpu.VMEM((tm, k), BF16)],
        compiler_params=_cparams("parallel", "parallel", "arbitrary"),
        name=name,
    )(x, norm_w.reshape(1, k), w)


def _layer_norm_rows(r, g, b):
    mu = jnp.mean(r, -1, keepdims=True)
    var = jnp.mean(jnp.square(r - mu), -1, keepdims=True)
    return (r - mu) * lax.rsqrt(var + 1e-5) * g + b


def _outln_kernel(y_ref, w_ref, xres_ref, g_ref, lng_ref, lnb_ref, sc_ref, sh_ref, wr_ref, br_ref,
                  xo_ref, hin_ref, ids_ref, gates_ref, *, alpha, n_groups, per_group, parts):
    tp = y_ref.shape[1] // parts
    for part in range(parts):
        _outln_rows(slice(part * tp, (part + 1) * tp), y_ref, w_ref, xres_ref, g_ref, lng_ref, lnb_ref,
                    sc_ref, sh_ref, wr_ref, br_ref, xo_ref, hin_ref, ids_ref, gates_ref,
                    alpha=alpha, n_groups=n_groups, per_group=per_group)


def _outln_rows(rows, y_ref, w_ref, xres_ref, g_ref, lng_ref, lnb_ref, sc_ref, sh_ref, wr_ref, br_ref,
                xo_ref, hin_ref, ids_ref, gates_ref, *, alpha, n_groups, per_group):
    y = jnp.dot(y_ref[0, rows, :].astype(BF16), w_ref[...], preferred_element_type=F32)
    r = alpha * xres_ref[0, rows, :] + (1.0 + g_ref[0]) * y
    xn = _layer_norm_rows(r, lng_ref[...], lnb_ref[...])
    xo_ref[0, rows, :] = xn
    hin = xn * (1.0 + sc_ref[0]) + sh_ref[0]
    half = hin.shape[1] // 2
    hin_ref[0, rows, :] = _pack_bf16_pair(hin[:, :half], hin[:, half:])
    h_hi = hin.astype(BF16)
    h_lo = (hin - h_hi.astype(F32)).astype(BF16)
    ph = jnp.dot(h_hi, wr_ref[...], preferred_element_type=F32)
    plo = jnp.dot(h_lo, wr_ref[...], preferred_element_type=F32)
    logits = (ph[:, :LANES] + ph[:, LANES:]) + (plo[:, :LANES] + plo[:, LANES:]) + br_ref[...]
    lane = lax.broadcasted_iota(I32, logits.shape, 1)
    big = jnp.int32(4 * LANES)
    neg = jnp.float32(-jnp.inf)
    n_exp = n_groups * per_group
    gmask = lane < n_groups
    lg = jnp.where(gmask, logits, neg)
    mg = jnp.max(lg, -1, keepdims=True)
    grp = jnp.min(jnp.where(gmask & (lg == mg), lane, big), -1, keepdims=True)
    pg_sel = 1.0 / jnp.sum(jnp.where(gmask, jnp.exp(lg - mg), 0.0), -1, keepdims=True)
    lo = n_groups + grp * per_group
    emask = (lane >= lo) & (lane < lo + per_group) & (lane < n_groups + n_exp)
    le = jnp.where(emask, logits, neg)
    me = jnp.max(le, -1, keepdims=True)
    ee = jnp.where(emask, jnp.exp(le - me), 0.0)
    p = ee / jnp.sum(ee, -1, keepdims=True)
    pm = jnp.where(emask, p, -1.0)
    p1 = jnp.max(pm, -1, keepdims=True)
    i1 = jnp.min(jnp.where(emask & (pm == p1), lane, big), -1, keepdims=True)
    pm2 = jnp.where(lane == i1, -1.0, pm)
    p2 = jnp.max(pm2, -1, keepdims=True)
    i2 = jnp.min(jnp.where(emask & (lane != i1) & (pm2 == p2), lane, big), -1, keepdims=True)
    den = p1 + p2
    g1 = pg_sel * p1 / den
    g2 = pg_sel * p2 / den
    ids_ref[0, rows, :] = jnp.where(lane == 0, i1 - n_groups, jnp.where(lane == 1, i2 - n_groups, 0))
    gates_ref[0, rows, :] = jnp.where(lane == 0, g1, jnp.where(lane == 1, g2, 0.0))


def _out_ln_route(y, w, xres, gate, ln_g, ln_b, sc, sh, wr, br, alpha, n_groups, per_group, name):
    b, s, k = y.shape
    d = w.shape[1]
    tm = _tile(s, 512)
    parts = 2 if tm % (2 * SUBLANES) == 0 else 1
    kern = functools.partial(_outln_kernel, alpha=alpha, n_groups=n_groups, per_group=per_group, parts=parts)
    row = lambda bi, i: (bi, i, 0)
    per_b = lambda bi, i: (bi, 0, 0)
    const = lambda bi, i: (0, 0)
    return pl.pallas_call(
        kern,
        out_shape=(jax.ShapeDtypeStruct((b, s, d), F32), jax.ShapeDtypeStruct((b, s, d // 2), U32),
                   jax.ShapeDtypeStruct((b, s, LANES), I32), jax.ShapeDtypeStruct((b, s, LANES), F32)),
        grid=(b, s // tm),
        in_specs=[pl.BlockSpec((1, tm, k), row),
                  pl.BlockSpec((k, d), const),
                  pl.BlockSpec((1, tm, d), row),
                  pl.BlockSpec((1, 1, d), per_b),
                  pl.BlockSpec((1, d), const),
                  pl.BlockSpec((1, d), const),
                  pl.BlockSpec((1, 1, d), per_b),
                  pl.BlockSpec((1, 1, d), per_b),
                  pl.BlockSpec((d, 2 * LANES), const),
                  pl.BlockSpec((1, LANES), const)],
        out_specs=(pl.BlockSpec((1, tm, d), row), pl.BlockSpec((1, tm, d // 2), row),
                   pl.BlockSpec((1, tm, LANES), row), pl.BlockSpec((1, tm, LANES), row)),
        compiler_params=_cparams("parallel", "parallel"),
        name=name,
    )(y, w, xres, gate, ln_g.reshape(1, d), ln_b.reshape(1, d), sc, sh, wr, br)


def _dispatch_kernel(ids_ref, dest_ref, meta_ref, rank_ref, *, t, tile, blk_shift):
    lane = lax.broadcasted_iota(I32, (tile, LANES), 1)
    ri = lax.broadcasted_iota(I32, (tile, tile), 0)
    ci = lax.broadcasted_iota(I32, (tile, tile), 1)
    before = (ri > ci).astype(BF16)

    def hits(rows):
        ids = ids_ref[rows, :]
        return lane == ids[:, 0:1], lane == ids[:, 1:2]

    def count(ti, run):
        rows = pl.ds(pl.multiple_of(ti * tile, tile), tile)
        h1, h2 = hits(rows)
        onehot = jnp.logical_or(h1, h2).astype(BF16)
        prefix = jnp.dot(before, onehot, preferred_element_type=F32) + run
        r1 = jnp.sum(jnp.where(h1, prefix, 0.0), -1, keepdims=True)
        r2 = jnp.sum(jnp.where(h2, prefix, 0.0), -1, keepdims=True)
        rank_ref[rows, :] = jnp.where(lane == 0, r1, jnp.where(lane == 1, r2, 0.0))
        return run + jnp.sum(onehot.astype(F32), 0, keepdims=True)

    counts = lax.fori_loop(0, t // tile, count, jnp.zeros((1, LANES), F32))
    cnt = jnp.broadcast_to(counts, (SUBLANES, LANES)).astype(I32)
    nblk = (cnt + ((1 << blk_shift) - 1)) >> blk_shift
    padded = nblk << blk_shift
    lane8 = lax.broadcasted_iota(I32, (SUBLANES, LANES), 1)
    incl = padded
    step = 1
    while step < LANES:
        incl = incl + jnp.where(lane8 >= step, pltpu.roll(incl, step, 1), 0)
        step *= 2
    pad_start = incl - padded
    row8 = lax.broadcasted_iota(I32, (SUBLANES, LANES), 0)
    meta_ref[...] = jnp.where(row8 == 0, pad_start, jnp.where(row8 == 1, nblk, cnt))
    start_f = pad_start[0:1, :].astype(F32)

    def place(ti, c):
        rows = pl.ds(pl.multiple_of(ti * tile, tile), tile)
        h1, h2 = hits(rows)
        rk = rank_ref[rows, :]
        d1 = jnp.sum(jnp.where(h1, start_f, 0.0), -1, keepdims=True) + rk[:, 0:1]
        d2 = jnp.sum(jnp.where(h2, start_f, 0.0), -1, keepdims=True) + rk[:, 1:2]
        dest_ref[rows, :] = jnp.where(lane == 0, d1, jnp.where(lane == 1, d2, 0.0)).astype(I32)
        return c

    lax.fori_loop(0, t // tile, place, 0)


def _dispatch(ids):
    t = ids.shape[0]
    tile = _tile(t, 256)
    blk_shift = MOE_BLOCK.bit_length() - 1
    assert (1 << blk_shift) == MOE_BLOCK
    return pl.pallas_call(
        functools.partial(_dispatch_kernel, t=t, tile=tile, blk_shift=blk_shift),
        out_shape=(jax.ShapeDtypeStruct((t, LANES), I32), jax.ShapeDtypeStruct((SUBLANES, LANES), I32)),
        grid=(1,),
        in_specs=[pl.BlockSpec((t, LANES), lambda i: (0, 0))],
        out_specs=(pl.BlockSpec((t, LANES), lambda i: (0, 0)), pl.BlockSpec((SUBLANES, LANES), lambda i: (0, 0))),
        scratch_shapes=[pltpu.VMEM((t, LANES), F32)],
        compiler_params=_cparams("arbitrary"),
        name="moe_dispatch",
    )(ids)


def _scatter_rows_kernel(dest_ref, hin_ref, xs_in, xs_out, sem, *, tm):
    del xs_in
    base = pl.program_id(0) * (tm * MOE_TOPK)
    for r in range(tm):
        for k in range(MOE_TOPK):
            pltpu.make_async_copy(hin_ref.at[pl.ds(r, 1), :],
                                  xs_out.at[pl.ds(dest_ref[base + r * MOE_TOPK + k], 1), :], sem).start()
    for _ in range(MOE_TOPK):
        pltpu.make_async_copy(hin_ref, xs_out.at[pl.ds(0, tm), :], sem).wait()


def _scatter_rows(dest_flat, hin2, xs_init):
    t, d = hin2.shape
    n_rows = xs_init.shape[0]
    tm = _tile(t, 256)
    grid_spec = pltpu.PrefetchScalarGridSpec(
        num_scalar_prefetch=1,
        grid=(t // tm,),
        in_specs=[pl.BlockSpec((tm, d), lambda i, dst: (i, 0)),
                  pl.BlockSpec(memory_space=pl.ANY)],
        out_specs=pl.BlockSpec(memory_space=pl.ANY),
        scratch_shapes=[pltpu.SemaphoreType.DMA],
    )
    return pl.pallas_call(
        functools.partial(_scatter_rows_kernel, tm=tm),
        out_shape=jax.ShapeDtypeStruct((n_rows, d), hin2.dtype),
        grid_spec=grid_spec,
        input_output_aliases={2: 0},
        compiler_params=_cparams("arbitrary"),
        name="moe_scatter_rows",
    )(dest_flat, hin2, xs_init)


def _ffn_kernel(eblk_ref, enb_ref, xs_hbm, wg_ref, wu_ref, wd_ref, ys_hbm,
                xbuf, ybuf, wgb, wub, wdb, isem, osem, *, blk, nb_total, n_exp):
    e = pl.program_id(0)
    n = enb_ref[e]
    b0 = eblk_ref[e]
    used = eblk_ref[n_exp - 1] + enb_ref[n_exp - 1]

    def in_cp(gb):
        sl = gb % FFN_RING
        return pltpu.make_async_copy(xs_hbm.at[pl.ds(gb * blk, blk), :], xbuf.at[sl], isem.at[sl])

    def out_cp(gb):
        sl = gb % FFN_RING
        return pltpu.make_async_copy(ybuf.at[sl], ys_hbm.at[pl.ds(gb * blk, blk), :], osem.at[sl])

    @pl.when(e == 0)
    def _():
        for k in range(FFN_RING - 1):
            @pl.when(k < used)
            def _(k=k):
                in_cp(k).start(priority=1)

    @pl.when(n > 0)
    def _():
        wgb[...] = wg_ref[0, 0].astype(BF16)
        wub[...] = wu_ref[0, 0].astype(BF16)
        wdb[...] = wd_ref[0, 0].astype(BF16)

        def body(j, c):
            gb = b0 + j
            sl = gb % FFN_RING

            @pl.when(gb + FFN_RING - 1 < used)
            def _():
                in_cp(gb + FFN_RING - 1).start(priority=1)

            in_cp(gb).wait()

            @pl.when(gb >= FFN_RING)
            def _():
                out_cp(gb - FFN_RING).wait()

            xb = jnp.concatenate(_unpack_bf16_pair(xbuf[sl]), axis=1)
            hg = jnp.dot(xb, wgb[...], preferred_element_type=F32)
            hu = jnp.dot(xb, wub[...], preferred_element_type=F32)
            hid = (_silu(hg) * hu).astype(BF16)
            ybuf[sl] = jnp.dot(hid, wdb[...], preferred_element_type=F32)
            out_cp(gb).start(priority=1)
            return c

        lax.fori_loop(0, n, body, 0)

    @pl.when(e == n_exp - 1)
    def _():
        for k in range(FFN_RING, 0, -1):
            @pl.when(used >= k)
            def _(k=k):
                out_cp(used - k).wait()

        ybuf[0] = jnp.zeros((blk, ybuf.shape[2]), F32)

        def fill(jb, c):
            cp = pltpu.make_async_copy(ybuf.at[0], ys_hbm.at[pl.ds(jb * blk, blk), :], osem.at[0])
            cp.start()
            cp.wait()
            return c

        lax.fori_loop(used, nb_total, fill, 0)


def _moe_ffn(eblk, enb, xs, w_gate, w_up, w_down, layer):
    n_rows = xs.shape[0]
    n_exp, d, ff = w_gate.shape[1], w_gate.shape[2], w_gate.shape[3]
    blk = MOE_BLOCK
    wmap = lambda e, eb, en: (layer, e, 0, 0)
    grid_spec = pltpu.PrefetchScalarGridSpec(
        num_scalar_prefetch=2,
        grid=(n_exp,),
        in_specs=[pl.BlockSpec(memory_space=pl.ANY),
                  pl.BlockSpec((1, 1, d, ff), wmap),
                  pl.BlockSpec((1, 1, d, ff), wmap),
                  pl.BlockSpec((1, 1, ff, d), wmap)],
        out_specs=pl.BlockSpec(memory_space=pl.ANY),
        scratch_shapes=[pltpu.VMEM((FFN_RING, blk, d // 2), U32),
                        pltpu.VMEM((FFN_RING, blk, d), F32),
                        pltpu.VMEM((d, ff), BF16),
                        pltpu.VMEM((d, ff), BF16),
                        pltpu.VMEM((ff, d), BF16),
                        pltpu.SemaphoreType.DMA((FFN_RING,)),
                        pltpu.SemaphoreType.DMA((FFN_RING,))],
    )
    return pl.pallas_call(
        functools.partial(_ffn_kernel, blk=blk, nb_total=n_rows // blk, n_exp=n_exp),
        out_shape=jax.ShapeDtypeStruct((n_rows, d), F32),
        grid_spec=grid_spec,
        compiler_params=_cparams("arbitrary"),
        name="moe_ffn",
    )(eblk, enb, xs, w_gate, w_up, w_down)


def _gather_ln_kernel(dest_ref, ys_hbm, rg_ref, x_ref, g_ref, lng_ref, lnb_ref, o_ref, ybuf, sem,
                      *, alpha, tm, nt, n_steps):
    step = pl.program_id(0) * nt + pl.program_id(1)
    slot = step % 2

    def start_gather(tile_idx, sl):
        base = tile_idx * (tm * MOE_TOPK)
        for r in range(tm):
            for k in range(MOE_TOPK):
                pltpu.make_async_copy(ys_hbm.at[pl.ds(dest_ref[base + r * MOE_TOPK + k], 1), :],
                                      ybuf.at[sl, k, pl.ds(r, 1), :], sem.at[sl]).start()

    @pl.when(step == 0)
    def _():
        start_gather(0, 0)

    for sl in range(2):
        @pl.when(jnp.logical_and(step + 1 < n_steps, slot == 1 - sl))
        def _(sl=sl):
            start_gather(step + 1, sl)

    for k in range(MOE_TOPK):
        pltpu.make_async_copy(ys_hbm.at[pl.ds(0, tm), :], ybuf.at[slot, k], sem.at[slot]).wait()

    rg = rg_ref[0]
    y = ybuf[slot, 0] * rg[:, 0:1] + ybuf[slot, 1] * rg[:, 1:2]
    r = alpha * x_ref[0] + (1.0 + g_ref[0]) * y
    o_ref[0] = _layer_norm_rows(r, lng_ref[...], lnb_ref[...])


def _gather_combine_ln(dest_flat, ys, route_gates, x, gate, ln_g, ln_b, alpha, name):
    b, s, d = x.shape
    tm = _tile(s, 256)
    nt = s // tm
    row = lambda bi, i, dst: (bi, i, 0)
    const = lambda bi, i, dst: (0, 0)
    grid_spec = pltpu.PrefetchScalarGridSpec(
        num_scalar_prefetch=1,
        grid=(b, nt),
        in_specs=[pl.BlockSpec(memory_space=pl.ANY),
                  pl.BlockSpec((1, tm, LANES), row),
                  pl.BlockSpec((1, tm, d), row),
                  pl.BlockSpec((1, 1, d), lambda bi, i, dst: (bi, 0, 0)),
                  pl.BlockSpec((1, d), const),
                  pl.BlockSpec((1, d), const)],
        out_specs=pl.BlockSpec((1, tm, d), row),
        scratch_shapes=[pltpu.VMEM((2, MOE_TOPK, tm, d), F32), pltpu.SemaphoreType.DMA((2,))],
    )
    return pl.pallas_call(
        functools.partial(_gather_ln_kernel, alpha=alpha, tm=tm, nt=nt, n_steps=b * nt),
        out_shape=jax.ShapeDtypeStruct((b, s, d), F32),
        grid_spec=grid_spec,
        compiler_params=_cparams("arbitrary", "arbitrary"),
        name=name,
    )(dest_flat, ys, route_gates, x, gate, ln_g.reshape(1, d), ln_b.reshape(1, d))


def _gdn_kernel(q_ref, k_ref, v_ref, z_ref, bg_ref, cg_ref, hh_ref, ba_ref,
                cw_ref, alog_ref, dtb_ref, nw_ref, scw_ref,
                o_ref,
                qa_ref, ka_ref, va_ref, hist_ref, hist2_ref, gate_ref, state_ref,
                *, heads, ts, gw):
    L = GDN_CHUNK
    dk = GDN_HEAD_DIM

    @pl.when(pl.program_id(1) == 0)
    def _():
        hist_ref[...] = jnp.zeros_like(hist_ref)
        hist2_ref[...] = jnp.zeros_like(hist2_ref)
        state_ref[...] = jnp.zeros_like(state_ref)

    def causal_conv(x, hist, w, taps):
        g, width = ts // SUBLANES, x.shape[1]
        x3 = x.reshape(g, SUBLANES, width)
        sub = lax.broadcasted_iota(I32, (1, SUBLANES, width), 1)
        y = None
        for j in range(taps - 1):
            shift = taps - 1 - j
            rot = pltpu.roll(x3, shift, 1)
            prev = jnp.concatenate([pltpu.roll(hist, shift, 0)[None], rot[:g - 1]], axis=0)
            term = w[j:j + 1] * jnp.where(sub < shift, prev, rot)
            y = term if y is None else y + term
        return (y + w[taps - 1:taps] * x3).reshape(ts, width)

    for idx, (src, dst) in enumerate(((q_ref, qa_ref), (k_ref, ka_ref), (v_ref, va_ref))):
        x = src[0]
        y = causal_conv(x, hist_ref[idx], cw_ref[:, idx * gw:(idx + 1) * gw], GDN_CONV)
        hist_ref[idx] = x[ts - SUBLANES:ts, :]
        dst[...] = _silu(y)

    c = cg_ref[0] * hh_ref[0]
    yb = causal_conv(c, hist2_ref[...], scw_ref[...], SC_CONV)
    hist2_ref[...] = c[ts - SUBLANES:ts, :]
    o_ref[0, :, gw:2 * gw] = bg_ref[0] * yb

    ba = ba_ref[0]
    gate_ref[0] = _sigmoid(ba)
    gate_ref[1] = -jnp.exp(alog_ref[...]) * _softplus(ba + dtb_ref[...])

    ii = lax.broadcasted_iota(I32, (L, L), 0)
    jj = lax.broadcasted_iota(I32, (L, L), 1)
    causal = ii >= jj
    strict = ii > jj
    tri_incl = causal.astype(F32)
    eye = (ii == jj).astype(F32)
    nchunk = ts // L
    pairs = [(c, h) for c in range(nchunk) for h in range(heads)]

    def tiles(ref, lead=()):
        return jnp.stack([ref[lead + (slice(c * L, (c + 1) * L), slice(h * dk, (h + 1) * dk))]
                          for c, h in pairs], axis=0)

    def bmm(a, b):
        return jnp.einsum('bij,bjk->bik', a.astype(BF16), b.astype(BF16), preferred_element_type=F32)

    def bmm_nt(a, b):
        return jnp.einsum('bik,bjk->bij', a.astype(BF16), b.astype(BF16), preferred_element_type=F32)

    qh = tiles(qa_ref)
    kh = tiles(ka_ref)
    vh = tiles(va_ref)
    beta = jnp.stack([gate_ref[0, c * L:(c + 1) * L, h:h + 1] for c, h in pairs], axis=0)
    cums = [jnp.dot(tri_incl, gate_ref[1, c * L:(c + 1) * L, :], precision=HIGHEST,
                    preferred_element_type=F32) for c in range(nchunk)]
    cums_t = [cm.T for cm in cums]
    gc = jnp.stack([cums[c][:, heads + h:heads + h + 1] for c, h in pairs], axis=0)
    gr = jnp.stack([cums_t[c][heads + h:heads + h + 1, :] for c, h in pairs], axis=0)
    gl = gc[:, L - 1:L, :]
    decay = jnp.where(causal, jnp.exp(jnp.where(causal, gc - gr, 0.0)), 0.0)
    egc = jnp.exp(gc)
    qn = qh * lax.rsqrt(jnp.sum(qh * qh, -1, keepdims=True) + 1e-6)
    kn = kh * lax.rsqrt(jnp.sum(kh * kh, -1, keepdims=True) + 1e-6)
    qc = qn * (dk ** -0.5)
    kb = kn * beta
    m = jnp.where(strict, bmm_nt(kb, kn) * decay, 0.0)
    t_inv = eye - m
    mp = m
    for _ in range(5):
        mp = bmm(mp, mp)
        t_inv = t_inv + bmm(t_inv, mp)
    sol = bmm(t_inv, jnp.concatenate([vh * beta, kb * egc], axis=2))
    u = sol[:, :, 0:dk]
    w = sol[:, :, dk:2 * dk]
    qk = jnp.where(causal, bmm_nt(qc, kn) * decay, 0.0)
    qg = qc * egc
    k_dec = kn * jnp.exp(gl - gc)
    egl = jnp.exp(gl)

    state = [state_ref[h] for h in range(heads)]
    outs = []
    for b, (c, h) in enumerate(pairs):
        st = state[h]
        v_new = u[b] - _dot(w[b], st)
        outs.append(_dot(qg[b], st) + _dot(qk[b], v_new))
        state[h] = st * egl[b] + _dot(k_dec[b].T, v_new)
    for h in range(heads):
        state_ref[h] = state[h]

    o = jnp.stack(outs, axis=0)
    on = o * lax.rsqrt(jnp.mean(o * o, -1, keepdims=True) + 1e-6) * nw_ref[...]
    y = on * _silu(tiles(z_ref, (0,)))
    for b, (c, h) in enumerate(pairs):
        o_ref[0, c * L:(c + 1) * L, h * dk:(h + 1) * dk] = y[b]


def _gdn_shortconv(proj, conv_w, a_log, dt_bias, norm_w, sc_w, heads):
    b, s, _ = proj.shape
    gw = heads * GDN_HEAD_DIM
    ts = _tile(s, 256)
    alog_p = jnp.zeros((1, LANES), F32).at[0, heads:2 * heads].set(a_log)
    dtb_p = jnp.zeros((1, LANES), F32).at[0, heads:2 * heads].set(dt_bias)
    col = lambda cb: (lambda bi, i: (bi, i, cb))
    const = lambda bi, i: (0, 0)
    kern = functools.partial(_gdn_kernel, heads=heads, ts=ts, gw=gw)
    return pl.pallas_call(
        kern,
        out_shape=jax.ShapeDtypeStruct((b, s, 2 * gw), F32),
        grid=(b, s // ts),
        in_specs=[pl.BlockSpec((1, ts, gw), col(cb)) for cb in range(7)]
        + [pl.BlockSpec((1, ts, LANES), col(7 * gw // LANES)),
           pl.BlockSpec((GDN_CONV, 3 * gw), const),
           pl.BlockSpec((1, LANES), const),
           pl.BlockSpec((1, LANES), const),
           pl.BlockSpec((1, GDN_HEAD_DIM), const),
           pl.BlockSpec((SC_CONV, gw), const)],
        out_specs=pl.BlockSpec((1, ts, 2 * gw), lambda bi, i: (bi, i, 0)),
        scratch_shapes=[pltpu.VMEM((ts, gw), F32),
                        pltpu.VMEM((ts, gw), F32),
                        pltpu.VMEM((ts, gw), F32),
                        pltpu.VMEM((3, SUBLANES, gw), F32),
                        pltpu.VMEM((SUBLANES, gw), F32),
                        pltpu.VMEM((2, ts, LANES), F32),
                        pltpu.VMEM((heads, GDN_HEAD_DIM, GDN_HEAD_DIM), F32)],
        compiler_params=_cparams("parallel", "arbitrary"),
        name="gdn_shortconv",
    )(proj, proj, proj, proj, proj, proj, proj, proj,
      conv_w, alog_p, dtb_p, norm_w.reshape(1, GDN_HEAD_DIM), sc_w)


def _rope_kernel(pos_ref, inv_ref, qpe_ref, kpe_ref, qo_ref, ko_ref, *, heads):
    ang = pos_ref[0].astype(F32) * inv_ref[...]
    lane = lax.broadcasted_iota(I32, ang.shape, 1)
    half = MLA_ROPE // 2
    cos = jnp.where(lane < MLA_ROPE, jnp.cos(ang), 0.0)
    sin = jnp.sin(ang)
    sgn = jnp.where(lane < half, -sin, jnp.where(lane < MLA_ROPE, sin, 0.0))

    def rot(x):
        swapped = jnp.where(lane < half, pltpu.roll(x, LANES - half, 1), pltpu.roll(x, half, 1))
        return x * cos + swapped * sgn

    for h in range(heads):
        cols = slice(h * LANES, (h + 1) * LANES)
        qo_ref[0, :, cols] = rot(qpe_ref[0, :, cols]).astype(BF16)
    ko_ref[0] = rot(kpe_ref[0]).astype(BF16)


def _rope(positions, qpe, heads, cproj, kpe_block):
    b, s = positions.shape
    ts = _tile(s, 512)
    half = MLA_ROPE // 2
    inv = ROPE_THETA ** (-jnp.arange(half, dtype=F32) * (2.0 / MLA_ROPE))
    inv_tab = jnp.zeros((1, LANES), F32).at[0, 0:half].set(inv).at[0, half:MLA_ROPE].set(inv)
    w = heads * LANES
    return pl.pallas_call(
        functools.partial(_rope_kernel, heads=heads),
        out_shape=(jax.ShapeDtypeStruct((b, s, w), BF16), jax.ShapeDtypeStruct((b, s, LANES), BF16)),
        grid=(b, s // ts),
        in_specs=[pl.BlockSpec((1, ts, 1), lambda bi, i: (bi, i, 0)),
                  pl.BlockSpec((1, LANES), lambda bi, i: (0, 0)),
                  pl.BlockSpec((1, ts, w), lambda bi, i: (bi, i, 0)),
                  pl.BlockSpec((1, ts, LANES), lambda bi, i: (bi, i, kpe_block))],
        out_specs=(pl.BlockSpec((1, ts, w), lambda bi, i: (bi, i, 0)),
                   pl.BlockSpec((1, ts, LANES), lambda bi, i: (bi, i, 0))),
        compiler_params=_cparams("parallel", "parallel"),
        name="mla_rope",
    )(positions.reshape(b, s, 1), inv_tab, qpe, cproj)


def _attn_kernel(qn_ref, qp_ref, kn_ref, kp_ref, v_ref, o_ref, kcat_ref, s_ref, *, tq, nq, scale):
    qi = pl.program_id(2)

    @pl.when(qi == 0)
    def _():
        kcat_ref[:, 0:MLA_NOPE] = kn_ref[0]
        kcat_ref[:, MLA_NOPE:MLA_NOPE + LANES] = kp_ref[0]

    qcat = jnp.concatenate([qn_ref[0], qp_ref[0]], axis=1)
    hq = tq // 2
    tri = lax.broadcasted_iota(I32, (hq, hq), 0) >= lax.broadcasted_iota(I32, (hq, hq), 1)

    for qv in range(nq):
        @pl.when(qi == qv)
        def _(qv=qv):
            for h in range(2):
                rows = slice(h * hq, (h + 1) * hq)
                nk = qv * tq + (h + 1) * hq
                for c0 in range(0, nk, tq):
                    c1 = min(c0 + tq, nk)
                    s = lax.dot_general(qcat[rows], kcat_ref[c0:c1, :], (((1,), (1,)), ((), ())),
                                        preferred_element_type=F32) * scale
                    if c1 == nk:
                        w = c1 - c0
                        last = jnp.where(tri, s[:, w - hq:], -jnp.inf)
                        s = last if w == hq else jnp.concatenate([s[:, :w - hq], last], axis=1)
                    s_ref[rows, c0:c1] = s
                    mb = jnp.max(s, -1, keepdims=True)
                    m = mb if c0 == 0 else jnp.maximum(m, mb)
                l = jnp.zeros((hq, 1), F32)
                acc = jnp.zeros((hq, MLA_V), F32)
                for c0 in range(0, nk, hq):
                    p = jnp.exp(s_ref[rows, c0:c0 + hq] - m)
                    l = l + jnp.sum(p, -1, keepdims=True)
                    acc = acc + jnp.dot(p.astype(BF16), v_ref[0, c0:c0 + hq, :], preferred_element_type=F32)
                o_ref[0, rows, :] = acc / l


def _attention(qn, qpe, kvup, kpe, heads):
    b, s, _ = qn.shape
    tq = _tile(s, 512)
    nq = s // tq
    scale = (MLA_NOPE + MLA_ROPE) ** -0.5
    return pl.pallas_call(
        functools.partial(_attn_kernel, tq=tq, nq=nq, scale=scale),
        out_shape=jax.ShapeDtypeStruct((b, s, heads * MLA_V), F32),
        grid=(b, heads, nq),
        in_specs=[pl.BlockSpec((1, tq, MLA_NOPE), lambda bi, h, i: (bi, i, h)),
                  pl.BlockSpec((1, tq, LANES), lambda bi, h, i: (bi, i, h)),
                  pl.BlockSpec((1, s, MLA_NOPE), lambda bi, h, i: (bi, 0, h)),
                  pl.BlockSpec((1, s, LANES), lambda bi, h, i: (bi, 0, 0)),
                  pl.BlockSpec((1, s, MLA_V), lambda bi, h, i: (bi, 0, heads + h))],
        out_specs=pl.BlockSpec((1, tq, MLA_V), lambda bi, h, i: (bi, i, h)),
        scratch_shapes=[pltpu.VMEM((s, MLA_NOPE + LANES), BF16), pltpu.VMEM((tq, s), F32)],
        compiler_params=_cparams("parallel", "parallel", "arbitrary"),
        name="mla_attention",
    )(qn, qpe, kvup, kpe, kvup)


def _hyb_in_weight(w_in, heads):
    d = w_in.shape[0]
    gw = heads * GDN_HEAD_DIM
    main = jnp.concatenate([w_in[:, :4 * gw], w_in[:, 4 * gw + 2 * heads:]], axis=1)
    gates = w_in[:, 4 * gw:4 * gw + 2 * heads]
    n = main.shape[1] + LANES
    n_pad = -(-n // 768) * 768
    tail = jnp.zeros((d, n_pad - main.shape[1]), w_in.dtype).at[:, :2 * heads].set(gates)
    return jnp.concatenate([main, tail], axis=1).astype(BF16)


def _mla_in_weight(w_in):
    d = w_in.shape[0]
    pad = jnp.zeros((d, LANES - MLA_ROPE), w_in.dtype)
    return jnp.concatenate([w_in, pad], axis=1).astype(BF16)


def _mla_uq_weights(w_uq, heads):
    r = w_uq.shape[0]
    w = w_uq.reshape(r, heads, MLA_NOPE + MLA_ROPE)
    nope = w[:, :, :MLA_NOPE].reshape(r, heads * MLA_NOPE)
    pe = jnp.concatenate([w[:, :, MLA_NOPE:], jnp.zeros((r, heads, LANES - MLA_ROPE), w.dtype)], axis=2)
    return nope.astype(BF16), pe.reshape(r, heads * LANES).astype(BF16)


def _mla_ukv_weight(w_ukv, heads):
    r = w_ukv.shape[0]
    w = w_ukv.reshape(r, heads, MLA_NOPE + MLA_V)
    return jnp.concatenate([w[:, :, :MLA_NOPE].reshape(r, heads * MLA_NOPE),
                            w[:, :, MLA_NOPE:].reshape(r, heads * MLA_V)], axis=1).astype(BF16)


def _router_weight(wr_g, br_g, wr_e, br_e):
    d = wr_g.shape[0]
    n = wr_g.shape[1] + wr_e.shape[1]
    w = jnp.zeros((d, LANES), F32).at[:, :n].set(jnp.concatenate([wr_g, wr_e], axis=1))
    bias = jnp.zeros((1, LANES), F32).at[0, :n].set(jnp.concatenate([br_g, br_e]))
    w_hi = w.astype(BF16)
    w_lo = (w - w_hi.astype(F32)).astype(BF16)
    return jnp.concatenate([w_hi, w_lo], axis=1), bias


def kernel(x, c, positions, ada_w, ada_b, ln_g, ln_b, hyb_w_in, gdn_conv_w, gdn_a_log, gdn_dt_bias,
           gdn_norm_w, sc_conv_w, hyb_w_out, mla_w_in, mla_q_norm, mla_kv_norm, mla_w_uq, mla_w_ukv,
           mla_w_out, moe_router_g, moe_bias_g, moe_router_e, moe_bias_e, moe_w_gate, moe_w_up,
           moe_w_down):
    b, s, d = x.shape
    depth = ada_w.shape[0]
    alpha = (2.0 * depth) ** 0.25
    gdn_heads = d // (2 * GDN_HEAD_DIM)
    mla_heads = mla_w_out.shape[1] // MLA_V
    n_groups = moe_router_g.shape[2]
    n_exp = moe_router_e.shape[2]
    per_group = n_exp // n_groups

    mod = _modulation(c, ada_w, ada_b)
    t = b * s
    xs = jnp.zeros((t * MOE_TOPK + n_exp * MOE_BLOCK, d // 2), U32)
    for layer in range(depth):
        sh1, sc1, g1, sh2, sc2, g2 = [mod[layer, :, j * d:(j + 1) * d].reshape(b, 1, d) for j in range(6)]
        i = layer // 2
        wr, br = _router_weight(moe_router_g[layer], moe_bias_g[layer], moe_router_e[layer], moe_bias_e[layer])
        if layer % 2 == 0:
            proj = _mod_matmul(x, sc1, sh1, _hyb_in_weight(hyb_w_in[i], gdn_heads), "hyb_in_proj")
            y = _gdn_shortconv(proj, gdn_conv_w[i], gdn_a_log[i], gdn_dt_bias[i], gdn_norm_w[i],
                               sc_conv_w[i], gdn_heads)
            w_out = hyb_w_out[i]
        else:
            cproj = _mod_matmul(x, sc1, sh1, _mla_in_weight(mla_w_in[i]), "mla_in_proj")
            w_qn, w_qpe = _mla_uq_weights(mla_w_uq[i], mla_heads)
            qn = _rms_matmul(cproj, 0, MLA_Q_RANK, mla_q_norm[i], w_qn, BF16, "mla_q_nope_up")
            qpe_raw = _rms_matmul(cproj, 0, MLA_Q_RANK, mla_q_norm[i], w_qpe, F32, "mla_q_rope_up")
            kvup = _rms_matmul(cproj, 1, MLA_KV_RANK, mla_kv_norm[i], _mla_ukv_weight(mla_w_ukv[i], mla_heads),
                               BF16, "mla_kv_up")
            qpe, kpe = _rope(positions, qpe_raw, mla_heads, cproj, (MLA_Q_RANK + MLA_KV_RANK) // LANES)
            y = _attention(qn, qpe, kvup, kpe, mla_heads)
            w_out = mla_w_out[i]
        x, hin2, ids, gates = _out_ln_route(y, w_out.astype(BF16), x, g1, ln_g[layer, 0], ln_b[layer, 0],
                                            sc2, sh2, wr, br, alpha, n_groups, per_group,
                                            "mixer_out_ln_route")
        dest, meta = _dispatch(ids.reshape(t, LANES))
        dest_flat = dest[:, :MOE_TOPK].reshape(-1)
        eblk = meta[0, :n_exp] // MOE_BLOCK
        enb = meta[1, :n_exp]
        xs = _scatter_rows(dest_flat, hin2.reshape(t, d // 2), xs)
        ys = _moe_ffn(eblk, enb, xs, moe_w_gate, moe_w_up, moe_w_down, layer)
        x = _gather_combine_ln(dest_flat, ys, gates, x, g2, ln_g[layer, 1], ln_b[layer, 1], alpha,
                               "moe_gather_combine_ln")
    return x
```

```python
import functools

import jax
import jax.numpy as jnp
from jax import lax
from jax.experimental import pallas as pl
from jax.experimental.pallas import tpu as pltpu

F32, BF16, I32, U32 = jnp.float32, jnp.bfloat16, jnp.int32, jnp.uint32
HIGHEST = lax.Precision.HIGHEST

LANES = 128
SUBLANES = 8
VMEM_LIMIT_BYTES = 56 * 1024 * 1024

GDN_HEAD_DIM = 128
GDN_CONV = 4
GDN_CHUNK = 64
SC_CONV = 3
MLA_NOPE = 128
MLA_ROPE = 64
MLA_V = 128
MLA_Q_RANK = 512
MLA_KV_RANK = 512
ROPE_THETA = 10000.0
MOE_TOPK = 2
MOE_BLOCK = 128
FFN_RING = 6
FFN_CAST_ROWS = 512


def _tile(n, pref):
    if n <= pref:
        return n
    for t in range(pref, 0, -LANES):
        if n % t == 0:
            return t
    return n


def _cparams(*sem):
    return pltpu.CompilerParams(dimension_semantics=sem, vmem_limit_bytes=VMEM_LIMIT_BYTES)


def _sigmoid(x):
    return 1.0 / (1.0 + jnp.exp(-x))


def _silu(x):
    return x * _sigmoid(x)


def _softplus(x):
    return jnp.maximum(x, 0.0) + jnp.log(1.0 + jnp.exp(-jnp.abs(x)))


def _pack_bf16_pair(lo, hi):
    lo_bits = lax.bitcast_convert_type(lo.astype(BF16).astype(F32), U32)
    hi_bits = lax.bitcast_convert_type(hi.astype(BF16).astype(F32), U32)
    return (lo_bits >> 16) | (hi_bits & jnp.uint32(0xFFFF0000))


def _unpack_bf16_pair(w):
    lo = lax.bitcast_convert_type(w << 16, F32)
    hi = lax.bitcast_convert_type(w & jnp.uint32(0xFFFF0000), F32)
    return lo.astype(BF16), hi.astype(BF16)


def _dot(a, b):
    return jnp.dot(a.astype(BF16), b.astype(BF16), preferred_element_type=F32)


def _dot_nt(a, b):
    return lax.dot_general(a.astype(BF16), b.astype(BF16), (((1,), (1,)), ((), ())),
                           preferred_element_type=F32)


def _causal_conv(x, hist, w, taps):
    rows, width = x.shape
    g = rows // SUBLANES
    x3 = x.reshape(g, SUBLANES, width)
    sub = lax.broadcasted_iota(I32, (1, SUBLANES, width), 1)
    y = None
    for j in range(taps - 1):
        shift = taps - 1 - j
        rot = pltpu.roll(x3, shift, 1)
        prev = jnp.concatenate([pltpu.roll(hist, shift, 0)[None], rot[:g - 1]], axis=0)
        term = w[j:j + 1] * jnp.where(sub < shift, prev, rot)
        y = term if y is None else y + term
    return (y + w[taps - 1:taps] * x3).reshape(rows, width)


def _mod_kernel(c_ref, w_ref, b_ref, o_ref):
    c = c_ref[...]
    o_ref[0] = _dot(_silu(c), w_ref[0]) + b_ref[0]


def _modulation(c, ada_w, ada_b):
    depth, d, n = ada_w.shape
    b = c.shape[0]
    rows = -(-b // SUBLANES) * SUBLANES
    cp = jnp.zeros((rows, d), F32).at[:b].set(c)
    tn = _tile(n, 1024)
    out = pl.pallas_call(
        _mod_kernel,
        out_shape=jax.ShapeDtypeStruct((depth, rows, n), F32),
        grid=(depth, n // tn),
        in_specs=[pl.BlockSpec((rows, d), lambda l, j: (0, 0)),
                  pl.BlockSpec((1, d, tn), lambda l, j: (l, 0, j)),
                  pl.BlockSpec((1, 1, tn), lambda l, j: (l, 0, j))],
        out_specs=pl.BlockSpec((1, rows, tn), lambda l, j: (l, 0, j)),
        compiler_params=_cparams("parallel", "parallel"),
        name="adaln_mod",
    )(cp, ada_w, ada_b.reshape(depth, 1, n))
    return out[:, :b]


def _modmm_kernel(x_ref, sc_ref, sh_ref, w_ref, cw_ref, o_ref, xb_ref, hist_ref, *, conv_tiles, taps):
    i = pl.program_id(1)
    j = pl.program_id(2)

    @pl.when(j == 0)
    def _():
        xb_ref[...] = (x_ref[0] * (1.0 + sc_ref[0]) + sh_ref[0]).astype(BF16)

    y = jnp.dot(xb_ref[...], w_ref[...], preferred_element_type=F32)
    if conv_tiles == 0:
        o_ref[0] = y
        return

    @pl.when(j < conv_tiles)
    def _():
        jc = jnp.minimum(j, conv_tiles - 1)
        hist = jnp.where(i == 0, 0.0, hist_ref[jc])
        o_ref[0] = _silu(_causal_conv(y, hist, cw_ref[...], taps))
        hist_ref[jc] = y[y.shape[0] - SUBLANES:, :]

    @pl.when(j >= conv_tiles)
    def _():
        o_ref[0] = y


def _mod_matmul(x, sc, sh, w, name, conv_w=None):
    b, s, d = x.shape
    n = w.shape[1]
    tm = _tile(s, 1024)
    tn = _tile(n, 768)
    if conv_w is None:
        conv_tiles, taps = 0, 1
        conv_w = jnp.zeros((taps, tn), F32)
    else:
        taps = conv_w.shape[0]
        assert conv_w.shape[1] % tn == 0 and tm % SUBLANES == 0
        conv_tiles = conv_w.shape[1] // tn
    last_conv = max(conv_tiles - 1, 0)
    return pl.pallas_call(
        functools.partial(_modmm_kernel, conv_tiles=conv_tiles, taps=taps),
        out_shape=jax.ShapeDtypeStruct((b, s, n), F32),
        grid=(b, s // tm, n // tn),
        in_specs=[pl.BlockSpec((1, tm, d), lambda bi, i, j: (bi, i, 0)),
                  pl.BlockSpec((1, 1, d), lambda bi, i, j: (bi, 0, 0)),
                  pl.BlockSpec((1, 1, d), lambda bi, i, j: (bi, 0, 0)),
                  pl.BlockSpec((d, tn), lambda bi, i, j: (0, j)),
                  pl.BlockSpec((taps, tn), lambda bi, i, j: (0, jnp.minimum(j, last_conv)))],
        out_specs=pl.BlockSpec((1, tm, tn), lambda bi, i, j: (bi, i, j)),
        scratch_shapes=[pltpu.VMEM((tm, d), BF16), pltpu.VMEM((max(conv_tiles, 1), SUBLANES, tn), F32)],
        compiler_params=_cparams("parallel", "arbitrary", "arbitrary"),
        name=name,
    )(x, sc, sh, w, conv_w)


def _rmsmm_kernel(x_ref, nw_ref, w_ref, o_ref, xb_ref):
    @pl.when(pl.program_id(2) == 0)
    def _():
        x = x_ref[0]
        xb_ref[...] = (x * lax.rsqrt(jnp.mean(x * x, -1, keepdims=True) + 1e-6) * nw_ref[...]).astype(BF16)

    o_ref[0] = jnp.dot(xb_ref[...], w_ref[...], preferred_element_type=F32).astype(o_ref.dtype)


def _rms_matmul(x, col_block, k, norm_w, w, out_dtype, name):
    b, s, _ = x.shape
    n = w.shape[1]
    tm = _tile(s, 1024)
    tn = _tile(n, 1024)
    return pl.pallas_call(
        _rmsmm_kernel,
        out_shape=jax.ShapeDtypeStruct((b, s, n), out_dtype),
        grid=(b, s // tm, n // tn),
        in_specs=[pl.BlockSpec((1, tm, k), lambda bi, i, j: (bi, i, col_block)),
                  pl.BlockSpec((1, k), lambda bi, i, j: (0, 0)),
                  pl.BlockSpec((k, tn), lambda bi, i, j: (0, j))],
        out_specs=pl.BlockSpec((1, tm, tn), lambda bi, i, j: (bi, i, j)),
        scratch_shapes=[pltpu.VMEM((tm, k), BF16)],
        compiler_params=_cparams("parallel", "parallel", "arbitrary"),
        name=name,
    )(x, norm_w.reshape(1, k), w)


def _layer_norm_rows(r, g, b):
    mu = jnp.mean(r, -1, keepdims=True)
    var = jnp.mean(jnp.square(r - mu), -1, keepdims=True)
    return (r - mu) * lax.rsqrt(var + 1e-5) * g + b


def _outln_kernel(y_ref, w_ref, xres_ref, g_ref, lng_ref, lnb_ref, sc_ref, sh_ref, wr_ref, br_ref,
                  xo_ref, hin_ref, ids_ref, gates_ref, *, alpha, n_groups, per_group, parts):
    tp = y_ref.shape[1] // parts
    for part in range(parts):
        _outln_rows(slice(part * tp, (part + 1) * tp), y_ref, w_ref, xres_ref, g_ref, lng_ref, lnb_ref,
                    sc_ref, sh_ref, wr_ref, br_ref, xo_ref, hin_ref, ids_ref, gates_ref,
                    alpha=alpha, n_groups=n_groups, per_group=per_group)


def _outln_rows(rows, y_ref, w_ref, xres_ref, g_ref, lng_ref, lnb_ref, sc_ref, sh_ref, wr_ref, br_ref,
                xo_ref, hin_ref, ids_ref, gates_ref, *, alpha, n_groups, per_group):
    y = jnp.dot(y_ref[0, rows, :].astype(BF16), w_ref[...], preferred_element_type=F32)
    r = alpha * xres_ref[0, rows, :] + (1.0 + g_ref[0]) * y
    xn = _layer_norm_rows(r, lng_ref[...], lnb_ref[...])
    xo_ref[0, rows, :] = xn
    hin = xn * (1.0 + sc_ref[0]) + sh_ref[0]
    half = hin.shape[1] // 2
    hin_ref[0, rows, :] = _pack_bf16_pair(hin[:, :half], hin[:, half:])
    h_hi = hin.astype(BF16)
    h_lo = (hin - h_hi.astype(F32)).astype(BF16)
    ph = jnp.dot(h_hi, wr_ref[...], preferred_element_type=F32)
    plo = jnp.dot(h_lo, wr_ref[...], preferred_element_type=F32)
    logits = (ph[:, :LANES] + ph[:, LANES:]) + (plo[:, :LANES] + plo[:, LANES:]) + br_ref[...]
    lane = lax.broadcasted_iota(I32, logits.shape, 1)
    big = jnp.int32(4 * LANES)
    neg = jnp.float32(-jnp.inf)
    n_exp = n_groups * per_group
    gmask = lane < n_groups
    lg = jnp.where(gmask, logits, neg)
    mg = jnp.max(lg, -1, keepdims=True)
    grp = jnp.min(jnp.where(gmask & (lg == mg), lane, big), -1, keepdims=True)
    pg_sel = 1.0 / jnp.sum(jnp.where(gmask, jnp.exp(lg - mg), 0.0), -1, keepdims=True)
    lo = n_groups + grp * per_group
    emask = (lane >= lo) & (lane < lo + per_group) & (lane < n_groups + n_exp)
    le = jnp.where(emask, logits, neg)
    me = jnp.max(le, -1, keepdims=True)
    ee = jnp.where(emask, jnp.exp(le - me), 0.0)
    p = ee / jnp.sum(ee, -1, keepdims=True)
    pm = jnp.where(emask, p, -1.0)
    p1 = jnp.max(pm, -1, keepdims=True)
    i1 = jnp.min(jnp.where(emask & (pm == p1), lane, big), -1, keepdims=True)
    pm2 = jnp.where(lane == i1, -1.0, pm)
    p2 = jnp.max(pm2, -1, keepdims=True)
    i2 = jnp.min(jnp.where(emask & (lane != i1) & (pm2 == p2), lane, big), -1, keepdims=True)
    den = p1 + p2
    g1 = pg_sel * p1 / den
    g2 = pg_sel * p2 / den
    ids_ref[0, rows, :] = jnp.where(lane == 0, i1 - n_groups, jnp.where(lane == 1, i2 - n_groups, 0))
    gates_ref[0, rows, :] = jnp.where(lane == 0, g1, jnp.where(lane == 1, g2, 0.0))


def _out_ln_route(y, w, xres, gate, ln_g, ln_b, sc, sh, wr, br, alpha, n_groups, per_group, name):
    b, s, k = y.shape
    d = w.shape[1]
    tm = _tile(s, 512)
    parts = 2 if tm % (2 * SUBLANES) == 0 else 1
    kern = functools.partial(_outln_kernel, alpha=alpha, n_groups=n_groups, per_group=per_group, parts=parts)
    row = lambda bi, i: (bi, i, 0)
    per_b = lambda bi, i: (bi, 0, 0)
    const = lambda bi, i: (0, 0)
    return pl.pallas_call(
        kern,
        out_shape=(jax.ShapeDtypeStruct((b, s, d), F32), jax.ShapeDtypeStruct((b, s, d // 2), U32),
                   jax.ShapeDtypeStruct((b, s, LANES), I32), jax.ShapeDtypeStruct((b, s, LANES), F32)),
        grid=(b, s // tm),
        in_specs=[pl.BlockSpec((1, tm, k), row),
                  pl.BlockSpec((k, d), const),
                  pl.BlockSpec((1, tm, d), row),
                  pl.BlockSpec((1, 1, d), per_b),
                  pl.BlockSpec((1, d), const),
                  pl.BlockSpec((1, d), const),
                  pl.BlockSpec((1, 1, d), per_b),
                  pl.BlockSpec((1, 1, d), per_b),
                  pl.BlockSpec((d, 2 * LANES), const),
                  pl.BlockSpec((1, LANES), const)],
        out_specs=(pl.BlockSpec((1, tm, d), row), pl.BlockSpec((1, tm, d // 2), row),
                   pl.BlockSpec((1, tm, LANES), row), pl.BlockSpec((1, tm, LANES), row)),
        compiler_params=_cparams("parallel", "parallel"),
        name=name,
    )(y, w, xres, gate, ln_g.reshape(1, d), ln_b.reshape(1, d), sc, sh, wr, br)


def _dispatch_kernel(ids_ref, dest_ref, meta_ref, rank_ref, *, t, tile, blk_shift):
    lane = lax.broadcasted_iota(I32, (tile, LANES), 1)
    ri = lax.broadcasted_iota(I32, (tile, tile), 0)
    ci = lax.broadcasted_iota(I32, (tile, tile), 1)
    before = (ri > ci).astype(BF16)

    def hits(rows):
        ids = ids_ref[rows, :]
        return lane == ids[:, 0:1], lane == ids[:, 1:2]

    def count(ti, run):
        rows = pl.ds(pl.multiple_of(ti * tile, tile), tile)
        h1, h2 = hits(rows)
        onehot = jnp.logical_or(h1, h2).astype(BF16)
        prefix = jnp.dot(before, onehot, preferred_element_type=F32) + run
        r1 = jnp.sum(jnp.where(h1, prefix, 0.0), -1, keepdims=True)
        r2 = jnp.sum(jnp.where(h2, prefix, 0.0), -1, keepdims=True)
        rank_ref[rows, :] = jnp.where(lane == 0, r1, jnp.where(lane == 1, r2, 0.0))
        return run + jnp.sum(onehot.astype(F32), 0, keepdims=True)

    counts = lax.fori_loop(0, t // tile, count, jnp.zeros((1, LANES), F32))
    cnt = jnp.broadcast_to(counts, (SUBLANES, LANES)).astype(I32)
    nblk = (cnt + ((1 << blk_shift) - 1)) >> blk_shift
    padded = nblk << blk_shift
    lane8 = lax.broadcasted_iota(I32, (SUBLANES, LANES), 1)
    incl = padded
    step = 1
    while step < LANES:
        incl = incl + jnp.where(lane8 >= step, pltpu.roll(incl, step, 1), 0)
        step *= 2
    pad_start = incl - padded
    row8 = lax.broadcasted_iota(I32, (SUBLANES, LANES), 0)
    meta_ref[...] = jnp.where(row8 == 0, pad_start, jnp.where(row8 == 1, nblk, cnt))
    start_f = pad_start[0:1, :].astype(F32)

    def place(ti, c):
        rows = pl.ds(pl.multiple_of(ti * tile, tile), tile)
        h1, h2 = hits(rows)
        rk = rank_ref[rows, :]
        d1 = jnp.sum(jnp.where(h1, start_f, 0.0), -1, keepdims=True) + rk[:, 0:1]
        d2 = jnp.sum(jnp.where(h2, start_f, 0.0), -1, keepdims=True) + rk[:, 1:2]
        dest_ref[rows, :] = jnp.where(lane == 0, d1, jnp.where(lane == 1, d2, 0.0)).astype(I32)
        return c

    lax.fori_loop(0, t // tile, place, 0)


def _dispatch(ids):
    t = ids.shape[0]
    tile = _tile(t, 256)
    blk_shift = MOE_BLOCK.bit_length() - 1
    assert (1 << blk_shift) == MOE_BLOCK
    return pl.pallas_call(
        functools.partial(_dispatch_kernel, t=t, tile=tile, blk_shift=blk_shift),
        out_shape=(jax.ShapeDtypeStruct((t, LANES), I32), jax.ShapeDtypeStruct((SUBLANES, LANES), I32)),
        grid=(1,),
        in_specs=[pl.BlockSpec((t, LANES), lambda i: (0, 0))],
        out_specs=(pl.BlockSpec((t, LANES), lambda i: (0, 0)), pl.BlockSpec((SUBLANES, LANES), lambda i: (0, 0))),
        scratch_shapes=[pltpu.VMEM((t, LANES), F32)],
        compiler_params=_cparams("arbitrary"),
        name="moe_dispatch",
    )(ids)


def _scatter_rows_kernel(dest_ref, hin_ref, xs_in, xs_out, sem, *, tm):
    del xs_in
    base = pl.program_id(0) * (tm * MOE_TOPK)
    for r in range(tm):
        for k in range(MOE_TOPK):
            pltpu.make_async_copy(hin_ref.at[pl.ds(r, 1), :],
                                  xs_out.at[pl.ds(dest_ref[base + r * MOE_TOPK + k], 1), :], sem).start()
    for _ in range(MOE_TOPK):
        pltpu.make_async_copy(hin_ref, xs_out.at[pl.ds(0, tm), :], sem).wait()


def _scatter_rows(dest_flat, hin2, xs_init):
    t, d = hin2.shape
    n_rows = xs_init.shape[0]
    tm = _tile(t, 256)
    grid_spec = pltpu.PrefetchScalarGridSpec(
        num_scalar_prefetch=1,
        grid=(t // tm,),
        in_specs=[pl.BlockSpec((tm, d), lambda i, dst: (i, 0)),
                  pl.BlockSpec(memory_space=pl.ANY)],
        out_specs=pl.BlockSpec(memory_space=pl.ANY),
        scratch_shapes=[pltpu.SemaphoreType.DMA],
    )
    return pl.pallas_call(
        functools.partial(_scatter_rows_kernel, tm=tm),
        out_shape=jax.ShapeDtypeStruct((n_rows, d), hin2.dtype),
        grid_spec=grid_spec,
        input_output_aliases={2: 0},
        compiler_params=_cparams("arbitrary"),
        name="moe_scatter_rows",
    )(dest_flat, hin2, xs_init)


def _ffn_kernel(eblk_ref, enb_ref, xs_hbm, wg_ref, wu_ref, wd_ref, ys_hbm,
                xbuf, ybuf, wgb, wub, wdb, isem, osem, *, blk, nb_total, n_exp):
    e = pl.program_id(0)
    n = enb_ref[e]
    b0 = eblk_ref[e]
    used = eblk_ref[n_exp - 1] + enb_ref[n_exp - 1]

    def in_cp(gb):
        sl = gb % FFN_RING
        return pltpu.make_async_copy(xs_hbm.at[pl.ds(gb * blk, blk), :], xbuf.at[sl], isem.at[sl])

    def out_cp(gb):
        sl = gb % FFN_RING
        return pltpu.make_async_copy(ybuf.at[sl], ys_hbm.at[pl.ds(gb * blk, blk), :], osem.at[sl])

    @pl.when(e == 0)
    def _():
        for k in range(FFN_RING - 1):
            @pl.when(k < used)
            def _(k=k):
                in_cp(k).start(priority=1)

    def ffn(xb, first):
        d, ff = wgb.shape
        if first:
            hg = jnp.zeros((blk, ff), F32)
            hu = jnp.zeros((blk, ff), F32)
            for k0 in range(0, d, FFN_CAST_ROWS):
                ks = slice(k0, k0 + FFN_CAST_ROWS)
                wg = wg_ref[0, 0, ks, :].astype(BF16)
                wu = wu_ref[0, 0, ks, :].astype(BF16)
                wgb[ks, :] = wg
                wub[ks, :] = wu
                hg = hg + jnp.dot(xb[:, ks], wg, preferred_element_type=F32)
                hu = hu + jnp.dot(xb[:, ks], wu, preferred_element_type=F32)
            wd = wd_ref[0, 0].astype(BF16)
            wdb[...] = wd
        else:
            hg = jnp.dot(xb, wgb[...], preferred_element_type=F32)
            hu = jnp.dot(xb, wub[...], preferred_element_type=F32)
            wd = wdb[...]
        hid = (_silu(hg) * hu).astype(BF16)
        return jnp.dot(hid, wd, preferred_element_type=F32)

    def body(j, c):
        gb = b0 + j
        sl = gb % FFN_RING

        @pl.when(gb + FFN_RING - 1 < used)
        def _():
            in_cp(gb + FFN_RING - 1).start(priority=1)

        in_cp(gb).wait()

        @pl.when(gb >= FFN_RING)
        def _():
            out_cp(gb - FFN_RING).wait()

        xb = jnp.concatenate(_unpack_bf16_pair(xbuf[sl]), axis=1)

        @pl.when(j == 0)
        def _():
            ybuf[sl] = ffn(xb, True)

        @pl.when(j > 0)
        def _():
            ybuf[sl] = ffn(xb, False)

        out_cp(gb).start(priority=1)
        return c

    lax.fori_loop(0, n, body, 0)

    @pl.when(e == n_exp - 1)
    def _():
        for k in range(FFN_RING, 0, -1):
            @pl.when(used >= k)
            def _(k=k):
                out_cp(used - k).wait()

        ybuf[0] = jnp.zeros((blk, ybuf.shape[2]), F32)

        def fill(jb, c):
            cp = pltpu.make_async_copy(ybuf.at[0], ys_hbm.at[pl.ds(jb * blk, blk), :], osem.at[0])
            cp.start()
            cp.wait()
            return c

        lax.fori_loop(used, nb_total, fill, 0)


def _moe_ffn(eblk, enb, xs, w_gate, w_up, w_down, layer):
    n_rows = xs.shape[0]
    n_exp, d, ff = w_gate.shape[1], w_gate.shape[2], w_gate.shape[3]
    blk = MOE_BLOCK
    wmap = lambda e, eb, en: (layer, e, 0, 0)
    grid_spec = pltpu.PrefetchScalarGridSpec(
        num_scalar_prefetch=2,
        grid=(n_exp,),
        in_specs=[pl.BlockSpec(memory_space=pl.ANY),
                  pl.BlockSpec((1, 1, d, ff), wmap),
                  pl.BlockSpec((1, 1, d, ff), wmap),
                  pl.BlockSpec((1, 1, ff, d), wmap)],
        out_specs=pl.BlockSpec(memory_space=pl.ANY),
        scratch_shapes=[pltpu.VMEM((FFN_RING, blk, d // 2), U32),
                        pltpu.VMEM((FFN_RING, blk, d), F32),
                        pltpu.VMEM((d, ff), BF16),
                        pltpu.VMEM((d, ff), BF16),
                        pltpu.VMEM((ff, d), BF16),
                        pltpu.SemaphoreType.DMA((FFN_RING,)),
                        pltpu.SemaphoreType.DMA((FFN_RING,))],
    )
    return pl.pallas_call(
        functools.partial(_ffn_kernel, blk=blk, nb_total=n_rows // blk, n_exp=n_exp),
        out_shape=jax.ShapeDtypeStruct((n_rows, d), F32),
        grid_spec=grid_spec,
        compiler_params=_cparams("arbitrary"),
        name="moe_ffn",
    )(eblk, enb, xs, w_gate, w_up, w_down)


def _gather_ln_kernel(dest_ref, ys_hbm, rg_ref, x_ref, g_ref, lng_ref, lnb_ref, o_ref, ybuf, sem,
                      *, alpha, tm, nt, n_steps):
    step = pl.program_id(0) * nt + pl.program_id(1)
    slot = step % 2

    def start_gather(tile_idx, sl):
        base = tile_idx * (tm * MOE_TOPK)
        for r in range(tm):
            for k in range(MOE_TOPK):
                pltpu.make_async_copy(ys_hbm.at[pl.ds(dest_ref[base + r * MOE_TOPK + k], 1), :],
                                      ybuf.at[sl, k, pl.ds(r, 1), :], sem.at[sl]).start()

    @pl.when(step == 0)
    def _():
        start_gather(0, 0)

    for sl in range(2):
        @pl.when(jnp.logical_and(step + 1 < n_steps, slot == 1 - sl))
        def _(sl=sl):
            start_gather(step + 1, sl)

    for k in range(MOE_TOPK):
        pltpu.make_async_copy(ys_hbm.at[pl.ds(0, tm), :], ybuf.at[slot, k], sem.at[slot]).wait()

    rg = rg_ref[0]
    y = ybuf[slot, 0] * rg[:, 0:1] + ybuf[slot, 1] * rg[:, 1:2]
    r = alpha * x_ref[0] + (1.0 + g_ref[0]) * y
    o_ref[0] = _layer_norm_rows(r, lng_ref[...], lnb_ref[...])


def _gather_combine_ln(dest_flat, ys, route_gates, x, gate, ln_g, ln_b, alpha, name):
    b, s, d = x.shape
    tm = _tile(s, 256)
    nt = s // tm
    row = lambda bi, i, dst: (bi, i, 0)
    const = lambda bi, i, dst: (0, 0)
    grid_spec = pltpu.PrefetchScalarGridSpec(
        num_scalar_prefetch=1,
        grid=(b, nt),
        in_specs=[pl.BlockSpec(memory_space=pl.ANY),
                  pl.BlockSpec((1, tm, LANES), row),
                  pl.BlockSpec((1, tm, d), row),
                  pl.BlockSpec((1, 1, d), lambda bi, i, dst: (bi, 0, 0)),
                  pl.BlockSpec((1, d), const),
                  pl.BlockSpec((1, d), const)],
        out_specs=pl.BlockSpec((1, tm, d), row),
        scratch_shapes=[pltpu.VMEM((2, MOE_TOPK, tm, d), F32), pltpu.SemaphoreType.DMA((2,))],
    )
    return pl.pallas_call(
        functools.partial(_gather_ln_kernel, alpha=alpha, tm=tm, nt=nt, n_steps=b * nt),
        out_shape=jax.ShapeDtypeStruct((b, s, d), F32),
        grid_spec=grid_spec,
        compiler_params=_cparams("arbitrary", "arbitrary"),
        name=name,
    )(dest_flat, ys, route_gates, x, gate, ln_g.reshape(1, d), ln_b.reshape(1, d))


def _gdn_kernel(q_ref, k_ref, v_ref, z_ref, bg_ref, cg_ref, hh_ref, ba_ref,
                alog_ref, dtb_ref, nw_ref, scw_ref,
                o_ref,
                hist2_ref, gate_ref, state_ref,
                *, heads, ts, gw):
    L = GDN_CHUNK
    dk = GDN_HEAD_DIM

    @pl.when(pl.program_id(1) == 0)
    def _():
        hist2_ref[...] = jnp.zeros_like(hist2_ref)
        state_ref[...] = jnp.zeros_like(state_ref)

    c = cg_ref[0] * hh_ref[0]
    yb = _causal_conv(c, hist2_ref[...], scw_ref[...], SC_CONV)
    hist2_ref[...] = c[ts - SUBLANES:ts, :]
    o_ref[0, :, gw:2 * gw] = bg_ref[0] * yb

    ba = ba_ref[0]
    gate_ref[0] = _sigmoid(ba)
    gate_ref[1] = -jnp.exp(alog_ref[...]) * _softplus(ba + dtb_ref[...])

    ii = lax.broadcasted_iota(I32, (L, L), 0)
    jj = lax.broadcasted_iota(I32, (L, L), 1)
    causal = ii >= jj
    strict = ii > jj
    tri_incl = causal.astype(F32)
    eye = (ii == jj).astype(F32)
    nchunk = ts // L
    pairs = [(c, h) for c in range(nchunk) for h in range(heads)]

    def tiles(ref, lead=()):
        return jnp.stack([ref[lead + (slice(c * L, (c + 1) * L), slice(h * dk, (h + 1) * dk))]
                          for c, h in pairs], axis=0)

    def bmm(a, b):
        return jnp.einsum('bij,bjk->bik', a.astype(BF16), b.astype(BF16), preferred_element_type=F32)

    def bmm_nt(a, b):
        return jnp.einsum('bik,bjk->bij', a.astype(BF16), b.astype(BF16), preferred_element_type=F32)

    qh = tiles(q_ref, (0,))
    kh = tiles(k_ref, (0,))
    vh = tiles(v_ref, (0,))
    beta = jnp.stack([gate_ref[0, c * L:(c + 1) * L, h:h + 1] for c, h in pairs], axis=0)
    cums = [jnp.dot(tri_incl, gate_ref[1, c * L:(c + 1) * L, :], precision=HIGHEST,
                    preferred_element_type=F32) for c in range(nchunk)]
    cums_t = [cm.T for cm in cums]
    gc = jnp.stack([cums[c][:, heads + h:heads + h + 1] for c, h in pairs], axis=0)
    gr = jnp.stack([cums_t[c][heads + h:heads + h + 1, :] for c, h in pairs], axis=0)
    gl = gc[:, L - 1:L, :]
    decay = jnp.where(causal, jnp.exp(jnp.where(causal, gc - gr, 0.0)), 0.0)
    egc = jnp.exp(gc)
    qn = qh * lax.rsqrt(jnp.sum(qh * qh, -1, keepdims=True) + 1e-6)
    kn = kh * lax.rsqrt(jnp.sum(kh * kh, -1, keepdims=True) + 1e-6)
    qc = qn * (dk ** -0.5)
    kb = kn * beta
    m = jnp.where(strict, bmm_nt(kb, kn) * decay, 0.0)
    t_inv = eye - m
    mp = m
    for _ in range(5):
        mp = bmm(mp, mp)
        t_inv = t_inv + bmm(t_inv, mp)
    sol = bmm(t_inv, jnp.concatenate([vh * beta, kb * egc], axis=2))
    u = sol[:, :, 0:dk]
    w = sol[:, :, dk:2 * dk]
    qk = jnp.where(causal, bmm_nt(qc, kn) * decay, 0.0)
    qg = qc * egc
    k_dec = kn * jnp.exp(gl - gc)
    egl = jnp.exp(gl)

    state = [state_ref[h] for h in range(heads)]
    outs = []
    for b, (c, h) in enumerate(pairs):
        st = state[h]
        v_new = u[b] - _dot(w[b], st)
        outs.append(_dot(qg[b], st) + _dot(qk[b], v_new))
        state[h] = st * egl[b] + _dot(k_dec[b].T, v_new)
    for h in range(heads):
        state_ref[h] = state[h]

    o = jnp.stack(outs, axis=0)
    on = o * lax.rsqrt(jnp.mean(o * o, -1, keepdims=True) + 1e-6) * nw_ref[...]
    y = on * _silu(tiles(z_ref, (0,)))
    for b, (c, h) in enumerate(pairs):
        o_ref[0, c * L:(c + 1) * L, h * dk:(h + 1) * dk] = y[b]


def _gdn_shortconv(proj, a_log, dt_bias, norm_w, sc_w, heads):
    b, s, _ = proj.shape
    gw = heads * GDN_HEAD_DIM
    ts = _tile(s, 256)
    alog_p = jnp.zeros((1, LANES), F32).at[0, heads:2 * heads].set(a_log)
    dtb_p = jnp.zeros((1, LANES), F32).at[0, heads:2 * heads].set(dt_bias)
    col = lambda cb: (lambda bi, i: (bi, i, cb))
    const = lambda bi, i: (0, 0)
    kern = functools.partial(_gdn_kernel, heads=heads, ts=ts, gw=gw)
    return pl.pallas_call(
        kern,
        out_shape=jax.ShapeDtypeStruct((b, s, 2 * gw), F32),
        grid=(b, s // ts),
        in_specs=[pl.BlockSpec((1, ts, gw), col(cb)) for cb in range(7)]
        + [pl.BlockSpec((1, ts, LANES), col(7 * gw // LANES)),
           pl.BlockSpec((1, LANES), const),
           pl.BlockSpec((1, LANES), const),
           pl.BlockSpec((1, GDN_HEAD_DIM), const),
           pl.BlockSpec((SC_CONV, gw), const)],
        out_specs=pl.BlockSpec((1, ts, 2 * gw), lambda bi, i: (bi, i, 0)),
        scratch_shapes=[pltpu.VMEM((SUBLANES, gw), F32),
                        pltpu.VMEM((2, ts, LANES), F32),
                        pltpu.VMEM((heads, GDN_HEAD_DIM, GDN_HEAD_DIM), F32)],
        compiler_params=_cparams("parallel", "arbitrary"),
        name="gdn_shortconv",
    )(proj, proj, proj, proj, proj, proj, proj, proj,
      alog_p, dtb_p, norm_w.reshape(1, GDN_HEAD_DIM), sc_w)


def _rope_kernel(pos_ref, inv_ref, qpe_ref, kpe_ref, qo_ref, ko_ref, *, heads):
    ang = pos_ref[0].astype(F32) * inv_ref[...]
    lane = lax.broadcasted_iota(I32, ang.shape, 1)
    half = MLA_ROPE // 2
    cos = jnp.where(lane < MLA_ROPE, jnp.cos(ang), 0.0)
    sin = jnp.sin(ang)
    sgn = jnp.where(lane < half, -sin, jnp.where(lane < MLA_ROPE, sin, 0.0))

    def rot(x):
        swapped = jnp.where(lane < half, pltpu.roll(x, LANES - half, 1), pltpu.roll(x, half, 1))
        return x * cos + swapped * sgn

    for h in range(heads):
        cols = slice(h * LANES, (h + 1) * LANES)
        qo_ref[0, :, cols] = rot(qpe_ref[0, :, cols]).astype(BF16)
    ko_ref[0] = rot(kpe_ref[0]).astype(BF16)


def _rope(positions, qpe, heads, cproj, kpe_block):
    b, s = positions.shape
    ts = _tile(s, 512)
    half = MLA_ROPE // 2
    inv = ROPE_THETA ** (-jnp.arange(half, dtype=F32) * (2.0 / MLA_ROPE))
    inv_tab = jnp.zeros((1, LANES), F32).at[0, 0:half].set(inv).at[0, half:MLA_ROPE].set(inv)
    w = heads * LANES
    return pl.pallas_call(
        functools.partial(_rope_kernel, heads=heads),
        out_shape=(jax.ShapeDtypeStruct((b, s, w), BF16), jax.ShapeDtypeStruct((b, s, LANES), BF16)),
        grid=(b, s // ts),
        in_specs=[pl.BlockSpec((1, ts, 1), lambda bi, i: (bi, i, 0)),
                  pl.BlockSpec((1, LANES), lambda bi, i: (0, 0)),
                  pl.BlockSpec((1, ts, w), lambda bi, i: (bi, i, 0)),
                  pl.BlockSpec((1, ts, LANES), lambda bi, i: (bi, i, kpe_block))],
        out_specs=(pl.BlockSpec((1, ts, w), lambda bi, i: (bi, i, 0)),
                   pl.BlockSpec((1, ts, LANES), lambda bi, i: (bi, i, 0))),
        compiler_params=_cparams("parallel", "parallel"),
        name="mla_rope",
    )(positions.reshape(b, s, 1), inv_tab, qpe, cproj)


def _attn_kernel(qn_ref, qp_ref, kn_ref, kp_ref, v_ref, o_ref, kcat_ref, s_ref, *, tq, nq, scale):
    qi = pl.program_id(2)

    @pl.when(qi == 0)
    def _():
        kcat_ref[:, 0:MLA_NOPE] = kn_ref[0]
        kcat_ref[:, MLA_NOPE:MLA_NOPE + LANES] = kp_ref[0]

    qcat = jnp.concatenate([qn_ref[0], qp_ref[0]], axis=1)
    hq = tq // 2
    tri = lax.broadcasted_iota(I32, (hq, hq), 0) >= lax.broadcasted_iota(I32, (hq, hq), 1)

    for qv in range(nq):
        @pl.when(qi == qv)
        def _(qv=qv):
            for h in range(2):
                rows = slice(h * hq, (h + 1) * hq)
                nk = qv * tq + (h + 1) * hq
                for c0 in range(0, nk, tq):
                    c1 = min(c0 + tq, nk)
                    s = lax.dot_general(qcat[rows], kcat_ref[c0:c1, :], (((1,), (1,)), ((), ())),
                                        preferred_element_type=F32) * scale
                    if c1 == nk:
                        w = c1 - c0
                        last = jnp.where(tri, s[:, w - hq:], -jnp.inf)
                        s = last if w == hq else jnp.concatenate([s[:, :w - hq], last], axis=1)
                    s_ref[rows, c0:c1] = s
                    mb = jnp.max(s, -1, keepdims=True)
                    m = mb if c0 == 0 else jnp.maximum(m, mb)
                l = jnp.zeros((hq, 1), F32)
                acc = jnp.zeros((hq, MLA_V), F32)
                for c0 in range(0, nk, hq):
                    p = jnp.exp(s_ref[rows, c0:c0 + hq] - m)
                    l = l + jnp.sum(p, -1, keepdims=True)
                    acc = acc + jnp.dot(p.astype(BF16), v_ref[0, c0:c0 + hq, :], preferred_element_type=F32)
                o_ref[0, rows, :] = acc / l


def _attention(qn, qpe, kvup, kpe, heads):
    b, s, _ = qn.shape
    tq = _tile(s, 512)
    nq = s // tq
    scale = (MLA_NOPE + MLA_ROPE) ** -0.5
    return pl.pallas_call(
        functools.partial(_attn_kernel, tq=tq, nq=nq, scale=scale),
        out_shape=jax.ShapeDtypeStruct((b, s, heads * MLA_V), F32),
        grid=(b, heads, nq),
        in_specs=[pl.BlockSpec((1, tq, MLA_NOPE), lambda bi, h, i: (bi, i, h)),
                  pl.BlockSpec((1, tq, LANES), lambda bi, h, i: (bi, i, h)),
                  pl.BlockSpec((1, s, MLA_NOPE), lambda bi, h, i: (bi, 0, h)),
                  pl.BlockSpec((1, s, LANES), lambda bi, h, i: (bi, 0, 0)),
                  pl.BlockSpec((1, s, MLA_V), lambda bi, h, i: (bi, 0, heads + h))],
        out_specs=pl.BlockSpec((1, tq, MLA_V), lambda bi, h, i: (bi, i, h)),
        scratch_shapes=[pltpu.VMEM((s, MLA_NOPE + LANES), BF16), pltpu.VMEM((tq, s), F32)],
        compiler_params=_cparams("parallel", "parallel", "arbitrary"),
        name="mla_attention",
    )(qn, qpe, kvup, kpe, kvup)


def _hyb_in_weight(w_in, heads):
    d = w_in.shape[0]
    gw = heads * GDN_HEAD_DIM
    main = jnp.concatenate([w_in[:, :4 * gw], w_in[:, 4 * gw + 2 * heads:]], axis=1)
    gates = w_in[:, 4 * gw:4 * gw + 2 * heads]
    n = main.shape[1] + LANES
    n_pad = -(-n // 768) * 768
    tail = jnp.zeros((d, n_pad - main.shape[1]), w_in.dtype).at[:, :2 * heads].set(gates)
    return jnp.concatenate([main, tail], axis=1).astype(BF16)


def _mla_in_weight(w_in):
    d = w_in.shape[0]
    pad = jnp.zeros((d, LANES - MLA_ROPE), w_in.dtype)
    return jnp.concatenate([w_in, pad], axis=1).astype(BF16)


def _mla_uq_weights(w_uq, heads):
    r = w_uq.shape[0]
    w = w_uq.reshape(r, heads, MLA_NOPE + MLA_ROPE)
    nope = w[:, :, :MLA_NOPE].reshape(r, heads * MLA_NOPE)
    pe = jnp.concatenate([w[:, :, MLA_NOPE:], jnp.zeros((r, heads, LANES - MLA_ROPE), w.dtype)], axis=2)
    return nope.astype(BF16), pe.reshape(r, heads * LANES).astype(BF16)


def _mla_ukv_weight(w_ukv, heads):
    r = w_ukv.shape[0]
    w = w_ukv.reshape(r, heads, MLA_NOPE + MLA_V)
    return jnp.concatenate([w[:, :, :MLA_NOPE].reshape(r, heads * MLA_NOPE),
                            w[:, :, MLA_NOPE:].reshape(r, heads * MLA_V)], axis=1).astype(BF16)


def _router_weight(wr_g, br_g, wr_e, br_e):
    d = wr_g.shape[0]
    n = wr_g.shape[1] + wr_e.shape[1]
    w = jnp.zeros((d, LANES), F32).at[:, :n].set(jnp.concatenate([wr_g, wr_e], axis=1))
    bias = jnp.zeros((1, LANES), F32).at[0, :n].set(jnp.concatenate([br_g, br_e]))
    w_hi = w.astype(BF16)
    w_lo = (w - w_hi.astype(F32)).astype(BF16)
    return jnp.concatenate([w_hi, w_lo], axis=1), bias


def kernel(x, c, positions, ada_w, ada_b, ln_g, ln_b, hyb_w_in, gdn_conv_w, gdn_a_log, gdn_dt_bias,
           gdn_norm_w, sc_conv_w, hyb_w_out, mla_w_in, mla_q_norm, mla_kv_norm, mla_w_uq, mla_w_ukv,
           mla_w_out, moe_router_g, moe_bias_g, moe_router_e, moe_bias_e, moe_w_gate, moe_w_up,
           moe_w_down):
    b, s, d = x.shape
    depth = ada_w.shape[0]
    alpha = (2.0 * depth) ** 0.25
    gdn_heads = d // (2 * GDN_HEAD_DIM)
    mla_heads = mla_w_out.shape[1] // MLA_V
    n_groups = moe_router_g.shape[2]
    n_exp = moe_router_e.shape[2]
    per_group = n_exp // n_groups

    mod = _modulation(c, ada_w, ada_b)
    t = b * s
    xs = jnp.zeros((t * MOE_TOPK + n_exp * MOE_BLOCK, d // 2), U32)
    for layer in range(depth):
        sh1, sc1, g1, sh2, sc2, g2 = [mod[layer, :, j * d:(j + 1) * d].reshape(b, 1, d) for j in range(6)]
        i = layer // 2
        wr, br = _router_weight(moe_router_g[layer], moe_bias_g[layer], moe_router_e[layer], moe_bias_e[layer])
        if layer % 2 == 0:
            proj = _mod_matmul(x, sc1, sh1, _hyb_in_weight(hyb_w_in[i], gdn_heads), "hyb_in_proj",
                               conv_w=gdn_conv_w[i])
            y = _gdn_shortconv(proj, gdn_a_log[i], gdn_dt_bias[i], gdn_norm_w[i], sc_conv_w[i], gdn_heads)
            w_out = hyb_w_out[i]
        else:
            cproj = _mod_matmul(x, sc1, sh1, _mla_in_weight(mla_w_in[i]), "mla_in_proj")
            w_qn, w_qpe = _mla_uq_weights(mla_w_uq[i], mla_heads)
            qn = _rms_matmul(cproj, 0, MLA_Q_RANK, mla_q_norm[i], w_qn, BF16, "mla_q_nope_up")
            qpe_raw = _rms_matmul(cproj, 0, MLA_Q_RANK, mla_q_norm[i], w_qpe, F32, "mla_q_rope_up")
            kvup = _rms_matmul(cproj, 1, MLA_KV_RANK, mla_kv_norm[i], _mla_ukv_weight(mla_w_ukv[i], mla_heads),
                               BF16, "mla_kv_up")
            qpe, kpe = _rope(positions, qpe_raw, mla_heads, cproj, (MLA_Q_RANK + MLA_KV_RANK) // LANES)
            y = _attention(qn, qpe, kvup, kpe, mla_heads)
            w_out = mla_w_out[i]
        x, hin2, ids, gates = _out_ln_route(y, w_out.astype(BF16), x, g1, ln_g[layer, 0], ln_b[layer, 0],
                                            sc2, sh2, wr, br, alpha, n_groups, per_group,
                                            "mixer_out_ln_route")
        dest, meta = _dispatch(ids.reshape(t, LANES))
        dest_flat = dest[:, :MOE_TOPK].reshape(-1)
        eblk = meta[0, :n_exp] // MOE_BLOCK
        enb = meta[1, :n_exp]
        xs = _scatter_rows(dest_flat, hin2.reshape(t, d // 2), xs)
        ys = _moe_ffn(eblk, enb, xs, moe_w_gate, moe_w_up, moe_w_down, layer)
        x = _gather_combine_ln(dest_flat, ys, gates, x, g2, ln_g[layer, 1], ln_b[layer, 1], alpha,
                               "moe_gather_combine_ln")
    return x
```

```python
import functools

import jax
import jax.numpy as jnp
from jax import lax
from jax.experimental import pallas as pl
from jax.experimental.pallas import tpu as pltpu

F32, BF16, I32, U32 = jnp.float32, jnp.bfloat16, jnp.int32, jnp.uint32
HIGHEST = lax.Precision.HIGHEST

LANES = 128
SUBLANES = 8
VMEM_LIMIT_BYTES = 56 * 1024 * 1024

GDN_HEAD_DIM = 128
GDN_CONV = 4
GDN_CHUNK = 64
SC_CONV = 3
MLA_NOPE = 128
MLA_ROPE = 64
MLA_V = 128
MLA_Q_RANK = 512
MLA_KV_RANK = 512
ROPE_THETA = 10000.0
MOE_TOPK = 2
MOE_BLOCK = 128
FFN_RING = 6
GATHER_GROUPS = 8


def _tile(n, pref):
    if n <= pref:
        return n
    for t in range(pref, 0, -LANES):
        if n % t == 0:
            return t
    return n


def _cparams(*sem):
    return pltpu.CompilerParams(dimension_semantics=sem, vmem_limit_bytes=VMEM_LIMIT_BYTES)


def _sigmoid(x):
    return 1.0 / (1.0 + jnp.exp(-x))


def _silu(x):
    return x * _sigmoid(x)


def _softplus(x):
    return jnp.maximum(x, 0.0) + jnp.log(1.0 + jnp.exp(-jnp.abs(x)))


def _pack_bf16_pair(lo, hi):
    lo_bits = lax.bitcast_convert_type(lo.astype(BF16).astype(F32), U32)
    hi_bits = lax.bitcast_convert_type(hi.astype(BF16).astype(F32), U32)
    return (lo_bits >> 16) | (hi_bits & jnp.uint32(0xFFFF0000))


def _unpack_bf16_pair(w):
    lo = lax.bitcast_convert_type(w << 16, F32)
    hi = lax.bitcast_convert_type(w & jnp.uint32(0xFFFF0000), F32)
    return lo.astype(BF16), hi.astype(BF16)


def _dot(a, b):
    return jnp.dot(a.astype(BF16), b.astype(BF16), preferred_element_type=F32)


def _dot_nt(a, b):
    return lax.dot_general(a.astype(BF16), b.astype(BF16), (((1,), (1,)), ((), ())),
                           preferred_element_type=F32)


def _causal_conv(x, hist, w, taps):
    rows, width = x.shape
    g = rows // SUBLANES
    x3 = x.reshape(g, SUBLANES, width)
    sub = lax.broadcasted_iota(I32, (1, SUBLANES, width), 1)
    y = None
    for j in range(taps - 1):
        shift = taps - 1 - j
        rot = pltpu.roll(x3, shift, 1)
        prev = jnp.concatenate([pltpu.roll(hist, shift, 0)[None], rot[:g - 1]], axis=0)
        term = w[j:j + 1] * jnp.where(sub < shift, prev, rot)
        y = term if y is None else y + term
    return (y + w[taps - 1:taps] * x3).reshape(rows, width)


def _mod_kernel(c_ref, w_ref, b_ref, o_ref):
    c = c_ref[...]
    o_ref[0] = _dot(_silu(c), w_ref[0]) + b_ref[0]


def _modulation(c, ada_w, ada_b):
    depth, d, n = ada_w.shape
    b = c.shape[0]
    rows = -(-b // SUBLANES) * SUBLANES
    cp = jnp.zeros((rows, d), F32).at[:b].set(c)
    tn = _tile(n, 1024)
    out = pl.pallas_call(
        _mod_kernel,
        out_shape=jax.ShapeDtypeStruct((depth, rows, n), F32),
        grid=(depth, n // tn),
        in_specs=[pl.BlockSpec((rows, d), lambda l, j: (0, 0)),
                  pl.BlockSpec((1, d, tn), lambda l, j: (l, 0, j)),
                  pl.BlockSpec((1, 1, tn), lambda l, j: (l, 0, j))],
        out_specs=pl.BlockSpec((1, rows, tn), lambda l, j: (l, 0, j)),
        compiler_params=_cparams("parallel", "parallel"),
        name="adaln_mod",
    )(cp, ada_w, ada_b.reshape(depth, 1, n))
    return out[:, :b]


def _modmm_kernel(x_ref, sc_ref, sh_ref, w_ref, o_ref, xb_ref):
    @pl.when(pl.program_id(2) == 0)
    def _():
        xb_ref[...] = (x_ref[0] * (1.0 + sc_ref[0]) + sh_ref[0]).astype(BF16)

    o_ref[0] = jnp.dot(xb_ref[...], w_ref[...], preferred_element_type=F32)


def _mod_matmul(x, sc, sh, w, name):
    b, s, d = x.shape
    n = w.shape[1]
    tm = _tile(s, 1024)
    tn = _tile(n, 768)
    return pl.pallas_call(
        _modmm_kernel,
        out_shape=jax.ShapeDtypeStruct((b, s, n), F32),
        grid=(b, s // tm, n // tn),
        in_specs=[pl.BlockSpec((1, tm, d), lambda bi, i, j: (bi, i, 0)),
                  pl.BlockSpec((1, 1, d), lambda bi, i, j: (bi, 0, 0)),
                  pl.BlockSpec((1, 1, d), lambda bi, i, j: (bi, 0, 0)),
                  pl.BlockSpec((d, tn), lambda bi, i, j: (0, j))],
        out_specs=pl.BlockSpec((1, tm, tn), lambda bi, i, j: (bi, i, j)),
        scratch_shapes=[pltpu.VMEM((tm, d), BF16)],
        compiler_params=_cparams("parallel", "parallel", "arbitrary"),
        name=name,
    )(x, sc, sh, w)


def _rmsmm_kernel(x_ref, nw_ref, w_ref, o_ref, xb_ref):
    @pl.when(pl.program_id(2) == 0)
    def _():
        x = x_ref[0]
        xb_ref[...] = (x * lax.rsqrt(jnp.mean(x * x, -1, keepdims=True) + 1e-6) * nw_ref[...]).astype(BF16)

    o_ref[0] = jnp.dot(xb_ref[...], w_ref[...], preferred_element_type=F32).astype(o_ref.dtype)


def _rms_matmul(x, col_block, k, norm_w, w, out_dtype, name):
    b, s, _ = x.shape
    n = w.shape[1]
    tm = _tile(s, 1024)
    tn = _tile(n, 1024)
    return pl.pallas_call(
        _rmsmm_kernel,
        out_shape=jax.ShapeDtypeStruct((b, s, n), out_dtype),
        grid=(b, s // tm, n // tn),
        in_specs=[pl.BlockSpec((1, tm, k), lambda bi, i, j: (bi, i, col_block)),
                  pl.BlockSpec((1, k), lambda bi, i, j: (0, 0)),
                  pl.BlockSpec((k, tn), lambda bi, i, j: (0, j))],
        out_specs=pl.BlockSpec((1, tm, tn), lambda bi, i, j: (bi, i, j)),
        scratch_shapes=[pltpu.VMEM((tm, k), BF16)],
        compiler_params=_cparams("parallel", "parallel", "arbitrary"),
        name=name,
    )(x, norm_w.reshape(1, k), w)


def _layer_norm_rows(r, g, b):
    mu = jnp.mean(r, -1, keepdims=True)
    var = jnp.mean(jnp.square(r - mu), -1, keepdims=True)
    return (r - mu) * lax.rsqrt(var + 1e-5) * g + b


def _outln_kernel(y_ref, w_ref, xres_ref, g_ref, lng_ref, lnb_ref, sc_ref, sh_ref, wr_ref, br_ref,
                  xo_ref, hin_ref, ids_ref, gates_ref, *, alpha, n_groups, per_group, parts):
    tp = y_ref.shape[1] // parts
    for part in range(parts):
        _outln_rows(slice(part * tp, (part + 1) * tp), y_ref, w_ref, xres_ref, g_ref, lng_ref, lnb_ref,
                    sc_ref, sh_ref, wr_ref, br_ref, xo_ref, hin_ref, ids_ref, gates_ref,
                    alpha=alpha, n_groups=n_groups, per_group=per_group)


def _outln_rows(rows, y_ref, w_ref, xres_ref, g_ref, lng_ref, lnb_ref, sc_ref, sh_ref, wr_ref, br_ref,
                xo_ref, hin_ref, ids_ref, gates_ref, *, alpha, n_groups, per_group):
    y = jnp.dot(y_ref[0, rows, :].astype(BF16), w_ref[...], preferred_element_type=F32)
    r = alpha * xres_ref[0, rows, :] + (1.0 + g_ref[0]) * y
    xn = _layer_norm_rows(r, lng_ref[...], lnb_ref[...])
    xo_ref[0, rows, :] = xn
    hin = xn * (1.0 + sc_ref[0]) + sh_ref[0]
    half = hin.shape[1] // 2
    hin_ref[0, rows, :] = _pack_bf16_pair(hin[:, :half], hin[:, half:])
    h_hi = hin.astype(BF16)
    h_lo = (hin - h_hi.astype(F32)).astype(BF16)
    ph = jnp.dot(h_hi, wr_ref[...], preferred_element_type=F32)
    plo = jnp.dot(h_lo, wr_ref[...], preferred_element_type=F32)
    logits = (ph[:, :LANES] + ph[:, LANES:]) + (plo[:, :LANES] + plo[:, LANES:]) + br_ref[...]
    lane = lax.broadcasted_iota(I32, logits.shape, 1)
    big = jnp.int32(4 * LANES)
    neg = jnp.float32(-jnp.inf)
    n_exp = n_groups * per_group
    gmask = lane < n_groups
    lg = jnp.where(gmask, logits, neg)
    mg = jnp.max(lg, -1, keepdims=True)
    grp = jnp.min(jnp.where(gmask & (lg == mg), lane, big), -1, keepdims=True)
    pg_sel = 1.0 / jnp.sum(jnp.where(gmask, jnp.exp(lg - mg), 0.0), -1, keepdims=True)
    lo = n_groups + grp * per_group
    emask = (lane >= lo) & (lane < lo + per_group) & (lane < n_groups + n_exp)
    le = jnp.where(emask, logits, neg)
    me = jnp.max(le, -1, keepdims=True)
    ee = jnp.where(emask, jnp.exp(le - me), 0.0)
    p = ee / jnp.sum(ee, -1, keepdims=True)
    pm = jnp.where(emask, p, -1.0)
    p1 = jnp.max(pm, -1, keepdims=True)
    i1 = jnp.min(jnp.where(emask & (pm == p1), lane, big), -1, keepdims=True)
    pm2 = jnp.where(lane == i1, -1.0, pm)
    p2 = jnp.max(pm2, -1, keepdims=True)
    i2 = jnp.min(jnp.where(emask & (lane != i1) & (pm2 == p2), lane, big), -1, keepdims=True)
    den = p1 + p2
    g1 = pg_sel * p1 / den
    g2 = pg_sel * p2 / den
    ids_ref[0, rows, :] = jnp.where(lane == 0, i1 - n_groups, jnp.where(lane == 1, i2 - n_groups, 0))
    gates_ref[0, rows, :] = jnp.where(lane == 0, g1, jnp.where(lane == 1, g2, 0.0))


def _out_ln_route(y, w, xres, gate, ln_g, ln_b, sc, sh, wr, br, alpha, n_groups, per_group, name):
    b, s, k = y.shape
    d = w.shape[1]
    tm = _tile(s, 512)
    parts = 2 if tm % (2 * SUBLANES) == 0 else 1
    kern = functools.partial(_outln_kernel, alpha=alpha, n_groups=n_groups, per_group=per_group, parts=parts)
    row = lambda bi, i: (bi, i, 0)
    per_b = lambda bi, i: (bi, 0, 0)
    const = lambda bi, i: (0, 0)
    return pl.pallas_call(
        kern,
        out_shape=(jax.ShapeDtypeStruct((b, s, d), F32), jax.ShapeDtypeStruct((b, s, d // 2), U32),
                   jax.ShapeDtypeStruct((b, s, LANES), I32), jax.ShapeDtypeStruct((b, s, LANES), F32)),
        grid=(b, s // tm),
        in_specs=[pl.BlockSpec((1, tm, k), row),
                  pl.BlockSpec((k, d), const),
                  pl.BlockSpec((1, tm, d), row),
                  pl.BlockSpec((1, 1, d), per_b),
                  pl.BlockSpec((1, d), const),
                  pl.BlockSpec((1, d), const),
                  pl.BlockSpec((1, 1, d), per_b),
                  pl.BlockSpec((1, 1, d), per_b),
                  pl.BlockSpec((d, 2 * LANES), const),
                  pl.BlockSpec((1, LANES), const)],
        out_specs=(pl.BlockSpec((1, tm, d), row), pl.BlockSpec((1, tm, d // 2), row),
                   pl.BlockSpec((1, tm, LANES), row), pl.BlockSpec((1, tm, LANES), row)),
        compiler_params=_cparams("parallel", "parallel"),
        name=name,
    )(y, w, xres, gate, ln_g.reshape(1, d), ln_b.reshape(1, d), sc, sh, wr, br)


def _dispatch_kernel(ids_ref, dest_ref, meta_ref, rank_ref, *, t, tile, blk_shift):
    lane = lax.broadcasted_iota(I32, (tile, LANES), 1)
    ri = lax.broadcasted_iota(I32, (tile, tile), 0)
    ci = lax.broadcasted_iota(I32, (tile, tile), 1)
    before = (ri > ci).astype(BF16)

    def hits(rows):
        ids = ids_ref[rows, :]
        return lane == ids[:, 0:1], lane == ids[:, 1:2]

    def count(ti, run):
        rows = pl.ds(pl.multiple_of(ti * tile, tile), tile)
        h1, h2 = hits(rows)
        onehot = jnp.logical_or(h1, h2).astype(BF16)
        prefix = jnp.dot(before, onehot, preferred_element_type=F32) + run
        r1 = jnp.sum(jnp.where(h1, prefix, 0.0), -1, keepdims=True)
        r2 = jnp.sum(jnp.where(h2, prefix, 0.0), -1, keepdims=True)
        rank_ref[rows, :] = jnp.where(lane == 0, r1, jnp.where(lane == 1, r2, 0.0))
        return run + jnp.sum(onehot.astype(F32), 0, keepdims=True)

    counts = lax.fori_loop(0, t // tile, count, jnp.zeros((1, LANES), F32))
    cnt = jnp.broadcast_to(counts, (SUBLANES, LANES)).astype(I32)
    nblk = (cnt + ((1 << blk_shift) - 1)) >> blk_shift
    padded = nblk << blk_shift
    lane8 = lax.broadcasted_iota(I32, (SUBLANES, LANES), 1)
    incl = padded
    step = 1
    while step < LANES:
        incl = incl + jnp.where(lane8 >= step, pltpu.roll(incl, step, 1), 0)
        step *= 2
    pad_start = incl - padded
    row8 = lax.broadcasted_iota(I32, (SUBLANES, LANES), 0)
    meta_ref[...] = jnp.where(row8 == 0, pad_start, jnp.where(row8 == 1, nblk, cnt))
    start_f = pad_start[0:1, :].astype(F32)

    def place(ti, c):
        rows = pl.ds(pl.multiple_of(ti * tile, tile), tile)
        h1, h2 = hits(rows)
        rk = rank_ref[rows, :]
        d1 = jnp.sum(jnp.where(h1, start_f, 0.0), -1, keepdims=True) + rk[:, 0:1]
        d2 = jnp.sum(jnp.where(h2, start_f, 0.0), -1, keepdims=True) + rk[:, 1:2]
        dest_ref[rows, :] = jnp.where(lane == 0, d1, jnp.where(lane == 1, d2, 0.0)).astype(I32)
        return c

    lax.fori_loop(0, t // tile, place, 0)


def _dispatch(ids):
    t = ids.shape[0]
    tile = _tile(t, 256)
    blk_shift = MOE_BLOCK.bit_length() - 1
    assert (1 << blk_shift) == MOE_BLOCK
    return pl.pallas_call(
        functools.partial(_dispatch_kernel, t=t, tile=tile, blk_shift=blk_shift),
        out_shape=(jax.ShapeDtypeStruct((t, LANES), I32), jax.ShapeDtypeStruct((SUBLANES, LANES), I32)),
        grid=(1,),
        in_specs=[pl.BlockSpec((t, LANES), lambda i: (0, 0))],
        out_specs=(pl.BlockSpec((t, LANES), lambda i: (0, 0)), pl.BlockSpec((SUBLANES, LANES), lambda i: (0, 0))),
        scratch_shapes=[pltpu.VMEM((t, LANES), F32)],
        compiler_params=_cparams("arbitrary"),
        name="moe_dispatch",
    )(ids)


def _scatter_rows_kernel(dest_ref, hin_ref, xs_in, xs_out, stage, sem, *, tm, n_steps):
    del xs_in
    step = pl.program_id(0)
    base = step * (tm * MOE_TOPK)

    def wait_slot(sl):
        for _ in range(MOE_TOPK):
            pltpu.make_async_copy(stage.at[sl], xs_out.at[pl.ds(0, tm), :], sem.at[sl]).wait()

    for sl in range(2):
        @pl.when(step % 2 == sl)
        def _(sl=sl):
            @pl.when(step >= 2)
            def _():
                wait_slot(sl)

            stage[sl] = hin_ref[...]
            for r in range(tm):
                for k in range(MOE_TOPK):
                    pltpu.make_async_copy(stage.at[sl, pl.ds(r, 1), :],
                                          xs_out.at[pl.ds(dest_ref[base + r * MOE_TOPK + k], 1), :],
                                          sem.at[sl]).start()

            @pl.when(step == n_steps - 1)
            def _():
                if n_steps >= 2:
                    wait_slot(1 - sl)
                wait_slot(sl)


def _scatter_rows(dest_flat, hin2, xs_init):
    t, d = hin2.shape
    n_rows = xs_init.shape[0]
    tm = _tile(t, 256)
    grid_spec = pltpu.PrefetchScalarGridSpec(
        num_scalar_prefetch=1,
        grid=(t // tm,),
        in_specs=[pl.BlockSpec((tm, d), lambda i, dst: (i, 0)),
                  pl.BlockSpec(memory_space=pl.ANY)],
        out_specs=pl.BlockSpec(memory_space=pl.ANY),
        scratch_shapes=[pltpu.VMEM((2, tm, d), hin2.dtype), pltpu.SemaphoreType.DMA((2,))],
    )
    return pl.pallas_call(
        functools.partial(_scatter_rows_kernel, tm=tm, n_steps=t // tm),
        out_shape=jax.ShapeDtypeStruct((n_rows, d), hin2.dtype),
        grid_spec=grid_spec,
        input_output_aliases={2: 0},
        compiler_params=_cparams("arbitrary"),
        name="moe_scatter_rows",
    )(dest_flat, hin2, xs_init)


def _ffn_kernel(eblk_ref, enb_ref, xs_hbm, wg_ref, wu_ref, wd_ref, ys_hbm,
                xbuf, ybuf, wgb, wub, wdb, isem, osem, *, blk, nb_total, n_exp):
    e = pl.program_id(0)
    n = enb_ref[e]
    b0 = eblk_ref[e]
    used = eblk_ref[n_exp - 1] + enb_ref[n_exp - 1]

    def in_cp(gb):
        sl = gb % FFN_RING
        return pltpu.make_async_copy(xs_hbm.at[pl.ds(gb * blk, blk), :], xbuf.at[sl], isem.at[sl])

    def out_cp(gb):
        sl = gb % FFN_RING
        return pltpu.make_async_copy(ybuf.at[sl], ys_hbm.at[pl.ds(gb * blk, blk), :], osem.at[sl])

    @pl.when(e == 0)
    def _():
        for k in range(FFN_RING - 1):
            @pl.when(k < used)
            def _(k=k):
                in_cp(k).start(priority=1)

    @pl.when(n > 0)
    def _():
        wgb[...] = wg_ref[0, 0].astype(BF16)
        wub[...] = wu_ref[0, 0].astype(BF16)
        wdb[...] = wd_ref[0, 0].astype(BF16)

        def body(j, c):
            gb = b0 + j
            sl = gb % FFN_RING

            @pl.when(gb + FFN_RING - 1 < used)
            def _():
                in_cp(gb + FFN_RING - 1).start(priority=1)

            in_cp(gb).wait()

            @pl.when(gb >= FFN_RING)
            def _():
                out_cp(gb - FFN_RING).wait()

            xb = jnp.concatenate(_unpack_bf16_pair(xbuf[sl]), axis=1)
            hg = jnp.dot(xb, wgb[...], preferred_element_type=F32)
            hu = jnp.dot(xb, wub[...], preferred_element_type=F32)
            hid = (_silu(hg) * hu).astype(BF16)
            ybuf[sl] = jnp.dot(hid, wdb[...], preferred_element_type=F32)
            out_cp(gb).start(priority=1)
            return c

        lax.fori_loop(0, n, body, 0)

    @pl.when(e == n_exp - 1)
    def _():
        for k in range(FFN_RING, 0, -1):
            @pl.when(used >= k)
            def _(k=k):
                out_cp(used - k).wait()

        ybuf[0] = jnp.zeros((blk, ybuf.shape[2]), F32)

        def fill(jb, c):
            cp = pltpu.make_async_copy(ybuf.at[0], ys_hbm.at[pl.ds(jb * blk, blk), :], osem.at[0])
            cp.start()
            cp.wait()
            return c

        lax.fori_loop(used, nb_total, fill, 0)


def _moe_ffn(eblk, enb, xs, w_gate, w_up, w_down, layer):
    n_rows = xs.shape[0]
    n_exp, d, ff = w_gate.shape[1], w_gate.shape[2], w_gate.shape[3]
    blk = MOE_BLOCK
    wmap = lambda e, eb, en: (layer, e, 0, 0)
    grid_spec = pltpu.PrefetchScalarGridSpec(
        num_scalar_prefetch=2,
        grid=(n_exp,),
        in_specs=[pl.BlockSpec(memory_space=pl.ANY),
                  pl.BlockSpec((1, 1, d, ff), wmap),
                  pl.BlockSpec((1, 1, d, ff), wmap),
                  pl.BlockSpec((1, 1, ff, d), wmap)],
        out_specs=pl.BlockSpec(memory_space=pl.ANY),
        scratch_shapes=[pltpu.VMEM((FFN_RING, blk, d // 2), U32),
                        pltpu.VMEM((FFN_RING, blk, d), F32),
                        pltpu.VMEM((d, ff), BF16),
                        pltpu.VMEM((d, ff), BF16),
                        pltpu.VMEM((ff, d), BF16),
                        pltpu.SemaphoreType.DMA((FFN_RING,)),
                        pltpu.SemaphoreType.DMA((FFN_RING,))],
    )
    return pl.pallas_call(
        functools.partial(_ffn_kernel, blk=blk, nb_total=n_rows // blk, n_exp=n_exp),
        out_shape=jax.ShapeDtypeStruct((n_rows, d), F32),
        grid_spec=grid_spec,
        compiler_params=_cparams("arbitrary"),
        name="moe_ffn",
    )(eblk, enb, xs, w_gate, w_up, w_down)


def _gather_ln_kernel(dest_ref, ys_hbm, rg_ref, x_ref, g_ref, lng_ref, lnb_ref, o_ref, ybuf, sem,
                      *, alpha, tm, nt, n_steps):
    step = pl.program_id(0) * nt + pl.program_id(1)
    slot = step % 2

    def start_gather(tile_idx, sl, rows):
        base = tile_idx * (tm * MOE_TOPK)
        for r in rows:
            for k in range(MOE_TOPK):
                pltpu.make_async_copy(ys_hbm.at[pl.ds(dest_ref[base + r * MOE_TOPK + k], 1), :],
                                      ybuf.at[sl, k, pl.ds(r, 1), :], sem.at[sl]).start()

    def wait_gather(sl):
        for k in range(MOE_TOPK):
            pltpu.make_async_copy(ys_hbm.at[pl.ds(0, tm), :], ybuf.at[sl, k], sem.at[sl]).wait()

    @pl.when(step == 0)
    def _():
        start_gather(0, 0, range(tm))

    wait_gather(slot)
    nxt = jnp.minimum(step + 1, n_steps - 1)
    ch = tm // GATHER_GROUPS
    for c in range(GATHER_GROUPS):
        rows = slice(c * ch, (c + 1) * ch)
        start_gather(nxt, 1 - slot, range(c * ch, (c + 1) * ch))
        rg = rg_ref[0, rows, :]
        y = ybuf[slot, 0, rows, :] * rg[:, 0:1] + ybuf[slot, 1, rows, :] * rg[:, 1:2]
        r = alpha * x_ref[0, rows, :] + (1.0 + g_ref[0]) * y
        o_ref[0, rows, :] = _layer_norm_rows(r, lng_ref[...], lnb_ref[...])

    @pl.when(step == n_steps - 1)
    def _():
        wait_gather(1 - slot)


def _gather_combine_ln(dest_flat, ys, route_gates, x, gate, ln_g, ln_b, alpha, name):
    b, s, d = x.shape
    tm = _tile(s, 256)
    nt = s // tm
    row = lambda bi, i, dst: (bi, i, 0)
    const = lambda bi, i, dst: (0, 0)
    grid_spec = pltpu.PrefetchScalarGridSpec(
        num_scalar_prefetch=1,
        grid=(b, nt),
        in_specs=[pl.BlockSpec(memory_space=pl.ANY),
                  pl.BlockSpec((1, tm, LANES), row),
                  pl.BlockSpec((1, tm, d), row),
                  pl.BlockSpec((1, 1, d), lambda bi, i, dst: (bi, 0, 0)),
                  pl.BlockSpec((1, d), const),
                  pl.BlockSpec((1, d), const)],
        out_specs=pl.BlockSpec((1, tm, d), row),
        scratch_shapes=[pltpu.VMEM((2, MOE_TOPK, tm, d), F32), pltpu.SemaphoreType.DMA((2,))],
    )
    return pl.pallas_call(
        functools.partial(_gather_ln_kernel, alpha=alpha, tm=tm, nt=nt, n_steps=b * nt),
        out_shape=jax.ShapeDtypeStruct((b, s, d), F32),
        grid_spec=grid_spec,
        compiler_params=_cparams("arbitrary", "arbitrary"),
        name=name,
    )(dest_flat, ys, route_gates, x, gate, ln_g.reshape(1, d), ln_b.reshape(1, d))


def _gdn_kernel(q_ref, k_ref, v_ref, z_ref, bg_ref, cg_ref, hh_ref, ba_ref,
                cw_ref, alog_ref, dtb_ref, nw_ref, scw_ref,
                o_ref,
                qa_ref, ka_ref, va_ref, hist_ref, hist2_ref, gate_ref, state_ref,
                *, heads, ts, gw):
    L = GDN_CHUNK
    dk = GDN_HEAD_DIM

    @pl.when(pl.program_id(1) == 0)
    def _():
        hist_ref[...] = jnp.zeros_like(hist_ref)
        hist2_ref[...] = jnp.zeros_like(hist2_ref)
        state_ref[...] = jnp.zeros_like(state_ref)

    for idx, (src, dst) in enumerate(((q_ref, qa_ref), (k_ref, ka_ref), (v_ref, va_ref))):
        x = src[0]
        y = _causal_conv(x, hist_ref[idx], cw_ref[:, idx * gw:(idx + 1) * gw], GDN_CONV)
        hist_ref[idx] = x[ts - SUBLANES:ts, :]
        dst[...] = _silu(y)

    c = cg_ref[0] * hh_ref[0]
    yb = _causal_conv(c, hist2_ref[...], scw_ref[...], SC_CONV)
    hist2_ref[...] = c[ts - SUBLANES:ts, :]
    o_ref[0, :, gw:2 * gw] = bg_ref[0] * yb

    ba = ba_ref[0]
    gate_ref[0] = _sigmoid(ba)
    gate_ref[1] = -jnp.exp(alog_ref[...]) * _softplus(ba + dtb_ref[...])

    ii = lax.broadcasted_iota(I32, (L, L), 0)
    jj = lax.broadcasted_iota(I32, (L, L), 1)
    causal = ii >= jj
    strict = ii > jj
    tri_incl = causal.astype(F32)
    eye = (ii == jj).astype(F32)
    nchunk = ts // L
    pairs = [(c, h) for c in range(nchunk) for h in range(heads)]

    def tiles(ref, lead=()):
        return jnp.stack([ref[lead + (slice(c * L, (c + 1) * L), slice(h * dk, (h + 1) * dk))]
                          for c, h in pairs], axis=0)

    def bmm(a, b):
        return jnp.einsum('bij,bjk->bik', a.astype(BF16), b.astype(BF16), preferred_element_type=F32)

    def bmm_nt(a, b):
        return jnp.einsum('bik,bjk->bij', a.astype(BF16), b.astype(BF16), preferred_element_type=F32)

    qh = tiles(qa_ref)
    kh = tiles(ka_ref)
    vh = tiles(va_ref)
    beta = jnp.stack([gate_ref[0, c * L:(c + 1) * L, h:h + 1] for c, h in pairs], axis=0)
    cums = [jnp.dot(tri_incl, gate_ref[1, c * L:(c + 1) * L, :], precision=HIGHEST,
                    preferred_element_type=F32) for c in range(nchunk)]
    cums_t = [cm.T for cm in cums]
    gc = jnp.stack([cums[c][:, heads + h:heads + h + 1] for c, h in pairs], axis=0)
    gr = jnp.stack([cums_t[c][heads + h:heads + h + 1, :] for c, h in pairs], axis=0)
    gl = gc[:, L - 1:L, :]
    decay = jnp.where(causal, jnp.exp(jnp.where(causal, gc - gr, 0.0)), 0.0)
    egc = jnp.exp(gc)
    qn = qh * lax.rsqrt(jnp.sum(qh * qh, -1, keepdims=True) + 1e-6)
    kn = kh * lax.rsqrt(jnp.sum(kh * kh, -1, keepdims=True) + 1e-6)
    qc = qn * (dk ** -0.5)
    kb = kn * beta
    m = jnp.where(strict, bmm_nt(kb, kn) * decay, 0.0)
    t_inv = eye - m
    mp = m
    for _ in range(5):
        mp = bmm(mp, mp)
        t_inv = t_inv + bmm(t_inv, mp)
    sol = bmm(t_inv, jnp.concatenate([vh * beta, kb * egc], axis=2))
    u = sol[:, :, 0:dk]
    w = sol[:, :, dk:2 * dk]
    qk = jnp.where(causal, bmm_nt(qc, kn) * decay, 0.0)
    qg = qc * egc
    k_dec = kn * jnp.exp(gl - gc)
    egl = jnp.exp(gl)

    state = [state_ref[h] for h in range(heads)]
    outs = []
    for b, (c, h) in enumerate(pairs):
        st = state[h]
        v_new = u[b] - _dot(w[b], st)
        outs.append(_dot(qg[b], st) + _dot(qk[b], v_new))
        state[h] = st * egl[b] + _dot(k_dec[b].T, v_new)
    for h in range(heads):
        state_ref[h] = state[h]

    o = jnp.stack(outs, axis=0)
    on = o * lax.rsqrt(jnp.mean(o * o, -1, keepdims=True) + 1e-6) * nw_ref[...]
    y = on * _silu(tiles(z_ref, (0,)))
    for b, (c, h) in enumerate(pairs):
        o_ref[0, c * L:(c + 1) * L, h * dk:(h + 1) * dk] = y[b]


def _gdn_shortconv(proj, conv_w, a_log, dt_bias, norm_w, sc_w, heads):
    b, s, _ = proj.shape
    gw = heads * GDN_HEAD_DIM
    ts = _tile(s, 256)
    alog_p = jnp.zeros((1, LANES), F32).at[0, heads:2 * heads].set(a_log)
    dtb_p = jnp.zeros((1, LANES), F32).at[0, heads:2 * heads].set(dt_bias)
    col = lambda cb: (lambda bi, i: (bi, i, cb))
    const = lambda bi, i: (0, 0)
    kern = functools.partial(_gdn_kernel, heads=heads, ts=ts, gw=gw)
    return pl.pallas_call(
        kern,
        out_shape=jax.ShapeDtypeStruct((b, s, 2 * gw), F32),
        grid=(b, s // ts),
        in_specs=[pl.BlockSpec((1, ts, gw), col(cb)) for cb in range(7)]
        + [pl.BlockSpec((1, ts, LANES), col(7 * gw // LANES)),
           pl.BlockSpec((GDN_CONV, 3 * gw), const),
           pl.BlockSpec((1, LANES), const),
           pl.BlockSpec((1, LANES), const),
           pl.BlockSpec((1, GDN_HEAD_DIM), const),
           pl.BlockSpec((SC_CONV, gw), const)],
        out_specs=pl.BlockSpec((1, ts, 2 * gw), lambda bi, i: (bi, i, 0)),
        scratch_shapes=[pltpu.VMEM((ts, gw), F32),
                        pltpu.VMEM((ts, gw), F32),
                        pltpu.VMEM((ts, gw), F32),
                        pltpu.VMEM((3, SUBLANES, gw), F32),
                        pltpu.VMEM((SUBLANES, gw), F32),
                        pltpu.VMEM((2, ts, LANES), F32),
                        pltpu.VMEM((heads, GDN_HEAD_DIM, GDN_HEAD_DIM), F32)],
        compiler_params=_cparams("parallel", "arbitrary"),
        name="gdn_shortconv",
    )(proj, proj, proj, proj, proj, proj, proj, proj,
      conv_w, alog_p, dtb_p, norm_w.reshape(1, GDN_HEAD_DIM), sc_w)


def _rope_kernel(pos_ref, inv_ref, qpe_ref, kpe_ref, qo_ref, ko_ref, *, heads):
    ang = pos_ref[0].astype(F32) * inv_ref[...]
    lane = lax.broadcasted_iota(I32, ang.shape, 1)
    half = MLA_ROPE // 2
    cos = jnp.where(lane < MLA_ROPE, jnp.cos(ang), 0.0)
    sin = jnp.sin(ang)
    sgn = jnp.where(lane < half, -sin, jnp.where(lane < MLA_ROPE, sin, 0.0))

    def rot(x):
        swapped = jnp.where(lane < half, pltpu.roll(x, LANES - half, 1), pltpu.roll(x, half, 1))
        return x * cos + swapped * sgn

    for h in range(heads):
        cols = slice(h * LANES, (h + 1) * LANES)
        qo_ref[0, :, cols] = rot(qpe_ref[0, :, cols]).astype(BF16)
    ko_ref[0] = rot(kpe_ref[0]).astype(BF16)


def _rope(positions, qpe, heads, cproj, kpe_block):
    b, s = positions.shape
    ts = _tile(s, 512)
    half = MLA_ROPE // 2
    inv = ROPE_THETA ** (-jnp.arange(half, dtype=F32) * (2.0 / MLA_ROPE))
    inv_tab = jnp.zeros((1, LANES), F32).at[0, 0:half].set(inv).at[0, half:MLA_ROPE].set(inv)
    w = heads * LANES
    return pl.pallas_call(
        functools.partial(_rope_kernel, heads=heads),
        out_shape=(jax.ShapeDtypeStruct((b, s, w), BF16), jax.ShapeDtypeStruct((b, s, LANES), BF16)),
        grid=(b, s // ts),
        in_specs=[pl.BlockSpec((1, ts, 1), lambda bi, i: (bi, i, 0)),
                  pl.BlockSpec((1, LANES), lambda bi, i: (0, 0)),
                  pl.BlockSpec((1, ts, w), lambda bi, i: (bi, i, 0)),
                  pl.BlockSpec((1, ts, LANES), lambda bi, i: (bi, i, kpe_block))],
        out_specs=(pl.BlockSpec((1, ts, w), lambda bi, i: (bi, i, 0)),
                   pl.BlockSpec((1, ts, LANES), lambda bi, i: (bi, i, 0))),
        compiler_params=_cparams("parallel", "parallel"),
        name="mla_rope",
    )(positions.reshape(b, s, 1), inv_tab, qpe, cproj)


def _attn_kernel(qn_ref, qp_ref, kn_ref, kp_ref, v_ref, o_ref, kcat_ref, s_ref, *, tq, nq, scale):
    qi = pl.program_id(2)

    @pl.when(qi == 0)
    def _():
        kcat_ref[:, 0:MLA_NOPE] = kn_ref[0]
        kcat_ref[:, MLA_NOPE:MLA_NOPE + LANES] = kp_ref[0]

    qcat = jnp.concatenate([qn_ref[0], qp_ref[0]], axis=1)
    hq = tq // 2
    tri = lax.broadcasted_iota(I32, (hq, hq), 0) >= lax.broadcasted_iota(I32, (hq, hq), 1)

    for qv in range(nq):
        @pl.when(qi == qv)
        def _(qv=qv):
            for h in range(2):
                rows = slice(h * hq, (h + 1) * hq)
                nk = qv * tq + (h + 1) * hq
                for c0 in range(0, nk, tq):
                    c1 = min(c0 + tq, nk)
                    s = lax.dot_general(qcat[rows], kcat_ref[c0:c1, :], (((1,), (1,)), ((), ())),
                                        preferred_element_type=F32) * scale
                    if c1 == nk:
                        w = c1 - c0
                        last = jnp.where(tri, s[:, w - hq:], -jnp.inf)
                        s = last if w == hq else jnp.concatenate([s[:, :w - hq], last], axis=1)
                    s_ref[rows, c0:c1] = s
                    mb = jnp.max(s, -1, keepdims=True)
                    m = mb if c0 == 0 else jnp.maximum(m, mb)
                l = jnp.zeros((hq, 1), F32)
                acc = jnp.zeros((hq, MLA_V), F32)
                for c0 in range(0, nk, hq):
                    p = jnp.exp(s_ref[rows, c0:c0 + hq] - m)
                    l = l + jnp.sum(p, -1, keepdims=True)
                    acc = acc + jnp.dot(p.astype(BF16), v_ref[0, c0:c0 + hq, :], preferred_element_type=F32)
                o_ref[0, rows, :] = acc / l


def _attention(qn, qpe, kvup, kpe, heads):
    b, s, _ = qn.shape
    tq = _tile(s, 512)
    nq = s // tq
    scale = (MLA_NOPE + MLA_ROPE) ** -0.5
    return pl.pallas_call(
        functools.partial(_attn_kernel, tq=tq, nq=nq, scale=scale),
        out_shape=jax.ShapeDtypeStruct((b, s, heads * MLA_V), F32),
        grid=(b, heads, nq),
        in_specs=[pl.BlockSpec((1, tq, MLA_NOPE), lambda bi, h, i: (bi, i, h)),
                  pl.BlockSpec((1, tq, LANES), lambda bi, h, i: (bi, i, h)),
                  pl.BlockSpec((1, s, MLA_NOPE), lambda bi, h, i: (bi, 0, h)),
                  pl.BlockSpec((1, s, LANES), lambda bi, h, i: (bi, 0, 0)),
                  pl.BlockSpec((1, s, MLA_V), lambda bi, h, i: (bi, 0, heads + h))],
        out_specs=pl.BlockSpec((1, tq, MLA_V), lambda bi, h, i: (bi, i, h)),
        scratch_shapes=[pltpu.VMEM((s, MLA_NOPE + LANES), BF16), pltpu.VMEM((tq, s), F32)],
        compiler_params=_cparams("parallel", "parallel", "arbitrary"),
        name="mla_attention",
    )(qn, qpe, kvup, kpe, kvup)


def _hyb_in_weight(w_in, heads):
    d = w_in.shape[0]
    gw = heads * GDN_HEAD_DIM
    main = jnp.concatenate([w_in[:, :4 * gw], w_in[:, 4 * gw + 2 * heads:]], axis=1)
    gates = w_in[:, 4 * gw:4 * gw + 2 * heads]
    n = main.shape[1] + LANES
    n_pad = -(-n // 768) * 768
    tail = jnp.zeros((d, n_pad - main.shape[1]), w_in.dtype).at[:, :2 * heads].set(gates)
    return jnp.concatenate([main, tail], axis=1).astype(BF16)


def _mla_in_weight(w_in):
    d = w_in.shape[0]
    pad = jnp.zeros((d, LANES - MLA_ROPE), w_in.dtype)
    return jnp.concatenate([w_in, pad], axis=1).astype(BF16)


def _mla_uq_weights(w_uq, heads):
    r = w_uq.shape[0]
    w = w_uq.reshape(r, heads, MLA_NOPE + MLA_ROPE)
    nope = w[:, :, :MLA_NOPE].reshape(r, heads * MLA_NOPE)
    pe = jnp.concatenate([w[:, :, MLA_NOPE:], jnp.zeros((r, heads, LANES - MLA_ROPE), w.dtype)], axis=2)
    return nope.astype(BF16), pe.reshape(r, heads * LANES).astype(BF16)


def _mla_ukv_weight(w_ukv, heads):
    r = w_ukv.shape[0]
    w = w_ukv.reshape(r, heads, MLA_NOPE + MLA_V)
    return jnp.concatenate([w[:, :, :MLA_NOPE].reshape(r, heads * MLA_NOPE),
                            w[:, :, MLA_NOPE:].reshape(r, heads * MLA_V)], axis=1).astype(BF16)


def _router_weight(wr_g, br_g, wr_e, br_e):
    d = wr_g.shape[0]
    n = wr_g.shape[1] + wr_e.shape[1]
    w = jnp.zeros((d, LANES), F32).at[:, :n].set(jnp.concatenate([wr_g, wr_e], axis=1))
    bias = jnp.zeros((1, LANES), F32).at[0, :n].set(jnp.concatenate([br_g, br_e]))
    w_hi = w.astype(BF16)
    w_lo = (w - w_hi.astype(F32)).astype(BF16)
    return jnp.concatenate([w_hi, w_lo], axis=1), bias


def kernel(x, c, positions, ada_w, ada_b, ln_g, ln_b, hyb_w_in, gdn_conv_w, gdn_a_log, gdn_dt_bias,
           gdn_norm_w, sc_conv_w, hyb_w_out, mla_w_in, mla_q_norm, mla_kv_norm, mla_w_uq, mla_w_ukv,
           mla_w_out, moe_router_g, moe_bias_g, moe_router_e, moe_bias_e, moe_w_gate, moe_w_up,
           moe_w_down):
    b, s, d = x.shape
    depth = ada_w.shape[0]
    alpha = (2.0 * depth) ** 0.25
    gdn_heads = d // (2 * GDN_HEAD_DIM)
    mla_heads = mla_w_out.shape[1] // MLA_V
    n_groups = moe_router_g.shape[2]
    n_exp = moe_router_e.shape[2]
    per_group = n_exp // n_groups

    mod = _modulation(c, ada_w, ada_b)
    t = b * s
    xs = jnp.zeros((t * MOE_TOPK + n_exp * MOE_BLOCK, d // 2), U32)
    for layer in range(depth):
        sh1, sc1, g1, sh2, sc2, g2 = [mod[layer, :, j * d:(j + 1) * d].reshape(b, 1, d) for j in range(6)]
        i = layer // 2
        wr, br = _router_weight(moe_router_g[layer], moe_bias_g[layer], moe_router_e[layer], moe_bias_e[layer])
        if layer % 2 == 0:
            proj = _mod_matmul(x, sc1, sh1, _hyb_in_weight(hyb_w_in[i], gdn_heads), "hyb_in_proj")
            y = _gdn_shortconv(proj, gdn_conv_w[i], gdn_a_log[i], gdn_dt_bias[i], gdn_norm_w[i],
                               sc_conv_w[i], gdn_heads)
            w_out = hyb_w_out[i]
        else:
            cproj = _mod_matmul(x, sc1, sh1, _mla_in_weight(mla_w_in[i]), "mla_in_proj")
            w_qn, w_qpe = _mla_uq_weights(mla_w_uq[i], mla_heads)
            qn = _rms_matmul(cproj, 0, MLA_Q_RANK, mla_q_norm[i], w_qn, BF16, "mla_q_nope_up")
            qpe_raw = _rms_matmul(cproj, 0, MLA_Q_RANK, mla_q_norm[i], w_qpe, F32, "mla_q_rope_up")
            kvup = _rms_matmul(cproj, 1, MLA_KV_RANK, mla_kv_norm[i], _mla_ukv_weight(mla_w_ukv[i], mla_heads),
                               BF16, "mla_kv_up")
            qpe, kpe = _rope(positions, qpe_raw, mla_heads, cproj, (MLA_Q_RANK + MLA_KV_RANK) // LANES)
            y = _attention(qn, qpe, kvup, kpe, mla_heads)
            w_out = mla_w_out[i]
        x, hin2, ids, gates = _out_ln_route(y, w_out.astype(BF16), x, g1, ln_g[layer, 0], ln_b[layer, 0],
                                            sc2, sh2, wr, br, alpha, n_groups, per_group,
                                            "mixer_out_ln_route")
        dest, meta = _dispatch(ids.reshape(t, LANES))
        dest_flat = dest[:, :MOE_TOPK].reshape(-1)
        eblk = meta[0, :n_exp] // MOE_BLOCK
        enb = meta[1, :n_exp]
        xs = _scatter_rows(dest_flat, hin2.reshape(t, d // 2), xs)
        ys = _moe_ffn(eblk, enb, xs, moe_w_gate, moe_w_up, moe_w_down, layer)
        x = _gather_combine_ln(dest_flat, ys, gates, x, g2, ln_g[layer, 1], ln_b[layer, 1], alpha,
                               "moe_gather_combine_ln")
    return x
```

```python
import functools

import jax
import jax.numpy as jnp
from jax import lax
from jax.experimental import pallas as pl
from jax.experimental.pallas import tpu as pltpu

F32, BF16, I32, U32 = jnp.float32, jnp.bfloat16, jnp.int32, jnp.uint32
HIGHEST = lax.Precision.HIGHEST

LANES = 128
SUBLANES = 8
VMEM_LIMIT_BYTES = 56 * 1024 * 1024

GDN_HEAD_DIM = 128
GDN_CONV = 4
GDN_CHUNK = 64
SC_CONV = 3
MLA_NOPE = 128
MLA_ROPE = 64
MLA_V = 128
MLA_Q_RANK = 512
MLA_KV_RANK = 512
ROPE_THETA = 10000.0
MOE_TOPK = 2
MOE_BLOCK = 128
FFN_RING = 6
ATTN_KEY_CHUNK = 512


def _tile(n, pref):
    if n <= pref:
        return n
    for t in range(pref, 0, -LANES):
        if n % t == 0:
            return t
    return n


def _cparams(*sem):
    return pltpu.CompilerParams(dimension_semantics=sem, vmem_limit_bytes=VMEM_LIMIT_BYTES)


def _sigmoid(x):
    return 1.0 / (1.0 + jnp.exp(-x))


def _silu(x):
    return x * _sigmoid(x)


def _softplus(x):
    return jnp.maximum(x, 0.0) + jnp.log(1.0 + jnp.exp(-jnp.abs(x)))


def _pack_bf16_pair(lo, hi):
    lo_bits = lax.bitcast_convert_type(lo.astype(BF16).astype(F32), U32)
    hi_bits = lax.bitcast_convert_type(hi.astype(BF16).astype(F32), U32)
    return (lo_bits >> 16) | (hi_bits & jnp.uint32(0xFFFF0000))


def _unpack_bf16_pair(w):
    lo = lax.bitcast_convert_type(w << 16, F32)
    hi = lax.bitcast_convert_type(w & jnp.uint32(0xFFFF0000), F32)
    return lo.astype(BF16), hi.astype(BF16)


def _dot(a, b):
    return jnp.dot(a.astype(BF16), b.astype(BF16), preferred_element_type=F32)


def _dot_nt(a, b):
    return lax.dot_general(a.astype(BF16), b.astype(BF16), (((1,), (1,)), ((), ())),
                           preferred_element_type=F32)


def _causal_conv(x, hist, w, taps):
    rows, width = x.shape
    g = rows // SUBLANES
    x3 = x.reshape(g, SUBLANES, width)
    sub = lax.broadcasted_iota(I32, (1, SUBLANES, width), 1)
    y = None
    for j in range(taps - 1):
        shift = taps - 1 - j
        rot = pltpu.roll(x3, shift, 1)
        prev = jnp.concatenate([pltpu.roll(hist, shift, 0)[None], rot[:g - 1]], axis=0)
        term = w[j:j + 1] * jnp.where(sub < shift, prev, rot)
        y = term if y is None else y + term
    return (y + w[taps - 1:taps] * x3).reshape(rows, width)


def _mod_kernel(c_ref, w_ref, b_ref, o_ref):
    c = c_ref[...]
    o_ref[0] = _dot(_silu(c), w_ref[0]) + b_ref[0]


def _modulation(c, ada_w, ada_b):
    depth, d, n = ada_w.shape
    b = c.shape[0]
    rows = -(-b // SUBLANES) * SUBLANES
    cp = jnp.zeros((rows, d), F32).at[:b].set(c)
    tn = _tile(n, 1024)
    out = pl.pallas_call(
        _mod_kernel,
        out_shape=jax.ShapeDtypeStruct((depth, rows, n), F32),
        grid=(depth, n // tn),
        in_specs=[pl.BlockSpec((rows, d), lambda l, j: (0, 0)),
                  pl.BlockSpec((1, d, tn), lambda l, j: (l, 0, j)),
                  pl.BlockSpec((1, 1, tn), lambda l, j: (l, 0, j))],
        out_specs=pl.BlockSpec((1, rows, tn), lambda l, j: (l, 0, j)),
        compiler_params=_cparams("parallel", "parallel"),
        name="adaln_mod",
    )(cp, ada_w, ada_b.reshape(depth, 1, n))
    return out[:, :b]


def _modmm_kernel(x_ref, sc_ref, sh_ref, w_ref, o_ref, xb_ref):
    @pl.when(pl.program_id(2) == 0)
    def _():
        xb_ref[...] = (x_ref[0] * (1.0 + sc_ref[0]) + sh_ref[0]).astype(BF16)

    o_ref[0] = jnp.dot(xb_ref[...], w_ref[...], preferred_element_type=F32)


def _mod_matmul(x, sc, sh, w, name):
    b, s, d = x.shape
    n = w.shape[1]
    tm = _tile(s, 1024)
    tn = _tile(n, 768)
    return pl.pallas_call(
        _modmm_kernel,
        out_shape=jax.ShapeDtypeStruct((b, s, n), F32),
        grid=(b, s // tm, n // tn),
        in_specs=[pl.BlockSpec((1, tm, d), lambda bi, i, j: (bi, i, 0)),
                  pl.BlockSpec((1, 1, d), lambda bi, i, j: (bi, 0, 0)),
                  pl.BlockSpec((1, 1, d), lambda bi, i, j: (bi, 0, 0)),
                  pl.BlockSpec((d, tn), lambda bi, i, j: (0, j))],
        out_specs=pl.BlockSpec((1, tm, tn), lambda bi, i, j: (bi, i, j)),
        scratch_shapes=[pltpu.VMEM((tm, d), BF16)],
        compiler_params=_cparams("parallel", "parallel", "arbitrary"),
        name=name,
    )(x, sc, sh, w)


def _rmsmm_kernel(x_ref, nw_ref, w_ref, o_ref, xb_ref):
    @pl.when(pl.program_id(2) == 0)
    def _():
        x = x_ref[0]
        xb_ref[...] = (x * lax.rsqrt(jnp.mean(x * x, -1, keepdims=True) + 1e-6) * nw_ref[...]).astype(BF16)

    o_ref[0] = jnp.dot(xb_ref[...], w_ref[...], preferred_element_type=F32).astype(o_ref.dtype)


def _rms_matmul(x, col_block, k, norm_w, w, out_dtype, name):
    b, s, _ = x.shape
    n = w.shape[1]
    tm = _tile(s, 1024)
    tn = _tile(n, 1024)
    return pl.pallas_call(
        _rmsmm_kernel,
        out_shape=jax.ShapeDtypeStruct((b, s, n), out_dtype),
        grid=(b, s // tm, n // tn),
        in_specs=[pl.BlockSpec((1, tm, k), lambda bi, i, j: (bi, i, col_block)),
                  pl.BlockSpec((1, k), lambda bi, i, j: (0, 0)),
                  pl.BlockSpec((k, tn), lambda bi, i, j: (0, j))],
        out_specs=pl.BlockSpec((1, tm, tn), lambda bi, i, j: (bi, i, j)),
        scratch_shapes=[pltpu.VMEM((tm, k), BF16)],
        compiler_params=_cparams("parallel", "parallel", "arbitrary"),
        name=name,
    )(x, norm_w.reshape(1, k), w)


def _layer_norm_rows(r, g, b):
    mu = jnp.mean(r, -1, keepdims=True)
    var = jnp.mean(jnp.square(r - mu), -1, keepdims=True)
    return (r - mu) * lax.rsqrt(var + 1e-5) * g + b


def _outln_kernel(y_ref, w_ref, xres_ref, g_ref, lng_ref, lnb_ref, sc_ref, sh_ref, wr_ref, br_ref,
                  xo_ref, hin_ref, ids_ref, gates_ref, *, alpha, n_groups, per_group, parts):
    tp = y_ref.shape[1] // parts
    for part in range(parts):
        _outln_rows(slice(part * tp, (part + 1) * tp), y_ref, w_ref, xres_ref, g_ref, lng_ref, lnb_ref,
                    sc_ref, sh_ref, wr_ref, br_ref, xo_ref, hin_ref, ids_ref, gates_ref,
                    alpha=alpha, n_groups=n_groups, per_group=per_group)


def _outln_rows(rows, y_ref, w_ref, xres_ref, g_ref, lng_ref, lnb_ref, sc_ref, sh_ref, wr_ref, br_ref,
                xo_ref, hin_ref, ids_ref, gates_ref, *, alpha, n_groups, per_group):
    y = jnp.dot(y_ref[0, rows, :].astype(BF16), w_ref[...], preferred_element_type=F32)
    r = alpha * xres_ref[0, rows, :] + (1.0 + g_ref[0]) * y
    xn = _layer_norm_rows(r, lng_ref[...], lnb_ref[...])
    xo_ref[0, rows, :] = xn
    hin = xn * (1.0 + sc_ref[0]) + sh_ref[0]
    half = hin.shape[1] // 2
    hin_ref[0, rows, :] = _pack_bf16_pair(hin[:, :half], hin[:, half:])
    h_hi = hin.astype(BF16)
    h_lo = (hin - h_hi.astype(F32)).astype(BF16)
    ph = jnp.dot(h_hi, wr_ref[...], preferred_element_type=F32)
    plo = jnp.dot(h_lo, wr_ref[...], preferred_element_type=F32)
    logits = (ph[:, :LANES] + ph[:, LANES:]) + (plo[:, :LANES] + plo[:, LANES:]) + br_ref[...]
    lane = lax.broadcasted_iota(I32, logits.shape, 1)
    big = jnp.int32(4 * LANES)
    neg = jnp.float32(-jnp.inf)
    n_exp = n_groups * per_group
    gmask = lane < n_groups
    lg = jnp.where(gmask, logits, neg)
    mg = jnp.max(lg, -1, keepdims=True)
    grp = jnp.min(jnp.where(gmask & (lg == mg), lane, big), -1, keepdims=True)
    pg_sel = 1.0 / jnp.sum(jnp.where(gmask, jnp.exp(lg - mg), 0.0), -1, keepdims=True)
    lo = n_groups + grp * per_group
    emask = (lane >= lo) & (lane < lo + per_group) & (lane < n_groups + n_exp)
    le = jnp.where(emask, logits, neg)
    me = jnp.max(le, -1, keepdims=True)
    ee = jnp.where(emask, jnp.exp(le - me), 0.0)
    p = ee / jnp.sum(ee, -1, keepdims=True)
    pm = jnp.where(emask, p, -1.0)
    p1 = jnp.max(pm, -1, keepdims=True)
    i1 = jnp.min(jnp.where(emask & (pm == p1), lane, big), -1, keepdims=True)
    pm2 = jnp.where(lane == i1, -1.0, pm)
    p2 = jnp.max(pm2, -1, keepdims=True)
    i2 = jnp.min(jnp.where(emask & (lane != i1) & (pm2 == p2), lane, big), -1, keepdims=True)
    den = p1 + p2
    g1 = pg_sel * p1 / den
    g2 = pg_sel * p2 / den
    ids_ref[0, rows, :] = jnp.where(lane == 0, i1 - n_groups, jnp.where(lane == 1, i2 - n_groups, 0))
    gates_ref[0, rows, :] = jnp.where(lane == 0, g1, jnp.where(lane == 1, g2, 0.0))


def _out_ln_route(y, w, xres, gate, ln_g, ln_b, sc, sh, wr, br, alpha, n_groups, per_group, name):
    b, s, k = y.shape
    d = w.shape[1]
    tm = _tile(s, 512)
    parts = 2 if tm % (2 * SUBLANES) == 0 else 1
    kern = functools.partial(_outln_kernel, alpha=alpha, n_groups=n_groups, per_group=per_group, parts=parts)
    row = lambda bi, i: (bi, i, 0)
    per_b = lambda bi, i: (bi, 0, 0)
    const = lambda bi, i: (0, 0)
    return pl.pallas_call(
        kern,
        out_shape=(jax.ShapeDtypeStruct((b, s, d), F32), jax.ShapeDtypeStruct((b, s, d // 2), U32),
                   jax.ShapeDtypeStruct((b, s, LANES), I32), jax.ShapeDtypeStruct((b, s, LANES), F32)),
        grid=(b, s // tm),
        in_specs=[pl.BlockSpec((1, tm, k), row),
                  pl.BlockSpec((k, d), const),
                  pl.BlockSpec((1, tm, d), row),
                  pl.BlockSpec((1, 1, d), per_b),
                  pl.BlockSpec((1, d), const),
                  pl.BlockSpec((1, d), const),
                  pl.BlockSpec((1, 1, d), per_b),
                  pl.BlockSpec((1, 1, d), per_b),
                  pl.BlockSpec((d, 2 * LANES), const),
                  pl.BlockSpec((1, LANES), const)],
        out_specs=(pl.BlockSpec((1, tm, d), row), pl.BlockSpec((1, tm, d // 2), row),
                   pl.BlockSpec((1, tm, LANES), row), pl.BlockSpec((1, tm, LANES), row)),
        compiler_params=_cparams("parallel", "parallel"),
        name=name,
    )(y, w, xres, gate, ln_g.reshape(1, d), ln_b.reshape(1, d), sc, sh, wr, br)


def _dispatch_kernel(ids_ref, dest_ref, meta_ref, rank_ref, *, t, tile, blk_shift):
    lane = lax.broadcasted_iota(I32, (tile, LANES), 1)
    ri = lax.broadcasted_iota(I32, (tile, tile), 0)
    ci = lax.broadcasted_iota(I32, (tile, tile), 1)
    before = (ri > ci).astype(BF16)

    def hits(rows):
        ids = ids_ref[rows, :]
        return lane == ids[:, 0:1], lane == ids[:, 1:2]

    def count(ti, run):
        rows = pl.ds(pl.multiple_of(ti * tile, tile), tile)
        h1, h2 = hits(rows)
        onehot = jnp.logical_or(h1, h2).astype(BF16)
        prefix = jnp.dot(before, onehot, preferred_element_type=F32) + run
        r1 = jnp.sum(jnp.where(h1, prefix, 0.0), -1, keepdims=True)
        r2 = jnp.sum(jnp.where(h2, prefix, 0.0), -1, keepdims=True)
        rank_ref[rows, :] = jnp.where(lane == 0, r1, jnp.where(lane == 1, r2, 0.0))
        return run + jnp.sum(onehot.astype(F32), 0, keepdims=True)

    counts = lax.fori_loop(0, t // tile, count, jnp.zeros((1, LANES), F32))
    cnt = jnp.broadcast_to(counts, (SUBLANES, LANES)).astype(I32)
    nblk = (cnt + ((1 << blk_shift) - 1)) >> blk_shift
    padded = nblk << blk_shift
    lane8 = lax.broadcasted_iota(I32, (SUBLANES, LANES), 1)
    incl = padded
    step = 1
    while step < LANES:
        incl = incl + jnp.where(lane8 >= step, pltpu.roll(incl, step, 1), 0)
        step *= 2
    pad_start = incl - padded
    row8 = lax.broadcasted_iota(I32, (SUBLANES, LANES), 0)
    meta_ref[...] = jnp.where(row8 == 0, pad_start, jnp.where(row8 == 1, nblk, cnt))
    start_f = pad_start[0:1, :].astype(F32)

    def place(ti, c):
        rows = pl.ds(pl.multiple_of(ti * tile, tile), tile)
        h1, h2 = hits(rows)
        rk = rank_ref[rows, :]
        d1 = jnp.sum(jnp.where(h1, start_f, 0.0), -1, keepdims=True) + rk[:, 0:1]
        d2 = jnp.sum(jnp.where(h2, start_f, 0.0), -1, keepdims=True) + rk[:, 1:2]
        dest_ref[rows, :] = jnp.where(lane == 0, d1, jnp.where(lane == 1, d2, 0.0)).astype(I32)
        return c

    lax.fori_loop(0, t // tile, place, 0)


def _dispatch(ids):
    t = ids.shape[0]
    tile = _tile(t, 256)
    blk_shift = MOE_BLOCK.bit_length() - 1
    assert (1 << blk_shift) == MOE_BLOCK
    return pl.pallas_call(
        functools.partial(_dispatch_kernel, t=t, tile=tile, blk_shift=blk_shift),
        out_shape=(jax.ShapeDtypeStruct((t, LANES), I32), jax.ShapeDtypeStruct((SUBLANES, LANES), I32)),
        grid=(1,),
        in_specs=[pl.BlockSpec((t, LANES), lambda i: (0, 0))],
        out_specs=(pl.BlockSpec((t, LANES), lambda i: (0, 0)), pl.BlockSpec((SUBLANES, LANES), lambda i: (0, 0))),
        scratch_shapes=[pltpu.VMEM((t, LANES), F32)],
        compiler_params=_cparams("arbitrary"),
        name="moe_dispatch",
    )(ids)


def _scatter_rows_kernel(dest_ref, hin_ref, xs_in, xs_out, stage, sem, *, tm, n_steps):
    del xs_in
    step = pl.program_id(0)
    base = step * (tm * MOE_TOPK)

    def wait_slot(sl):
        for _ in range(MOE_TOPK):
            pltpu.make_async_copy(stage.at[sl], xs_out.at[pl.ds(0, tm), :], sem.at[sl]).wait()

    for sl in range(2):
        @pl.when(step % 2 == sl)
        def _(sl=sl):
            @pl.when(step >= 2)
            def _():
                wait_slot(sl)

            stage[sl] = hin_ref[...]
            for r in range(tm):
                for k in range(MOE_TOPK):
                    pltpu.make_async_copy(stage.at[sl, pl.ds(r, 1), :],
                                          xs_out.at[pl.ds(dest_ref[base + r * MOE_TOPK + k], 1), :],
                                          sem.at[sl]).start()

            @pl.when(step == n_steps - 1)
            def _():
                if n_steps >= 2:
                    wait_slot(1 - sl)
                wait_slot(sl)


def _scatter_rows(dest_flat, hin2, xs_init):
    t, d = hin2.shape
    n_rows = xs_init.shape[0]
    tm = _tile(t, 256)
    grid_spec = pltpu.PrefetchScalarGridSpec(
        num_scalar_prefetch=1,
        grid=(t // tm,),
        in_specs=[pl.BlockSpec((tm, d), lambda i, dst: (i, 0)),
                  pl.BlockSpec(memory_space=pl.ANY)],
        out_specs=pl.BlockSpec(memory_space=pl.ANY),
        scratch_shapes=[pltpu.VMEM((2, tm, d), hin2.dtype), pltpu.SemaphoreType.DMA((2,))],
    )
    return pl.pallas_call(
        functools.partial(_scatter_rows_kernel, tm=tm, n_steps=t // tm),
        out_shape=jax.ShapeDtypeStruct((n_rows, d), hin2.dtype),
        grid_spec=grid_spec,
        input_output_aliases={2: 0},
        compiler_params=_cparams("arbitrary"),
        name="moe_scatter_rows",
    )(dest_flat, hin2, xs_init)


def _ffn_kernel(eblk_ref, enb_ref, xs_hbm, wg_ref, wu_ref, wd_ref, ys_hbm,
                xbuf, ybuf, wgb, wub, wdb, isem, osem, *, blk, nb_total, n_exp):
    e = pl.program_id(0)
    n = enb_ref[e]
    b0 = eblk_ref[e]
    used = eblk_ref[n_exp - 1] + enb_ref[n_exp - 1]

    def in_cp(gb):
        sl = gb % FFN_RING
        return pltpu.make_async_copy(xs_hbm.at[pl.ds(gb * blk, blk), :], xbuf.at[sl], isem.at[sl])

    def out_cp(gb):
        sl = gb % FFN_RING
        return pltpu.make_async_copy(ybuf.at[sl], ys_hbm.at[pl.ds(gb * blk, blk), :], osem.at[sl])

    @pl.when(e == 0)
    def _():
        for k in range(FFN_RING - 1):
            @pl.when(k < used)
            def _(k=k):
                in_cp(k).start(priority=1)

    @pl.when(n > 0)
    def _():
        wgb[...] = wg_ref[0, 0].astype(BF16)
        wub[...] = wu_ref[0, 0].astype(BF16)
        wdb[...] = wd_ref[0, 0].astype(BF16)

        def body(j, c):
            gb = b0 + j
            sl = gb % FFN_RING

            @pl.when(gb + FFN_RING - 1 < used)
            def _():
                in_cp(gb + FFN_RING - 1).start(priority=1)

            in_cp(gb).wait()

            @pl.when(gb >= FFN_RING)
            def _():
                out_cp(gb - FFN_RING).wait()

            xb = jnp.concatenate(_unpack_bf16_pair(xbuf[sl]), axis=1)
            hg = jnp.dot(xb, wgb[...], preferred_element_type=F32)
            hu = jnp.dot(xb, wub[...], preferred_element_type=F32)
            hid = (_silu(hg) * hu).astype(BF16)
            ybuf[sl] = jnp.dot(hid, wdb[...], preferred_element_type=F32)
            out_cp(gb).start(priority=1)
            return c

        lax.fori_loop(0, n, body, 0)

    @pl.when(e == n_exp - 1)
    def _():
        for k in range(FFN_RING, 0, -1):
            @pl.when(used >= k)
            def _(k=k):
                out_cp(used - k).wait()

        ybuf[0] = jnp.zeros((blk, ybuf.shape[2]), F32)

        def fill(jb, c):
            cp = pltpu.make_async_copy(ybuf.at[0], ys_hbm.at[pl.ds(jb * blk, blk), :], osem.at[0])
            cp.start()
            cp.wait()
            return c

        lax.fori_loop(used, nb_total, fill, 0)


def _moe_ffn(eblk, enb, xs, w_gate, w_up, w_down, layer):
    n_rows = xs.shape[0]
    n_exp, d, ff = w_gate.shape[1], w_gate.shape[2], w_gate.shape[3]
    blk = MOE_BLOCK
    wmap = lambda e, eb, en: (layer, e, 0, 0)
    grid_spec = pltpu.PrefetchScalarGridSpec(
        num_scalar_prefetch=2,
        grid=(n_exp,),
        in_specs=[pl.BlockSpec(memory_space=pl.ANY),
                  pl.BlockSpec((1, 1, d, ff), wmap),
                  pl.BlockSpec((1, 1, d, ff), wmap),
                  pl.BlockSpec((1, 1, ff, d), wmap)],
        out_specs=pl.BlockSpec(memory_space=pl.ANY),
        scratch_shapes=[pltpu.VMEM((FFN_RING, blk, d // 2), U32),
                        pltpu.VMEM((FFN_RING, blk, d), F32),
                        pltpu.VMEM((d, ff), BF16),
                        pltpu.VMEM((d, ff), BF16),
                        pltpu.VMEM((ff, d), BF16),
                        pltpu.SemaphoreType.DMA((FFN_RING,)),
                        pltpu.SemaphoreType.DMA((FFN_RING,))],
    )
    return pl.pallas_call(
        functools.partial(_ffn_kernel, blk=blk, nb_total=n_rows // blk, n_exp=n_exp),
        out_shape=jax.ShapeDtypeStruct((n_rows, d), F32),
        grid_spec=grid_spec,
        compiler_params=_cparams("arbitrary"),
        name="moe_ffn",
    )(eblk, enb, xs, w_gate, w_up, w_down)


def _gather_ln_kernel(dest_ref, ys_hbm, rg_ref, x_ref, g_ref, lng_ref, lnb_ref, o_ref, ybuf, sem,
                      *, alpha, tm, nt, n_steps):
    step = pl.program_id(0) * nt + pl.program_id(1)
    slot = step % 2

    def start_gather(tile_idx, sl):
        base = tile_idx * (tm * MOE_TOPK)
        for r in range(tm):
            for k in range(MOE_TOPK):
                pltpu.make_async_copy(ys_hbm.at[pl.ds(dest_ref[base + r * MOE_TOPK + k], 1), :],
                                      ybuf.at[sl, k, pl.ds(r, 1), :], sem.at[sl]).start()

    @pl.when(step == 0)
    def _():
        start_gather(0, 0)

    for sl in range(2):
        @pl.when(jnp.logical_and(step + 1 < n_steps, slot == 1 - sl))
        def _(sl=sl):
            start_gather(step + 1, sl)

    for k in range(MOE_TOPK):
        pltpu.make_async_copy(ys_hbm.at[pl.ds(0, tm), :], ybuf.at[slot, k], sem.at[slot]).wait()

    rg = rg_ref[0]
    y = ybuf[slot, 0] * rg[:, 0:1] + ybuf[slot, 1] * rg[:, 1:2]
    r = alpha * x_ref[0] + (1.0 + g_ref[0]) * y
    o_ref[0] = _layer_norm_rows(r, lng_ref[...], lnb_ref[...])


def _gather_combine_ln(dest_flat, ys, route_gates, x, gate, ln_g, ln_b, alpha, name):
    b, s, d = x.shape
    tm = _tile(s, 256)
    nt = s // tm
    row = lambda bi, i, dst: (bi, i, 0)
    const = lambda bi, i, dst: (0, 0)
    grid_spec = pltpu.PrefetchScalarGridSpec(
        num_scalar_prefetch=1,
        grid=(b, nt),
        in_specs=[pl.BlockSpec(memory_space=pl.ANY),
                  pl.BlockSpec((1, tm, LANES), row),
                  pl.BlockSpec((1, tm, d), row),
                  pl.BlockSpec((1, 1, d), lambda bi, i, dst: (bi, 0, 0)),
                  pl.BlockSpec((1, d), const),
                  pl.BlockSpec((1, d), const)],
        out_specs=pl.BlockSpec((1, tm, d), row),
        scratch_shapes=[pltpu.VMEM((2, MOE_TOPK, tm, d), F32), pltpu.SemaphoreType.DMA((2,))],
    )
    return pl.pallas_call(
        functools.partial(_gather_ln_kernel, alpha=alpha, tm=tm, nt=nt, n_steps=b * nt),
        out_shape=jax.ShapeDtypeStruct((b, s, d), F32),
        grid_spec=grid_spec,
        compiler_params=_cparams("arbitrary", "arbitrary"),
        name=name,
    )(dest_flat, ys, route_gates, x, gate, ln_g.reshape(1, d), ln_b.reshape(1, d))


def _gdn_kernel(q_ref, k_ref, v_ref, z_ref, bg_ref, cg_ref, hh_ref, ba_ref,
                cw_ref, alog_ref, dtb_ref, nw_ref, scw_ref,
                o_ref,
                qa_ref, ka_ref, va_ref, hist_ref, hist2_ref, gate_ref, state_ref,
                *, heads, ts, gw):
    L = GDN_CHUNK
    dk = GDN_HEAD_DIM

    @pl.when(pl.program_id(1) == 0)
    def _():
        hist_ref[...] = jnp.zeros_like(hist_ref)
        hist2_ref[...] = jnp.zeros_like(hist2_ref)
        state_ref[...] = jnp.zeros_like(state_ref)

    for idx, (src, dst) in enumerate(((q_ref, qa_ref), (k_ref, ka_ref), (v_ref, va_ref))):
        x = src[0]
        y = _causal_conv(x, hist_ref[idx], cw_ref[:, idx * gw:(idx + 1) * gw], GDN_CONV)
        hist_ref[idx] = x[ts - SUBLANES:ts, :]
        dst[...] = _silu(y)

    c = cg_ref[0] * hh_ref[0]
    yb = _causal_conv(c, hist2_ref[...], scw_ref[...], SC_CONV)
    hist2_ref[...] = c[ts - SUBLANES:ts, :]
    o_ref[0, :, gw:2 * gw] = bg_ref[0] * yb

    ba = ba_ref[0]
    gate_ref[0] = _sigmoid(ba)
    gate_ref[1] = -jnp.exp(alog_ref[...]) * _softplus(ba + dtb_ref[...])

    ii = lax.broadcasted_iota(I32, (L, L), 0)
    jj = lax.broadcasted_iota(I32, (L, L), 1)
    causal = ii >= jj
    strict = ii > jj
    tri_incl = causal.astype(F32)
    eye = (ii == jj).astype(F32)
    nchunk = ts // L
    pairs = [(c, h) for c in range(nchunk) for h in range(heads)]

    def tiles(ref, lead=()):
        return jnp.stack([ref[lead + (slice(c * L, (c + 1) * L), slice(h * dk, (h + 1) * dk))]
                          for c, h in pairs], axis=0)

    def bmm(a, b):
        return jnp.einsum('bij,bjk->bik', a.astype(BF16), b.astype(BF16), preferred_element_type=F32)

    def bmm_nt(a, b):
        return jnp.einsum('bik,bjk->bij', a.astype(BF16), b.astype(BF16), preferred_element_type=F32)

    qh = tiles(qa_ref)
    kh = tiles(ka_ref)
    vh = tiles(va_ref)
    beta = jnp.stack([gate_ref[0, c * L:(c + 1) * L, h:h + 1] for c, h in pairs], axis=0)
    cums = [jnp.dot(tri_incl, gate_ref[1, c * L:(c + 1) * L, :], precision=HIGHEST,
                    preferred_element_type=F32) for c in range(nchunk)]
    cums_t = [cm.T for cm in cums]
    gc = jnp.stack([cums[c][:, heads + h:heads + h + 1] for c, h in pairs], axis=0)
    gr = jnp.stack([cums_t[c][heads + h:heads + h + 1, :] for c, h in pairs], axis=0)
    gl = gc[:, L - 1:L, :]
    decay = jnp.where(causal, jnp.exp(jnp.where(causal, gc - gr, 0.0)), 0.0)
    egc = jnp.exp(gc)
    qn = qh * lax.rsqrt(jnp.sum(qh * qh, -1, keepdims=True) + 1e-6)
    kn = kh * lax.rsqrt(jnp.sum(kh * kh, -1, keepdims=True) + 1e-6)
    qc = qn * (dk ** -0.5)
    kb = kn * beta
    m = jnp.where(strict, bmm_nt(kb, kn) * decay, 0.0)
    t_inv = eye - m
    mp = m
    for _ in range(5):
        mp = bmm(mp, mp)
        t_inv = t_inv + bmm(t_inv, mp)
    sol = bmm(t_inv, jnp.concatenate([vh * beta, kb * egc], axis=2))
    u = sol[:, :, 0:dk]
    w = sol[:, :, dk:2 * dk]
    qk = jnp.where(causal, bmm_nt(qc, kn) * decay, 0.0)
    qg = qc * egc
    k_dec = kn * jnp.exp(gl - gc)
    egl = jnp.exp(gl)

    state = [state_ref[h] for h in range(heads)]
    outs = []
    for b, (c, h) in enumerate(pairs):
        st = state[h]
        v_new = u[b] - _dot(w[b], st)
        outs.append(_dot(qg[b], st) + _dot(qk[b], v_new))
        state[h] = st * egl[b] + _dot(k_dec[b].T, v_new)
    for h in range(heads):
        state_ref[h] = state[h]

    o = jnp.stack(outs, axis=0)
    on = o * lax.rsqrt(jnp.mean(o * o, -1, keepdims=True) + 1e-6) * nw_ref[...]
    y = on * _silu(tiles(z_ref, (0,)))
    for b, (c, h) in enumerate(pairs):
        o_ref[0, c * L:(c + 1) * L, h * dk:(h + 1) * dk] = y[b]


def _gdn_shortconv(proj, conv_w, a_log, dt_bias, norm_w, sc_w, heads):
    b, s, _ = proj.shape
    gw = heads * GDN_HEAD_DIM
    ts = _tile(s, 256)
    alog_p = jnp.zeros((1, LANES), F32).at[0, heads:2 * heads].set(a_log)
    dtb_p = jnp.zeros((1, LANES), F32).at[0, heads:2 * heads].set(dt_bias)
    col = lambda cb: (lambda bi, i: (bi, i, cb))
    const = lambda bi, i: (0, 0)
    kern = functools.partial(_gdn_kernel, heads=heads, ts=ts, gw=gw)
    return pl.pallas_call(
        kern,
        out_shape=jax.ShapeDtypeStruct((b, s, 2 * gw), F32),
        grid=(b, s // ts),
        in_specs=[pl.BlockSpec((1, ts, gw), col(cb)) for cb in range(7)]
        + [pl.BlockSpec((1, ts, LANES), col(7 * gw // LANES)),
           pl.BlockSpec((GDN_CONV, 3 * gw), const),
           pl.BlockSpec((1, LANES), const),
           pl.BlockSpec((1, LANES), const),
           pl.BlockSpec((1, GDN_HEAD_DIM), const),
           pl.BlockSpec((SC_CONV, gw), const)],
        out_specs=pl.BlockSpec((1, ts, 2 * gw), lambda bi, i: (bi, i, 0)),
        scratch_shapes=[pltpu.VMEM((ts, gw), F32),
                        pltpu.VMEM((ts, gw), F32),
                        pltpu.VMEM((ts, gw), F32),
                        pltpu.VMEM((3, SUBLANES, gw), F32),
                        pltpu.VMEM((SUBLANES, gw), F32),
                        pltpu.VMEM((2, ts, LANES), F32),
                        pltpu.VMEM((heads, GDN_HEAD_DIM, GDN_HEAD_DIM), F32)],
        compiler_params=_cparams("parallel", "arbitrary"),
        name="gdn_shortconv",
    )(proj, proj, proj, proj, proj, proj, proj, proj,
      conv_w, alog_p, dtb_p, norm_w.reshape(1, GDN_HEAD_DIM), sc_w)


def _rope_kernel(pos_ref, inv_ref, qpe_ref, kpe_ref, qo_ref, ko_ref, *, heads):
    ang = pos_ref[0].astype(F32) * inv_ref[...]
    lane = lax.broadcasted_iota(I32, ang.shape, 1)
    half = MLA_ROPE // 2
    cos = jnp.where(lane < MLA_ROPE, jnp.cos(ang), 0.0)
    sin = jnp.sin(ang)
    sgn = jnp.where(lane < half, -sin, jnp.where(lane < MLA_ROPE, sin, 0.0))

    def rot(x):
        swapped = jnp.where(lane < half, pltpu.roll(x, LANES - half, 1), pltpu.roll(x, half, 1))
        return x * cos + swapped * sgn

    for h in range(heads):
        cols = slice(h * LANES, (h + 1) * LANES)
        qo_ref[0, :, cols] = rot(qpe_ref[0, :, cols]).astype(BF16)
    ko_ref[0] = rot(kpe_ref[0]).astype(BF16)


def _rope(positions, qpe, heads, cproj, kpe_block):
    b, s = positions.shape
    ts = _tile(s, 512)
    half = MLA_ROPE // 2
    inv = ROPE_THETA ** (-jnp.arange(half, dtype=F32) * (2.0 / MLA_ROPE))
    inv_tab = jnp.zeros((1, LANES), F32).at[0, 0:half].set(inv).at[0, half:MLA_ROPE].set(inv)
    w = heads * LANES
    return pl.pallas_call(
        functools.partial(_rope_kernel, heads=heads),
        out_shape=(jax.ShapeDtypeStruct((b, s, w), BF16), jax.ShapeDtypeStruct((b, s, LANES), BF16)),
        grid=(b, s // ts),
        in_specs=[pl.BlockSpec((1, ts, 1), lambda bi, i: (bi, i, 0)),
                  pl.BlockSpec((1, LANES), lambda bi, i: (0, 0)),
                  pl.BlockSpec((1, ts, w), lambda bi, i: (bi, i, 0)),
                  pl.BlockSpec((1, ts, LANES), lambda bi, i: (bi, i, kpe_block))],
        out_specs=(pl.BlockSpec((1, ts, w), lambda bi, i: (bi, i, 0)),
                   pl.BlockSpec((1, ts, LANES), lambda bi, i: (bi, i, 0))),
        compiler_params=_cparams("parallel", "parallel"),
        name="mla_rope",
    )(positions.reshape(b, s, 1), inv_tab, qpe, cproj)


def _attn_kernel(qn_ref, qp_ref, kn_ref, kp_ref, v_ref, o_ref, kcat_ref, s_ref, *, tq, hq, nq, scale):
    qi = pl.program_id(2)

    @pl.when(qi == 0)
    def _():
        kcat_ref[:, 0:MLA_NOPE] = kn_ref[0]
        kcat_ref[:, MLA_NOPE:MLA_NOPE + LANES] = kp_ref[0]

    qcat = jnp.concatenate([qn_ref[0], qp_ref[0]], axis=1)
    tri = lax.broadcasted_iota(I32, (hq, hq), 0) >= lax.broadcasted_iota(I32, (hq, hq), 1)

    for qv in range(nq):
        @pl.when(qi == qv)
        def _(qv=qv):
            for h in range(tq // hq):
                rows = slice(h * hq, (h + 1) * hq)
                nk = qv * tq + (h + 1) * hq
                for c0 in range(0, nk, ATTN_KEY_CHUNK):
                    c1 = min(c0 + ATTN_KEY_CHUNK, nk)
                    s = lax.dot_general(qcat[rows], kcat_ref[c0:c1, :], (((1,), (1,)), ((), ())),
                                        preferred_element_type=F32) * scale
                    if c1 == nk:
                        w = c1 - c0
                        last = jnp.where(tri, s[:, w - hq:], -jnp.inf)
                        s = last if w == hq else jnp.concatenate([s[:, :w - hq], last], axis=1)
                    s_ref[rows, c0:c1] = s
                    mb = jnp.max(s, -1, keepdims=True)
                    m = mb if c0 == 0 else jnp.maximum(m, mb)
                l = jnp.zeros((hq, 1), F32)
                acc = jnp.zeros((hq, MLA_V), F32)
                for c0 in range(0, nk, hq):
                    p = jnp.exp(s_ref[rows, c0:c0 + hq] - m)
                    l = l + jnp.sum(p, -1, keepdims=True)
                    acc = acc + jnp.dot(p.astype(BF16), v_ref[0, c0:c0 + hq, :], preferred_element_type=F32)
                o_ref[0, rows, :] = acc / l


def _attention(qn, qpe, kvup, kpe, heads):
    b, s, _ = qn.shape
    tq = _tile(s, 1024)
    hq = _tile(tq, 256)
    nq = s // tq
    scale = (MLA_NOPE + MLA_ROPE) ** -0.5
    return pl.pallas_call(
        functools.partial(_attn_kernel, tq=tq, hq=hq, nq=nq, scale=scale),
        out_shape=jax.ShapeDtypeStruct((b, s, heads * MLA_V), F32),
        grid=(b, heads, nq),
        in_specs=[pl.BlockSpec((1, tq, MLA_NOPE), lambda bi, h, i: (bi, i, h)),
                  pl.BlockSpec((1, tq, LANES), lambda bi, h, i: (bi, i, h)),
                  pl.BlockSpec((1, s, MLA_NOPE), lambda bi, h, i: (bi, 0, h)),
                  pl.BlockSpec((1, s, LANES), lambda bi, h, i: (bi, 0, 0)),
                  pl.BlockSpec((1, s, MLA_V), lambda bi, h, i: (bi, 0, heads + h))],
        out_specs=pl.BlockSpec((1, tq, MLA_V), lambda bi, h, i: (bi, i, h)),
        scratch_shapes=[pltpu.VMEM((s, MLA_NOPE + LANES), BF16), pltpu.VMEM((tq, s), F32)],
        compiler_params=_cparams("parallel", "parallel", "arbitrary"),
        name="mla_attention",
    )(qn, qpe, kvup, kpe, kvup)


def _hyb_in_weight(w_in, heads):
    d = w_in.shape[0]
    gw = heads * GDN_HEAD_DIM
    main = jnp.concatenate([w_in[:, :4 * gw], w_in[:, 4 * gw + 2 * heads:]], axis=1)
    gates = w_in[:, 4 * gw:4 * gw + 2 * heads]
    n = main.shape[1] + LANES
    n_pad = -(-n // 768) * 768
    tail = jnp.zeros((d, n_pad - main.shape[1]), w_in.dtype).at[:, :2 * heads].set(gates)
    return jnp.concatenate([main, tail], axis=1).astype(BF16)


def _mla_in_weight(w_in):
    d = w_in.shape[0]
    pad = jnp.zeros((d, LANES - MLA_ROPE), w_in.dtype)
    return jnp.concatenate([w_in, pad], axis=1).astype(BF16)


def _mla_uq_weights(w_uq, heads):
    r = w_uq.shape[0]
    w = w_uq.reshape(r, heads, MLA_NOPE + MLA_ROPE)
    nope = w[:, :, :MLA_NOPE].reshape(r, heads * MLA_NOPE)
    pe = jnp.concatenate([w[:, :, MLA_NOPE:], jnp.zeros((r, heads, LANES - MLA_ROPE), w.dtype)], axis=2)
    return nope.astype(BF16), pe.reshape(r, heads * LANES).astype(BF16)


def _mla_ukv_weight(w_ukv, heads):
    r = w_ukv.shape[0]
    w = w_ukv.reshape(r, heads, MLA_NOPE + MLA_V)
    return jnp.concatenate([w[:, :, :MLA_NOPE].reshape(r, heads * MLA_NOPE),
                            w[:, :, MLA_NOPE:].reshape(r, heads * MLA_V)], axis=1).astype(BF16)


def _router_weight(wr_g, br_g, wr_e, br_e):
    d = wr_g.shape[0]
    n = wr_g.shape[1] + wr_e.shape[1]
    w = jnp.zeros((d, LANES), F32).at[:, :n].set(jnp.concatenate([wr_g, wr_e], axis=1))
    bias = jnp.zeros((1, LANES), F32).at[0, :n].set(jnp.concatenate([br_g, br_e]))
    w_hi = w.astype(BF16)
    w_lo = (w - w_hi.astype(F32)).astype(BF16)
    return jnp.concatenate([w_hi, w_lo], axis=1), bias


def kernel(x, c, positions, ada_w, ada_b, ln_g, ln_b, hyb_w_in, gdn_conv_w, gdn_a_log, gdn_dt_bias,
           gdn_norm_w, sc_conv_w, hyb_w_out, mla_w_in, mla_q_norm, mla_kv_norm, mla_w_uq, mla_w_ukv,
           mla_w_out, moe_router_g, moe_bias_g, moe_router_e, moe_bias_e, moe_w_gate, moe_w_up,
           moe_w_down):
    b, s, d = x.shape
    depth = ada_w.shape[0]
    alpha = (2.0 * depth) ** 0.25
    gdn_heads = d // (2 * GDN_HEAD_DIM)
    mla_heads = mla_w_out.shape[1] // MLA_V
    n_groups = moe_router_g.shape[2]
    n_exp = moe_router_e.shape[2]
    per_group = n_exp // n_groups

    mod = _modulation(c, ada_w, ada_b)
    t = b * s
    xs = jnp.zeros((t * MOE_TOPK + n_exp * MOE_BLOCK, d // 2), U32)
    for layer in range(depth):
        sh1, sc1, g1, sh2, sc2, g2 = [mod[layer, :, j * d:(j + 1) * d].reshape(b, 1, d) for j in range(6)]
        i = layer // 2
        wr, br = _router_weight(moe_router_g[layer], moe_bias_g[layer], moe_router_e[layer], moe_bias_e[layer])
        if layer % 2 == 0:
            proj = _mod_matmul(x, sc1, sh1, _hyb_in_weight(hyb_w_in[i], gdn_heads), "hyb_in_proj")
            y = _gdn_shortconv(proj, gdn_conv_w[i], gdn_a_log[i], gdn_dt_bias[i], gdn_norm_w[i],
                               sc_conv_w[i], gdn_heads)
            w_out = hyb_w_out[i]
        else:
            cproj = _mod_matmul(x, sc1, sh1, _mla_in_weight(mla_w_in[i]), "mla_in_proj")
            w_qn, w_qpe = _mla_uq_weights(mla_w_uq[i], mla_heads)
            qn = _rms_matmul(cproj, 0, MLA_Q_RANK, mla_q_norm[i], w_qn, BF16, "mla_q_nope_up")
            qpe_raw = _rms_matmul(cproj, 0, MLA_Q_RANK, mla_q_norm[i], w_qpe, F32, "mla_q_rope_up")
            kvup = _rms_matmul(cproj, 1, MLA_KV_RANK, mla_kv_norm[i], _mla_ukv_weight(mla_w_ukv[i], mla_heads),
                               BF16, "mla_kv_up")
            qpe, kpe = _rope(positions, qpe_raw, mla_heads, cproj, (MLA_Q_RANK + MLA_KV_RANK) // LANES)
            y = _attention(qn, qpe, kvup, kpe, mla_heads)
            w_out = mla_w_out[i]
        x, hin2, ids, gates = _out_ln_route(y, w_out.astype(BF16), x, g1, ln_g[layer, 0], ln_b[layer, 0],
                                            sc2, sh2, wr, br, alpha, n_groups, per_group,
                                            "mixer_out_ln_route")
        dest, meta = _dispatch(ids.reshape(t, LANES))
        dest_flat = dest[:, :MOE_TOPK].reshape(-1)
        eblk = meta[0, :n_exp] // MOE_BLOCK
        enb = meta[1, :n_exp]
        xs = _scatter_rows(dest_flat, hin2.reshape(t, d // 2), xs)
        ys = _moe_ffn(eblk, enb, xs, moe_w_gate, moe_w_up, moe_w_down, layer)
        x = _gather_combine_ln(dest_flat, ys, gates, x, g2, ln_g[layer, 1], ln_b[layer, 1], alpha,
                               "moe_gather_combine_ln")
    return x
```

```python
import functools

import jax
import jax.numpy as jnp
from jax import lax
from jax.experimental import pallas as pl
from jax.experimental.pallas import tpu as pltpu

F32, BF16, I32, U32 = jnp.float32, jnp.bfloat16, jnp.int32, jnp.uint32
HIGHEST = lax.Precision.HIGHEST

LANES = 128
SUBLANES = 8
VMEM_LIMIT_BYTES = 56 * 1024 * 1024

GDN_HEAD_DIM = 128
GDN_CONV = 4
GDN_CHUNK = 64
SC_CONV = 3
MLA_NOPE = 128
MLA_ROPE = 64
MLA_V = 128
MLA_Q_RANK = 512
MLA_KV_RANK = 512
ROPE_THETA = 10000.0
MOE_TOPK = 2
MOE_BLOCK = 128
FFN_RING = 6
ATTN_KEY_CHUNK = 512


def _tile(n, pref):
    if n <= pref:
        return n
    for t in range(pref, 0, -LANES):
        if n % t == 0:
            return t
    return n


def _cparams(*sem):
    return pltpu.CompilerParams(dimension_semantics=sem, vmem_limit_bytes=VMEM_LIMIT_BYTES)


def _sigmoid(x):
    return 1.0 / (1.0 + jnp.exp(-x))


def _silu(x):
    return x * _sigmoid(x)


def _softplus(x):
    return jnp.maximum(x, 0.0) + jnp.log(1.0 + jnp.exp(-jnp.abs(x)))


def _pack_bf16_pair(lo, hi):
    lo_bits = lax.bitcast_convert_type(lo.astype(BF16).astype(F32), U32)
    hi_bits = lax.bitcast_convert_type(hi.astype(BF16).astype(F32), U32)
    return (lo_bits >> 16) | (hi_bits & jnp.uint32(0xFFFF0000))


def _unpack_bf16_pair(w):
    lo = lax.bitcast_convert_type(w << 16, F32)
    hi = lax.bitcast_convert_type(w & jnp.uint32(0xFFFF0000), F32)
    return lo.astype(BF16), hi.astype(BF16)


def _dot(a, b):
    return jnp.dot(a.astype(BF16), b.astype(BF16), preferred_element_type=F32)


def _dot_nt(a, b):
    return lax.dot_general(a.astype(BF16), b.astype(BF16), (((1,), (1,)), ((), ())),
                           preferred_element_type=F32)


def _causal_conv(x, hist, w, taps):
    rows, width = x.shape
    g = rows // SUBLANES
    x3 = x.reshape(g, SUBLANES, width)
    sub = lax.broadcasted_iota(I32, (1, SUBLANES, width), 1)
    y = None
    for j in range(taps - 1):
        shift = taps - 1 - j
        rot = pltpu.roll(x3, shift, 1)
        prev = jnp.concatenate([pltpu.roll(hist, shift, 0)[None], rot[:g - 1]], axis=0)
        term = w[j:j + 1] * jnp.where(sub < shift, prev, rot)
        y = term if y is None else y + term
    return (y + w[taps - 1:taps] * x3).reshape(rows, width)


def _mod_kernel(c_ref, w_ref, b_ref, o_ref):
    c = c_ref[...]
    o_ref[0] = _dot(_silu(c), w_ref[0]) + b_ref[0]


def _modulation(c, ada_w, ada_b):
    depth, d, n = ada_w.shape
    b = c.shape[0]
    rows = -(-b // SUBLANES) * SUBLANES
    cp = jnp.zeros((rows, d), F32).at[:b].set(c)
    tn = _tile(n, 1024)
    out = pl.pallas_call(
        _mod_kernel,
        out_shape=jax.ShapeDtypeStruct((depth, rows, n), F32),
        grid=(depth, n // tn),
        in_specs=[pl.BlockSpec((rows, d), lambda l, j: (0, 0)),
                  pl.BlockSpec((1, d, tn), lambda l, j: (l, 0, j)),
                  pl.BlockSpec((1, 1, tn), lambda l, j: (l, 0, j))],
        out_specs=pl.BlockSpec((1, rows, tn), lambda l, j: (l, 0, j)),
        compiler_params=_cparams("parallel", "parallel"),
        name="adaln_mod",
    )(cp, ada_w, ada_b.reshape(depth, 1, n))
    return out[:, :b]


def _modmm_kernel(x_ref, sc_ref, sh_ref, w_ref, o_ref, xb_ref):
    @pl.when(pl.program_id(2) == 0)
    def _():
        xb_ref[...] = (x_ref[0] * (1.0 + sc_ref[0]) + sh_ref[0]).astype(BF16)

    o_ref[0] = jnp.dot(xb_ref[...], w_ref[...], preferred_element_type=F32)


def _mod_matmul(x, sc, sh, w, name):
    b, s, d = x.shape
    n = w.shape[1]
    tm = _tile(s, 1024)
    tn = _tile(n, 768)
    return pl.pallas_call(
        _modmm_kernel,
        out_shape=jax.ShapeDtypeStruct((b, s, n), F32),
        grid=(b, s // tm, n // tn),
        in_specs=[pl.BlockSpec((1, tm, d), lambda bi, i, j: (bi, i, 0)),
                  pl.BlockSpec((1, 1, d), lambda bi, i, j: (bi, 0, 0)),
                  pl.BlockSpec((1, 1, d), lambda bi, i, j: (bi, 0, 0)),
                  pl.BlockSpec((d, tn), lambda bi, i, j: (0, j))],
        out_specs=pl.BlockSpec((1, tm, tn), lambda bi, i, j: (bi, i, j)),
        scratch_shapes=[pltpu.VMEM((tm, d), BF16)],
        compiler_params=_cparams("parallel", "parallel", "arbitrary"),
        name=name,
    )(x, sc, sh, w)


def _rmsmm_kernel(x_ref, nw_ref, w_ref, o_ref, xb_ref):
    @pl.when(pl.program_id(2) == 0)
    def _():
        x = x_ref[0]
        xb_ref[...] = (x * lax.rsqrt(jnp.mean(x * x, -1, keepdims=True) + 1e-6) * nw_ref[...]).astype(BF16)

    o_ref[0] = jnp.dot(xb_ref[...], w_ref[...], preferred_element_type=F32).astype(o_ref.dtype)


def _rms_matmul(x, col_block, k, norm_w, w, out_dtype, name):
    b, s, _ = x.shape
    n = w.shape[1]
    tm = _tile(s, 1024)
    tn = _tile(n, 1024)
    return pl.pallas_call(
        _rmsmm_kernel,
        out_shape=jax.ShapeDtypeStruct((b, s, n), out_dtype),
        grid=(b, s // tm, n // tn),
        in_specs=[pl.BlockSpec((1, tm, k), lambda bi, i, j: (bi, i, col_block)),
                  pl.BlockSpec((1, k), lambda bi, i, j: (0, 0)),
                  pl.BlockSpec((k, tn), lambda bi, i, j: (0, j))],
        out_specs=pl.BlockSpec((1, tm, tn), lambda bi, i, j: (bi, i, j)),
        scratch_shapes=[pltpu.VMEM((tm, k), BF16)],
        compiler_params=_cparams("parallel", "parallel", "arbitrary"),
        name=name,
    )(x, norm_w.reshape(1, k), w)


def _layer_norm_rows(r, g, b):
    mu = jnp.mean(r, -1, keepdims=True)
    var = jnp.mean(jnp.square(r - mu), -1, keepdims=True)
    return (r - mu) * lax.rsqrt(var + 1e-5) * g + b


def _outln_kernel(y_ref, w_ref, xres_ref, g_ref, lng_ref, lnb_ref, sc_ref, sh_ref, wr_ref, br_ref,
                  xo_ref, hin_ref, ids_ref, gates_ref, *, alpha, n_groups, per_group, parts):
    tp = y_ref.shape[1] // parts
    for part in range(parts):
        _outln_rows(slice(part * tp, (part + 1) * tp), y_ref, w_ref, xres_ref, g_ref, lng_ref, lnb_ref,
                    sc_ref, sh_ref, wr_ref, br_ref, xo_ref, hin_ref, ids_ref, gates_ref,
                    alpha=alpha, n_groups=n_groups, per_group=per_group)


def _outln_rows(rows, y_ref, w_ref, xres_ref, g_ref, lng_ref, lnb_ref, sc_ref, sh_ref, wr_ref, br_ref,
                xo_ref, hin_ref, ids_ref, gates_ref, *, alpha, n_groups, per_group):
    y = jnp.dot(y_ref[0, rows, :].astype(BF16), w_ref[...], preferred_element_type=F32)
    r = alpha * xres_ref[0, rows, :] + (1.0 + g_ref[0]) * y
    xn = _layer_norm_rows(r, lng_ref[...], lnb_ref[...])
    xo_ref[0, rows, :] = xn
    hin = xn * (1.0 + sc_ref[0]) + sh_ref[0]
    half = hin.shape[1] // 2
    hin_ref[0, rows, :] = _pack_bf16_pair(hin[:, :half], hin[:, half:])
    h_hi = hin.astype(BF16)
    h_lo = (hin - h_hi.astype(F32)).astype(BF16)
    ph = jnp.dot(h_hi, wr_ref[...], preferred_element_type=F32)
    plo = jnp.dot(h_lo, wr_ref[...], preferred_element_type=F32)
    logits = (ph[:, :LANES] + ph[:, LANES:]) + (plo[:, :LANES] + plo[:, LANES:]) + br_ref[...]
    lane = lax.broadcasted_iota(I32, logits.shape, 1)
    big = jnp.int32(4 * LANES)
    neg = jnp.float32(-jnp.inf)
    n_exp = n_groups * per_group
    gmask = lane < n_groups
    lg = jnp.where(gmask, logits, neg)
    mg = jnp.max(lg, -1, keepdims=True)
    grp = jnp.min(jnp.where(gmask & (lg == mg), lane, big), -1, keepdims=True)
    pg_sel = 1.0 / jnp.sum(jnp.where(gmask, jnp.exp(lg - mg), 0.0), -1, keepdims=True)
    lo = n_groups + grp * per_group
    emask = (lane >= lo) & (lane < lo + per_group) & (lane < n_groups + n_exp)
    le = jnp.where(emask, logits, neg)
    me = jnp.max(le, -1, keepdims=True)
    ee = jnp.where(emask, jnp.exp(le - me), 0.0)
    p = ee / jnp.sum(ee, -1, keepdims=True)
    pm = jnp.where(emask, p, -1.0)
    p1 = jnp.max(pm, -1, keepdims=True)
    i1 = jnp.min(jnp.where(emask & (pm == p1), lane, big), -1, keepdims=True)
    pm2 = jnp.where(lane == i1, -1.0, pm)
    p2 = jnp.max(pm2, -1, keepdims=True)
    i2 = jnp.min(jnp.where(emask & (lane != i1) & (pm2 == p2), lane, big), -1, keepdims=True)
    den = p1 + p2
    g1 = pg_sel * p1 / den
    g2 = pg_sel * p2 / den
    ids_ref[0, rows, :] = jnp.where(lane == 0, i1 - n_groups, jnp.where(lane == 1, i2 - n_groups, 0))
    gates_ref[0, rows, :] = jnp.where(lane == 0, g1, jnp.where(lane == 1, g2, 0.0))


def _out_ln_route(y, w, xres, gate, ln_g, ln_b, sc, sh, wr, br, alpha, n_groups, per_group, name):
    b, s, k = y.shape
    d = w.shape[1]
    tm = _tile(s, 512)
    parts = 2 if tm % (2 * SUBLANES) == 0 else 1
    kern = functools.partial(_outln_kernel, alpha=alpha, n_groups=n_groups, per_group=per_group, parts=parts)
    row = lambda bi, i: (bi, i, 0)
    per_b = lambda bi, i: (bi, 0, 0)
    const = lambda bi, i: (0, 0)
    return pl.pallas_call(
        kern,
        out_shape=(jax.ShapeDtypeStruct((b, s, d), F32), jax.ShapeDtypeStruct((b, s, d // 2), U32),
                   jax.ShapeDtypeStruct((b, s, LANES), I32), jax.ShapeDtypeStruct((b, s, LANES), F32)),
        grid=(b, s // tm),
        in_specs=[pl.BlockSpec((1, tm, k), row),
                  pl.BlockSpec((k, d), const),
                  pl.BlockSpec((1, tm, d), row),
                  pl.BlockSpec((1, 1, d), per_b),
                  pl.BlockSpec((1, d), const),
                  pl.BlockSpec((1, d), const),
                  pl.BlockSpec((1, 1, d), per_b),
                  pl.BlockSpec((1, 1, d), per_b),
                  pl.BlockSpec((d, 2 * LANES), const),
                  pl.BlockSpec((1, LANES), const)],
        out_specs=(pl.BlockSpec((1, tm, d), row), pl.BlockSpec((1, tm, d // 2), row),
                   pl.BlockSpec((1, tm, LANES), row), pl.BlockSpec((1, tm, LANES), row)),
        compiler_params=_cparams("parallel", "parallel"),
        name=name,
    )(y, w, xres, gate, ln_g.reshape(1, d), ln_b.reshape(1, d), sc, sh, wr, br)


def _dispatch_kernel(ids_ref, dest_ref, meta_ref, rank_ref, *, t, tile, blk_shift):
    lane = lax.broadcasted_iota(I32, (tile, LANES), 1)
    ri = lax.broadcasted_iota(I32, (tile, tile), 0)
    ci = lax.broadcasted_iota(I32, (tile, tile), 1)
    before = (ri > ci).astype(BF16)

    def hits(rows):
        ids = ids_ref[rows, :]
        return lane == ids[:, 0:1], lane == ids[:, 1:2]

    def count(ti, run):
        rows = pl.ds(pl.multiple_of(ti * tile, tile), tile)
        h1, h2 = hits(rows)
        onehot = jnp.logical_or(h1, h2).astype(BF16)
        prefix = jnp.dot(before, onehot, preferred_element_type=F32) + run
        r1 = jnp.sum(jnp.where(h1, prefix, 0.0), -1, keepdims=True)
        r2 = jnp.sum(jnp.where(h2, prefix, 0.0), -1, keepdims=True)
        rank_ref[rows, :] = jnp.where(lane == 0, r1, jnp.where(lane == 1, r2, 0.0))
        return run + jnp.sum(onehot.astype(F32), 0, keepdims=True)

    counts = lax.fori_loop(0, t // tile, count, jnp.zeros((1, LANES), F32))
    cnt = jnp.broadcast_to(counts, (SUBLANES, LANES)).astype(I32)
    nblk = (cnt + ((1 << blk_shift) - 1)) >> blk_shift
    padded = nblk << blk_shift
    lane8 = lax.broadcasted_iota(I32, (SUBLANES, LANES), 1)
    incl = padded
    step = 1
    while step < LANES:
        incl = incl + jnp.where(lane8 >= step, pltpu.roll(incl, step, 1), 0)
        step *= 2
    pad_start = incl - padded
    row8 = lax.broadcasted_iota(I32, (SUBLANES, LANES), 0)
    meta_ref[...] = jnp.where(row8 == 0, pad_start, jnp.where(row8 == 1, nblk, cnt))
    start_f = pad_start[0:1, :].astype(F32)

    def place(ti, c):
        rows = pl.ds(pl.multiple_of(ti * tile, tile), tile)
        h1, h2 = hits(rows)
        rk = rank_ref[rows, :]
        d1 = jnp.sum(jnp.where(h1, start_f, 0.0), -1, keepdims=True) + rk[:, 0:1]
        d2 = jnp.sum(jnp.where(h2, start_f, 0.0), -1, keepdims=True) + rk[:, 1:2]
        dest_ref[rows, :] = jnp.where(lane == 0, d1, jnp.where(lane == 1, d2, 0.0)).astype(I32)
        return c

    lax.fori_loop(0, t // tile, place, 0)


def _dispatch(ids):
    t = ids.shape[0]
    tile = _tile(t, 256)
    blk_shift = MOE_BLOCK.bit_length() - 1
    assert (1 << blk_shift) == MOE_BLOCK
    return pl.pallas_call(
        functools.partial(_dispatch_kernel, t=t, tile=tile, blk_shift=blk_shift),
        out_shape=(jax.ShapeDtypeStruct((t, LANES), I32), jax.ShapeDtypeStruct((SUBLANES, LANES), I32)),
        grid=(1,),
        in_specs=[pl.BlockSpec((t, LANES), lambda i: (0, 0))],
        out_specs=(pl.BlockSpec((t, LANES), lambda i: (0, 0)), pl.BlockSpec((SUBLANES, LANES), lambda i: (0, 0))),
        scratch_shapes=[pltpu.VMEM((t, LANES), F32)],
        compiler_params=_cparams("arbitrary"),
        name="moe_dispatch",
    )(ids)


def _scatter_rows_kernel(dest_ref, hin_ref, xs_in, xs_out, stage, sem, *, tm, n_steps):
    del xs_in
    step = pl.program_id(0)
    base = step * (tm * MOE_TOPK)

    def wait_slot(sl):
        for _ in range(MOE_TOPK):
            pltpu.make_async_copy(stage.at[sl], xs_out.at[pl.ds(0, tm), :], sem.at[sl]).wait()

    for sl in range(2):
        @pl.when(step % 2 == sl)
        def _(sl=sl):
            @pl.when(step >= 2)
            def _():
                wait_slot(sl)

            stage[sl] = hin_ref[...]
            for r in range(tm):
                for k in range(MOE_TOPK):
                    pltpu.make_async_copy(stage.at[sl, pl.ds(r, 1), :],
                                          xs_out.at[pl.ds(dest_ref[base + r * MOE_TOPK + k], 1), :],
                                          sem.at[sl]).start()

            @pl.when(step == n_steps - 1)
            def _():
                if n_steps >= 2:
                    wait_slot(1 - sl)
                wait_slot(sl)


def _scatter_rows(dest_flat, hin2, xs_init):
    t, d = hin2.shape
    n_rows = xs_init.shape[0]
    tm = _tile(t, 256)
    grid_spec = pltpu.PrefetchScalarGridSpec(
        num_scalar_prefetch=1,
        grid=(t // tm,),
        in_specs=[pl.BlockSpec((tm, d), lambda i, dst: (i, 0)),
                  pl.BlockSpec(memory_space=pl.ANY)],
        out_specs=pl.BlockSpec(memory_space=pl.ANY),
        scratch_shapes=[pltpu.VMEM((2, tm, d), hin2.dtype), pltpu.SemaphoreType.DMA((2,))],
    )
    return pl.pallas_call(
        functools.partial(_scatter_rows_kernel, tm=tm, n_steps=t // tm),
        out_shape=jax.ShapeDtypeStruct((n_rows, d), hin2.dtype),
        grid_spec=grid_spec,
        input_output_aliases={2: 0},
        compiler_params=_cparams("arbitrary"),
        name="moe_scatter_rows",
    )(dest_flat, hin2, xs_init)


def _ffn_kernel(eblk_ref, enb_ref, xs_hbm, wg_ref, wu_ref, wd_ref, ys_hbm,
                xbuf, ybuf, wgb, wub, wdb, isem, osem, *, blk, nb_total, n_exp):
    e = pl.program_id(0)
    n = enb_ref[e]
    b0 = eblk_ref[e]
    used = eblk_ref[n_exp - 1] + enb_ref[n_exp - 1]

    def in_cp(gb):
        sl = gb % FFN_RING
        return pltpu.make_async_copy(xs_hbm.at[pl.ds(gb * blk, blk), :], xbuf.at[sl], isem.at[sl])

    def out_cp(gb):
        sl = gb % FFN_RING
        return pltpu.make_async_copy(ybuf.at[sl], ys_hbm.at[pl.ds(gb * blk, blk), :], osem.at[sl])

    @pl.when(e == 0)
    def _():
        for k in range(FFN_RING - 1):
            @pl.when(k < used)
            def _(k=k):
                in_cp(k).start(priority=1)

    @pl.when(n > 0)
    def _():
        wgb[...] = wg_ref[0, 0].astype(BF16)
        wub[...] = wu_ref[0, 0].astype(BF16)
        wdb[...] = wd_ref[0, 0].astype(BF16)

        def body(j, c):
            gb = b0 + j
            sl = gb % FFN_RING

            @pl.when(gb + FFN_RING - 1 < used)
            def _():
                in_cp(gb + FFN_RING - 1).start(priority=1)

            in_cp(gb).wait()

            @pl.when(gb >= FFN_RING)
            def _():
                out_cp(gb - FFN_RING).wait()

            xb = jnp.concatenate(_unpack_bf16_pair(xbuf[sl]), axis=1)
            hg = jnp.dot(xb, wgb[...], preferred_element_type=F32)
            hu = jnp.dot(xb, wub[...], preferred_element_type=F32)
            hid = (_silu(hg) * hu).astype(BF16)
            ybuf[sl] = jnp.dot(hid, wdb[...], preferred_element_type=F32)
            out_cp(gb).start(priority=1)
            return c

        lax.fori_loop(0, n, body, 0)

    @pl.when(e == n_exp - 1)
    def _():
        for k in range(FFN_RING, 0, -1):
            @pl.when(used >= k)
            def _(k=k):
                out_cp(used - k).wait()

        ybuf[0] = jnp.zeros((blk, ybuf.shape[2]), F32)

        def fill(jb, c):
            cp = pltpu.make_async_copy(ybuf.at[0], ys_hbm.at[pl.ds(jb * blk, blk), :], osem.at[0])
            cp.start()
            cp.wait()
            return c

        lax.fori_loop(used, nb_total, fill, 0)


def _moe_ffn(eblk, enb, xs, w_gate, w_up, w_down, layer):
    n_rows = xs.shape[0]
    n_exp, d, ff = w_gate.shape[1], w_gate.shape[2], w_gate.shape[3]
    blk = MOE_BLOCK
    wmap = lambda e, eb, en: (layer, e, 0, 0)
    grid_spec = pltpu.PrefetchScalarGridSpec(
        num_scalar_prefetch=2,
        grid=(n_exp,),
        in_specs=[pl.BlockSpec(memory_space=pl.ANY),
                  pl.BlockSpec((1, 1, d, ff), wmap),
                  pl.BlockSpec((1, 1, d, ff), wmap),
                  pl.BlockSpec((1, 1, ff, d), wmap)],
        out_specs=pl.BlockSpec(memory_space=pl.ANY),
        scratch_shapes=[pltpu.VMEM((FFN_RING, blk, d // 2), U32),
                        pltpu.VMEM((FFN_RING, blk, d), F32),
                        pltpu.VMEM((d, ff), BF16),
                        pltpu.VMEM((d, ff), BF16),
                        pltpu.VMEM((ff, d), BF16),
                        pltpu.SemaphoreType.DMA((FFN_RING,)),
                        pltpu.SemaphoreType.DMA((FFN_RING,))],
    )
    return pl.pallas_call(
        functools.partial(_ffn_kernel, blk=blk, nb_total=n_rows // blk, n_exp=n_exp),
        out_shape=jax.ShapeDtypeStruct((n_rows, d), F32),
        grid_spec=grid_spec,
        compiler_params=_cparams("arbitrary"),
        name="moe_ffn",
    )(eblk, enb, xs, w_gate, w_up, w_down)


def _gather_ln_kernel(dest_ref, ys_hbm, rg_ref, x_ref, g_ref, lng_ref, lnb_ref, o_ref, ybuf, sem,
                      *, alpha, tm, nt, n_steps):
    step = pl.program_id(0) * nt + pl.program_id(1)
    slot = step % 2

    def start_gather(tile_idx, sl):
        base = tile_idx * (tm * MOE_TOPK)
        for r in range(tm):
            for k in range(MOE_TOPK):
                pltpu.make_async_copy(ys_hbm.at[pl.ds(dest_ref[base + r * MOE_TOPK + k], 1), :],
                                      ybuf.at[sl, k, pl.ds(r, 1), :], sem.at[sl]).start()

    @pl.when(step == 0)
    def _():
        start_gather(0, 0)

    for sl in range(2):
        @pl.when(jnp.logical_and(step + 1 < n_steps, slot == 1 - sl))
        def _(sl=sl):
            start_gather(step + 1, sl)

    for k in range(MOE_TOPK):
        pltpu.make_async_copy(ys_hbm.at[pl.ds(0, tm), :], ybuf.at[slot, k], sem.at[slot]).wait()

    rg = rg_ref[0]
    y = ybuf[slot, 0] * rg[:, 0:1] + ybuf[slot, 1] * rg[:, 1:2]
    r = alpha * x_ref[0] + (1.0 + g_ref[0]) * y
    o_ref[0] = _layer_norm_rows(r, lng_ref[...], lnb_ref[...])


def _gather_combine_ln(dest_flat, ys, route_gates, x, gate, ln_g, ln_b, alpha, name):
    b, s, d = x.shape
    tm = _tile(s, 256)
    nt = s // tm
    row = lambda bi, i, dst: (bi, i, 0)
    const = lambda bi, i, dst: (0, 0)
    grid_spec = pltpu.PrefetchScalarGridSpec(
        num_scalar_prefetch=1,
        grid=(b, nt),
        in_specs=[pl.BlockSpec(memory_space=pl.ANY),
                  pl.BlockSpec((1, tm, LANES), row),
                  pl.BlockSpec((1, tm, d), row),
                  pl.BlockSpec((1, 1, d), lambda bi, i, dst: (bi, 0, 0)),
                  pl.BlockSpec((1, d), const),
                  pl.BlockSpec((1, d), const)],
        out_specs=pl.BlockSpec((1, tm, d), row),
        scratch_shapes=[pltpu.VMEM((2, MOE_TOPK, tm, d), F32), pltpu.SemaphoreType.DMA((2,))],
    )
    return pl.pallas_call(
        functools.partial(_gather_ln_kernel, alpha=alpha, tm=tm, nt=nt, n_steps=b * nt),
        out_shape=jax.ShapeDtypeStruct((b, s, d), F32),
        grid_spec=grid_spec,
        compiler_params=_cparams("arbitrary", "arbitrary"),
        name=name,
    )(dest_flat, ys, route_gates, x, gate, ln_g.reshape(1, d), ln_b.reshape(1, d))


def _gdn_kernel(q_ref, k_ref, v_ref, z_ref, bg_ref, cg_ref, hh_ref, ba_ref,
                cw_ref, alog_ref, dtb_ref, nw_ref, scw_ref,
                o_ref,
                qa_ref, ka_ref, va_ref, hist_ref, hist2_ref, gate_ref, state_ref,
                *, heads, ts, gw):
    L = GDN_CHUNK
    dk = GDN_HEAD_DIM

    @pl.when(pl.program_id(1) == 0)
    def _():
        hist_ref[...] = jnp.zeros_like(hist_ref)
        hist2_ref[...] = jnp.zeros_like(hist2_ref)
        state_ref[...] = jnp.zeros_like(state_ref)

    for idx, (src, dst) in enumerate(((q_ref, qa_ref), (k_ref, ka_ref), (v_ref, va_ref))):
        x = src[0]
        y = _causal_conv(x, hist_ref[idx], cw_ref[:, idx * gw:(idx + 1) * gw], GDN_CONV)
        hist_ref[idx] = x[ts - SUBLANES:ts, :]
        dst[...] = _silu(y)

    c = cg_ref[0] * hh_ref[0]
    yb = _causal_conv(c, hist2_ref[...], scw_ref[...], SC_CONV)
    hist2_ref[...] = c[ts - SUBLANES:ts, :]
    o_ref[0, :, gw:2 * gw] = bg_ref[0] * yb

    ba = ba_ref[0]
    gate_ref[0] = _sigmoid(ba)
    gate_ref[1] = -jnp.exp(alog_ref[...]) * _softplus(ba + dtb_ref[...])

    ii = lax.broadcasted_iota(I32, (L, L), 0)
    jj = lax.broadcasted_iota(I32, (L, L), 1)
    causal = ii >= jj
    strict = ii > jj
    tri_incl = causal.astype(F32)
    eye = (ii == jj).astype(F32)
    nchunk = ts // L
    pairs = [(c, h) for c in range(nchunk) for h in range(heads)]

    def tiles(ref, lead=()):
        return jnp.stack([ref[lead + (slice(c * L, (c + 1) * L), slice(h * dk, (h + 1) * dk))]
                          for c, h in pairs], axis=0)

    def bmm(a, b):
        return jnp.einsum('bij,bjk->bik', a.astype(BF16), b.astype(BF16), preferred_element_type=F32)

    def bmm_nt(a, b):
        return jnp.einsum('bik,bjk->bij', a.astype(BF16), b.astype(BF16), preferred_element_type=F32)

    qh = tiles(qa_ref)
    kh = tiles(ka_ref)
    vh = tiles(va_ref)
    beta = jnp.stack([gate_ref[0, c * L:(c + 1) * L, h:h + 1] for c, h in pairs], axis=0)
    cums = [jnp.dot(tri_incl, gate_ref[1, c * L:(c + 1) * L, :], precision=HIGHEST,
                    preferred_element_type=F32) for c in range(nchunk)]
    cums_t = [cm.T for cm in cums]
    gc = jnp.stack([cums[c][:, heads + h:heads + h + 1] for c, h in pairs], axis=0)
    gl = gc[:, L - 1:L, :]
    egc = jnp.exp(gc)
    qn = qh * lax.rsqrt(jnp.sum(qh * qh, -1, keepdims=True) + 1e-6)
    kn = kh * lax.rsqrt(jnp.sum(kh * kh, -1, keepdims=True) + 1e-6)
    qc = qn * (dk ** -0.5)
    kb = kn * beta

    nb2 = len(pairs) // 2

    def pair(x):
        return x.reshape(nb2, 2 * L, x.shape[-1])

    i2 = lax.broadcasted_iota(I32, (2 * L, 2 * L), 0)
    j2 = lax.broadcasted_iota(I32, (2 * L, 2 * L), 1)
    same = (i2 < L) == (j2 < L)
    causal2 = same & (i2 >= j2)
    strict2 = same & (i2 > j2)
    eye2 = (i2 == j2).astype(F32)
    gc2, beta2, egc2, kn2, kb2 = pair(gc), pair(beta), pair(egc), pair(kn), pair(kb)
    gr2 = jnp.stack([jnp.concatenate([cums_t[c][heads + h:heads + h + 1, :],
                                      cums_t[c][heads + h + 1:heads + h + 2, :]], axis=1)
                     for c in range(nchunk) for h in range(0, heads, 2)], axis=0)
    decay2 = jnp.where(causal2, jnp.exp(jnp.where(causal2, gc2 - gr2, 0.0)), 0.0)
    m = jnp.where(strict2, bmm_nt(kb2, kn2) * decay2, 0.0)
    t_inv = eye2 - m
    mp = m
    for _ in range(5):
        mp = bmm(mp, mp)
        t_inv = t_inv + bmm(t_inv, mp)
    sol = bmm(t_inv, jnp.concatenate([pair(vh) * beta2, kb2 * egc2], axis=2)).reshape(len(pairs), L, 2 * dk)
    u = sol[:, :, 0:dk]
    w = sol[:, :, dk:2 * dk]
    qk2 = jnp.where(causal2, bmm_nt(pair(qc), kn2) * decay2, 0.0)
    qg = qc * egc
    k_dec = kn * jnp.exp(gl - gc)
    egl = jnp.exp(gl)

    state = [state_ref[h] for h in range(heads)]
    outs = []
    for p in range(nb2):
        ba, bb = 2 * p, 2 * p + 1
        ha, hb = pairs[ba][1], pairs[bb][1]
        v_a = u[ba] - _dot(w[ba], state[ha])
        v_b = u[bb] - _dot(w[bb], state[hb])
        inter = jnp.concatenate([_dot(qg[ba], state[ha]), _dot(qg[bb], state[hb])], axis=0)
        outs.append(inter + _dot(qk2[p], jnp.concatenate([v_a, v_b], axis=0)))
        state[ha] = state[ha] * egl[ba] + _dot(k_dec[ba].T, v_a)
        state[hb] = state[hb] * egl[bb] + _dot(k_dec[bb].T, v_b)
    for h in range(heads):
        state_ref[h] = state[h]

    o = jnp.stack(outs, axis=0).reshape(len(pairs), L, dk)
    on = o * lax.rsqrt(jnp.mean(o * o, -1, keepdims=True) + 1e-6) * nw_ref[...]
    y = on * _silu(tiles(z_ref, (0,)))
    for b, (c, h) in enumerate(pairs):
        o_ref[0, c * L:(c + 1) * L, h * dk:(h + 1) * dk] = y[b]


def _gdn_shortconv(proj, conv_w, a_log, dt_bias, norm_w, sc_w, heads):
    b, s, _ = proj.shape
    gw = heads * GDN_HEAD_DIM
    ts = _tile(s, 256)
    alog_p = jnp.zeros((1, LANES), F32).at[0, heads:2 * heads].set(a_log)
    dtb_p = jnp.zeros((1, LANES), F32).at[0, heads:2 * heads].set(dt_bias)
    col = lambda cb: (lambda bi, i: (bi, i, cb))
    const = lambda bi, i: (0, 0)
    kern = functools.partial(_gdn_kernel, heads=heads, ts=ts, gw=gw)
    return pl.pallas_call(
        kern,
        out_shape=jax.ShapeDtypeStruct((b, s, 2 * gw), F32),
        grid=(b, s // ts),
        in_specs=[pl.BlockSpec((1, ts, gw), col(cb)) for cb in range(7)]
        + [pl.BlockSpec((1, ts, LANES), col(7 * gw // LANES)),
           pl.BlockSpec((GDN_CONV, 3 * gw), const),
           pl.BlockSpec((1, LANES), const),
           pl.BlockSpec((1, LANES), const),
           pl.BlockSpec((1, GDN_HEAD_DIM), const),
           pl.BlockSpec((SC_CONV, gw), const)],
        out_specs=pl.BlockSpec((1, ts, 2 * gw), lambda bi, i: (bi, i, 0)),
        scratch_shapes=[pltpu.VMEM((ts, gw), F32),
                        pltpu.VMEM((ts, gw), F32),
                        pltpu.VMEM((ts, gw), F32),
                        pltpu.VMEM((3, SUBLANES, gw), F32),
                        pltpu.VMEM((SUBLANES, gw), F32),
                        pltpu.VMEM((2, ts, LANES), F32),
                        pltpu.VMEM((heads, GDN_HEAD_DIM, GDN_HEAD_DIM), F32)],
        compiler_params=_cparams("parallel", "arbitrary"),
        name="gdn_shortconv",
    )(proj, proj, proj, proj, proj, proj, proj, proj,
      conv_w, alog_p, dtb_p, norm_w.reshape(1, GDN_HEAD_DIM), sc_w)


def _rope_kernel(pos_ref, inv_ref, qpe_ref, kpe_ref, qo_ref, ko_ref, *, heads):
    ang = pos_ref[0].astype(F32) * inv_ref[...]
    lane = lax.broadcasted_iota(I32, ang.shape, 1)
    half = MLA_ROPE // 2
    cos = jnp.where(lane < MLA_ROPE, jnp.cos(ang), 0.0)
    sin = jnp.sin(ang)
    sgn = jnp.where(lane < half, -sin, jnp.where(lane < MLA_ROPE, sin, 0.0))

    def rot(x):
        swapped = jnp.where(lane < half, pltpu.roll(x, LANES - half, 1), pltpu.roll(x, half, 1))
        return x * cos + swapped * sgn

    for h in range(heads):
        cols = slice(h * LANES, (h + 1) * LANES)
        qo_ref[0, :, cols] = rot(qpe_ref[0, :, cols]).astype(BF16)
    ko_ref[0] = rot(kpe_ref[0]).astype(BF16)


def _rope(positions, qpe, heads, cproj, kpe_block):
    b, s = positions.shape
    ts = _tile(s, 512)
    half = MLA_ROPE // 2
    inv = ROPE_THETA ** (-jnp.arange(half, dtype=F32) * (2.0 / MLA_ROPE))
    inv_tab = jnp.zeros((1, LANES), F32).at[0, 0:half].set(inv).at[0, half:MLA_ROPE].set(inv)
    w = heads * LANES
    return pl.pallas_call(
        functools.partial(_rope_kernel, heads=heads),
        out_shape=(jax.ShapeDtypeStruct((b, s, w), BF16), jax.ShapeDtypeStruct((b, s, LANES), BF16)),
        grid=(b, s // ts),
        in_specs=[pl.BlockSpec((1, ts, 1), lambda bi, i: (bi, i, 0)),
                  pl.BlockSpec((1, LANES), lambda bi, i: (0, 0)),
                  pl.BlockSpec((1, ts, w), lambda bi, i: (bi, i, 0)),
                  pl.BlockSpec((1, ts, LANES), lambda bi, i: (bi, i, kpe_block))],
        out_specs=(pl.BlockSpec((1, ts, w), lambda bi, i: (bi, i, 0)),
                   pl.BlockSpec((1, ts, LANES), lambda bi, i: (bi, i, 0))),
        compiler_params=_cparams("parallel", "parallel"),
        name="mla_rope",
    )(positions.reshape(b, s, 1), inv_tab, qpe, cproj)


def _attn_kernel(qn_ref, qp_ref, kn_ref, kp_ref, v_ref, o_ref, kcat_ref, s_ref, *, tq, hq, nq, scale):
    qi = pl.program_id(2)

    @pl.when(qi == 0)
    def _():
        kcat_ref[:, 0:MLA_NOPE] = kn_ref[0]
        kcat_ref[:, MLA_NOPE:MLA_NOPE + LANES] = kp_ref[0]

    qcat = jnp.concatenate([qn_ref[0], qp_ref[0]], axis=1)
    tri = lax.broadcasted_iota(I32, (hq, hq), 0) >= lax.broadcasted_iota(I32, (hq, hq), 1)

    for qv in range(nq):
        @pl.when(qi == qv)
        def _(qv=qv):
            for h in range(tq // hq):
                rows = slice(h * hq, (h + 1) * hq)
                nk = qv * tq + (h + 1) * hq
                for c0 in range(0, nk, ATTN_KEY_CHUNK):
                    c1 = min(c0 + ATTN_KEY_CHUNK, nk)
                    s = lax.dot_general(qcat[rows], kcat_ref[c0:c1, :], (((1,), (1,)), ((), ())),
                                        preferred_element_type=F32) * scale
                    if c1 == nk:
                        w = c1 - c0
                        last = jnp.where(tri, s[:, w - hq:], -jnp.inf)
                        s = last if w == hq else jnp.concatenate([s[:, :w - hq], last], axis=1)
                    s_ref[rows, c0:c1] = s
                    mb = jnp.max(s, -1, keepdims=True)
                    m = mb if c0 == 0 else jnp.maximum(m, mb)
                l = jnp.zeros((hq, 1), F32)
                acc = jnp.zeros((hq, MLA_V), F32)
                for c0 in range(0, nk, hq):
                    p = jnp.exp(s_ref[rows, c0:c0 + hq] - m)
                    l = l + jnp.sum(p, -1, keepdims=True)
                    acc = acc + jnp.dot(p.astype(BF16), v_ref[0, c0:c0 + hq, :], preferred_element_type=F32)
                o_ref[0, rows, :] = acc / l


def _attention(qn, qpe, kvup, kpe, heads):
    b, s, _ = qn.shape
    tq = _tile(s, 1024)
    hq = _tile(tq, 256)
    nq = s // tq
    scale = (MLA_NOPE + MLA_ROPE) ** -0.5
    return pl.pallas_call(
        functools.partial(_attn_kernel, tq=tq, hq=hq, nq=nq, scale=scale),
        out_shape=jax.ShapeDtypeStruct((b, s, heads * MLA_V), F32),
        grid=(b, heads, nq),
        in_specs=[pl.BlockSpec((1, tq, MLA_NOPE), lambda bi, h, i: (bi, i, h)),
                  pl.BlockSpec((1, tq, LANES), lambda bi, h, i: (bi, i, h)),
                  pl.BlockSpec((1, s, MLA_NOPE), lambda bi, h, i: (bi, 0, h)),
                  pl.BlockSpec((1, s, LANES), lambda bi, h, i: (bi, 0, 0)),
                  pl.BlockSpec((1, s, MLA_V), lambda bi, h, i: (bi, 0, heads + h))],
        out_specs=pl.BlockSpec((1, tq, MLA_V), lambda bi, h, i: (bi, i, h)),
        scratch_shapes=[pltpu.VMEM((s, MLA_NOPE + LANES), BF16), pltpu.VMEM((tq, s), F32)],
        compiler_params=_cparams("parallel", "parallel", "arbitrary"),
        name="mla_attention",
    )(qn, qpe, kvup, kpe, kvup)


def _hyb_in_weight(w_in, heads):
    d = w_in.shape[0]
    gw = heads * GDN_HEAD_DIM
    main = jnp.concatenate([w_in[:, :4 * gw], w_in[:, 4 * gw + 2 * heads:]], axis=1)
    gates = w_in[:, 4 * gw:4 * gw + 2 * heads]
    n = main.shape[1] + LANES
    n_pad = -(-n // 768) * 768
    tail = jnp.zeros((d, n_pad - main.shape[1]), w_in.dtype).at[:, :2 * heads].set(gates)
    return jnp.concatenate([main, tail], axis=1).astype(BF16)


def _mla_in_weight(w_in):
    d = w_in.shape[0]
    pad = jnp.zeros((d, LANES - MLA_ROPE), w_in.dtype)
    return jnp.concatenate([w_in, pad], axis=1).astype(BF16)


def _mla_uq_weights(w_uq, heads):
    r = w_uq.shape[0]
    w = w_uq.reshape(r, heads, MLA_NOPE + MLA_ROPE)
    nope = w[:, :, :MLA_NOPE].reshape(r, heads * MLA_NOPE)
    pe = jnp.concatenate([w[:, :, MLA_NOPE:], jnp.zeros((r, heads, LANES - MLA_ROPE), w.dtype)], axis=2)
    return nope.astype(BF16), pe.reshape(r, heads * LANES).astype(BF16)


def _mla_ukv_weight(w_ukv, heads):
    r = w_ukv.shape[0]
    w = w_ukv.reshape(r, heads, MLA_NOPE + MLA_V)
    return jnp.concatenate([w[:, :, :MLA_NOPE].reshape(r, heads * MLA_NOPE),
                            w[:, :, MLA_NOPE:].reshape(r, heads * MLA_V)], axis=1).astype(BF16)


def _router_weight(wr_g, br_g, wr_e, br_e):
    d = wr_g.shape[0]
    n = wr_g.shape[1] + wr_e.shape[1]
    w = jnp.zeros((d, LANES), F32).at[:, :n].set(jnp.concatenate([wr_g, wr_e], axis=1))
    bias = jnp.zeros((1, LANES), F32).at[0, :n].set(jnp.concatenate([br_g, br_e]))
    w_hi = w.astype(BF16)
    w_lo = (w - w_hi.astype(F32)).astype(BF16)
    return jnp.concatenate([w_hi, w_lo], axis=1), bias


def kernel(x, c, positions, ada_w, ada_b, ln_g, ln_b, hyb_w_in, gdn_conv_w, gdn_a_log, gdn_dt_bias,
           gdn_norm_w, sc_conv_w, hyb_w_out, mla_w_in, mla_q_norm, mla_kv_norm, mla_w_uq, mla_w_ukv,
           mla_w_out, moe_router_g, moe_bias_g, moe_router_e, moe_bias_e, moe_w_gate, moe_w_up,
           moe_w_down):
    b, s, d = x.shape
    depth = ada_w.shape[0]
    alpha = (2.0 * depth) ** 0.25
    gdn_heads = d // (2 * GDN_HEAD_DIM)
    mla_heads = mla_w_out.shape[1] // MLA_V
    n_groups = moe_router_g.shape[2]
    n_exp = moe_router_e.shape[2]
    per_group = n_exp // n_groups

    mod = _modulation(c, ada_w, ada_b)
    t = b * s
    xs = jnp.zeros((t * MOE_TOPK + n_exp * MOE_BLOCK, d // 2), U32)
    for layer in range(depth):
        sh1, sc1, g1, sh2, sc2, g2 = [mod[layer, :, j * d:(j + 1) * d].reshape(b, 1, d) for j in range(6)]
        i = layer // 2
        wr, br = _router_weight(moe_router_g[layer], moe_bias_g[layer], moe_router_e[layer], moe_bias_e[layer])
        if layer % 2 == 0:
            proj = _mod_matmul(x, sc1, sh1, _hyb_in_weight(hyb_w_in[i], gdn_heads), "hyb_in_proj")
            y = _gdn_shortconv(proj, gdn_conv_w[i], gdn_a_log[i], gdn_dt_bias[i], gdn_norm_w[i],
                               sc_conv_w[i], gdn_heads)
            w_out = hyb_w_out[i]
        else:
            cproj = _mod_matmul(x, sc1, sh1, _mla_in_weight(mla_w_in[i]), "mla_in_proj")
            w_qn, w_qpe = _mla_uq_weights(mla_w_uq[i], mla_heads)
            qn = _rms_matmul(cproj, 0, MLA_Q_RANK, mla_q_norm[i], w_qn, BF16, "mla_q_nope_up")
            qpe_raw = _rms_matmul(cproj, 0, MLA_Q_RANK, mla_q_norm[i], w_qpe, F32, "mla_q_rope_up")
            kvup = _rms_matmul(cproj, 1, MLA_KV_RANK, mla_kv_norm[i], _mla_ukv_weight(mla_w_ukv[i], mla_heads),
                               BF16, "mla_kv_up")
            qpe, kpe = _rope(positions, qpe_raw, mla_heads, cproj, (MLA_Q_RANK + MLA_KV_RANK) // LANES)
            y = _attention(qn, qpe, kvup, kpe, mla_heads)
            w_out = mla_w_out[i]
        x, hin2, ids, gates = _out_ln_route(y, w_out.astype(BF16), x, g1, ln_g[layer, 0], ln_b[layer, 0],
                                            sc2, sh2, wr, br, alpha, n_groups, per_group,
                                            "mixer_out_ln_route")
        dest, meta = _dispatch(ids.reshape(t, LANES))
        dest_flat = dest[:, :MOE_TOPK].reshape(-1)
        eblk = meta[0, :n_exp] // MOE_BLOCK
        enb = meta[1, :n_exp]
        xs = _scatter_rows(dest_flat, hin2.reshape(t, d // 2), xs)
        ys = _moe_ffn(eblk, enb, xs, moe_w_gate, moe_w_up, moe_w_down, layer)
        x = _gather_combine_ln(dest_flat, ys, gates, x, g2, ln_g[layer, 1], ln_b[layer, 1], alpha,
                               "moe_gather_combine_ln")
    return x
```

```python
import functools

import jax
import jax.numpy as jnp
from jax import lax
from jax.experimental import pallas as pl
from jax.experimental.pallas import tpu as pltpu

F32, BF16, I32, U32 = jnp.float32, jnp.bfloat16, jnp.int32, jnp.uint32
HIGHEST = lax.Precision.HIGHEST

LANES = 128
SUBLANES = 8
VMEM_LIMIT_BYTES = 56 * 1024 * 1024

GDN_HEAD_DIM = 128
GDN_CONV = 4
GDN_CHUNK = 64
SC_CONV = 3
MLA_NOPE = 128
MLA_ROPE = 64
MLA_V = 128
MLA_Q_RANK = 512
MLA_KV_RANK = 512
ROPE_THETA = 10000.0
MOE_TOPK = 2
MOE_BLOCK = 128
FFN_RING = 6
ATTN_KEY_CHUNK = 512


def _tile(n, pref):
    if n <= pref:
        return n
    for t in range(pref, 0, -LANES):
        if n % t == 0:
            return t
    return n


def _cparams(*sem):
    return pltpu.CompilerParams(dimension_semantics=sem, vmem_limit_bytes=VMEM_LIMIT_BYTES)


def _sigmoid(x):
    return 1.0 / (1.0 + jnp.exp(-x))


def _silu(x):
    return x * _sigmoid(x)


def _softplus(x):
    return jnp.maximum(x, 0.0) + jnp.log(1.0 + jnp.exp(-jnp.abs(x)))


def _pack_bf16_pair(lo, hi):
    lo_bits = lax.bitcast_convert_type(lo.astype(BF16).astype(F32), U32)
    hi_bits = lax.bitcast_convert_type(hi.astype(BF16).astype(F32), U32)
    return (lo_bits >> 16) | (hi_bits & jnp.uint32(0xFFFF0000))


def _unpack_bf16_pair(w):
    lo = lax.bitcast_convert_type(w << 16, F32)
    hi = lax.bitcast_convert_type(w & jnp.uint32(0xFFFF0000), F32)
    return lo.astype(BF16), hi.astype(BF16)


def _dot(a, b):
    return jnp.dot(a.astype(BF16), b.astype(BF16), preferred_element_type=F32)


def _dot_nt(a, b):
    return lax.dot_general(a.astype(BF16), b.astype(BF16), (((1,), (1,)), ((), ())),
                           preferred_element_type=F32)


def _causal_conv(x, hist, w, taps):
    rows, width = x.shape
    g = rows // SUBLANES
    x3 = x.reshape(g, SUBLANES, width)
    sub = lax.broadcasted_iota(I32, (1, SUBLANES, width), 1)
    y = None
    for j in range(taps - 1):
        shift = taps - 1 - j
        rot = pltpu.roll(x3, shift, 1)
        prev = jnp.concatenate([pltpu.roll(hist, shift, 0)[None], rot[:g - 1]], axis=0)
        term = w[j:j + 1] * jnp.where(sub < shift, prev, rot)
        y = term if y is None else y + term
    return (y + w[taps - 1:taps] * x3).reshape(rows, width)


def _mod_kernel(c_ref, w_ref, b_ref, o_ref):
    c = c_ref[...]
    o_ref[0] = _dot(_silu(c), w_ref[0]) + b_ref[0]


def _modulation(c, ada_w, ada_b):
    depth, d, n = ada_w.shape
    b = c.shape[0]
    rows = -(-b // SUBLANES) * SUBLANES
    cp = jnp.zeros((rows, d), F32).at[:b].set(c)
    tn = _tile(n, 1024)
    out = pl.pallas_call(
        _mod_kernel,
        out_shape=jax.ShapeDtypeStruct((depth, rows, n), F32),
        grid=(depth, n // tn),
        in_specs=[pl.BlockSpec((rows, d), lambda l, j: (0, 0)),
                  pl.BlockSpec((1, d, tn), lambda l, j: (l, 0, j)),
                  pl.BlockSpec((1, 1, tn), lambda l, j: (l, 0, j))],
        out_specs=pl.BlockSpec((1, rows, tn), lambda l, j: (l, 0, j)),
        compiler_params=_cparams("parallel", "parallel"),
        name="adaln_mod",
    )(cp, ada_w, ada_b.reshape(depth, 1, n))
    return out[:, :b]


def _modmm_kernel(x_ref, sc_ref, sh_ref, w_ref, o_ref, xb_ref):
    @pl.when(pl.program_id(2) == 0)
    def _():
        xb_ref[...] = (x_ref[0] * (1.0 + sc_ref[0]) + sh_ref[0]).astype(BF16)

    o_ref[0] = jnp.dot(xb_ref[...], w_ref[...], preferred_element_type=F32)


def _mod_matmul(x, sc, sh, w, name):
    b, s, d = x.shape
    n = w.shape[1]
    tm = _tile(s, 1024)
    tn = _tile(n, 768)
    return pl.pallas_call(
        _modmm_kernel,
        out_shape=jax.ShapeDtypeStruct((b, s, n), F32),
        grid=(b, s // tm, n // tn),
        in_specs=[pl.BlockSpec((1, tm, d), lambda bi, i, j: (bi, i, 0)),
                  pl.BlockSpec((1, 1, d), lambda bi, i, j: (bi, 0, 0)),
                  pl.BlockSpec((1, 1, d), lambda bi, i, j: (bi, 0, 0)),
                  pl.BlockSpec((d, tn), lambda bi, i, j: (0, j))],
        out_specs=pl.BlockSpec((1, tm, tn), lambda bi, i, j: (bi, i, j)),
        scratch_shapes=[pltpu.VMEM((tm, d), BF16)],
        compiler_params=_cparams("parallel", "parallel", "arbitrary"),
        name=name,
    )(x, sc, sh, w)


def _rmsmm_kernel(x_ref, nw_ref, w_ref, o_ref, xb_ref):
    @pl.when(pl.program_id(2) == 0)
    def _():
        x = x_ref[0]
        xb_ref[...] = (x * lax.rsqrt(jnp.mean(x * x, -1, keepdims=True) + 1e-6) * nw_ref[...]).astype(BF16)

    o_ref[0] = jnp.dot(xb_ref[...], w_ref[...], preferred_element_type=F32).astype(o_ref.dtype)


def _rms_matmul(x, col_block, k, norm_w, w, out_dtype, name):
    b, s, _ = x.shape
    n = w.shape[1]
    tm = _tile(s, 1024)
    tn = _tile(n, 1024)
    return pl.pallas_call(
        _rmsmm_kernel,
        out_shape=jax.ShapeDtypeStruct((b, s, n), out_dtype),
        grid=(b, s // tm, n // tn),
        in_specs=[pl.BlockSpec((1, tm, k), lambda bi, i, j: (bi, i, col_block)),
                  pl.BlockSpec((1, k), lambda bi, i, j: (0, 0)),
                  pl.BlockSpec((k, tn), lambda bi, i, j: (0, j))],
        out_specs=pl.BlockSpec((1, tm, tn), lambda bi, i, j: (bi, i, j)),
        scratch_shapes=[pltpu.VMEM((tm, k), BF16)],
        compiler_params=_cparams("parallel", "parallel", "arbitrary"),
        name=name,
    )(x, norm_w.reshape(1, k), w)


def _layer_norm_rows(r, g, b):
    mu = jnp.mean(r, -1, keepdims=True)
    var = jnp.mean(jnp.square(r - mu), -1, keepdims=True)
    return (r - mu) * lax.rsqrt(var + 1e-5) * g + b


def _outln_kernel(y_ref, w_ref, xres_ref, g_ref, lng_ref, lnb_ref, sc_ref, sh_ref, wr_ref, br_ref,
                  xo_ref, hin_ref, ids_ref, gates_ref, *, alpha, n_groups, per_group, parts):
    tp = y_ref.shape[1] // parts
    for part in range(parts):
        _outln_rows(slice(part * tp, (part + 1) * tp), y_ref, w_ref, xres_ref, g_ref, lng_ref, lnb_ref,
                    sc_ref, sh_ref, wr_ref, br_ref, xo_ref, hin_ref, ids_ref, gates_ref,
                    alpha=alpha, n_groups=n_groups, per_group=per_group)


def _outln_rows(rows, y_ref, w_ref, xres_ref, g_ref, lng_ref, lnb_ref, sc_ref, sh_ref, wr_ref, br_ref,
                xo_ref, hin_ref, ids_ref, gates_ref, *, alpha, n_groups, per_group):
    y = jnp.dot(y_ref[0, rows, :].astype(BF16), w_ref[...], preferred_element_type=F32)
    r = alpha * xres_ref[0, rows, :] + (1.0 + g_ref[0]) * y
    xn = _layer_norm_rows(r, lng_ref[...], lnb_ref[...])
    xo_ref[0, rows, :] = xn
    hin = xn * (1.0 + sc_ref[0]) + sh_ref[0]
    half = hin.shape[1] // 2
    hin_ref[0, rows, :] = _pack_bf16_pair(hin[:, :half], hin[:, half:])
    h_hi = hin.astype(BF16)
    h_lo = (hin - h_hi.astype(F32)).astype(BF16)
    ph = jnp.dot(h_hi, wr_ref[...], preferred_element_type=F32)
    plo = jnp.dot(h_lo, wr_ref[...], preferred_element_type=F32)
    logits = (ph[:, :LANES] + ph[:, LANES:]) + (plo[:, :LANES] + plo[:, LANES:]) + br_ref[...]
    lane = lax.broadcasted_iota(I32, logits.shape, 1)
    big = jnp.int32(4 * LANES)
    neg = jnp.float32(-jnp.inf)
    n_exp = n_groups * per_group
    gmask = lane < n_groups
    lg = jnp.where(gmask, logits, neg)
    mg = jnp.max(lg, -1, keepdims=True)
    grp = jnp.min(jnp.where(gmask & (lg == mg), lane, big), -1, keepdims=True)
    pg_sel = 1.0 / jnp.sum(jnp.where(gmask, jnp.exp(lg - mg), 0.0), -1, keepdims=True)
    lo = n_groups + grp * per_group
    emask = (lane >= lo) & (lane < lo + per_group) & (lane < n_groups + n_exp)
    le = jnp.where(emask, logits, neg)
    me = jnp.max(le, -1, keepdims=True)
    ee = jnp.where(emask, jnp.exp(le - me), 0.0)
    p = ee / jnp.sum(ee, -1, keepdims=True)
    pm = jnp.where(emask, p, -1.0)
    p1 = jnp.max(pm, -1, keepdims=True)
    i1 = jnp.min(jnp.where(emask & (pm == p1), lane, big), -1, keepdims=True)
    pm2 = jnp.where(lane == i1, -1.0, pm)
    p2 = jnp.max(pm2, -1, keepdims=True)
    i2 = jnp.min(jnp.where(emask & (lane != i1) & (pm2 == p2), lane, big), -1, keepdims=True)
    den = p1 + p2
    g1 = pg_sel * p1 / den
    g2 = pg_sel * p2 / den
    ids_ref[0, rows, :] = jnp.where(lane == 0, i1 - n_groups, jnp.where(lane == 1, i2 - n_groups, 0))
    gates_ref[0, rows, :] = jnp.where(lane == 0, g1, jnp.where(lane == 1, g2, 0.0))


def _out_ln_route(y, w, xres, gate, ln_g, ln_b, sc, sh, wr, br, alpha, n_groups, per_group, name):
    b, s, k = y.shape
    d = w.shape[1]
    tm = _tile(s, 512)
    parts = 2 if tm % (2 * SUBLANES) == 0 else 1
    kern = functools.partial(_outln_kernel, alpha=alpha, n_groups=n_groups, per_group=per_group, parts=parts)
    row = lambda bi, i: (bi, i, 0)
    per_b = lambda bi, i: (bi, 0, 0)
    const = lambda bi, i: (0, 0)
    return pl.pallas_call(
        kern,
        out_shape=(jax.ShapeDtypeStruct((b, s, d), F32), jax.ShapeDtypeStruct((b, s, d // 2), U32),
                   jax.ShapeDtypeStruct((b, s, LANES), I32), jax.ShapeDtypeStruct((b, s, LANES), F32)),
        grid=(b, s // tm),
        in_specs=[pl.BlockSpec((1, tm, k), row),
                  pl.BlockSpec((k, d), const),
                  pl.BlockSpec((1, tm, d), row),
                  pl.BlockSpec((1, 1, d), per_b),
                  pl.BlockSpec((1, d), const),
                  pl.BlockSpec((1, d), const),
                  pl.BlockSpec((1, 1, d), per_b),
                  pl.BlockSpec((1, 1, d), per_b),
                  pl.BlockSpec((d, 2 * LANES), const),
                  pl.BlockSpec((1, LANES), const)],
        out_specs=(pl.BlockSpec((1, tm, d), row), pl.BlockSpec((1, tm, d // 2), row),
                   pl.BlockSpec((1, tm, LANES), row), pl.BlockSpec((1, tm, LANES), row)),
        compiler_params=_cparams("parallel", "parallel"),
        name=name,
    )(y, w, xres, gate, ln_g.reshape(1, d), ln_b.reshape(1, d), sc, sh, wr, br)


def _dispatch_kernel(ids_ref, dest_ref, meta_ref, rank_ref, *, t, tile, blk_shift):
    lane = lax.broadcasted_iota(I32, (tile, LANES), 1)
    ri = lax.broadcasted_iota(I32, (tile, tile), 0)
    ci = lax.broadcasted_iota(I32, (tile, tile), 1)
    before = (ri > ci).astype(BF16)

    def hits(rows):
        ids = ids_ref[rows, :]
        return lane == ids[:, 0:1], lane == ids[:, 1:2]

    def count(ti, run):
        rows = pl.ds(pl.multiple_of(ti * tile, tile), tile)
        h1, h2 = hits(rows)
        onehot = jnp.logical_or(h1, h2).astype(BF16)
        prefix = jnp.dot(before, onehot, preferred_element_type=F32) + run
        r1 = jnp.sum(jnp.where(h1, prefix, 0.0), -1, keepdims=True)
        r2 = jnp.sum(jnp.where(h2, prefix, 0.0), -1, keepdims=True)
        rank_ref[rows, :] = jnp.where(lane == 0, r1, jnp.where(lane == 1, r2, 0.0))
        return run + jnp.sum(onehot.astype(F32), 0, keepdims=True)

    counts = lax.fori_loop(0, t // tile, count, jnp.zeros((1, LANES), F32))
    cnt = jnp.broadcast_to(counts, (SUBLANES, LANES)).astype(I32)
    nblk = (cnt + ((1 << blk_shift) - 1)) >> blk_shift
    padded = nblk << blk_shift
    lane8 = lax.broadcasted_iota(I32, (SUBLANES, LANES), 1)
    incl = padded
    step = 1
    while step < LANES:
        incl = incl + jnp.where(lane8 >= step, pltpu.roll(incl, step, 1), 0)
        step *= 2
    pad_start = incl - padded
    row8 = lax.broadcasted_iota(I32, (SUBLANES, LANES), 0)
    meta_ref[...] = jnp.where(row8 == 0, pad_start, jnp.where(row8 == 1, nblk, cnt))
    start_f = pad_start[0:1, :].astype(F32)

    def place(ti, c):
        rows = pl.ds(pl.multiple_of(ti * tile, tile), tile)
        h1, h2 = hits(rows)
        rk = rank_ref[rows, :]
        d1 = jnp.sum(jnp.where(h1, start_f, 0.0), -1, keepdims=True) + rk[:, 0:1]
        d2 = jnp.sum(jnp.where(h2, start_f, 0.0), -1, keepdims=True) + rk[:, 1:2]
        dest_ref[rows, :] = jnp.where(lane == 0, d1, jnp.where(lane == 1, d2, 0.0)).astype(I32)
        return c

    lax.fori_loop(0, t // tile, place, 0)


def _dispatch(ids):
    t = ids.shape[0]
    tile = _tile(t, 256)
    blk_shift = MOE_BLOCK.bit_length() - 1
    assert (1 << blk_shift) == MOE_BLOCK
    return pl.pallas_call(
        functools.partial(_dispatch_kernel, t=t, tile=tile, blk_shift=blk_shift),
        out_shape=(jax.ShapeDtypeStruct((t, LANES), I32), jax.ShapeDtypeStruct((SUBLANES, LANES), I32)),
        grid=(1,),
        in_specs=[pl.BlockSpec((t, LANES), lambda i: (0, 0))],
        out_specs=(pl.BlockSpec((t, LANES), lambda i: (0, 0)), pl.BlockSpec((SUBLANES, LANES), lambda i: (0, 0))),
        scratch_shapes=[pltpu.VMEM((t, LANES), F32)],
        compiler_params=_cparams("arbitrary"),
        name="moe_dispatch",
    )(ids)


def _scatter_rows_kernel(dest_ref, hin_ref, xs_in, xs_out, stage, sem, *, tm, n_steps):
    del xs_in
    step = pl.program_id(0)
    base = step * (tm * MOE_TOPK)

    def wait_slot(sl):
        for _ in range(MOE_TOPK):
            pltpu.make_async_copy(stage.at[sl], xs_out.at[pl.ds(0, tm), :], sem.at[sl]).wait()

    for sl in range(2):
        @pl.when(step % 2 == sl)
        def _(sl=sl):
            @pl.when(step >= 2)
            def _():
                wait_slot(sl)

            stage[sl] = hin_ref[...]
            for r in range(tm):
                for k in range(MOE_TOPK):
                    pltpu.make_async_copy(stage.at[sl, pl.ds(r, 1), :],
                                          xs_out.at[pl.ds(dest_ref[base + r * MOE_TOPK + k], 1), :],
                                          sem.at[sl]).start()

            @pl.when(step == n_steps - 1)
            def _():
                if n_steps >= 2:
                    wait_slot(1 - sl)
                wait_slot(sl)


def _scatter_rows(dest_flat, hin2, xs_init):
    t, d = hin2.shape
    n_rows = xs_init.shape[0]
    tm = _tile(t, 256)
    grid_spec = pltpu.PrefetchScalarGridSpec(
        num_scalar_prefetch=1,
        grid=(t // tm,),
        in_specs=[pl.BlockSpec((tm, d), lambda i, dst: (i, 0)),
                  pl.BlockSpec(memory_space=pl.ANY)],
        out_specs=pl.BlockSpec(memory_space=pl.ANY),
        scratch_shapes=[pltpu.VMEM((2, tm, d), hin2.dtype), pltpu.SemaphoreType.DMA((2,))],
    )
    return pl.pallas_call(
        functools.partial(_scatter_rows_kernel, tm=tm, n_steps=t // tm),
        out_shape=jax.ShapeDtypeStruct((n_rows, d), hin2.dtype),
        grid_spec=grid_spec,
        input_output_aliases={2: 0},
        compiler_params=_cparams("arbitrary"),
        name="moe_scatter_rows",
    )(dest_flat, hin2, xs_init)


def _ffn_kernel(eblk_ref, enb_ref, xs_hbm, wg_ref, wu_ref, wd_ref, ys_hbm,
                xbuf, ybuf, wgb, wub, wdb, isem, osem, *, blk, nb_total, n_exp):
    e = pl.program_id(0)
    n = enb_ref[e]
    b0 = eblk_ref[e]
    used = eblk_ref[n_exp - 1] + enb_ref[n_exp - 1]

    def in_cp(gb):
        sl = gb % FFN_RING
        return pltpu.make_async_copy(xs_hbm.at[pl.ds(gb * blk, blk), :], xbuf.at[sl], isem.at[sl])

    def out_cp(gb):
        sl = gb % FFN_RING
        return pltpu.make_async_copy(ybuf.at[sl], ys_hbm.at[pl.ds(gb * blk, blk), :], osem.at[sl])

    @pl.when(e == 0)
    def _():
        for k in range(FFN_RING - 1):
            @pl.when(k < used)
            def _(k=k):
                in_cp(k).start(priority=1)

    @pl.when(n > 0)
    def _():
        wgb[...] = wg_ref[0, 0].astype(BF16)
        wub[...] = wu_ref[0, 0].astype(BF16)
        wdb[...] = wd_ref[0, 0].astype(BF16)

        def body(j, c):
            gb = b0 + j
            sl = gb % FFN_RING

            @pl.when(gb + FFN_RING - 1 < used)
            def _():
                in_cp(gb + FFN_RING - 1).start(priority=1)

            in_cp(gb).wait()

            @pl.when(gb >= FFN_RING)
            def _():
                out_cp(gb - FFN_RING).wait()

            xb = jnp.concatenate(_unpack_bf16_pair(xbuf[sl]), axis=1)
            hg = jnp.dot(xb, wgb[...], preferred_element_type=F32)
            hu = jnp.dot(xb, wub[...], preferred_element_type=F32)
            hid = (_silu(hg) * hu).astype(BF16)
            ybuf[sl] = jnp.dot(hid, wdb[...], preferred_element_type=F32)
            out_cp(gb).start(priority=1)
            return c

        lax.fori_loop(0, n, body, 0)

    @pl.when(e == n_exp - 1)
    def _():
        for k in range(FFN_RING, 0, -1):
            @pl.when(used >= k)
            def _(k=k):
                out_cp(used - k).wait()

        ybuf[0] = jnp.zeros((blk, ybuf.shape[2]), F32)

        def fill(jb, c):
            cp = pltpu.make_async_copy(ybuf.at[0], ys_hbm.at[pl.ds(jb * blk, blk), :], osem.at[0])
            cp.start()
            cp.wait()
            return c

        lax.fori_loop(used, nb_total, fill, 0)


def _moe_ffn(eblk, enb, xs, w_gate, w_up, w_down, layer):
    n_rows = xs.shape[0]
    n_exp, d, ff = w_gate.shape[1], w_gate.shape[2], w_gate.shape[3]
    blk = MOE_BLOCK
    wmap = lambda e, eb, en: (layer, e, 0, 0)
    grid_spec = pltpu.PrefetchScalarGridSpec(
        num_scalar_prefetch=2,
        grid=(n_exp,),
        in_specs=[pl.BlockSpec(memory_space=pl.ANY),
                  pl.BlockSpec((1, 1, d, ff), wmap),
                  pl.BlockSpec((1, 1, d, ff), wmap),
                  pl.BlockSpec((1, 1, ff, d), wmap)],
        out_specs=pl.BlockSpec(memory_space=pl.ANY),
        scratch_shapes=[pltpu.VMEM((FFN_RING, blk, d // 2), U32),
                        pltpu.VMEM((FFN_RING, blk, d), F32),
                        pltpu.VMEM((d, ff), BF16),
                        pltpu.VMEM((d, ff), BF16),
                        pltpu.VMEM((ff, d), BF16),
                        pltpu.SemaphoreType.DMA((FFN_RING,)),
                        pltpu.SemaphoreType.DMA((FFN_RING,))],
    )
    return pl.pallas_call(
        functools.partial(_ffn_kernel, blk=blk, nb_total=n_rows // blk, n_exp=n_exp),
        out_shape=jax.ShapeDtypeStruct((n_rows, d), F32),
        grid_spec=grid_spec,
        compiler_params=_cparams("arbitrary"),
        name="moe_ffn",
    )(eblk, enb, xs, w_gate, w_up, w_down)


def _gather_ln_kernel(dest_ref, ys_hbm, rg_ref, x_ref, g_ref, lng_ref, lnb_ref, o_ref, ybuf, sem,
                      *, alpha, tm, nt, n_steps):
    step = pl.program_id(0) * nt + pl.program_id(1)
    slot = step % 2

    def start_gather(tile_idx, sl):
        base = tile_idx * (tm * MOE_TOPK)
        for r in range(tm):
            for k in range(MOE_TOPK):
                pltpu.make_async_copy(ys_hbm.at[pl.ds(dest_ref[base + r * MOE_TOPK + k], 1), :],
                                      ybuf.at[sl, k, pl.ds(r, 1), :], sem.at[sl]).start()

    @pl.when(step == 0)
    def _():
        start_gather(0, 0)

    for sl in range(2):
        @pl.when(jnp.logical_and(step + 1 < n_steps, slot == 1 - sl))
        def _(sl=sl):
            start_gather(step + 1, sl)

    for k in range(MOE_TOPK):
        pltpu.make_async_copy(ys_hbm.at[pl.ds(0, tm), :], ybuf.at[slot, k], sem.at[slot]).wait()

    rg = rg_ref[0]
    y = ybuf[slot, 0] * rg[:, 0:1] + ybuf[slot, 1] * rg[:, 1:2]
    r = alpha * x_ref[0] + (1.0 + g_ref[0]) * y
    o_ref[0] = _layer_norm_rows(r, lng_ref[...], lnb_ref[...])


def _gather_combine_ln(dest_flat, ys, route_gates, x, gate, ln_g, ln_b, alpha, name):
    b, s, d = x.shape
    tm = _tile(s, 256)
    nt = s // tm
    row = lambda bi, i, dst: (bi, i, 0)
    const = lambda bi, i, dst: (0, 0)
    grid_spec = pltpu.PrefetchScalarGridSpec(
        num_scalar_prefetch=1,
        grid=(b, nt),
        in_specs=[pl.BlockSpec(memory_space=pl.ANY),
                  pl.BlockSpec((1, tm, LANES), row),
                  pl.BlockSpec((1, tm, d), row),
                  pl.BlockSpec((1, 1, d), lambda bi, i, dst: (bi, 0, 0)),
                  pl.BlockSpec((1, d), const),
                  pl.BlockSpec((1, d), const)],
        out_specs=pl.BlockSpec((1, tm, d), row),
        scratch_shapes=[pltpu.VMEM((2, MOE_TOPK, tm, d), F32), pltpu.SemaphoreType.DMA((2,))],
    )
    return pl.pallas_call(
        functools.partial(_gather_ln_kernel, alpha=alpha, tm=tm, nt=nt, n_steps=b * nt),
        out_shape=jax.ShapeDtypeStruct((b, s, d), F32),
        grid_spec=grid_spec,
        compiler_params=_cparams("arbitrary", "arbitrary"),
        name=name,
    )(dest_flat, ys, route_gates, x, gate, ln_g.reshape(1, d), ln_b.reshape(1, d))


def _gdn_kernel(q_ref, k_ref, v_ref, z_ref, bg_ref, cg_ref, hh_ref, ba_ref,
                cw_ref, alog_ref, dtb_ref, nw_ref, scw_ref,
                o_ref,
                qa_ref, ka_ref, va_ref, hist_ref, hist2_ref, gate_ref, state_ref,
                *, heads, ts, gw):
    L = GDN_CHUNK
    dk = GDN_HEAD_DIM

    @pl.when(pl.program_id(1) == 0)
    def _():
        hist_ref[...] = jnp.zeros_like(hist_ref)
        hist2_ref[...] = jnp.zeros_like(hist2_ref)
        state_ref[...] = jnp.zeros_like(state_ref)

    for idx, (src, dst) in enumerate(((q_ref, qa_ref), (k_ref, ka_ref), (v_ref, va_ref))):
        x = src[0]
        y = _causal_conv(x, hist_ref[idx], cw_ref[:, idx * gw:(idx + 1) * gw], GDN_CONV)
        hist_ref[idx] = x[ts - SUBLANES:ts, :]
        dst[...] = _silu(y)

    c = cg_ref[0] * hh_ref[0]
    yb = _causal_conv(c, hist2_ref[...], scw_ref[...], SC_CONV)
    hist2_ref[...] = c[ts - SUBLANES:ts, :]
    o_ref[0, :, gw:2 * gw] = bg_ref[0] * yb

    ba = ba_ref[0]
    gate_ref[0] = _sigmoid(ba)
    gate_ref[1] = -jnp.exp(alog_ref[...]) * _softplus(ba + dtb_ref[...])

    ii = lax.broadcasted_iota(I32, (L, L), 0)
    jj = lax.broadcasted_iota(I32, (L, L), 1)
    causal = ii >= jj
    strict = ii > jj
    tri_incl = causal.astype(F32)
    eye = (ii == jj).astype(F32)
    nchunk = ts // L
    pairs = [(c, h) for c in range(nchunk) for h in range(heads)]

    def tiles(ref, lead=()):
        return jnp.stack([ref[lead + (slice(c * L, (c + 1) * L), slice(h * dk, (h + 1) * dk))]
                          for c, h in pairs], axis=0)

    def bmm(a, b):
        return jnp.einsum('bij,bjk->bik', a.astype(BF16), b.astype(BF16), preferred_element_type=F32)

    def bmm_nt(a, b):
        return jnp.einsum('bik,bjk->bij', a.astype(BF16), b.astype(BF16), preferred_element_type=F32)

    qh = tiles(qa_ref)
    kh = tiles(ka_ref)
    vh = tiles(va_ref)
    beta = jnp.stack([gate_ref[0, c * L:(c + 1) * L, h:h + 1] for c, h in pairs], axis=0)
    cums = [jnp.dot(tri_incl, gate_ref[1, c * L:(c + 1) * L, :], precision=HIGHEST,
                    preferred_element_type=F32) for c in range(nchunk)]
    cums_t = [cm.T for cm in cums]
    gc = jnp.stack([cums[c][:, heads + h:heads + h + 1] for c, h in pairs], axis=0)
    gl = gc[:, L - 1:L, :]
    egc = jnp.exp(gc)
    qn = qh * lax.rsqrt(jnp.sum(qh * qh, -1, keepdims=True) + 1e-6)
    kn = kh * lax.rsqrt(jnp.sum(kh * kh, -1, keepdims=True) + 1e-6)
    qc = qn * (dk ** -0.5)
    kb = kn * beta

    nb2 = len(pairs) // 2

    def pair(x):
        return x.reshape(nb2, 2 * L, x.shape[-1])

    i2 = lax.broadcasted_iota(I32, (2 * L, 2 * L), 0)
    j2 = lax.broadcasted_iota(I32, (2 * L, 2 * L), 1)
    same = (i2 < L) == (j2 < L)
    causal2 = same & (i2 >= j2)
    strict2 = same & (i2 > j2)
    eye2 = (i2 == j2).astype(F32)
    gc2, beta2, egc2, kn2, kb2 = pair(gc), pair(beta), pair(egc), pair(kn), pair(kb)
    gr2 = jnp.stack([jnp.concatenate([cums_t[c][heads + h:heads + h + 1, :],
                                      cums_t[c][heads + h + 1:heads + h + 2, :]], axis=1)
                     for c in range(nchunk) for h in range(0, heads, 2)], axis=0)
    decay2 = jnp.where(causal2, jnp.exp(jnp.where(causal2, gc2 - gr2, 0.0)), 0.0)
    m = jnp.where(strict2, bmm_nt(kb2, kn2) * decay2, 0.0)
    t_inv = eye2 - m
    mp = m
    for _ in range(5):
        mp = bmm(mp, mp)
        t_inv = t_inv + bmm(t_inv, mp)
    sol = bmm(t_inv, jnp.concatenate([pair(vh) * beta2, kb2 * egc2], axis=2))
    u2 = sol[:, :, 0:dk]
    w2 = sol[:, :, dk:2 * dk]
    qk2 = jnp.where(causal2, bmm_nt(pair(qc), kn2) * decay2, 0.0)
    qg2 = pair(qc * egc)
    k_dec2 = pair(kn * jnp.exp(gl - gc))
    egl = jnp.exp(gl)
    zero_blk = jnp.zeros((L, dk), F32)

    def own_blocks(x):
        return jnp.concatenate([x[0:L, 0:dk], x[L:2 * L, dk:2 * dk]], axis=0)

    state = [state_ref[h] for h in range(heads)]
    outs = []
    for p in range(nb2):
        ba, bb = 2 * p, 2 * p + 1
        ha, hb = pairs[ba][1], pairs[bb][1]
        s_cat = jnp.concatenate([state[ha], state[hb]], axis=1)
        v_new = u2[p] - own_blocks(_dot(w2[p], s_cat))
        outs.append(own_blocks(_dot(qg2[p], s_cat)) + _dot(qk2[p], v_new))
        kd = k_dec2[p]
        k_bd = jnp.concatenate([jnp.concatenate([kd[0:L], zero_blk], axis=1),
                                jnp.concatenate([zero_blk, kd[L:2 * L]], axis=1)], axis=0)
        upd = _dot(k_bd.T, v_new)
        state[ha] = state[ha] * egl[ba] + upd[0:dk]
        state[hb] = state[hb] * egl[bb] + upd[dk:2 * dk]
    for h in range(heads):
        state_ref[h] = state[h]

    o = jnp.stack(outs, axis=0).reshape(len(pairs), L, dk)
    on = o * lax.rsqrt(jnp.mean(o * o, -1, keepdims=True) + 1e-6) * nw_ref[...]
    y = on * _silu(tiles(z_ref, (0,)))
    for b, (c, h) in enumerate(pairs):
        o_ref[0, c * L:(c + 1) * L, h * dk:(h + 1) * dk] = y[b]


def _gdn_shortconv(proj, conv_w, a_log, dt_bias, norm_w, sc_w, heads):
    b, s, _ = proj.shape
    gw = heads * GDN_HEAD_DIM
    ts = _tile(s, 256)
    alog_p = jnp.zeros((1, LANES), F32).at[0, heads:2 * heads].set(a_log)
    dtb_p = jnp.zeros((1, LANES), F32).at[0, heads:2 * heads].set(dt_bias)
    col = lambda cb: (lambda bi, i: (bi, i, cb))
    const = lambda bi, i: (0, 0)
    kern = functools.partial(_gdn_kernel, heads=heads, ts=ts, gw=gw)
    return pl.pallas_call(
        kern,
        out_shape=jax.ShapeDtypeStruct((b, s, 2 * gw), F32),
        grid=(b, s // ts),
        in_specs=[pl.BlockSpec((1, ts, gw), col(cb)) for cb in range(7)]
        + [pl.BlockSpec((1, ts, LANES), col(7 * gw // LANES)),
           pl.BlockSpec((GDN_CONV, 3 * gw), const),
           pl.BlockSpec((1, LANES), const),
           pl.BlockSpec((1, LANES), const),
           pl.BlockSpec((1, GDN_HEAD_DIM), const),
           pl.BlockSpec((SC_CONV, gw), const)],
        out_specs=pl.BlockSpec((1, ts, 2 * gw), lambda bi, i: (bi, i, 0)),
        scratch_shapes=[pltpu.VMEM((ts, gw), F32),
                        pltpu.VMEM((ts, gw), F32),
                        pltpu.VMEM((ts, gw), F32),
                        pltpu.VMEM((3, SUBLANES, gw), F32),
                        pltpu.VMEM((SUBLANES, gw), F32),
                        pltpu.VMEM((2, ts, LANES), F32),
                        pltpu.VMEM((heads, GDN_HEAD_DIM, GDN_HEAD_DIM), F32)],
        compiler_params=_cparams("parallel", "arbitrary"),
        name="gdn_shortconv",
    )(proj, proj, proj, proj, proj, proj, proj, proj,
      conv_w, alog_p, dtb_p, norm_w.reshape(1, GDN_HEAD_DIM), sc_w)


def _rope_kernel(pos_ref, inv_ref, qpe_ref, kpe_ref, qo_ref, ko_ref, *, heads):
    ang = pos_ref[0].astype(F32) * inv_ref[...]
    lane = lax.broadcasted_iota(I32, ang.shape, 1)
    half = MLA_ROPE // 2
    cos = jnp.where(lane < MLA_ROPE, jnp.cos(ang), 0.0)
    sin = jnp.sin(ang)
    sgn = jnp.where(lane < half, -sin, jnp.where(lane < MLA_ROPE, sin, 0.0))

    def rot(x):
        swapped = jnp.where(lane < half, pltpu.roll(x, LANES - half, 1), pltpu.roll(x, half, 1))
        return x * cos + swapped * sgn

    for h in range(heads):
        cols = slice(h * LANES, (h + 1) * LANES)
        qo_ref[0, :, cols] = rot(qpe_ref[0, :, cols]).astype(BF16)
    ko_ref[0] = rot(kpe_ref[0]).astype(BF16)


def _rope(positions, qpe, heads, cproj, kpe_block):
    b, s = positions.shape
    ts = _tile(s, 512)
    half = MLA_ROPE // 2
    inv = ROPE_THETA ** (-jnp.arange(half, dtype=F32) * (2.0 / MLA_ROPE))
    inv_tab = jnp.zeros((1, LANES), F32).at[0, 0:half].set(inv).at[0, half:MLA_ROPE].set(inv)
    w = heads * LANES
    return pl.pallas_call(
        functools.partial(_rope_kernel, heads=heads),
        out_shape=(jax.ShapeDtypeStruct((b, s, w), BF16), jax.ShapeDtypeStruct((b, s, LANES), BF16)),
        grid=(b, s // ts),
        in_specs=[pl.BlockSpec((1, ts, 1), lambda bi, i: (bi, i, 0)),
                  pl.BlockSpec((1, LANES), lambda bi, i: (0, 0)),
                  pl.BlockSpec((1, ts, w), lambda bi, i: (bi, i, 0)),
                  pl.BlockSpec((1, ts, LANES), lambda bi, i: (bi, i, kpe_block))],
        out_specs=(pl.BlockSpec((1, ts, w), lambda bi, i: (bi, i, 0)),
                   pl.BlockSpec((1, ts, LANES), lambda bi, i: (bi, i, 0))),
        compiler_params=_cparams("parallel", "parallel"),
        name="mla_rope",
    )(positions.reshape(b, s, 1), inv_tab, qpe, cproj)


def _attn_kernel(qn_ref, qp_ref, kn_ref, kp_ref, v_ref, o_ref, kcat_ref, s_ref, *, tq, hq, nq, scale):
    qi = pl.program_id(2)

    @pl.when(qi == 0)
    def _():
        kcat_ref[:, 0:MLA_NOPE] = kn_ref[0]
        kcat_ref[:, MLA_NOPE:MLA_NOPE + LANES] = kp_ref[0]

    qcat = jnp.concatenate([qn_ref[0], qp_ref[0]], axis=1)
    tri = lax.broadcasted_iota(I32, (hq, hq), 0) >= lax.broadcasted_iota(I32, (hq, hq), 1)

    for qv in range(nq):
        @pl.when(qi == qv)
        def _(qv=qv):
            for h in range(tq // hq):
                rows = slice(h * hq, (h + 1) * hq)
                nk = qv * tq + (h + 1) * hq
                for c0 in range(0, nk, ATTN_KEY_CHUNK):
                    c1 = min(c0 + ATTN_KEY_CHUNK, nk)
                    s = lax.dot_general(qcat[rows], kcat_ref[c0:c1, :], (((1,), (1,)), ((), ())),
                                        preferred_element_type=F32) * scale
                    if c1 == nk:
                        w = c1 - c0
                        last = jnp.where(tri, s[:, w - hq:], -jnp.inf)
                        s = last if w == hq else jnp.concatenate([s[:, :w - hq], last], axis=1)
                    s_ref[rows, c0:c1] = s
                    mb = jnp.max(s, -1, keepdims=True)
                    m = mb if c0 == 0 else jnp.maximum(m, mb)
                l = jnp.zeros((hq, 1), F32)
                acc = jnp.zeros((hq, MLA_V), F32)
                for c0 in range(0, nk, hq):
                    p = jnp.exp(s_ref[rows, c0:c0 + hq] - m)
                    l = l + jnp.sum(p, -1, keepdims=True)
                    acc = acc + jnp.dot(p.astype(BF16), v_ref[0, c0:c0 + hq, :], preferred_element_type=F32)
                o_ref[0, rows, :] = acc / l


def _attention(qn, qpe, kvup, kpe, heads):
    b, s, _ = qn.shape
    tq = _tile(s, 1024)
    hq = _tile(tq, 256)
    nq = s // tq
    scale = (MLA_NOPE + MLA_ROPE) ** -0.5
    return pl.pallas_call(
        functools.partial(_attn_kernel, tq=tq, hq=hq, nq=nq, scale=scale),
        out_shape=jax.ShapeDtypeStruct((b, s, heads * MLA_V), F32),
        grid=(b, heads, nq),
        in_specs=[pl.BlockSpec((1, tq, MLA_NOPE), lambda bi, h, i: (bi, i, h)),
                  pl.BlockSpec((1, tq, LANES), lambda bi, h, i: (bi, i, h)),
                  pl.BlockSpec((1, s, MLA_NOPE), lambda bi, h, i: (bi, 0, h)),
                  pl.BlockSpec((1, s, LANES), lambda bi, h, i: (bi, 0, 0)),
                  pl.BlockSpec((1, s, MLA_V), lambda bi, h, i: (bi, 0, heads + h))],
        out_specs=pl.BlockSpec((1, tq, MLA_V), lambda bi, h, i: (bi, i, h)),
        scratch_shapes=[pltpu.VMEM((s, MLA_NOPE + LANES), BF16), pltpu.VMEM((tq, s), F32)],
        compiler_params=_cparams("parallel", "parallel", "arbitrary"),
        name="mla_attention",
    )(qn, qpe, kvup, kpe, kvup)


def _hyb_in_weight(w_in, heads):
    d = w_in.shape[0]
    gw = heads * GDN_HEAD_DIM
    main = jnp.concatenate([w_in[:, :4 * gw], w_in[:, 4 * gw + 2 * heads:]], axis=1)
    gates = w_in[:, 4 * gw:4 * gw + 2 * heads]
    n = main.shape[1] + LANES
    n_pad = -(-n // 768) * 768
    tail = jnp.zeros((d, n_pad - main.shape[1]), w_in.dtype).at[:, :2 * heads].set(gates)
    return jnp.concatenate([main, tail], axis=1).astype(BF16)


def _mla_in_weight(w_in):
    d = w_in.shape[0]
    pad = jnp.zeros((d, LANES - MLA_ROPE), w_in.dtype)
    return jnp.concatenate([w_in, pad], axis=1).astype(BF16)


def _mla_uq_weights(w_uq, heads):
    r = w_uq.shape[0]
    w = w_uq.reshape(r, heads, MLA_NOPE + MLA_ROPE)
    nope = w[:, :, :MLA_NOPE].reshape(r, heads * MLA_NOPE)
    pe = jnp.concatenate([w[:, :, MLA_NOPE:], jnp.zeros((r, heads, LANES - MLA_ROPE), w.dtype)], axis=2)
    return nope.astype(BF16), pe.reshape(r, heads * LANES).astype(BF16)


def _mla_ukv_weight(w_ukv, heads):
    r = w_ukv.shape[0]
    w = w_ukv.reshape(r, heads, MLA_NOPE + MLA_V)
    return jnp.concatenate([w[:, :, :MLA_NOPE].reshape(r, heads * MLA_NOPE),
                            w[:, :, MLA_NOPE:].reshape(r, heads * MLA_V)], axis=1).astype(BF16)


def _router_weight(wr_g, br_g, wr_e, br_e):
    d = wr_g.shape[0]
    n = wr_g.shape[1] + wr_e.shape[1]
    w = jnp.zeros((d, LANES), F32).at[:, :n].set(jnp.concatenate([wr_g, wr_e], axis=1))
    bias = jnp.zeros((1, LANES), F32).at[0, :n].set(jnp.concatenate([br_g, br_e]))
    w_hi = w.astype(BF16)
    w_lo = (w - w_hi.astype(F32)).astype(BF16)
    return jnp.concatenate([w_hi, w_lo], axis=1), bias


def kernel(x, c, positions, ada_w, ada_b, ln_g, ln_b, hyb_w_in, gdn_conv_w, gdn_a_log, gdn_dt_bias,
           gdn_norm_w, sc_conv_w, hyb_w_out, mla_w_in, mla_q_norm, mla_kv_norm, mla_w_uq, mla_w_ukv,
           mla_w_out, moe_router_g, moe_bias_g, moe_router_e, moe_bias_e, moe_w_gate, moe_w_up,
           moe_w_down):
    b, s, d = x.shape
    depth = ada_w.shape[0]
    alpha = (2.0 * depth) ** 0.25
    gdn_heads = d // (2 * GDN_HEAD_DIM)
    mla_heads = mla_w_out.shape[1] // MLA_V
    n_groups = moe_router_g.shape[2]
    n_exp = moe_router_e.shape[2]
    per_group = n_exp // n_groups

    mod = _modulation(c, ada_w, ada_b)
    t = b * s
    xs = jnp.zeros((t * MOE_TOPK + n_exp * MOE_BLOCK, d // 2), U32)
    for layer in range(depth):
        sh1, sc1, g1, sh2, sc2, g2 = [mod[layer, :, j * d:(j + 1) * d].reshape(b, 1, d) for j in range(6)]
        i = layer // 2
        wr, br = _router_weight(moe_router_g[layer], moe_bias_g[layer], moe_router_e[layer], moe_bias_e[layer])
        if layer % 2 == 0:
            proj = _mod_matmul(x, sc1, sh1, _hyb_in_weight(hyb_w_in[i], gdn_heads), "hyb_in_proj")
            y = _gdn_shortconv(proj, gdn_conv_w[i], gdn_a_log[i], gdn_dt_bias[i], gdn_norm_w[i],
                               sc_conv_w[i], gdn_heads)
            w_out = hyb_w_out[i]
        else:
            cproj = _mod_matmul(x, sc1, sh1, _mla_in_weight(mla_w_in[i]), "mla_in_proj")
            w_qn, w_qpe = _mla_uq_weights(mla_w_uq[i], mla_heads)
            qn = _rms_matmul(cproj, 0, MLA_Q_RANK, mla_q_norm[i], w_qn, BF16, "mla_q_nope_up")
            qpe_raw = _rms_matmul(cproj, 0, MLA_Q_RANK, mla_q_norm[i], w_qpe, F32, "mla_q_rope_up")
            kvup = _rms_matmul(cproj, 1, MLA_KV_RANK, mla_kv_norm[i], _mla_ukv_weight(mla_w_ukv[i], mla_heads),
                               BF16, "mla_kv_up")
            qpe, kpe = _rope(positions, qpe_raw, mla_heads, cproj, (MLA_Q_RANK + MLA_KV_RANK) // LANES)
            y = _attention(qn, qpe, kvup, kpe, mla_heads)
            w_out = mla_w_out[i]
        x, hin2, ids, gates = _out_ln_route(y, w_out.astype(BF16), x, g1, ln_g[layer, 0], ln_b[layer, 0],
                                            sc2, sh2, wr, br, alpha, n_groups, per_group,
                                            "mixer_out_ln_route")
        dest, meta = _dispatch(ids.reshape(t, LANES))
        dest_flat = dest[:, :MOE_TOPK].reshape(-1)
        eblk = meta[0, :n_exp] // MOE_BLOCK
        enb = meta[1, :n_exp]
        xs = _scatter_rows(dest_flat, hin2.reshape(t, d // 2), xs)
        ys = _moe_ffn(eblk, enb, xs, moe_w_gate, moe_w_up, moe_w_down, layer)
        x = _gather_combine_ln(dest_flat, ys, gates, x, g2, ln_g[layer, 1], ln_b[layer, 1], alpha,
                               "moe_gather_combine_ln")
    return x
```

```python
import functools

import jax
import jax.numpy as jnp
from jax import lax
from jax.experimental import pallas as pl
from jax.experimental.pallas import tpu as pltpu

F32, BF16, I32, U32 = jnp.float32, jnp.bfloat16, jnp.int32, jnp.uint32
HIGHEST = lax.Precision.HIGHEST

LANES = 128
SUBLANES = 8
VMEM_LIMIT_BYTES = 56 * 1024 * 1024

GDN_HEAD_DIM = 128
GDN_CONV = 4
GDN_CHUNK = 64
SC_CONV = 3
MLA_NOPE = 128
MLA_ROPE = 64
MLA_V = 128
MLA_Q_RANK = 512
MLA_KV_RANK = 512
ROPE_THETA = 10000.0
MOE_TOPK = 2
MOE_BLOCK = 128
FFN_RING = 6
ATTN_KEY_CHUNK = 512


def _tile(n, pref):
    if n <= pref:
        return n
    for t in range(pref, 0, -LANES):
        if n % t == 0:
            return t
    return n


def _cparams(*sem):
    return pltpu.CompilerParams(dimension_semantics=sem, vmem_limit_bytes=VMEM_LIMIT_BYTES)


def _sigmoid(x):
    return 1.0 / (1.0 + jnp.exp(-x))


def _silu(x):
    return x * _sigmoid(x)


def _softplus(x):
    return jnp.maximum(x, 0.0) + jnp.log(1.0 + jnp.exp(-jnp.abs(x)))


def _pack_bf16_pair(lo, hi):
    lo_bits = lax.bitcast_convert_type(lo.astype(BF16).astype(F32), U32)
    hi_bits = lax.bitcast_convert_type(hi.astype(BF16).astype(F32), U32)
    return (lo_bits >> 16) | (hi_bits & jnp.uint32(0xFFFF0000))


def _unpack_bf16_pair(w):
    lo = lax.bitcast_convert_type(w << 16, F32)
    hi = lax.bitcast_convert_type(w & jnp.uint32(0xFFFF0000), F32)
    return lo.astype(BF16), hi.astype(BF16)


def _dot(a, b):
    return jnp.dot(a.astype(BF16), b.astype(BF16), preferred_element_type=F32)


def _dot_nt(a, b):
    return lax.dot_general(a.astype(BF16), b.astype(BF16), (((1,), (1,)), ((), ())),
                           preferred_element_type=F32)


def _causal_conv(x, hist, w, taps):
    rows, width = x.shape
    g = rows // SUBLANES
    x3 = x.reshape(g, SUBLANES, width)
    sub = lax.broadcasted_iota(I32, (1, SUBLANES, width), 1)
    y = None
    for j in range(taps - 1):
        shift = taps - 1 - j
        rot = pltpu.roll(x3, shift, 1)
        prev = jnp.concatenate([pltpu.roll(hist, shift, 0)[None], rot[:g - 1]], axis=0)
        term = w[j:j + 1] * jnp.where(sub < shift, prev, rot)
        y = term if y is None else y + term
    return (y + w[taps - 1:taps] * x3).reshape(rows, width)


def _mod_kernel(c_ref, w_ref, b_ref, o_ref):
    c = c_ref[...]
    o_ref[0] = _dot(_silu(c), w_ref[0]) + b_ref[0]


def _modulation(c, ada_w, ada_b):
    depth, d, n = ada_w.shape
    b = c.shape[0]
    rows = -(-b // SUBLANES) * SUBLANES
    cp = jnp.zeros((rows, d), F32).at[:b].set(c)
    tn = _tile(n, 1024)
    out = pl.pallas_call(
        _mod_kernel,
        out_shape=jax.ShapeDtypeStruct((depth, rows, n), F32),
        grid=(depth, n // tn),
        in_specs=[pl.BlockSpec((rows, d), lambda l, j: (0, 0)),
                  pl.BlockSpec((1, d, tn), lambda l, j: (l, 0, j)),
                  pl.BlockSpec((1, 1, tn), lambda l, j: (l, 0, j))],
        out_specs=pl.BlockSpec((1, rows, tn), lambda l, j: (l, 0, j)),
        compiler_params=_cparams("parallel", "parallel"),
        name="adaln_mod",
    )(cp, ada_w, ada_b.reshape(depth, 1, n))
    return out[:, :b]


def _modmm_kernel(x_ref, sc_ref, sh_ref, w_ref, o_ref, xb_ref):
    @pl.when(pl.program_id(2) == 0)
    def _():
        xb_ref[...] = (x_ref[0] * (1.0 + sc_ref[0]) + sh_ref[0]).astype(BF16)

    o_ref[0] = jnp.dot(xb_ref[...], w_ref[...], preferred_element_type=F32)


def _mod_matmul(x, sc, sh, w, name):
    b, s, d = x.shape
    n = w.shape[1]
    tm = _tile(s, 1024)
    tn = _tile(n, 768)
    return pl.pallas_call(
        _modmm_kernel,
        out_shape=jax.ShapeDtypeStruct((b, s, n), F32),
        grid=(b, s // tm, n // tn),
        in_specs=[pl.BlockSpec((1, tm, d), lambda bi, i, j: (bi, i, 0)),
                  pl.BlockSpec((1, 1, d), lambda bi, i, j: (bi, 0, 0)),
                  pl.BlockSpec((1, 1, d), lambda bi, i, j: (bi, 0, 0)),
                  pl.BlockSpec((d, tn), lambda bi, i, j: (0, j))],
        out_specs=pl.BlockSpec((1, tm, tn), lambda bi, i, j: (bi, i, j)),
        scratch_shapes=[pltpu.VMEM((tm, d), BF16)],
        compiler_params=_cparams("parallel", "parallel", "arbitrary"),
        name=name,
    )(x, sc, sh, w)


def _rmsmm_kernel(x_ref, nw_ref, w_ref, o_ref, xb_ref):
    @pl.when(pl.program_id(2) == 0)
    def _():
        x = x_ref[0]
        xb_ref[...] = (x * lax.rsqrt(jnp.mean(x * x, -1, keepdims=True) + 1e-6) * nw_ref[...]).astype(BF16)

    o_ref[0] = jnp.dot(xb_ref[...], w_ref[...], preferred_element_type=F32).astype(o_ref.dtype)


def _rms_matmul(x, col_block, k, norm_w, w, out_dtype, name):
    b, s, _ = x.shape
    n = w.shape[1]
    tm = _tile(s, 1024)
    tn = _tile(n, 1024)
    return pl.pallas_call(
        _rmsmm_kernel,
        out_shape=jax.ShapeDtypeStruct((b, s, n), out_dtype),
        grid=(b, s // tm, n // tn),
        in_specs=[pl.BlockSpec((1, tm, k), lambda bi, i, j: (bi, i, col_block)),
                  pl.BlockSpec((1, k), lambda bi, i, j: (0, 0)),
                  pl.BlockSpec((k, tn), lambda bi, i, j: (0, j))],
        out_specs=pl.BlockSpec((1, tm, tn), lambda bi, i, j: (bi, i, j)),
        scratch_shapes=[pltpu.VMEM((tm, k), BF16)],
        compiler_params=_cparams("parallel", "parallel", "arbitrary"),
        name=name,
    )(x, norm_w.reshape(1, k), w)


def _layer_norm_rows(r, g, b):
    mu = jnp.mean(r, -1, keepdims=True)
    var = jnp.mean(jnp.square(r - mu), -1, keepdims=True)
    return (r - mu) * lax.rsqrt(var + 1e-5) * g + b


def _outln_kernel(y_ref, w_ref, xres_ref, g_ref, lng_ref, lnb_ref, sc_ref, sh_ref, wr_ref, br_ref,
                  xo_ref, hin_ref, ids_ref, gates_ref, *, alpha, n_groups, per_group, parts):
    tp = y_ref.shape[1] // parts
    for part in range(parts):
        _outln_rows(slice(part * tp, (part + 1) * tp), y_ref, w_ref, xres_ref, g_ref, lng_ref, lnb_ref,
                    sc_ref, sh_ref, wr_ref, br_ref, xo_ref, hin_ref, ids_ref, gates_ref,
                    alpha=alpha, n_groups=n_groups, per_group=per_group)


def _outln_rows(rows, y_ref, w_ref, xres_ref, g_ref, lng_ref, lnb_ref, sc_ref, sh_ref, wr_ref, br_ref,
                xo_ref, hin_ref, ids_ref, gates_ref, *, alpha, n_groups, per_group):
    y = jnp.dot(y_ref[0, rows, :].astype(BF16), w_ref[...], preferred_element_type=F32)
    r = alpha * xres_ref[0, rows, :] + (1.0 + g_ref[0]) * y
    xn = _layer_norm_rows(r, lng_ref[...], lnb_ref[...])
    xo_ref[0, rows, :] = xn
    hin = xn * (1.0 + sc_ref[0]) + sh_ref[0]
    half = hin.shape[1] // 2
    hin_ref[0, rows, :] = _pack_bf16_pair(hin[:, :half], hin[:, half:])
    h_hi = hin.astype(BF16)
    h_lo = (hin - h_hi.astype(F32)).astype(BF16)
    ph = jnp.dot(h_hi, wr_ref[...], preferred_element_type=F32)
    plo = jnp.dot(h_lo, wr_ref[...], preferred_element_type=F32)
    logits = (ph[:, :LANES] + ph[:, LANES:]) + (plo[:, :LANES] + plo[:, LANES:]) + br_ref[...]
    lane = lax.broadcasted_iota(I32, logits.shape, 1)
    big = jnp.int32(4 * LANES)
    neg = jnp.float32(-jnp.inf)
    n_exp = n_groups * per_group
    gmask = lane < n_groups
    lg = jnp.where(gmask, logits, neg)
    mg = jnp.max(lg, -1, keepdims=True)
    grp = jnp.min(jnp.where(gmask & (lg == mg), lane, big), -1, keepdims=True)
    pg_sel = 1.0 / jnp.sum(jnp.where(gmask, jnp.exp(lg - mg), 0.0), -1, keepdims=True)
    lo = n_groups + grp * per_group
    emask = (lane >= lo) & (lane < lo + per_group) & (lane < n_groups + n_exp)
    le = jnp.where(emask, logits, neg)
    me = jnp.max(le, -1, keepdims=True)
    ee = jnp.where(emask, jnp.exp(le - me), 0.0)
    p = ee / jnp.sum(ee, -1, keepdims=True)
    pm = jnp.where(emask, p, -1.0)
    p1 = jnp.max(pm, -1, keepdims=True)
    i1 = jnp.min(jnp.where(emask & (pm == p1), lane, big), -1, keepdims=True)
    pm2 = jnp.where(lane == i1, -1.0, pm)
    p2 = jnp.max(pm2, -1, keepdims=True)
    i2 = jnp.min(jnp.where(emask & (lane != i1) & (pm2 == p2), lane, big), -1, keepdims=True)
    den = p1 + p2
    g1 = pg_sel * p1 / den
    g2 = pg_sel * p2 / den
    ids_ref[0, rows, :] = jnp.where(lane == 0, i1 - n_groups, jnp.where(lane == 1, i2 - n_groups, 0))
    gates_ref[0, rows, :] = jnp.where(lane == 0, g1, jnp.where(lane == 1, g2, 0.0))


def _out_ln_route(y, w, xres, gate, ln_g, ln_b, sc, sh, wr, br, alpha, n_groups, per_group, name):
    b, s, k = y.shape
    d = w.shape[1]
    tm = _tile(s, 512)
    parts = 2 if tm % (2 * SUBLANES) == 0 else 1
    kern = functools.partial(_outln_kernel, alpha=alpha, n_groups=n_groups, per_group=per_group, parts=parts)
    row = lambda bi, i: (bi, i, 0)
    per_b = lambda bi, i: (bi, 0, 0)
    const = lambda bi, i: (0, 0)
    return pl.pallas_call(
        kern,
        out_shape=(jax.ShapeDtypeStruct((b, s, d), F32), jax.ShapeDtypeStruct((b, s, d // 2), U32),
                   jax.ShapeDtypeStruct((b, s, LANES), I32), jax.ShapeDtypeStruct((b, s, LANES), F32)),
        grid=(b, s // tm),
        in_specs=[pl.BlockSpec((1, tm, k), row),
                  pl.BlockSpec((k, d), const),
                  pl.BlockSpec((1, tm, d), row),
                  pl.BlockSpec((1, 1, d), per_b),
                  pl.BlockSpec((1, d), const),
                  pl.BlockSpec((1, d), const),
                  pl.BlockSpec((1, 1, d), per_b),
                  pl.BlockSpec((1, 1, d), per_b),
                  pl.BlockSpec((d, 2 * LANES), const),
                  pl.BlockSpec((1, LANES), const)],
        out_specs=(pl.BlockSpec((1, tm, d), row), pl.BlockSpec((1, tm, d // 2), row),
                   pl.BlockSpec((1, tm, LANES), row), pl.BlockSpec((1, tm, LANES), row)),
        compiler_params=_cparams("parallel", "parallel"),
        name=name,
    )(y, w, xres, gate, ln_g.reshape(1, d), ln_b.reshape(1, d), sc, sh, wr, br)


def _dispatch_kernel(ids_ref, dest_ref, meta_ref, rank_ref, *, t, tile, blk_shift):
    lane = lax.broadcasted_iota(I32, (tile, LANES), 1)
    ri = lax.broadcasted_iota(I32, (tile, tile), 0)
    ci = lax.broadcasted_iota(I32, (tile, tile), 1)
    before = (ri > ci).astype(BF16)

    def hits(rows):
        ids = ids_ref[rows, :]
        return lane == ids[:, 0:1], lane == ids[:, 1:2]

    def count(ti, run):
        rows = pl.ds(pl.multiple_of(ti * tile, tile), tile)
        h1, h2 = hits(rows)
        onehot = jnp.logical_or(h1, h2).astype(BF16)
        prefix = jnp.dot(before, onehot, preferred_element_type=F32) + run
        r1 = jnp.sum(jnp.where(h1, prefix, 0.0), -1, keepdims=True)
        r2 = jnp.sum(jnp.where(h2, prefix, 0.0), -1, keepdims=True)
        rank_ref[rows, :] = jnp.where(lane == 0, r1, jnp.where(lane == 1, r2, 0.0))
        return run + jnp.sum(onehot.astype(F32), 0, keepdims=True)

    counts = lax.fori_loop(0, t // tile, count, jnp.zeros((1, LANES), F32))
    cnt = jnp.broadcast_to(counts, (SUBLANES, LANES)).astype(I32)
    nblk = (cnt + ((1 << blk_shift) - 1)) >> blk_shift
    padded = nblk << blk_shift
    lane8 = lax.broadcasted_iota(I32, (SUBLANES, LANES), 1)
    incl = padded
    step = 1
    while step < LANES:
        incl = incl + jnp.where(lane8 >= step, pltpu.roll(incl, step, 1), 0)
        step *= 2
    pad_start = incl - padded
    row8 = lax.broadcasted_iota(I32, (SUBLANES, LANES), 0)
    meta_ref[...] = jnp.where(row8 == 0, pad_start, jnp.where(row8 == 1, nblk, cnt))
    start_f = pad_start[0:1, :].astype(F32)

    def place(ti, c):
        rows = pl.ds(pl.multiple_of(ti * tile, tile), tile)
        h1, h2 = hits(rows)
        rk = rank_ref[rows, :]
        d1 = jnp.sum(jnp.where(h1, start_f, 0.0), -1, keepdims=True) + rk[:, 0:1]
        d2 = jnp.sum(jnp.where(h2, start_f, 0.0), -1, keepdims=True) + rk[:, 1:2]
        dest_ref[rows, :] = jnp.where(lane == 0, d1, jnp.where(lane == 1, d2, 0.0)).astype(I32)
        return c

    lax.fori_loop(0, t // tile, place, 0)


def _dispatch(ids):
    t = ids.shape[0]
    tile = _tile(t, 256)
    blk_shift = MOE_BLOCK.bit_length() - 1
    assert (1 << blk_shift) == MOE_BLOCK
    return pl.pallas_call(
        functools.partial(_dispatch_kernel, t=t, tile=tile, blk_shift=blk_shift),
        out_shape=(jax.ShapeDtypeStruct((t, LANES), I32), jax.ShapeDtypeStruct((SUBLANES, LANES), I32)),
        grid=(1,),
        in_specs=[pl.BlockSpec((t, LANES), lambda i: (0, 0))],
        out_specs=(pl.BlockSpec((t, LANES), lambda i: (0, 0)), pl.BlockSpec((SUBLANES, LANES), lambda i: (0, 0))),
        scratch_shapes=[pltpu.VMEM((t, LANES), F32)],
        compiler_params=_cparams("arbitrary"),
        name="moe_dispatch",
    )(ids)


def _scatter_rows_kernel(dest_ref, hin_ref, xs_in, xs_out, stage, sem, *, tm, n_steps):
    del xs_in
    step = pl.program_id(0)
    base = step * (tm * MOE_TOPK)

    def wait_slot(sl):
        for _ in range(MOE_TOPK):
            pltpu.make_async_copy(stage.at[sl], xs_out.at[pl.ds(0, tm), :], sem.at[sl]).wait()

    for sl in range(2):
        @pl.when(step % 2 == sl)
        def _(sl=sl):
            @pl.when(step >= 2)
            def _():
                wait_slot(sl)

            stage[sl] = hin_ref[...]
            for r in range(tm):
                for k in range(MOE_TOPK):
                    pltpu.make_async_copy(stage.at[sl, pl.ds(r, 1), :],
                                          xs_out.at[pl.ds(dest_ref[base + r * MOE_TOPK + k], 1), :],
                                          sem.at[sl]).start()

            @pl.when(step == n_steps - 1)
            def _():
                if n_steps >= 2:
                    wait_slot(1 - sl)
                wait_slot(sl)


def _scatter_rows(dest_flat, hin2, xs_init):
    t, d = hin2.shape
    n_rows = xs_init.shape[0]
    tm = _tile(t, 256)
    grid_spec = pltpu.PrefetchScalarGridSpec(
        num_scalar_prefetch=1,
        grid=(t // tm,),
        in_specs=[pl.BlockSpec((tm, d), lambda i, dst: (i, 0)),
                  pl.BlockSpec(memory_space=pl.ANY)],
        out_specs=pl.BlockSpec(memory_space=pl.ANY),
        scratch_shapes=[pltpu.VMEM((2, tm, d), hin2.dtype), pltpu.SemaphoreType.DMA((2,))],
    )
    return pl.pallas_call(
        functools.partial(_scatter_rows_kernel, tm=tm, n_steps=t // tm),
        out_shape=jax.ShapeDtypeStruct((n_rows, d), hin2.dtype),
        grid_spec=grid_spec,
        input_output_aliases={2: 0},
        compiler_params=_cparams("arbitrary"),
        name="moe_scatter_rows",
    )(dest_flat, hin2, xs_init)


def _ffn_kernel(eblk_ref, enb_ref, xs_hbm, wg_ref, wu_ref, wd_ref, ys_hbm,
                xbuf, ybuf, wgb, wub, wdb, isem, osem, *, blk, nb_total, n_exp):
    e = pl.program_id(0)
    n = enb_ref[e]
    b0 = eblk_ref[e]
    used = eblk_ref[n_exp - 1] + enb_ref[n_exp - 1]

    def in_cp(gb):
        sl = gb % FFN_RING
        return pltpu.make_async_copy(xs_hbm.at[pl.ds(gb * blk, blk), :], xbuf.at[sl], isem.at[sl])

    def out_cp(gb):
        sl = gb % FFN_RING
        return pltpu.make_async_copy(ybuf.at[sl], ys_hbm.at[pl.ds(gb * blk, blk), :], osem.at[sl])

    @pl.when(e == 0)
    def _():
        for k in range(FFN_RING - 1):
            @pl.when(k < used)
            def _(k=k):
                in_cp(k).start(priority=1)

    @pl.when(n > 0)
    def _():
        wgb[...] = wg_ref[0, 0].astype(BF16)
        wub[...] = wu_ref[0, 0].astype(BF16)
        wdb[...] = wd_ref[0, 0].astype(BF16)

        def body(j, c):
            gb = b0 + j
            sl = gb % FFN_RING

            @pl.when(gb + FFN_RING - 1 < used)
            def _():
                in_cp(gb + FFN_RING - 1).start(priority=1)

            in_cp(gb).wait()

            @pl.when(gb >= FFN_RING)
            def _():
                out_cp(gb - FFN_RING).wait()

            xb = jnp.concatenate(_unpack_bf16_pair(xbuf[sl]), axis=1)
            hg = jnp.dot(xb, wgb[...], preferred_element_type=F32)
            hu = jnp.dot(xb, wub[...], preferred_element_type=F32)
            hid = (_silu(hg) * hu).astype(BF16)
            ybuf[sl] = jnp.dot(hid, wdb[...], preferred_element_type=F32)
            out_cp(gb).start(priority=1)
            return c

        lax.fori_loop(0, n, body, 0)

    @pl.when(e == n_exp - 1)
    def _():
        for k in range(FFN_RING, 0, -1):
            @pl.when(used >= k)
            def _(k=k):
                out_cp(used - k).wait()

        ybuf[0] = jnp.zeros((blk, ybuf.shape[2]), F32)

        def fill(jb, c):
            cp = pltpu.make_async_copy(ybuf.at[0], ys_hbm.at[pl.ds(jb * blk, blk), :], osem.at[0])
            cp.start()
            cp.wait()
            return c

        lax.fori_loop(used, nb_total, fill, 0)


def _moe_ffn(eblk, enb, xs, w_gate, w_up, w_down, layer):
    n_rows = xs.shape[0]
    n_exp, d, ff = w_gate.shape[1], w_gate.shape[2], w_gate.shape[3]
    blk = MOE_BLOCK
    wmap = lambda e, eb, en: (layer, e, 0, 0)
    grid_spec = pltpu.PrefetchScalarGridSpec(
        num_scalar_prefetch=2,
        grid=(n_exp,),
        in_specs=[pl.BlockSpec(memory_space=pl.ANY),
                  pl.BlockSpec((1, 1, d, ff), wmap),
                  pl.BlockSpec((1, 1, d, ff), wmap),
                  pl.BlockSpec((1, 1, ff, d), wmap)],
        out_specs=pl.BlockSpec(memory_space=pl.ANY),
        scratch_shapes=[pltpu.VMEM((FFN_RING, blk, d // 2), U32),
                        pltpu.VMEM((FFN_RING, blk, d), F32),
                        pltpu.VMEM((d, ff), BF16),
                        pltpu.VMEM((d, ff), BF16),
                        pltpu.VMEM((ff, d), BF16),
                        pltpu.SemaphoreType.DMA((FFN_RING,)),
                        pltpu.SemaphoreType.DMA((FFN_RING,))],
    )
    return pl.pallas_call(
        functools.partial(_ffn_kernel, blk=blk, nb_total=n_rows // blk, n_exp=n_exp),
        out_shape=jax.ShapeDtypeStruct((n_rows, d), F32),
        grid_spec=grid_spec,
        compiler_params=_cparams("arbitrary"),
        name="moe_ffn",
    )(eblk, enb, xs, w_gate, w_up, w_down)


def _gather_ln_kernel(dest_ref, ys_hbm, rg_ref, x_ref, g_ref, lng_ref, lnb_ref, o_ref, ybuf, sem,
                      *, alpha, tm, nt, n_steps):
    step = pl.program_id(0) * nt + pl.program_id(1)
    slot = step % 2

    def start_gather(tile_idx, sl):
        base = tile_idx * (tm * MOE_TOPK)
        for r in range(tm):
            for k in range(MOE_TOPK):
                pltpu.make_async_copy(ys_hbm.at[pl.ds(dest_ref[base + r * MOE_TOPK + k], 1), :],
                                      ybuf.at[sl, k, pl.ds(r, 1), :], sem.at[sl]).start()

    @pl.when(step == 0)
    def _():
        start_gather(0, 0)

    for sl in range(2):
        @pl.when(jnp.logical_and(step + 1 < n_steps, slot == 1 - sl))
        def _(sl=sl):
            start_gather(step + 1, sl)

    for k in range(MOE_TOPK):
        pltpu.make_async_copy(ys_hbm.at[pl.ds(0, tm), :], ybuf.at[slot, k], sem.at[slot]).wait()

    rg = rg_ref[0]
    y = ybuf[slot, 0] * rg[:, 0:1] + ybuf[slot, 1] * rg[:, 1:2]
    r = alpha * x_ref[0] + (1.0 + g_ref[0]) * y
    o_ref[0] = _layer_norm_rows(r, lng_ref[...], lnb_ref[...])


def _gather_combine_ln(dest_flat, ys, route_gates, x, gate, ln_g, ln_b, alpha, name):
    b, s, d = x.shape
    tm = _tile(s, 256)
    nt = s // tm
    row = lambda bi, i, dst: (bi, i, 0)
    const = lambda bi, i, dst: (0, 0)
    grid_spec = pltpu.PrefetchScalarGridSpec(
        num_scalar_prefetch=1,
        grid=(b, nt),
        in_specs=[pl.BlockSpec(memory_space=pl.ANY),
                  pl.BlockSpec((1, tm, LANES), row),
                  pl.BlockSpec((1, tm, d), row),
                  pl.BlockSpec((1, 1, d), lambda bi, i, dst: (bi, 0, 0)),
                  pl.BlockSpec((1, d), const),
                  pl.BlockSpec((1, d), const)],
        out_specs=pl.BlockSpec((1, tm, d), row),
        scratch_shapes=[pltpu.VMEM((2, MOE_TOPK, tm, d), F32), pltpu.SemaphoreType.DMA((2,))],
    )
    return pl.pallas_call(
        functools.partial(_gather_ln_kernel, alpha=alpha, tm=tm, nt=nt, n_steps=b * nt),
        out_shape=jax.ShapeDtypeStruct((b, s, d), F32),
        grid_spec=grid_spec,
        compiler_params=_cparams("arbitrary", "arbitrary"),
        name=name,
    )(dest_flat, ys, route_gates, x, gate, ln_g.reshape(1, d), ln_b.reshape(1, d))


def _gdn_kernel(q_ref, k_ref, v_ref, z_ref, bg_ref, cg_ref, hh_ref, ba_ref,
                cw_ref, alog_ref, dtb_ref, nw_ref, scw_ref,
                o_ref,
                qa_ref, ka_ref, va_ref, hist_ref, hist2_ref, gate_ref, state_ref,
                *, heads, ts, gw):
    L = GDN_CHUNK
    dk = GDN_HEAD_DIM

    @pl.when(pl.program_id(1) == 0)
    def _():
        hist_ref[...] = jnp.zeros_like(hist_ref)
        hist2_ref[...] = jnp.zeros_like(hist2_ref)
        state_ref[...] = jnp.zeros_like(state_ref)

    for idx, (src, dst) in enumerate(((q_ref, qa_ref), (k_ref, ka_ref), (v_ref, va_ref))):
        x = src[0]
        y = _causal_conv(x, hist_ref[idx], cw_ref[:, idx * gw:(idx + 1) * gw], GDN_CONV)
        hist_ref[idx] = x[ts - SUBLANES:ts, :]
        dst[...] = _silu(y)

    c = cg_ref[0] * hh_ref[0]
    yb = _causal_conv(c, hist2_ref[...], scw_ref[...], SC_CONV)
    hist2_ref[...] = c[ts - SUBLANES:ts, :]
    o_ref[0, :, gw:2 * gw] = bg_ref[0] * yb

    ba = ba_ref[0]
    gate_ref[0] = _sigmoid(ba)
    gate_ref[1] = -jnp.exp(alog_ref[...]) * _softplus(ba + dtb_ref[...])

    ii = lax.broadcasted_iota(I32, (L, L), 0)
    jj = lax.broadcasted_iota(I32, (L, L), 1)
    tri_incl = (ii >= jj).astype(F32)
    nchunk = ts // L
    pairs = [(c, h) for c in range(nchunk) for h in range(heads)]

    def tiles(ref, lead=()):
        return jnp.stack([ref[lead + (slice(c * L, (c + 1) * L), slice(h * dk, (h + 1) * dk))]
                          for c, h in pairs], axis=0)

    def bmm(a, b):
        return jnp.einsum('bij,bjk->bik', a.astype(BF16), b.astype(BF16), preferred_element_type=F32)

    def bmm_nt(a, b):
        return jnp.einsum('bik,bjk->bij', a.astype(BF16), b.astype(BF16), preferred_element_type=F32)

    qh = tiles(qa_ref)
    kh = tiles(ka_ref)
    vh = tiles(va_ref)
    beta = jnp.stack([gate_ref[0, c * L:(c + 1) * L, h:h + 1] for c, h in pairs], axis=0)
    cums = [jnp.dot(tri_incl, gate_ref[1, c * L:(c + 1) * L, :], precision=HIGHEST,
                    preferred_element_type=F32) for c in range(nchunk)]
    cums_t = [cm.T for cm in cums]
    gc = jnp.stack([cums[c][:, heads + h:heads + h + 1] for c, h in pairs], axis=0)
    gl = gc[:, L - 1:L, :]
    egc = jnp.exp(gc)
    qn = qh * lax.rsqrt(jnp.sum(qh * qh, -1, keepdims=True) + 1e-6)
    kn = kh * lax.rsqrt(jnp.sum(kh * kh, -1, keepdims=True) + 1e-6)
    qc = qn * (dk ** -0.5)
    kb = kn * beta

    nb2 = len(pairs) // 2

    def pair(x):
        return x.reshape(nb2, 2 * L, x.shape[-1])

    i2 = lax.broadcasted_iota(I32, (2 * L, 2 * L), 0)
    j2 = lax.broadcasted_iota(I32, (2 * L, 2 * L), 1)
    same = (i2 < L) == (j2 < L)
    causal2 = same & (i2 >= j2)
    strict2 = same & (i2 > j2)
    eye2 = (i2 == j2).astype(F32)
    gc2, beta2, egc2, kn2, kb2 = pair(gc), pair(beta), pair(egc), pair(kn), pair(kb)
    gr2 = jnp.stack([jnp.concatenate([cums_t[c][heads + h:heads + h + 1, :],
                                      cums_t[c][heads + h + 1:heads + h + 2, :]], axis=1)
                     for c in range(nchunk) for h in range(0, heads, 2)], axis=0)
    decay2 = jnp.where(causal2, jnp.exp(jnp.where(causal2, gc2 - gr2, 0.0)), 0.0)
    m = jnp.where(strict2, bmm_nt(kb2, kn2) * decay2, 0.0)
    t_inv = eye2 - m
    mp = m
    for _ in range(5):
        mp = bmm(mp, mp)
        t_inv = t_inv + bmm(t_inv, mp)
    sol = bmm(t_inv, jnp.concatenate([pair(vh) * beta2, kb2 * egc2], axis=2))
    u2 = sol[:, :, 0:dk]
    w2 = sol[:, :, dk:2 * dk]
    qk2 = jnp.where(causal2, bmm_nt(pair(qc), kn2) * decay2, 0.0)
    qg2 = pair(qc * egc)
    k_dec2 = pair(kn * jnp.exp(gl - gc))
    egl = jnp.exp(gl)
    zero_blk = jnp.zeros((L, dk), F32)

    def own_blocks(x):
        return jnp.concatenate([x[0:L, 0:dk], x[L:2 * L, dk:2 * dk]], axis=0)

    state = [state_ref[h] for h in range(heads)]
    outs = []
    for p in range(nb2):
        ba, bb = 2 * p, 2 * p + 1
        ha, hb = pairs[ba][1], pairs[bb][1]
        s_cat = jnp.concatenate([state[ha], state[hb]], axis=1)
        v_new = u2[p] - own_blocks(_dot(w2[p], s_cat))
        outs.append(own_blocks(_dot(qg2[p], s_cat)) + _dot(qk2[p], v_new))
        kd = k_dec2[p]
        k_bd = jnp.concatenate([jnp.concatenate([kd[0:L], zero_blk], axis=1),
                                jnp.concatenate([zero_blk, kd[L:2 * L]], axis=1)], axis=0)
        upd = _dot(k_bd.T, v_new)
        state[ha] = state[ha] * egl[ba] + upd[0:dk]
        state[hb] = state[hb] * egl[bb] + upd[dk:2 * dk]
    for h in range(heads):
        state_ref[h] = state[h]

    o = jnp.stack(outs, axis=0).reshape(len(pairs), L, dk)
    on = o * lax.rsqrt(jnp.mean(o * o, -1, keepdims=True) + 1e-6) * nw_ref[...]
    y = on * _silu(tiles(z_ref, (0,)))
    for b, (c, h) in enumerate(pairs):
        o_ref[0, c * L:(c + 1) * L, h * dk:(h + 1) * dk] = y[b]


def _gdn_shortconv(proj, conv_w, a_log, dt_bias, norm_w, sc_w, heads):
    b, s, _ = proj.shape
    gw = heads * GDN_HEAD_DIM
    ts = _tile(s, 256)
    alog_p = jnp.zeros((1, LANES), F32).at[0, heads:2 * heads].set(a_log)
    dtb_p = jnp.zeros((1, LANES), F32).at[0, heads:2 * heads].set(dt_bias)
    col = lambda cb: (lambda bi, i: (bi, i, cb))
    const = lambda bi, i: (0, 0)
    kern = functools.partial(_gdn_kernel, heads=heads, ts=ts, gw=gw)
    return pl.pallas_call(
        kern,
        out_shape=jax.ShapeDtypeStruct((b, s, 2 * gw), F32),
        grid=(b, s // ts),
        in_specs=[pl.BlockSpec((1, ts, gw), col(cb)) for cb in range(7)]
        + [pl.BlockSpec((1, ts, LANES), col(7 * gw // LANES)),
           pl.BlockSpec((GDN_CONV, 3 * gw), const),
           pl.BlockSpec((1, LANES), const),
           pl.BlockSpec((1, LANES), const),
           pl.BlockSpec((1, GDN_HEAD_DIM), const),
           pl.BlockSpec((SC_CONV, gw), const)],
        out_specs=pl.BlockSpec((1, ts, 2 * gw), lambda bi, i: (bi, i, 0)),
        scratch_shapes=[pltpu.VMEM((ts, gw), F32),
                        pltpu.VMEM((ts, gw), F32),
                        pltpu.VMEM((ts, gw), F32),
                        pltpu.VMEM((3, SUBLANES, gw), F32),
                        pltpu.VMEM((SUBLANES, gw), F32),
                        pltpu.VMEM((2, ts, LANES), F32),
                        pltpu.VMEM((heads, GDN_HEAD_DIM, GDN_HEAD_DIM), F32)],
        compiler_params=_cparams("parallel", "arbitrary"),
        name="gdn_shortconv",
    )(proj, proj, proj, proj, proj, proj, proj, proj,
      conv_w, alog_p, dtb_p, norm_w.reshape(1, GDN_HEAD_DIM), sc_w)


def _rope_kernel(pos_ref, inv_ref, qpe_ref, kpe_ref, qo_ref, ko_ref, *, heads):
    ang = pos_ref[0].astype(F32) * inv_ref[...]
    lane = lax.broadcasted_iota(I32, ang.shape, 1)
    half = MLA_ROPE // 2
    cos = jnp.where(lane < MLA_ROPE, jnp.cos(ang), 0.0)
    sin = jnp.sin(ang)
    sgn = jnp.where(lane < half, -sin, jnp.where(lane < MLA_ROPE, sin, 0.0))

    def rot(x):
        swapped = jnp.where(lane < half, pltpu.roll(x, LANES - half, 1), pltpu.roll(x, half, 1))
        return x * cos + swapped * sgn

    for h in range(heads):
        cols = slice(h * LANES, (h + 1) * LANES)
        qo_ref[0, :, cols] = rot(qpe_ref[0, :, cols]).astype(BF16)
    ko_ref[0] = rot(kpe_ref[0]).astype(BF16)


def _rope(positions, qpe, heads, cproj, kpe_block):
    b, s = positions.shape
    ts = _tile(s, 512)
    half = MLA_ROPE // 2
    inv = ROPE_THETA ** (-jnp.arange(half, dtype=F32) * (2.0 / MLA_ROPE))
    inv_tab = jnp.zeros((1, LANES), F32).at[0, 0:half].set(inv).at[0, half:MLA_ROPE].set(inv)
    w = heads * LANES
    return pl.pallas_call(
        functools.partial(_rope_kernel, heads=heads),
        out_shape=(jax.ShapeDtypeStruct((b, s, w), BF16), jax.ShapeDtypeStruct((b, s, LANES), BF16)),
        grid=(b, s // ts),
        in_specs=[pl.BlockSpec((1, ts, 1), lambda bi, i: (bi, i, 0)),
                  pl.BlockSpec((1, LANES), lambda bi, i: (0, 0)),
                  pl.BlockSpec((1, ts, w), lambda bi, i: (bi, i, 0)),
                  pl.BlockSpec((1, ts, LANES), lambda bi, i: (bi, i, kpe_block))],
        out_specs=(pl.BlockSpec((1, ts, w), lambda bi, i: (bi, i, 0)),
                   pl.BlockSpec((1, ts, LANES), lambda bi, i: (bi, i, 0))),
        compiler_params=_cparams("parallel", "parallel"),
        name="mla_rope",
    )(positions.reshape(b, s, 1), inv_tab, qpe, cproj)


def _attn_kernel(qn_ref, qp_ref, kn_ref, kp_ref, v_ref, o_ref, kcat_ref, s_ref, *, tq, hq, nq, scale):
    qi = pl.program_id(2)

    @pl.when(qi == 0)
    def _():
        kcat_ref[:, 0:MLA_NOPE] = kn_ref[0]
        kcat_ref[:, MLA_NOPE:MLA_NOPE + LANES] = kp_ref[0]

    qcat = jnp.concatenate([qn_ref[0], qp_ref[0]], axis=1)
    tri = lax.broadcasted_iota(I32, (hq, hq), 0) >= lax.broadcasted_iota(I32, (hq, hq), 1)

    for qv in range(nq):
        @pl.when(qi == qv)
        def _(qv=qv):
            for h in range(tq // hq):
                rows = slice(h * hq, (h + 1) * hq)
                nk = qv * tq + (h + 1) * hq
                for c0 in range(0, nk, ATTN_KEY_CHUNK):
                    c1 = min(c0 + ATTN_KEY_CHUNK, nk)
                    s = lax.dot_general(qcat[rows], kcat_ref[c0:c1, :], (((1,), (1,)), ((), ())),
                                        preferred_element_type=F32) * scale
                    if c1 == nk:
                        w = c1 - c0
                        last = jnp.where(tri, s[:, w - hq:], -jnp.inf)
                        s = last if w == hq else jnp.concatenate([s[:, :w - hq], last], axis=1)
                    s_ref[rows, c0:c1] = s
                    mb = jnp.max(s, -1, keepdims=True)
                    m = mb if c0 == 0 else jnp.maximum(m, mb)
                l = jnp.zeros((hq, 1), F32)
                acc = jnp.zeros((hq, MLA_V), F32)
                for c0 in range(0, nk, hq):
                    p = jnp.exp(s_ref[rows, c0:c0 + hq] - m)
                    l = l + jnp.sum(p, -1, keepdims=True)
                    acc = acc + jnp.dot(p.astype(BF16), v_ref[0, c0:c0 + hq, :], preferred_element_type=F32)
                o_ref[0, rows, :] = acc / l


def _attention(qn, qpe, kvup, kpe, heads):
    b, s, _ = qn.shape
    tq = _tile(s, 1024)
    hq = _tile(tq, 256)
    nq = s // tq
    scale = (MLA_NOPE + MLA_ROPE) ** -0.5
    return pl.pallas_call(
        functools.partial(_attn_kernel, tq=tq, hq=hq, nq=nq, scale=scale),
        out_shape=jax.ShapeDtypeStruct((b, s, heads * MLA_V), F32),
        grid=(b, heads, nq),
        in_specs=[pl.BlockSpec((1, tq, MLA_NOPE), lambda bi, h, i: (bi, i, h)),
                  pl.BlockSpec((1, tq, LANES), lambda bi, h, i: (bi, i, h)),
                  pl.BlockSpec((1, s, MLA_NOPE), lambda bi, h, i: (bi, 0, h)),
                  pl.BlockSpec((1, s, LANES), lambda bi, h, i: (bi, 0, 0)),
                  pl.BlockSpec((1, s, MLA_V), lambda bi, h, i: (bi, 0, heads + h))],
        out_specs=pl.BlockSpec((1, tq, MLA_V), lambda bi, h, i: (bi, i, h)),
        scratch_shapes=[pltpu.VMEM((s, MLA_NOPE + LANES), BF16), pltpu.VMEM((tq, s), F32)],
        compiler_params=_cparams("parallel", "parallel", "arbitrary"),
        name="mla_attention",
    )(qn, qpe, kvup, kpe, kvup)


def _hyb_in_weight(w_in, heads):
    d = w_in.shape[0]
    gw = heads * GDN_HEAD_DIM
    main = jnp.concatenate([w_in[:, :4 * gw], w_in[:, 4 * gw + 2 * heads:]], axis=1)
    gates = w_in[:, 4 * gw:4 * gw + 2 * heads]
    n = main.shape[1] + LANES
    n_pad = -(-n // 768) * 768
    tail = jnp.zeros((d, n_pad - main.shape[1]), w_in.dtype).at[:, :2 * heads].set(gates)
    return jnp.concatenate([main, tail], axis=1).astype(BF16)


def _mla_in_weight(w_in):
    d = w_in.shape[0]
    pad = jnp.zeros((d, LANES - MLA_ROPE), w_in.dtype)
    return jnp.concatenate([w_in, pad], axis=1).astype(BF16)


def _mla_uq_weights(w_uq, heads):
    r = w_uq.shape[0]
    w = w_uq.reshape(r, heads, MLA_NOPE + MLA_ROPE)
    nope = w[:, :, :MLA_NOPE].reshape(r, heads * MLA_NOPE)
    pe = jnp.concatenate([w[:, :, MLA_NOPE:], jnp.zeros((r, heads, LANES - MLA_ROPE), w.dtype)], axis=2)
    return nope.astype(BF16), pe.reshape(r, heads * LANES).astype(BF16)


def _mla_ukv_weight(w_ukv, heads):
    r = w_ukv.shape[0]
    w = w_ukv.reshape(r, heads, MLA_NOPE + MLA_V)
    return jnp.concatenate([w[:, :, :MLA_NOPE].reshape(r, heads * MLA_NOPE),
                            w[:, :, MLA_NOPE:].reshape(r, heads * MLA_V)], axis=1).astype(BF16)


def _router_weight(wr_g, br_g, wr_e, br_e):
    d = wr_g.shape[0]
    n = wr_g.shape[1] + wr_e.shape[1]
    w = jnp.zeros((d, LANES), F32).at[:, :n].set(jnp.concatenate([wr_g, wr_e], axis=1))
    bias = jnp.zeros((1, LANES), F32).at[0, :n].set(jnp.concatenate([br_g, br_e]))
    w_hi = w.astype(BF16)
    w_lo = (w - w_hi.astype(F32)).astype(BF16)
    return jnp.concatenate([w_hi, w_lo], axis=1), bias


def kernel(x, c, positions, ada_w, ada_b, ln_g, ln_b, hyb_w_in, gdn_conv_w, gdn_a_log, gdn_dt_bias,
           gdn_norm_w, sc_conv_w, hyb_w_out, mla_w_in, mla_q_norm, mla_kv_norm, mla_w_uq, mla_w_ukv,
           mla_w_out, moe_router_g, moe_bias_g, moe_router_e, moe_bias_e, moe_w_gate, moe_w_up,
           moe_w_down):
    b, s, d = x.shape
    depth = ada_w.shape[0]
    alpha = (2.0 * depth) ** 0.25
    gdn_heads = d // (2 * GDN_HEAD_DIM)
    mla_heads = mla_w_out.shape[1] // MLA_V
    n_groups = moe_router_g.shape[2]
    n_exp = moe_router_e.shape[2]
    per_group = n_exp // n_groups

    mod = _modulation(c, ada_w, ada_b)
    t = b * s
    xs = jnp.zeros((t * MOE_TOPK + n_exp * MOE_BLOCK, d // 2), U32)
    for layer in range(depth):
        sh1, sc1, g1, sh2, sc2, g2 = [mod[layer, :, j * d:(j + 1) * d].reshape(b, 1, d) for j in range(6)]
        i = layer // 2
        wr, br = _router_weight(moe_router_g[layer], moe_bias_g[layer], moe_router_e[layer], moe_bias_e[layer])
        if layer % 2 == 0:
            proj = _mod_matmul(x, sc1, sh1, _hyb_in_weight(hyb_w_in[i], gdn_heads), "hyb_in_proj")
            y = _gdn_shortconv(proj, gdn_conv_w[i], gdn_a_log[i], gdn_dt_bias[i], gdn_norm_w[i],
                               sc_conv_w[i], gdn_heads)
            w_out = hyb_w_out[i]
        else:
            cproj = _mod_matmul(x, sc1, sh1, _mla_in_weight(mla_w_in[i]), "mla_in_proj")
            w_qn, w_qpe = _mla_uq_weights(mla_w_uq[i], mla_heads)
            qn = _rms_matmul(cproj, 0, MLA_Q_RANK, mla_q_norm[i], w_qn, BF16, "mla_q_nope_up")
            qpe_raw = _rms_matmul(cproj, 0, MLA_Q_RANK, mla_q_norm[i], w_qpe, F32, "mla_q_rope_up")
            kvup = _rms_matmul(cproj, 1, MLA_KV_RANK, mla_kv_norm[i], _mla_ukv_weight(mla_w_ukv[i], mla_heads),
                               BF16, "mla_kv_up")
            qpe, kpe = _rope(positions, qpe_raw, mla_heads, cproj, (MLA_Q_RANK + MLA_KV_RANK) // LANES)
            y = _attention(qn, qpe, kvup, kpe, mla_heads)
            w_out = mla_w_out[i]
        x, hin2, ids, gates = _out_ln_route(y, w_out.astype(BF16), x, g1, ln_g[layer, 0], ln_b[layer, 0],
                                            sc2, sh2, wr, br, alpha, n_groups, per_group,
                                            "mixer_out_ln_route")
        dest, meta = _dispatch(ids.reshape(t, LANES))
        dest_flat = dest[:, :MOE_TOPK].reshape(-1)
        eblk = meta[0, :n_exp] // MOE_BLOCK
        enb = meta[1, :n_exp]
        xs = _scatter_rows(dest_flat, hin2.reshape(t, d // 2), xs)
        ys = _moe_ffn(eblk, enb, xs, moe_w_gate, moe_w_up, moe_w_down, layer)
        x = _gather_combine_ln(dest_flat, ys, gates, x, g2, ln_g[layer, 1], ln_b[layer, 1], alpha,
                               "moe_gather_combine_ln")
    return x
```

```python
import functools

import jax
import jax.numpy as jnp
from jax import lax
from jax.experimental import pallas as pl
from jax.experimental.pallas import tpu as pltpu

F32, BF16, I32, U32 = jnp.float32, jnp.bfloat16, jnp.int32, jnp.uint32
HIGHEST = lax.Precision.HIGHEST

LANES = 128
SUBLANES = 8
VMEM_LIMIT_BYTES = 56 * 1024 * 1024

GDN_HEAD_DIM = 128
GDN_CONV = 4
GDN_CHUNK = 64
SC_CONV = 3
MLA_NOPE = 128
MLA_ROPE = 64
MLA_V = 128
MLA_Q_RANK = 512
MLA_KV_RANK = 512
ROPE_THETA = 10000.0
MOE_TOPK = 2
MOE_BLOCK = 128
FFN_RING = 6
ATTN_KEY_CHUNK = 512


def _tile(n, pref):
    if n <= pref:
        return n
    for t in range(pref, 0, -LANES):
        if n % t == 0:
            return t
    return n


def _cparams(*sem):
    return pltpu.CompilerParams(dimension_semantics=sem, vmem_limit_bytes=VMEM_LIMIT_BYTES)


def _sigmoid(x):
    return 1.0 / (1.0 + jnp.exp(-x))


def _silu(x):
    return x * _sigmoid(x)


def _softplus(x):
    return jnp.maximum(x, 0.0) + jnp.log(1.0 + jnp.exp(-jnp.abs(x)))


def _pack_bf16_pair(lo, hi):
    lo_bits = lax.bitcast_convert_type(lo.astype(BF16).astype(F32), U32)
    hi_bits = lax.bitcast_convert_type(hi.astype(BF16).astype(F32), U32)
    return (lo_bits >> 16) | (hi_bits & jnp.uint32(0xFFFF0000))


def _unpack_bf16_pair(w):
    lo = lax.bitcast_convert_type(w << 16, F32)
    hi = lax.bitcast_convert_type(w & jnp.uint32(0xFFFF0000), F32)
    return lo.astype(BF16), hi.astype(BF16)


def _dot(a, b):
    return jnp.dot(a.astype(BF16), b.astype(BF16), preferred_element_type=F32)


def _dot_nt(a, b):
    return lax.dot_general(a.astype(BF16), b.astype(BF16), (((1,), (1,)), ((), ())),
                           preferred_element_type=F32)


def _causal_conv(x, hist, w, taps):
    rows, width = x.shape
    g = rows // SUBLANES
    x3 = x.reshape(g, SUBLANES, width)
    sub = lax.broadcasted_iota(I32, (1, SUBLANES, width), 1)
    y = None
    for j in range(taps - 1):
        shift = taps - 1 - j
        rot = pltpu.roll(x3, shift, 1)
        prev = jnp.concatenate([pltpu.roll(hist, shift, 0)[None], rot[:g - 1]], axis=0)
        term = w[j:j + 1] * jnp.where(sub < shift, prev, rot)
        y = term if y is None else y + term
    return (y + w[taps - 1:taps] * x3).reshape(rows, width)


def _mod_kernel(c_ref, w_ref, b_ref, o_ref):
    c = c_ref[...]
    o_ref[0] = _dot(_silu(c), w_ref[0]) + b_ref[0]


def _modulation(c, ada_w, ada_b):
    depth, d, n = ada_w.shape
    b = c.shape[0]
    rows = -(-b // SUBLANES) * SUBLANES
    cp = jnp.zeros((rows, d), F32).at[:b].set(c)
    tn = _tile(n, 1024)
    out = pl.pallas_call(
        _mod_kernel,
        out_shape=jax.ShapeDtypeStruct((depth, rows, n), F32),
        grid=(depth, n // tn),
        in_specs=[pl.BlockSpec((rows, d), lambda l, j: (0, 0)),
                  pl.BlockSpec((1, d, tn), lambda l, j: (l, 0, j)),
                  pl.BlockSpec((1, 1, tn), lambda l, j: (l, 0, j))],
        out_specs=pl.BlockSpec((1, rows, tn), lambda l, j: (l, 0, j)),
        compiler_params=_cparams("parallel", "parallel"),
        name="adaln_mod",
    )(cp, ada_w, ada_b.reshape(depth, 1, n))
    return out[:, :b]


def _modmm_kernel(x_ref, sc_ref, sh_ref, w_ref, o_ref, xb_ref):
    @pl.when(pl.program_id(2) == 0)
    def _():
        xb_ref[...] = (x_ref[0] * (1.0 + sc_ref[0]) + sh_ref[0]).astype(BF16)

    o_ref[0] = jnp.dot(xb_ref[...], w_ref[...], preferred_element_type=F32)


def _mod_matmul(x, sc, sh, w, name):
    b, s, d = x.shape
    n = w.shape[1]
    tm = _tile(s, 1024)
    tn = _tile(n, 768)
    return pl.pallas_call(
        _modmm_kernel,
        out_shape=jax.ShapeDtypeStruct((b, s, n), F32),
        grid=(b, s // tm, n // tn),
        in_specs=[pl.BlockSpec((1, tm, d), lambda bi, i, j: (bi, i, 0)),
                  pl.BlockSpec((1, 1, d), lambda bi, i, j: (bi, 0, 0)),
                  pl.BlockSpec((1, 1, d), lambda bi, i, j: (bi, 0, 0)),
                  pl.BlockSpec((d, tn), lambda bi, i, j: (0, j))],
        out_specs=pl.BlockSpec((1, tm, tn), lambda bi, i, j: (bi, i, j)),
        scratch_shapes=[pltpu.VMEM((tm, d), BF16)],
        compiler_params=_cparams("parallel", "parallel", "arbitrary"),
        name=name,
    )(x, sc, sh, w)


def _rmsmm_kernel(x_ref, nw_ref, w_ref, o_ref, xb_ref):
    @pl.when(pl.program_id(2) == 0)
    def _():
        x = x_ref[0]
        xb_ref[...] = (x * lax.rsqrt(jnp.mean(x * x, -1, keepdims=True) + 1e-6) * nw_ref[...]).astype(BF16)

    o_ref[0] = jnp.dot(xb_ref[...], w_ref[...], preferred_element_type=F32).astype(o_ref.dtype)


def _rms_matmul(x, col_block, k, norm_w, w, out_dtype, name):
    b, s, _ = x.shape
    n = w.shape[1]
    tm = _tile(s, 1024)
    tn = _tile(n, 1024)
    return pl.pallas_call(
        _rmsmm_kernel,
        out_shape=jax.ShapeDtypeStruct((b, s, n), out_dtype),
        grid=(b, s // tm, n // tn),
        in_specs=[pl.BlockSpec((1, tm, k), lambda bi, i, j: (bi, i, col_block)),
                  pl.BlockSpec((1, k), lambda bi, i, j: (0, 0)),
                  pl.BlockSpec((k, tn), lambda bi, i, j: (0, j))],
        out_specs=pl.BlockSpec((1, tm, tn), lambda bi, i, j: (bi, i, j)),
        scratch_shapes=[pltpu.VMEM((tm, k), BF16)],
        compiler_params=_cparams("parallel", "parallel", "arbitrary"),
        name=name,
    )(x, norm_w.reshape(1, k), w)


def _layer_norm_rows(r, g, b):
    mu = jnp.mean(r, -1, keepdims=True)
    var = jnp.mean(jnp.square(r - mu), -1, keepdims=True)
    return (r - mu) * lax.rsqrt(var + 1e-5) * g + b


def _outln_kernel(y_ref, w_ref, xres_ref, g_ref, lng_ref, lnb_ref, sc_ref, sh_ref, wr_ref, br_ref,
                  xo_ref, hin_ref, ids_ref, gates_ref, *, alpha, n_groups, per_group, parts):
    tp = y_ref.shape[1] // parts
    for part in range(parts):
        _outln_rows(slice(part * tp, (part + 1) * tp), y_ref, w_ref, xres_ref, g_ref, lng_ref, lnb_ref,
                    sc_ref, sh_ref, wr_ref, br_ref, xo_ref, hin_ref, ids_ref, gates_ref,
                    alpha=alpha, n_groups=n_groups, per_group=per_group)


def _outln_rows(rows, y_ref, w_ref, xres_ref, g_ref, lng_ref, lnb_ref, sc_ref, sh_ref, wr_ref, br_ref,
                xo_ref, hin_ref, ids_ref, gates_ref, *, alpha, n_groups, per_group):
    y = jnp.dot(y_ref[0, rows, :].astype(BF16), w_ref[...], preferred_element_type=F32)
    r = alpha * xres_ref[0, rows, :] + (1.0 + g_ref[0]) * y
    xn = _layer_norm_rows(r, lng_ref[...], lnb_ref[...])
    xo_ref[0, rows, :] = xn
    hin = xn * (1.0 + sc_ref[0]) + sh_ref[0]
    half = hin.shape[1] // 2
    hin_ref[0, rows, :] = _pack_bf16_pair(hin[:, :half], hin[:, half:])
    h_hi = hin.astype(BF16)
    h_lo = (hin - h_hi.astype(F32)).astype(BF16)
    ph = jnp.dot(h_hi, wr_ref[...], preferred_element_type=F32)
    plo = jnp.dot(h_lo, wr_ref[...], preferred_element_type=F32)
    logits = (ph[:, :LANES] + ph[:, LANES:]) + (plo[:, :LANES] + plo[:, LANES:]) + br_ref[...]
    lane = lax.broadcasted_iota(I32, logits.shape, 1)
    big = jnp.int32(4 * LANES)
    neg = jnp.float32(-jnp.inf)
    n_exp = n_groups * per_group
    gmask = lane < n_groups
    lg = jnp.where(gmask, logits, neg)
    mg = jnp.max(lg, -1, keepdims=True)
    grp = jnp.min(jnp.where(gmask & (lg == mg), lane, big), -1, keepdims=True)
    pg_sel = 1.0 / jnp.sum(jnp.where(gmask, jnp.exp(lg - mg), 0.0), -1, keepdims=True)
    lo = n_groups + grp * per_group
    emask = (lane >= lo) & (lane < lo + per_group) & (lane < n_groups + n_exp)
    le = jnp.where(emask, logits, neg)
    me = jnp.max(le, -1, keepdims=True)
    ee = jnp.where(emask, jnp.exp(le - me), 0.0)
    p = ee / jnp.sum(ee, -1, keepdims=True)
    pm = jnp.where(emask, p, -1.0)
    p1 = jnp.max(pm, -1, keepdims=True)
    i1 = jnp.min(jnp.where(emask & (pm == p1), lane, big), -1, keepdims=True)
    pm2 = jnp.where(lane == i1, -1.0, pm)
    p2 = jnp.max(pm2, -1, keepdims=True)
    i2 = jnp.min(jnp.where(emask & (lane != i1) & (pm2 == p2), lane, big), -1, keepdims=True)
    den = p1 + p2
    g1 = pg_sel * p1 / den
    g2 = pg_sel * p2 / den
    ids_ref[0, rows, :] = jnp.where(lane == 0, i1 - n_groups, jnp.where(lane == 1, i2 - n_groups, 0))
    gates_ref[0, rows, :] = jnp.where(lane == 0, g1, jnp.where(lane == 1, g2, 0.0))


def _out_ln_route(y, w, xres, gate, ln_g, ln_b, sc, sh, wr, br, alpha, n_groups, per_group, name):
    b, s, k = y.shape
    d = w.shape[1]
    tm = _tile(s, 512)
    parts = 2 if tm % (2 * SUBLANES) == 0 else 1
    kern = functools.partial(_outln_kernel, alpha=alpha, n_groups=n_groups, per_group=per_group, parts=parts)
    row = lambda bi, i: (bi, i, 0)
    per_b = lambda bi, i: (bi, 0, 0)
    const = lambda bi, i: (0, 0)
    return pl.pallas_call(
        kern,
        out_shape=(jax.ShapeDtypeStruct((b, s, d), F32), jax.ShapeDtypeStruct((b, s, d // 2), U32),
                   jax.ShapeDtypeStruct((b, s, LANES), I32), jax.ShapeDtypeStruct((b, s, LANES), F32)),
        grid=(b, s // tm),
        in_specs=[pl.BlockSpec((1, tm, k), row),
                  pl.BlockSpec((k, d), const),
                  pl.BlockSpec((1, tm, d), row),
                  pl.BlockSpec((1, 1, d), per_b),
                  pl.BlockSpec((1, d), const),
                  pl.BlockSpec((1, d), const),
                  pl.BlockSpec((1, 1, d), per_b),
                  pl.BlockSpec((1, 1, d), per_b),
                  pl.BlockSpec((d, 2 * LANES), const),
                  pl.BlockSpec((1, LANES), const)],
        out_specs=(pl.BlockSpec((1, tm, d), row), pl.BlockSpec((1, tm, d // 2), row),
                   pl.BlockSpec((1, tm, LANES), row), pl.BlockSpec((1, tm, LANES), row)),
        compiler_params=_cparams("parallel", "parallel"),
        name=name,
    )(y, w, xres, gate, ln_g.reshape(1, d), ln_b.reshape(1, d), sc, sh, wr, br)


def _dispatch_kernel(ids_ref, dest_ref, meta_ref, rank_ref, *, t, tile, blk_shift):
    lane = lax.broadcasted_iota(I32, (tile, LANES), 1)
    ri = lax.broadcasted_iota(I32, (tile, tile), 0)
    ci = lax.broadcasted_iota(I32, (tile, tile), 1)
    before = (ri > ci).astype(BF16)

    def hits(rows):
        ids = ids_ref[rows, :]
        return lane == ids[:, 0:1], lane == ids[:, 1:2]

    def count(ti, run):
        rows = pl.ds(pl.multiple_of(ti * tile, tile), tile)
        h1, h2 = hits(rows)
        onehot = jnp.logical_or(h1, h2).astype(BF16)
        prefix = jnp.dot(before, onehot, preferred_element_type=F32) + run
        r1 = jnp.sum(jnp.where(h1, prefix, 0.0), -1, keepdims=True)
        r2 = jnp.sum(jnp.where(h2, prefix, 0.0), -1, keepdims=True)
        rank_ref[rows, :] = jnp.where(lane == 0, r1, jnp.where(lane == 1, r2, 0.0))
        return run + jnp.sum(onehot.astype(F32), 0, keepdims=True)

    counts = lax.fori_loop(0, t // tile, count, jnp.zeros((1, LANES), F32))
    cnt = jnp.broadcast_to(counts, (SUBLANES, LANES)).astype(I32)
    nblk = (cnt + ((1 << blk_shift) - 1)) >> blk_shift
    padded = nblk << blk_shift
    lane8 = lax.broadcasted_iota(I32, (SUBLANES, LANES), 1)
    incl = padded
    step = 1
    while step < LANES:
        incl = incl + jnp.where(lane8 >= step, pltpu.roll(incl, step, 1), 0)
        step *= 2
    pad_start = incl - padded
    row8 = lax.broadcasted_iota(I32, (SUBLANES, LANES), 0)
    meta_ref[...] = jnp.where(row8 == 0, pad_start, jnp.where(row8 == 1, nblk, cnt))
    start_f = pad_start[0:1, :].astype(F32)

    def place(ti, c):
        rows = pl.ds(pl.multiple_of(ti * tile, tile), tile)
        h1, h2 = hits(rows)
        rk = rank_ref[rows, :]
        d1 = jnp.sum(jnp.where(h1, start_f, 0.0), -1, keepdims=True) + rk[:, 0:1]
        d2 = jnp.sum(jnp.where(h2, start_f, 0.0), -1, keepdims=True) + rk[:, 1:2]
        dest_ref[rows, :] = jnp.where(lane == 0, d1, jnp.where(lane == 1, d2, 0.0)).astype(I32)
        return c

    lax.fori_loop(0, t // tile, place, 0)


def _dispatch(ids):
    t = ids.shape[0]
    tile = _tile(t, 256)
    blk_shift = MOE_BLOCK.bit_length() - 1
    assert (1 << blk_shift) == MOE_BLOCK
    return pl.pallas_call(
        functools.partial(_dispatch_kernel, t=t, tile=tile, blk_shift=blk_shift),
        out_shape=(jax.ShapeDtypeStruct((t, LANES), I32), jax.ShapeDtypeStruct((SUBLANES, LANES), I32)),
        grid=(1,),
        in_specs=[pl.BlockSpec((t, LANES), lambda i: (0, 0))],
        out_specs=(pl.BlockSpec((t, LANES), lambda i: (0, 0)), pl.BlockSpec((SUBLANES, LANES), lambda i: (0, 0))),
        scratch_shapes=[pltpu.VMEM((t, LANES), F32)],
        compiler_params=_cparams("arbitrary"),
        name="moe_dispatch",
    )(ids)


def _scatter_rows_kernel(dest_ref, hin_ref, xs_in, xs_out, stage, sem, *, tm, n_steps):
    del xs_in
    step = pl.program_id(0)
    base = step * (tm * MOE_TOPK)

    def wait_slot(sl):
        for _ in range(MOE_TOPK):
            pltpu.make_async_copy(stage.at[sl], xs_out.at[pl.ds(0, tm), :], sem.at[sl]).wait()

    for sl in range(2):
        @pl.when(step % 2 == sl)
        def _(sl=sl):
            @pl.when(step >= 2)
            def _():
                wait_slot(sl)

            stage[sl] = hin_ref[...]
            for r in range(tm):
                for k in range(MOE_TOPK):
                    pltpu.make_async_copy(stage.at[sl, pl.ds(r, 1), :],
                                          xs_out.at[pl.ds(dest_ref[base + r * MOE_TOPK + k], 1), :],
                                          sem.at[sl]).start(priority=k % 2)

            @pl.when(step == n_steps - 1)
            def _():
                if n_steps >= 2:
                    wait_slot(1 - sl)
                wait_slot(sl)


def _scatter_rows(dest_flat, hin2, xs_init):
    t, d = hin2.shape
    n_rows = xs_init.shape[0]
    tm = _tile(t, 256)
    grid_spec = pltpu.PrefetchScalarGridSpec(
        num_scalar_prefetch=1,
        grid=(t // tm,),
        in_specs=[pl.BlockSpec((tm, d), lambda i, dst: (i, 0)),
                  pl.BlockSpec(memory_space=pl.ANY)],
        out_specs=pl.BlockSpec(memory_space=pl.ANY),
        scratch_shapes=[pltpu.VMEM((2, tm, d), hin2.dtype), pltpu.SemaphoreType.DMA((2,))],
    )
    return pl.pallas_call(
        functools.partial(_scatter_rows_kernel, tm=tm, n_steps=t // tm),
        out_shape=jax.ShapeDtypeStruct((n_rows, d), hin2.dtype),
        grid_spec=grid_spec,
        input_output_aliases={2: 0},
        compiler_params=_cparams("arbitrary"),
        name="moe_scatter_rows",
    )(dest_flat, hin2, xs_init)


def _ffn_kernel(eblk_ref, enb_ref, xs_hbm, wg_ref, wu_ref, wd_ref, ys_hbm,
                xbuf, ybuf, wgb, wub, wdb, isem, osem, *, blk, nb_total, n_exp):
    e = pl.program_id(0)
    n = enb_ref[e]
    b0 = eblk_ref[e]
    used = eblk_ref[n_exp - 1] + enb_ref[n_exp - 1]

    def in_cp(gb):
        sl = gb % FFN_RING
        return pltpu.make_async_copy(xs_hbm.at[pl.ds(gb * blk, blk), :], xbuf.at[sl], isem.at[sl])

    def out_cp(gb):
        sl = gb % FFN_RING
        return pltpu.make_async_copy(ybuf.at[sl], ys_hbm.at[pl.ds(gb * blk, blk), :], osem.at[sl])

    @pl.when(e == 0)
    def _():
        for k in range(FFN_RING - 1):
            @pl.when(k < used)
            def _(k=k):
                in_cp(k).start(priority=1)

    @pl.when(n > 0)
    def _():
        wgb[...] = wg_ref[0, 0].astype(BF16)
        wub[...] = wu_ref[0, 0].astype(BF16)
        wdb[...] = wd_ref[0, 0].astype(BF16)

        def body(j, c):
            gb = b0 + j
            sl = gb % FFN_RING

            @pl.when(gb + FFN_RING - 1 < used)
            def _():
                in_cp(gb + FFN_RING - 1).start(priority=1)

            in_cp(gb).wait()

            @pl.when(gb >= FFN_RING)
            def _():
                out_cp(gb - FFN_RING).wait()

            xb = jnp.concatenate(_unpack_bf16_pair(xbuf[sl]), axis=1)
            hg = jnp.dot(xb, wgb[...], preferred_element_type=F32)
            hu = jnp.dot(xb, wub[...], preferred_element_type=F32)
            hid = (_silu(hg) * hu).astype(BF16)
            ybuf[sl] = jnp.dot(hid, wdb[...], preferred_element_type=F32)
            out_cp(gb).start(priority=1)
            return c

        lax.fori_loop(0, n, body, 0)

    @pl.when(e == n_exp - 1)
    def _():
        for k in range(FFN_RING, 0, -1):
            @pl.when(used >= k)
            def _(k=k):
                out_cp(used - k).wait()

        ybuf[0] = jnp.zeros((blk, ybuf.shape[2]), F32)

        def fill(jb, c):
            cp = pltpu.make_async_copy(ybuf.at[0], ys_hbm.at[pl.ds(jb * blk, blk), :], osem.at[0])
            cp.start()
            cp.wait()
            return c

        lax.fori_loop(used, nb_total, fill, 0)


def _moe_ffn(eblk, enb, xs, w_gate, w_up, w_down, layer):
    n_rows = xs.shape[0]
    n_exp, d, ff = w_gate.shape[1], w_gate.shape[2], w_gate.shape[3]
    blk = MOE_BLOCK
    wmap = lambda e, eb, en: (layer, e, 0, 0)
    grid_spec = pltpu.PrefetchScalarGridSpec(
        num_scalar_prefetch=2,
        grid=(n_exp,),
        in_specs=[pl.BlockSpec(memory_space=pl.ANY),
                  pl.BlockSpec((1, 1, d, ff), wmap),
                  pl.BlockSpec((1, 1, d, ff), wmap),
                  pl.BlockSpec((1, 1, ff, d), wmap)],
        out_specs=pl.BlockSpec(memory_space=pl.ANY),
        scratch_shapes=[pltpu.VMEM((FFN_RING, blk, d // 2), U32),
                        pltpu.VMEM((FFN_RING, blk, d), F32),
                        pltpu.VMEM((d, ff), BF16),
                        pltpu.VMEM((d, ff), BF16),
                        pltpu.VMEM((ff, d), BF16),
                        pltpu.SemaphoreType.DMA((FFN_RING,)),
                        pltpu.SemaphoreType.DMA((FFN_RING,))],
    )
    return pl.pallas_call(
        functools.partial(_ffn_kernel, blk=blk, nb_total=n_rows // blk, n_exp=n_exp),
        out_shape=jax.ShapeDtypeStruct((n_rows, d), F32),
        grid_spec=grid_spec,
        compiler_params=_cparams("arbitrary"),
        name="moe_ffn",
    )(eblk, enb, xs, w_gate, w_up, w_down)


def _gather_ln_kernel(dest_ref, ys_hbm, rg_ref, x_ref, g_ref, lng_ref, lnb_ref, o_ref, ybuf, sem,
                      *, alpha, tm, nt, n_steps):
    step = pl.program_id(0) * nt + pl.program_id(1)
    slot = step % 2

    def start_gather(tile_idx, sl):
        base = tile_idx * (tm * MOE_TOPK)
        for r in range(tm):
            for k in range(MOE_TOPK):
                pltpu.make_async_copy(ys_hbm.at[pl.ds(dest_ref[base + r * MOE_TOPK + k], 1), :],
                                      ybuf.at[sl, k, pl.ds(r, 1), :], sem.at[sl]).start()

    @pl.when(step == 0)
    def _():
        start_gather(0, 0)

    for sl in range(2):
        @pl.when(jnp.logical_and(step + 1 < n_steps, slot == 1 - sl))
        def _(sl=sl):
            start_gather(step + 1, sl)

    for k in range(MOE_TOPK):
        pltpu.make_async_copy(ys_hbm.at[pl.ds(0, tm), :], ybuf.at[slot, k], sem.at[slot]).wait()

    rg = rg_ref[0]
    y = ybuf[slot, 0] * rg[:, 0:1] + ybuf[slot, 1] * rg[:, 1:2]
    r = alpha * x_ref[0] + (1.0 + g_ref[0]) * y
    o_ref[0] = _layer_norm_rows(r, lng_ref[...], lnb_ref[...])


def _gather_combine_ln(dest_flat, ys, route_gates, x, gate, ln_g, ln_b, alpha, name):
    b, s, d = x.shape
    tm = _tile(s, 256)
    nt = s // tm
    row = lambda bi, i, dst: (bi, i, 0)
    const = lambda bi, i, dst: (0, 0)
    grid_spec = pltpu.PrefetchScalarGridSpec(
        num_scalar_prefetch=1,
        grid=(b, nt),
        in_specs=[pl.BlockSpec(memory_space=pl.ANY),
                  pl.BlockSpec((1, tm, LANES), row),
                  pl.BlockSpec((1, tm, d), row),
                  pl.BlockSpec((1, 1, d), lambda bi, i, dst: (bi, 0, 0)),
                  pl.BlockSpec((1, d), const),
                  pl.BlockSpec((1, d), const)],
        out_specs=pl.BlockSpec((1, tm, d), row),
        scratch_shapes=[pltpu.VMEM((2, MOE_TOPK, tm, d), F32), pltpu.SemaphoreType.DMA((2,))],
    )
    return pl.pallas_call(
        functools.partial(_gather_ln_kernel, alpha=alpha, tm=tm, nt=nt, n_steps=b * nt),
        out_shape=jax.ShapeDtypeStruct((b, s, d), F32),
        grid_spec=grid_spec,
        compiler_params=_cparams("arbitrary", "arbitrary"),
        name=name,
    )(dest_flat, ys, route_gates, x, gate, ln_g.reshape(1, d), ln_b.reshape(1, d))


def _gdn_kernel(q_ref, k_ref, v_ref, z_ref, bg_ref, cg_ref, hh_ref, ba_ref,
                cw_ref, alog_ref, dtb_ref, nw_ref, scw_ref,
                o_ref,
                qa_ref, ka_ref, va_ref, hist_ref, hist2_ref, gate_ref, state_ref,
                *, heads, ts, gw):
    L = GDN_CHUNK
    dk = GDN_HEAD_DIM

    @pl.when(pl.program_id(1) == 0)
    def _():
        hist_ref[...] = jnp.zeros_like(hist_ref)
        hist2_ref[...] = jnp.zeros_like(hist2_ref)
        state_ref[...] = jnp.zeros_like(state_ref)

    for idx, (src, dst) in enumerate(((q_ref, qa_ref), (k_ref, ka_ref), (v_ref, va_ref))):
        x = src[0]
        y = _causal_conv(x, hist_ref[idx], cw_ref[:, idx * gw:(idx + 1) * gw], GDN_CONV)
        hist_ref[idx] = x[ts - SUBLANES:ts, :]
        dst[...] = _silu(y)

    c = cg_ref[0] * hh_ref[0]
    yb = _causal_conv(c, hist2_ref[...], scw_ref[...], SC_CONV)
    hist2_ref[...] = c[ts - SUBLANES:ts, :]
    o_ref[0, :, gw:2 * gw] = bg_ref[0] * yb

    ba = ba_ref[0]
    gate_ref[0] = _sigmoid(ba)
    gate_ref[1] = -jnp.exp(alog_ref[...]) * _softplus(ba + dtb_ref[...])

    ii = lax.broadcasted_iota(I32, (L, L), 0)
    jj = lax.broadcasted_iota(I32, (L, L), 1)
    tri_incl = (ii >= jj).astype(F32)
    nchunk = ts // L
    pairs = [(c, h) for c in range(nchunk) for h in range(heads)]

    def tiles(ref, lead=()):
        return jnp.stack([ref[lead + (slice(c * L, (c + 1) * L), slice(h * dk, (h + 1) * dk))]
                          for c, h in pairs], axis=0)

    def bmm(a, b):
        return jnp.einsum('bij,bjk->bik', a.astype(BF16), b.astype(BF16), preferred_element_type=F32)

    def bmm_nt(a, b):
        return jnp.einsum('bik,bjk->bij', a.astype(BF16), b.astype(BF16), preferred_element_type=F32)

    qh = tiles(qa_ref)
    kh = tiles(ka_ref)
    vh = tiles(va_ref)
    beta = jnp.stack([gate_ref[0, c * L:(c + 1) * L, h:h + 1] for c, h in pairs], axis=0)
    cums = [jnp.dot(tri_incl, gate_ref[1, c * L:(c + 1) * L, :], precision=HIGHEST,
                    preferred_element_type=F32) for c in range(nchunk)]
    cums_t = [cm.T for cm in cums]
    gc = jnp.stack([cums[c][:, heads + h:heads + h + 1] for c, h in pairs], axis=0)
    gl = gc[:, L - 1:L, :]
    egc = jnp.exp(gc)
    qn = qh * lax.rsqrt(jnp.sum(qh * qh, -1, keepdims=True) + 1e-6)
    kn = kh * lax.rsqrt(jnp.sum(kh * kh, -1, keepdims=True) + 1e-6)
    qc = qn * (dk ** -0.5)
    kb = kn * beta

    nb2 = len(pairs) // 2

    def pair(x):
        return x.reshape(nb2, 2 * L, x.shape[-1])

    i2 = lax.broadcasted_iota(I32, (2 * L, 2 * L), 0)
    j2 = lax.broadcasted_iota(I32, (2 * L, 2 * L), 1)
    same = (i2 < L) == (j2 < L)
    causal2 = same & (i2 >= j2)
    strict2 = same & (i2 > j2)
    eye2 = (i2 == j2).astype(F32)
    gc2, beta2, egc2, kn2, kb2 = pair(gc), pair(beta), pair(egc), pair(kn), pair(kb)
    gr2 = jnp.stack([jnp.concatenate([cums_t[c][heads + h:heads + h + 1, :],
                                      cums_t[c][heads + h + 1:heads + h + 2, :]], axis=1)
                     for c in range(nchunk) for h in range(0, heads, 2)], axis=0)
    decay2 = jnp.where(causal2, jnp.exp(jnp.where(causal2, gc2 - gr2, 0.0)), 0.0)
    m = jnp.where(strict2, bmm_nt(kb2, kn2) * decay2, 0.0)
    t_inv = eye2 - m
    mp = m
    for _ in range(5):
        mp = bmm(mp, mp)
        t_inv = t_inv + bmm(t_inv, mp)
    sol = bmm(t_inv, jnp.concatenate([pair(vh) * beta2, kb2 * egc2], axis=2))
    u2 = sol[:, :, 0:dk]
    w2 = sol[:, :, dk:2 * dk]
    qk2 = jnp.where(causal2, bmm_nt(pair(qc), kn2) * decay2, 0.0)
    qg2 = pair(qc * egc)
    k_dec2 = pair(kn * jnp.exp(gl - gc))
    egl = jnp.exp(gl)
    zero_blk = jnp.zeros((L, dk), F32)

    def own_blocks(x):
        return jnp.concatenate([x[0:L, 0:dk], x[L:2 * L, dk:2 * dk]], axis=0)

    state = [state_ref[h] for h in range(heads)]
    outs = []
    for p in range(nb2):
        ba, bb = 2 * p, 2 * p + 1
        ha, hb = pairs[ba][1], pairs[bb][1]
        s_cat = jnp.concatenate([state[ha], state[hb]], axis=1)
        v_new = u2[p] - own_blocks(_dot(w2[p], s_cat))
        outs.append(own_blocks(_dot(qg2[p], s_cat)) + _dot(qk2[p], v_new))
        kd = k_dec2[p]
        k_bd = jnp.concatenate([jnp.concatenate([kd[0:L], zero_blk], axis=1),
                                jnp.concatenate([zero_blk, kd[L:2 * L]], axis=1)], axis=0)
        upd = _dot(k_bd.T, v_new)
        state[ha] = state[ha] * egl[ba] + upd[0:dk]
        state[hb] = state[hb] * egl[bb] + upd[dk:2 * dk]
    for h in range(heads):
        state_ref[h] = state[h]

    o = jnp.stack(outs, axis=0).reshape(len(pairs), L, dk)
    on = o * lax.rsqrt(jnp.mean(o * o, -1, keepdims=True) + 1e-6) * nw_ref[...]
    y = on * _silu(tiles(z_ref, (0,)))
    for b, (c, h) in enumerate(pairs):
        o_ref[0, c * L:(c + 1) * L, h * dk:(h + 1) * dk] = y[b]


def _gdn_shortconv(proj, conv_w, a_log, dt_bias, norm_w, sc_w, heads):
    b, s, _ = proj.shape
    gw = heads * GDN_HEAD_DIM
    ts = _tile(s, 256)
    alog_p = jnp.zeros((1, LANES), F32).at[0, heads:2 * heads].set(a_log)
    dtb_p = jnp.zeros((1, LANES), F32).at[0, heads:2 * heads].set(dt_bias)
    col = lambda cb: (lambda bi, i: (bi, i, cb))
    const = lambda bi, i: (0, 0)
    kern = functools.partial(_gdn_kernel, heads=heads, ts=ts, gw=gw)
    return pl.pallas_call(
        kern,
        out_shape=jax.ShapeDtypeStruct((b, s, 2 * gw), F32),
        grid=(b, s // ts),
        in_specs=[pl.BlockSpec((1, ts, gw), col(cb)) for cb in range(7)]
        + [pl.BlockSpec((1, ts, LANES), col(7 * gw // LANES)),
           pl.BlockSpec((GDN_CONV, 3 * gw), const),
           pl.BlockSpec((1, LANES), const),
           pl.BlockSpec((1, LANES), const),
           pl.BlockSpec((1, GDN_HEAD_DIM), const),
           pl.BlockSpec((SC_CONV, gw), const)],
        out_specs=pl.BlockSpec((1, ts, 2 * gw), lambda bi, i: (bi, i, 0)),
        scratch_shapes=[pltpu.VMEM((ts, gw), F32),
                        pltpu.VMEM((ts, gw), F32),
                        pltpu.VMEM((ts, gw), F32),
                        pltpu.VMEM((3, SUBLANES, gw), F32),
                        pltpu.VMEM((SUBLANES, gw), F32),
                        pltpu.VMEM((2, ts, LANES), F32),
                        pltpu.VMEM((heads, GDN_HEAD_DIM, GDN_HEAD_DIM), F32)],
        compiler_params=_cparams("parallel", "arbitrary"),
        name="gdn_shortconv",
    )(proj, proj, proj, proj, proj, proj, proj, proj,
      conv_w, alog_p, dtb_p, norm_w.reshape(1, GDN_HEAD_DIM), sc_w)


def _rope_kernel(pos_ref, inv_ref, qpe_ref, kpe_ref, qo_ref, ko_ref, *, heads):
    ang = pos_ref[0].astype(F32) * inv_ref[...]
    lane = lax.broadcasted_iota(I32, ang.shape, 1)
    half = MLA_ROPE // 2
    cos = jnp.where(lane < MLA_ROPE, jnp.cos(ang), 0.0)
    sin = jnp.sin(ang)
    sgn = jnp.where(lane < half, -sin, jnp.where(lane < MLA_ROPE, sin, 0.0))

    def rot(x):
        swapped = jnp.where(lane < half, pltpu.roll(x, LANES - half, 1), pltpu.roll(x, half, 1))
        return x * cos + swapped * sgn

    for h in range(heads):
        cols = slice(h * LANES, (h + 1) * LANES)
        qo_ref[0, :, cols] = rot(qpe_ref[0, :, cols]).astype(BF16)
    ko_ref[0] = rot(kpe_ref[0]).astype(BF16)


def _rope(positions, qpe, heads, cproj, kpe_block):
    b, s = positions.shape
    ts = _tile(s, 512)
    half = MLA_ROPE // 2
    inv = ROPE_THETA ** (-jnp.arange(half, dtype=F32) * (2.0 / MLA_ROPE))
    inv_tab = jnp.zeros((1, LANES), F32).at[0, 0:half].set(inv).at[0, half:MLA_ROPE].set(inv)
    w = heads * LANES
    return pl.pallas_call(
        functools.partial(_rope_kernel, heads=heads),
        out_shape=(jax.ShapeDtypeStruct((b, s, w), BF16), jax.ShapeDtypeStruct((b, s, LANES), BF16)),
        grid=(b, s // ts),
        in_specs=[pl.BlockSpec((1, ts, 1), lambda bi, i: (bi, i, 0)),
                  pl.BlockSpec((1, LANES), lambda bi, i: (0, 0)),
                  pl.BlockSpec((1, ts, w), lambda bi, i: (bi, i, 0)),
                  pl.BlockSpec((1, ts, LANES), lambda bi, i: (bi, i, kpe_block))],
        out_specs=(pl.BlockSpec((1, ts, w), lambda bi, i: (bi, i, 0)),
                   pl.BlockSpec((1, ts, LANES), lambda bi, i: (bi, i, 0))),
        compiler_params=_cparams("parallel", "parallel"),
        name="mla_rope",
    )(positions.reshape(b, s, 1), inv_tab, qpe, cproj)


def _attn_kernel(qn_ref, qp_ref, kn_ref, kp_ref, v_ref, o_ref, kcat_ref, s_ref, *, tq, hq, nq, scale):
    qi = pl.program_id(2)

    @pl.when(qi == 0)
    def _():
        kcat_ref[:, 0:MLA_NOPE] = kn_ref[0]
        kcat_ref[:, MLA_NOPE:MLA_NOPE + LANES] = kp_ref[0]

    qcat = jnp.concatenate([qn_ref[0], qp_ref[0]], axis=1)
    tri = lax.broadcasted_iota(I32, (hq, hq), 0) >= lax.broadcasted_iota(I32, (hq, hq), 1)

    for qv in range(nq):
        @pl.when(qi == qv)
        def _(qv=qv):
            for h in range(tq // hq):
                rows = slice(h * hq, (h + 1) * hq)
                nk = qv * tq + (h + 1) * hq
                for c0 in range(0, nk, ATTN_KEY_CHUNK):
                    c1 = min(c0 + ATTN_KEY_CHUNK, nk)
                    s = lax.dot_general(qcat[rows], kcat_ref[c0:c1, :], (((1,), (1,)), ((), ())),
                                        preferred_element_type=F32) * scale
                    if c1 == nk:
                        w = c1 - c0
                        last = jnp.where(tri, s[:, w - hq:], -jnp.inf)
                        s = last if w == hq else jnp.concatenate([s[:, :w - hq], last], axis=1)
                    s_ref[rows, c0:c1] = s
                    mb = jnp.max(s, -1, keepdims=True)
                    m = mb if c0 == 0 else jnp.maximum(m, mb)
                l = jnp.zeros((hq, 1), F32)
                acc = jnp.zeros((hq, MLA_V), F32)
                for c0 in range(0, nk, hq):
                    p = jnp.exp(s_ref[rows, c0:c0 + hq] - m)
                    l = l + jnp.sum(p, -1, keepdims=True)
                    acc = acc + jnp.dot(p.astype(BF16), v_ref[0, c0:c0 + hq, :], preferred_element_type=F32)
                o_ref[0, rows, :] = acc / l


def _attention(qn, qpe, kvup, kpe, heads):
    b, s, _ = qn.shape
    tq = _tile(s, 1024)
    hq = _tile(tq, 256)
    nq = s // tq
    scale = (MLA_NOPE + MLA_ROPE) ** -0.5
    return pl.pallas_call(
        functools.partial(_attn_kernel, tq=tq, hq=hq, nq=nq, scale=scale),
        out_shape=jax.ShapeDtypeStruct((b, s, heads * MLA_V), F32),
        grid=(b, heads, nq),
        in_specs=[pl.BlockSpec((1, tq, MLA_NOPE), lambda bi, h, i: (bi, i, h)),
                  pl.BlockSpec((1, tq, LANES), lambda bi, h, i: (bi, i, h)),
                  pl.BlockSpec((1, s, MLA_NOPE), lambda bi, h, i: (bi, 0, h)),
                  pl.BlockSpec((1, s, LANES), lambda bi, h, i: (bi, 0, 0)),
                  pl.BlockSpec((1, s, MLA_V), lambda bi, h, i: (bi, 0, heads + h))],
        out_specs=pl.BlockSpec((1, tq, MLA_V), lambda bi, h, i: (bi, i, h)),
        scratch_shapes=[pltpu.VMEM((s, MLA_NOPE + LANES), BF16), pltpu.VMEM((tq, s), F32)],
        compiler_params=_cparams("parallel", "parallel", "arbitrary"),
        name="mla_attention",
    )(qn, qpe, kvup, kpe, kvup)


def _hyb_in_weight(w_in, heads):
    d = w_in.shape[0]
    gw = heads * GDN_HEAD_DIM
    main = jnp.concatenate([w_in[:, :4 * gw], w_in[:, 4 * gw + 2 * heads:]], axis=1)
    gates = w_in[:, 4 * gw:4 * gw + 2 * heads]
    n = main.shape[1] + LANES
    n_pad = -(-n // 768) * 768
    tail = jnp.zeros((d, n_pad - main.shape[1]), w_in.dtype).at[:, :2 * heads].set(gates)
    return jnp.concatenate([main, tail], axis=1).astype(BF16)


def _mla_in_weight(w_in):
    d = w_in.shape[0]
    pad = jnp.zeros((d, LANES - MLA_ROPE), w_in.dtype)
    return jnp.concatenate([w_in, pad], axis=1).astype(BF16)


def _mla_uq_weights(w_uq, heads):
    r = w_uq.shape[0]
    w = w_uq.reshape(r, heads, MLA_NOPE + MLA_ROPE)
    nope = w[:, :, :MLA_NOPE].reshape(r, heads * MLA_NOPE)
    pe = jnp.concatenate([w[:, :, MLA_NOPE:], jnp.zeros((r, heads, LANES - MLA_ROPE), w.dtype)], axis=2)
    return nope.astype(BF16), pe.reshape(r, heads * LANES).astype(BF16)


def _mla_ukv_weight(w_ukv, heads):
    r = w_ukv.shape[0]
    w = w_ukv.reshape(r, heads, MLA_NOPE + MLA_V)
    return jnp.concatenate([w[:, :, :MLA_NOPE].reshape(r, heads * MLA_NOPE),
                            w[:, :, MLA_NOPE:].reshape(r, heads * MLA_V)], axis=1).astype(BF16)


def _router_weight(wr_g, br_g, wr_e, br_e):
    d = wr_g.shape[0]
    n = wr_g.shape[1] + wr_e.shape[1]
    w = jnp.zeros((d, LANES), F32).at[:, :n].set(jnp.concatenate([wr_g, wr_e], axis=1))
    bias = jnp.zeros((1, LANES), F32).at[0, :n].set(jnp.concatenate([br_g, br_e]))
    w_hi = w.astype(BF16)
    w_lo = (w - w_hi.astype(F32)).astype(BF16)
    return jnp.concatenate([w_hi, w_lo], axis=1), bias


def kernel(x, c, positions, ada_w, ada_b, ln_g, ln_b, hyb_w_in, gdn_conv_w, gdn_a_log, gdn_dt_bias,
           gdn_norm_w, sc_conv_w, hyb_w_out, mla_w_in, mla_q_norm, mla_kv_norm, mla_w_uq, mla_w_ukv,
           mla_w_out, moe_router_g, moe_bias_g, moe_router_e, moe_bias_e, moe_w_gate, moe_w_up,
           moe_w_down):
    b, s, d = x.shape
    depth = ada_w.shape[0]
    alpha = (2.0 * depth) ** 0.25
    gdn_heads = d // (2 * GDN_HEAD_DIM)
    mla_heads = mla_w_out.shape[1] // MLA_V
    n_groups = moe_router_g.shape[2]
    n_exp = moe_router_e.shape[2]
    per_group = n_exp // n_groups

    mod = _modulation(c, ada_w, ada_b)
    t = b * s
    xs = jnp.zeros((t * MOE_TOPK + n_exp * MOE_BLOCK, d // 2), U32)
    for layer in range(depth):
        sh1, sc1, g1, sh2, sc2, g2 = [mod[layer, :, j * d:(j + 1) * d].reshape(b, 1, d) for j in range(6)]
        i = layer // 2
        wr, br = _router_weight(moe_router_g[layer], moe_bias_g[layer], moe_router_e[layer], moe_bias_e[layer])
        if layer % 2 == 0:
            proj = _mod_matmul(x, sc1, sh1, _hyb_in_weight(hyb_w_in[i], gdn_heads), "hyb_in_proj")
            y = _gdn_shortconv(proj, gdn_conv_w[i], gdn_a_log[i], gdn_dt_bias[i], gdn_norm_w[i],
                               sc_conv_w[i], gdn_heads)
            w_out = hyb_w_out[i]
        else:
            cproj = _mod_matmul(x, sc1, sh1, _mla_in_weight(mla_w_in[i]), "mla_in_proj")
            w_qn, w_qpe = _mla_uq_weights(mla_w_uq[i], mla_heads)
            qn = _rms_matmul(cproj, 0, MLA_Q_RANK, mla_q_norm[i], w_qn, BF16, "mla_q_nope_up")
            qpe_raw = _rms_matmul(cproj, 0, MLA_Q_RANK, mla_q_norm[i], w_qpe, F32, "mla_q_rope_up")
            kvup = _rms_matmul(cproj, 1, MLA_KV_RANK, mla_kv_norm[i], _mla_ukv_weight(mla_w_ukv[i], mla_heads),
                               BF16, "mla_kv_up")
            qpe, kpe = _rope(positions, qpe_raw, mla_heads, cproj, (MLA_Q_RANK + MLA_KV_RANK) // LANES)
            y = _attention(qn, qpe, kvup, kpe, mla_heads)
            w_out = mla_w_out[i]
        x, hin2, ids, gates = _out_ln_route(y, w_out.astype(BF16), x, g1, ln_g[layer, 0], ln_b[layer, 0],
                                            sc2, sh2, wr, br, alpha, n_groups, per_group,
                                            "mixer_out_ln_route")
        dest, meta = _dispatch(ids.reshape(t, LANES))
        dest_flat = dest[:, :MOE_TOPK].reshape(-1)
        eblk = meta[0, :n_exp] // MOE_BLOCK
        enb = meta[1, :n_exp]
        xs = _scatter_rows(dest_flat, hin2.reshape(t, d // 2), xs)
        ys = _moe_ffn(eblk, enb, xs, moe_w_gate, moe_w_up, moe_w_down, layer)
        x = _gather_combine_ln(dest_flat, ys, gates, x, g2, ln_g[layer, 1], ln_b[layer, 1], alpha,
                               "moe_gather_combine_ln")
    return x
```
